```python
import jax, jax.numpy as jnp
from jax import lax
import numpy as np

D_MODEL = 2048
BATCH = 2
SEQ = 16384
DEPTH = 2

GRID_W = 64
CTX_LEN = 256
N_MIXERS = 2
D_FF = 4 * D_MODEL
ROPE_BASE = 10000.0
NORM_EPS = 1e-6
Q_BLOCK = 128
NEG_INF = -1e30

MLA_HEADS = 16
MLA_Q_LORA = 512
MLA_KV_LORA = 512
MLA_NOPE = 128
MLA_ROPE = 64
MLA_V = 128

SWA_HEADS = 32
SWA_KV_HEADS = 4
SWA_HEAD_DIM = 64
SWA_WINDOW = 128

N_MLA_LAYERS = (DEPTH + 1) // 2
N_SWA_LAYERS = DEPTH // 2

kernel_name = 'hybrid_mla_swa_dit'


def rmsnorm(x, g):
    xf = x.astype(jnp.float32)
    y = xf * lax.rsqrt(jnp.mean(xf * xf, axis=-1, keepdims=True) + NORM_EPS)
    return (y * g.astype(jnp.float32)).astype(x.dtype)


def axial_angles(n_tokens, rot_dim):
    rows = n_tokens // GRID_W
    row = jnp.repeat(jnp.arange(rows, dtype=jnp.float32), GRID_W)
    col = jnp.tile(jnp.arange(GRID_W, dtype=jnp.float32), rows)
    n_freq = rot_dim // 4
    freqs = ROPE_BASE ** (-jnp.arange(n_freq, dtype=jnp.float32) / n_freq)
    return jnp.concatenate([row[:, None] * freqs, col[:, None] * freqs], axis=-1)


def apply_rope(x, cos, sin):
    half = x.shape[-1] // 2
    x1, x2 = x[..., :half], x[..., half:]
    return jnp.concatenate([x1 * cos - x2 * sin, x1 * sin + x2 * cos], axis=-1)


def mla_mixer(h_lat, h_ctx, w_in, g_qa, g_kva, w_qb, w_kvb, w_out, cos, sin, need_ctx):
    B, S, _ = h_lat.shape
    n_qk = MLA_NOPE + MLA_ROPE
    scale = n_qk ** -0.5

    def queries(q_a):
        q = (rmsnorm(q_a, g_qa) @ w_qb).reshape(q_a.shape[0], q_a.shape[1], MLA_HEADS, n_qk)
        return q[..., :MLA_NOPE], q[..., MLA_NOPE:]

    def keys_values(c_kv):
        kv = (rmsnorm(c_kv, g_kva) @ w_kvb).reshape(c_kv.shape[0], c_kv.shape[1], MLA_HEADS, MLA_NOPE + MLA_V)
        return kv[..., :MLA_NOPE], kv[..., MLA_NOPE:]

    def attend(qn, qr, kn, kr, v):
        s = jnp.einsum('bqhd,bkhd->bhqk', qn, kn) + jnp.einsum('bqhr,bkr->bhqk', qr, kr)
        p = jax.nn.softmax(s.astype(jnp.float32) * scale, axis=-1).astype(v.dtype)
        o = jnp.einsum('bhqk,bkhv->bqhv', p, v)
        return o.reshape(o.shape[0], o.shape[1], MLA_HEADS * MLA_V)

    p_l = h_lat @ w_in
    qn_l, qr_l = queries(p_l[..., :MLA_Q_LORA])
    qr_l = apply_rope(qr_l, cos[:, None], sin[:, None])
    kn_l, v_l = keys_values(p_l[..., MLA_Q_LORA:MLA_Q_LORA + MLA_KV_LORA])
    kr_l = apply_rope(p_l[..., MLA_Q_LORA + MLA_KV_LORA:], cos, sin)

    p_c = h_ctx @ (w_in if need_ctx else w_in[:, MLA_Q_LORA:])
    p_c_kv = p_c[..., p_c.shape[-1] - (MLA_KV_LORA + MLA_ROPE):]
    kn_c, v_c = keys_values(p_c_kv[..., :MLA_KV_LORA])
    kr_c = p_c_kv[..., MLA_KV_LORA:]

    kn_all = jnp.concatenate([kn_c, kn_l], axis=1)
    kr_all = jnp.concatenate([kr_c, kr_l], axis=1)
    v_all = jnp.concatenate([v_c, v_l], axis=1)

    def block(i):
        st = i * Q_BLOCK
        qn = lax.dynamic_slice_in_dim(qn_l, st, Q_BLOCK, axis=1)
        qr = lax.dynamic_slice_in_dim(qr_l, st, Q_BLOCK, axis=1)
        return attend(qn, qr, kn_all, kr_all, v_all)

    o_l = lax.map(block, jnp.arange(S // Q_BLOCK))
    o_l = jnp.moveaxis(o_l, 0, 1).reshape(B, S, MLA_HEADS * MLA_V)
    out_l = o_l @ w_out
    if need_ctx:
        qn_c, qr_c = queries(p_c[..., :MLA_Q_LORA])
        out_c = attend(qn_c, qr_c, kn_c, kr_c, v_c) @ w_out
    else:
        out_c = None
    return out_l, out_c


def swa_mixer(h_lat, h_ctx, w_qkv, sink, w_out, cos, sin, need_ctx):
    B, S, _ = h_lat.shape
    L = h_ctx.shape[1]
    G = SWA_HEADS // SWA_KV_HEADS
    dq = SWA_HEADS * SWA_HEAD_DIM
    dkv = SWA_KV_HEADS * SWA_HEAD_DIM
    scale = SWA_HEAD_DIM ** -0.5
    sink_g = sink.astype(jnp.float32).reshape(SWA_KV_HEADS, G)

    def attend(q, k, v, mask):
        s = jnp.einsum('bqkgd,bskd->bkgqs', q, k).astype(jnp.float32) * scale
        if mask is not None:
            s = jnp.where(mask, s, NEG_INF)
        sk = jnp.broadcast_to(sink_g[None, :, :, None, None], s.shape[:-1] + (1,))
        p = jax.nn.softmax(jnp.concatenate([s, sk], axis=-1), axis=-1)[..., :-1].astype(v.dtype)
        o = jnp.einsum('bkgqs,bskd->bqkgd', p, v)
        return o.reshape(o.shape[0], o.shape[1], dq)

    p_l = h_lat @ w_qkv
    q_l = apply_rope(p_l[..., :dq].reshape(B, S, SWA_KV_HEADS, G, SWA_HEAD_DIM), cos[:, None, None], sin[:, None, None])
    k_l = apply_rope(p_l[..., dq:dq + dkv].reshape(B, S, SWA_KV_HEADS, SWA_HEAD_DIM), cos[:, None], sin[:, None])
    v_l = p_l[..., dq + dkv:].reshape(B, S, SWA_KV_HEADS, SWA_HEAD_DIM)

    p_c = h_ctx @ (w_qkv if need_ctx else w_qkv[:, dq:])
    p_c_kv = p_c[..., p_c.shape[-1] - 2 * dkv:]
    k_c = p_c_kv[..., :dkv].reshape(B, L, SWA_KV_HEADS, SWA_HEAD_DIM)
    v_c = p_c_kv[..., dkv:].reshape(B, L, SWA_KV_HEADS, SWA_HEAD_DIM)

    pad = ((0, 0), (SWA_WINDOW, SWA_WINDOW), (0, 0), (0, 0))
    kp = jnp.pad(k_l, pad)
    vp = jnp.pad(v_l, pad)
    span = Q_BLOCK + 2 * SWA_WINDOW
    ctx_mask = jnp.ones((Q_BLOCK, L), dtype=bool)

    def block(i):
        st = i * Q_BLOCK
        q = lax.dynamic_slice_in_dim(q_l, st, Q_BLOCK, axis=1)
        kw = lax.dynamic_slice_in_dim(kp, st, span, axis=1)
        vw = lax.dynamic_slice_in_dim(vp, st, span, axis=1)
        qpos = st + jnp.arange(Q_BLOCK)
        kpos = st - SWA_WINDOW + jnp.arange(span)
        band = (jnp.abs(qpos[:, None] - kpos[None, :]) <= SWA_WINDOW) & (kpos >= 0)[None, :] & (kpos < S)[None, :]
        mask = jnp.concatenate([ctx_mask, band], axis=1)
        return attend(q, jnp.concatenate([k_c, kw], axis=1), jnp.concatenate([v_c, vw], axis=1), mask)

    o_l = lax.map(block, jnp.arange(S // Q_BLOCK))
    o_l = jnp.moveaxis(o_l, 0, 1).reshape(B, S, dq)
    out_l = o_l @ w_out
    if need_ctx:
        q_c = p_c[..., :dq].reshape(B, L, SWA_KV_HEADS, G, SWA_HEAD_DIM)
        out_c = attend(q_c, k_c, v_c, None) @ w_out
    else:
        out_c = None
    return out_l, out_c


def sq_relu_mlp(h, w_in, w_out):
    return jnp.square(jax.nn.relu(h @ w_in)) @ w_out


def _w(key, shape, fan_in, scale=1.0):
    return jax.random.normal(key, shape, jnp.float32) * (scale * fan_in ** -0.5)


def setup_inputs(seed: int = 0) -> dict:
    key = jax.random.key(seed)
    ks = jax.random.split(key, 20)
    D = D_MODEL
    return {
        'x': jax.random.normal(ks[0], (BATCH, SEQ, D), jnp.float32),
        'c': jax.random.normal(ks[1], (BATCH, D), jnp.float32),
        'ctx': jax.random.normal(ks[2], (BATCH, CTX_LEN, D), jnp.float32),
        'c_ctx': jax.random.normal(ks[3], (D,), jnp.float32),
        'w_mod': _w(ks[4], (DEPTH, D, 6 * D), D, 0.5),
        'b_mod': 0.02 * jax.random.normal(ks[5], (DEPTH, 6 * D), jnp.float32),
        'g_norm': 1.0 + 0.1 * jax.random.normal(ks[6], (DEPTH, 4, D), jnp.float32),
        'w_ff_in': _w(ks[7], (DEPTH, D, D_FF), D),
        'w_ff_out': _w(ks[8], (DEPTH, D_FF, D), D_FF),
        'mla_w_in': _w(ks[9], (N_MLA_LAYERS, D, MLA_Q_LORA + MLA_KV_LORA + MLA_ROPE), D),
        'mla_g_qa': 1.0 + 0.1 * jax.random.normal(ks[10], (N_MLA_LAYERS, MLA_Q_LORA), jnp.float32),
        'mla_g_kva': 1.0 + 0.1 * jax.random.normal(ks[11], (N_MLA_LAYERS, MLA_KV_LORA), jnp.float32),
        'mla_w_qb': _w(ks[12], (N_MLA_LAYERS, MLA_Q_LORA, MLA_HEADS * (MLA_NOPE + MLA_ROPE)), MLA_Q_LORA),
        'mla_w_kvb': _w(ks[13], (N_MLA_LAYERS, MLA_KV_LORA, MLA_HEADS * (MLA_NOPE + MLA_V)), MLA_KV_LORA),
        'mla_w_out': _w(ks[14], (N_MLA_LAYERS, MLA_HEADS * MLA_V, D), MLA_HEADS * MLA_V),
        'swa_w_qkv': _w(ks[15], (N_SWA_LAYERS, D, (SWA_HEADS + 2 * SWA_KV_HEADS) * SWA_HEAD_DIM), D),
        'swa_sink': 0.5 * jax.random.normal(ks[16], (N_SWA_LAYERS, SWA_HEADS), jnp.float32),
        'swa_w_out': _w(ks[17], (N_SWA_LAYERS, SWA_HEADS * SWA_HEAD_DIM, D), SWA_HEADS * SWA_HEAD_DIM),
    }


def reference(x, c, ctx, c_ctx, w_mod, b_mod, g_norm, w_ff_in, w_ff_out,
              mla_w_in, mla_g_qa, mla_g_kva, mla_w_qb, mla_w_kvb, mla_w_out,
              swa_w_qkv, swa_sink, swa_w_out):
    S = x.shape[1]
    ang_mla = axial_angles(S, MLA_ROPE)
    ang_swa = axial_angles(S, SWA_HEAD_DIM)
    cos_mla, sin_mla = jnp.cos(ang_mla).astype(x.dtype), jnp.sin(ang_mla).astype(x.dtype)
    cos_swa, sin_swa = jnp.cos(ang_swa).astype(x.dtype), jnp.sin(ang_swa).astype(x.dtype)

    s = ctx
    for i in range(DEPTH):
        need_ctx = i < DEPTH - 1
        mod_l = (jax.nn.silu(c) @ w_mod[i] + b_mod[i])[:, None, :]
        mod_c = jax.nn.silu(c_ctx) @ w_mod[i] + b_mod[i]
        sh_a, sc_a, gt_a, sh_f, sc_f, gt_f = jnp.split(mod_l, 6, axis=-1)
        csh_a, csc_a, cgt_a, csh_f, csc_f, cgt_f = jnp.split(mod_c, 6, axis=-1)

        h_l = rmsnorm(x, g_norm[i, 0]) * (1.0 + sc_a) + sh_a
        h_c = rmsnorm(s, g_norm[i, 0]) * (1.0 + csc_a) + csh_a
        j = i // N_MIXERS
        if i % N_MIXERS == 0:
            y_l, y_c = mla_mixer(h_l, h_c, mla_w_in[j], mla_g_qa[j], mla_g_kva[j], mla_w_qb[j],
                                 mla_w_kvb[j], mla_w_out[j], cos_mla, sin_mla, need_ctx)
        else:
            y_l, y_c = swa_mixer(h_l, h_c, swa_w_qkv[j], swa_sink[j], swa_w_out[j],
                                 cos_swa, sin_swa, need_ctx)
        x = x + gt_a * rmsnorm(y_l, g_norm[i, 1])

        f_l = rmsnorm(x, g_norm[i, 2]) * (1.0 + sc_f) + sh_f
        x = x + gt_f * rmsnorm(sq_relu_mlp(f_l, w_ff_in[i], w_ff_out[i]), g_norm[i, 3])

        if need_ctx:
            s = s + cgt_a * rmsnorm(y_c, g_norm[i, 1])
            f_c = rmsnorm(s, g_norm[i, 2]) * (1.0 + csc_f) + csh_f
            s = s + cgt_f * rmsnorm(sq_relu_mlp(f_c, w_ff_in[i], w_ff_out[i]), g_norm[i, 3])
    return x
```

```python
import functools
import math

import jax
import jax.numpy as jnp
from jax import lax
from jax.experimental import pallas as pl
from jax.experimental.pallas import tpu as pltpu

F32 = jnp.float32
BF16 = jnp.bfloat16

GRID_W = 64
ROPE_BASE = 10000.0
NORM_EPS = 1e-6
LOG2E = math.log2(math.e)
NEG_BIG = -1e30

MLA_HEADS = 16
MLA_Q_LORA = 512
MLA_KV_LORA = 512
MLA_NOPE = 128
MLA_ROPE = 64
MLA_V = 128

SWA_HEADS = 32
SWA_KV_HEADS = 4
SWA_HEAD_DIM = 64
SWA_WINDOW = 128
SWA_GROUP = SWA_HEADS // SWA_KV_HEADS
SWA_QBLK = 128

LANES = 128
HALF_TILE = 64

MOD_ROWS = 8
MOD_TN = 1024
TOKEN_TILE = 512
FF_TILE = 512
ATTN_TQ = 512
VMEM_LIMIT = 56 * 1024 * 1024

NT_DIMS = (((1,), (1,)), ((), ()))


def _rms(xf, g):
    ms = jnp.mean(xf * xf, axis=-1, keepdims=True)
    return xf * lax.rsqrt(ms + NORM_EPS) * g


def _rot(tile, cos, sin):
    return tile * cos + pltpu.roll(tile, HALF_TILE, 1) * sin


def _first_head_lanes(shape):
    lane = lax.broadcasted_iota(jnp.int32, shape, 1)
    return (lane & 32) == 0


def _params(sem):
    return pltpu.CompilerParams(dimension_semantics=sem, vmem_limit_bytes=VMEM_LIMIT)


def _const_spec(shape):
    nd = len(shape)
    return pl.BlockSpec(shape, lambda *_: (0,) * nd, pipeline_mode=pl.Buffered(1))


def _mod_spec(grp, which, d):
    return pl.BlockSpec((1, 1, d), lambda i, *_: (grp(i), 0, which))


def _mod_kernel(c_ref, w_ref, b_ref, o_ref):
    c = c_ref[...]
    a = c / (1.0 + jnp.exp(-c))
    o_ref[0] = jnp.dot(a, w_ref[0], preferred_element_type=F32,
                       precision=lax.Precision.HIGHEST) + b_ref[0]


def _modulation(cmat, w_mod, b_mod):
    depth, d, n = w_mod.shape
    return pl.pallas_call(
        _mod_kernel,
        grid=(depth, n // MOD_TN),
        in_specs=[pl.BlockSpec((MOD_ROWS, d), lambda l, j: (0, 0)),
                  pl.BlockSpec((1, d, MOD_TN), lambda l, j: (l, 0, j)),
                  pl.BlockSpec((1, 1, MOD_TN), lambda l, j: (l, 0, j))],
        out_specs=pl.BlockSpec((1, MOD_ROWS, MOD_TN), lambda l, j: (l, 0, j)),
        out_shape=jax.ShapeDtypeStruct((depth, MOD_ROWS, n), F32),
        compiler_params=_params(("arbitrary", "arbitrary")),
        name="modulation",
    )(cmat, w_mod, b_mod.reshape(depth, 1, n))


def _mla_proj_kernel(*refs, rope, qscale):
    if rope:
        (x_ref, g_ref, sc_ref, sh_ref, win_ref, gqa_ref, gkva_ref, wqn_ref, wqr_ref, wkn_ref,
         wvt_ref, cos_ref, sin_ref, qn_ref, qr_ref, kn_ref, kr_ref, vt_ref) = refs
        cos, sin = cos_ref[...], sin_ref[...]
    else:
        (x_ref, g_ref, sc_ref, sh_ref, win_ref, gqa_ref, gkva_ref, wqn_ref, wqr_ref, wkn_ref,
         wvt_ref, qn_ref, qr_ref, kn_ref, kr_ref, vt_ref) = refs
    h = (_rms(x_ref[...], g_ref[...]) * (1.0 + sc_ref[0]) + sh_ref[0]).astype(BF16)
    p = jnp.dot(h, win_ref[...], preferred_element_type=F32)
    qa = _rms(p[:, :MLA_Q_LORA], gqa_ref[...]).astype(BF16)
    ckv = _rms(p[:, MLA_Q_LORA:MLA_Q_LORA + MLA_KV_LORA], gkva_ref[...]).astype(BF16)
    kr = p[:, MLA_Q_LORA + MLA_KV_LORA:]

    qn = jnp.dot(qa, wqn_ref[...], preferred_element_type=F32) * qscale
    qn_ref[...] = qn.astype(BF16)
    qr = jnp.dot(qa, wqr_ref[...], preferred_element_type=F32) * qscale
    for t in range(qr.shape[1] // LANES):
        tile = qr[:, t * LANES:(t + 1) * LANES]
        if rope:
            tile = _rot(tile, cos, sin)
        qr_ref[:, t * LANES:(t + 1) * LANES] = tile.astype(BF16)

    if rope:
        kr = _rot(kr, cos, sin)
    first = _first_head_lanes(kr.shape)
    kr_ref[:, :LANES] = jnp.where(first, kr, 0.0).astype(BF16)
    kr_ref[:, LANES:] = jnp.where(first, 0.0, kr).astype(BF16)

    kn_ref[...] = jnp.dot(ckv, wkn_ref[...], preferred_element_type=F32).astype(BF16)
    vt = lax.dot_general(wvt_ref[...], ckv, NT_DIMS, preferred_element_type=F32).astype(BF16)
    for hd in range(MLA_HEADS):
        vt_ref[0, hd, 0] = vt[hd * MLA_V:(hd + 1) * MLA_V, :]


def _mla_proj(x, mod, grp, g0, w, rope_tabs, tm, n_per_batch):
    t, d = x.shape
    n_tiles = t // tm
    nb = n_tiles // n_per_batch
    rope = rope_tabs is not None
    qscale = (MLA_NOPE + MLA_ROPE) ** -0.5 * LOG2E
    row = lambda i: (i, 0)
    in_specs = [pl.BlockSpec((tm, d), row), _const_spec((1, d)),
                _mod_spec(grp, 1, d), _mod_spec(grp, 0, d),
                _const_spec(w["w_in"].shape), _const_spec((1, MLA_Q_LORA)),
                _const_spec((1, MLA_KV_LORA)), _const_spec(w["w_qn"].shape),
                _const_spec(w["w_qr"].shape), _const_spec(w["w_kn"].shape),
                _const_spec(w["w_vt"].shape)]
    args = [x, g0, mod, mod, w["w_in"], w["g_qa"], w["g_kva"], w["w_qn"], w["w_qr"], w["w_kn"],
            w["w_vt"]]
    if rope:
        pos = lambda i: (i % n_per_batch, 0)
        in_specs += [pl.BlockSpec((tm, LANES), pos), pl.BlockSpec((tm, LANES), pos)]
        args += list(rope_tabs)
    hn = MLA_HEADS * MLA_NOPE
    hr = MLA_HEADS * MLA_ROPE
    out_shape = [jax.ShapeDtypeStruct((t, hn), BF16), jax.ShapeDtypeStruct((t, hr), BF16),
                 jax.ShapeDtypeStruct((t, hn), BF16), jax.ShapeDtypeStruct((t, 2 * LANES), BF16),
                 jax.ShapeDtypeStruct((nb, MLA_HEADS, n_per_batch, MLA_V, tm), BF16)]
    out_specs = [pl.BlockSpec((tm, hn), row), pl.BlockSpec((tm, hr), row),
                 pl.BlockSpec((tm, hn), row), pl.BlockSpec((tm, 2 * LANES), row),
                 pl.BlockSpec((1, MLA_HEADS, 1, MLA_V, tm),
                              lambda i: (i // n_per_batch, 0, i % n_per_batch, 0, 0))]
    return pl.pallas_call(
        functools.partial(_mla_proj_kernel, rope=rope, qscale=qscale),
        grid=(n_tiles,), in_specs=in_specs, out_specs=out_specs, out_shape=out_shape,
        compiler_params=_params(("arbitrary",)),
        name="mla_proj_lat" if rope else "mla_proj_ctx",
    )(*args)


def _mla_attn_kernel(*refs, n_lat, tk):
    if n_lat:
        (qn_ref, qr_ref, knc_ref, krc_ref, vtc_ref, knl_ref, krl_ref, vtl_ref, o_ref,
         acc_ref) = refs
    else:
        qn_ref, qr_ref, knc_ref, krc_ref, vtc_ref, o_ref, acc_ref = refs
    q = jnp.concatenate([qn_ref[...], qr_ref[...]], axis=1)
    tq = q.shape[0]

    def step(kblk, vt, m, l):
        s = lax.dot_general(kblk, q, NT_DIMS, preferred_element_type=F32)
        m_new = jnp.maximum(m, jnp.max(s, axis=0, keepdims=True))
        alpha = jnp.exp2(m - m_new)
        p = jnp.exp2(s - m_new)
        l_new = alpha * l + jnp.sum(p, axis=0, keepdims=True)
        acc_ref[...] = acc_ref[...] * alpha + jnp.dot(vt, p.astype(BF16),
                                                      preferred_element_type=F32)
        return m_new, l_new

    acc_ref[...] = jnp.zeros_like(acc_ref)
    m = jnp.full((1, tq), NEG_BIG, F32)
    l = jnp.zeros((1, tq), F32)
    kc = jnp.concatenate([knc_ref[0], krc_ref[0]], axis=1)
    m, l = step(kc, vtc_ref[0, 0, 0], m, l)
    if n_lat:
        def body(j, carry):
            off = pl.multiple_of(j * tk, tk)
            kblk = jnp.concatenate([knl_ref[0, pl.ds(off, tk), :],
                                    krl_ref[0, pl.ds(off, tk), :]], axis=1)
            return step(kblk, vtl_ref[0, 0, j], *carry)
        m, l = lax.fori_loop(0, n_lat, body, (m, l))
    o = acc_ref[...] * (1.0 / l)
    o_ref[...] = o.T.astype(BF16)


def _mla_attn(qn, qr, ctx_kv, lat_kv, nb, tq):
    t = qn.shape[0]
    nq = t // nb // tq
    knc, krc, vtc = ctx_kv
    lc = knc.shape[1]
    qrow = lambda b, h, i: (b * nq + i, h)
    in_specs = [pl.BlockSpec((tq, LANES), qrow),
                pl.BlockSpec((tq, LANES), lambda b, h, i: (b * nq + i, h // 2)),
                pl.BlockSpec((1, lc, LANES), lambda b, h, i: (b, 0, h)),
                pl.BlockSpec((1, lc, LANES), lambda b, h, i: (b, 0, h % 2)),
                pl.BlockSpec((1, 1, 1, MLA_V, lc), lambda b, h, i: (b, h, 0, 0, 0))]
    args = [qn, qr, knc, krc, vtc]
    n_lat, tk = 0, 0
    if lat_kv is not None:
        knl, krl, vtl = lat_kv
        s = knl.shape[1]
        n_lat, tk = vtl.shape[2], vtl.shape[4]
        in_specs += [pl.BlockSpec((1, s, LANES), lambda b, h, i: (b, 0, h)),
                     pl.BlockSpec((1, s, LANES), lambda b, h, i: (b, 0, h % 2)),
                     pl.BlockSpec((1, 1, n_lat, MLA_V, tk), lambda b, h, i: (b, h, 0, 0, 0))]
        args += [knl, krl, vtl]
    return pl.pallas_call(
        functools.partial(_mla_attn_kernel, n_lat=n_lat, tk=tk),
        grid=(nb, MLA_HEADS, nq), in_specs=in_specs,
        out_specs=pl.BlockSpec((tq, MLA_V), qrow),
        out_shape=jax.ShapeDtypeStruct((t, MLA_HEADS * MLA_V), BF16),
        scratch_shapes=[pltpu.VMEM((MLA_V, tq), F32)],
        compiler_params=_params(("arbitrary", "arbitrary", "arbitrary")),
        name="mla_attn_lat" if n_lat else "mla_attn_ctx",
    )(*args)


def _outproj_kernel(o_ref, w_ref, x_ref, g_ref, gt_ref, out_ref):
    y = jnp.dot(o_ref[...], w_ref[...], preferred_element_type=F32)
    out_ref[...] = x_ref[...] + gt_ref[0] * _rms(y, g_ref[...])


def _outproj(o, w_out, x, g1, mod, grp, tm):
    t, d = x.shape
    row = lambda i: (i, 0)
    return pl.pallas_call(
        _outproj_kernel,
        grid=(t // tm,),
        in_specs=[pl.BlockSpec((tm, o.shape[1]), row), _const_spec(w_out.shape),
                  pl.BlockSpec((tm, d), row), _const_spec((1, d)), _mod_spec(grp, 2, d)],
        out_specs=pl.BlockSpec((tm, d), row),
        out_shape=jax.ShapeDtypeStruct((t, d), F32),
        compiler_params=_params(("arbitrary",)),
        name="outproj",
    )(o, w_out, x, g1, mod)


def _mlp_kernel(x_ref, g2_ref, sc_ref, sh_ref, w1_ref, w2_ref, g3_ref, gt_ref, out_ref,
                f_ref, acc_ref):
    k = pl.program_id(1)

    @pl.when(k == 0)
    def _():
        f = _rms(x_ref[...], g2_ref[...]) * (1.0 + sc_ref[0]) + sh_ref[0]
        f_ref[...] = f.astype(BF16)
        acc_ref[...] = jnp.zeros_like(acc_ref)

    u = jnp.maximum(jnp.dot(f_ref[...], w1_ref[...], preferred_element_type=F32), 0.0)
    acc_ref[...] += jnp.dot((u * u).astype(BF16), w2_ref[...], preferred_element_type=F32)

    @pl.when(k == pl.num_programs(1) - 1)
    def _():
        out_ref[...] = x_ref[...] + gt_ref[0] * _rms(acc_ref[...], g3_ref[...])


def _mlp(x, g2, g3, mod, grp, w1, w2, tm):
    t, d = x.shape
    dff = w1.shape[1]
    row = lambda i, k: (i, 0)
    return pl.pallas_call(
        _mlp_kernel,
        grid=(t // tm, dff // FF_TILE),
        in_specs=[pl.BlockSpec((tm, d), row), _const_spec((1, d)),
                  _mod_spec(grp, 4, d), _mod_spec(grp, 3, d),
                  pl.BlockSpec((d, FF_TILE), lambda i, k: (0, k)),
                  pl.BlockSpec((FF_TILE, d), lambda i, k: (k, 0)),
                  _const_spec((1, d)), _mod_spec(grp, 5, d)],
        out_specs=pl.BlockSpec((tm, d), row),
        out_shape=jax.ShapeDtypeStruct((t, d), F32),
        scratch_shapes=[pltpu.VMEM((tm, d), BF16), pltpu.VMEM((tm, d), F32)],
        compiler_params=_params(("arbitrary", "arbitrary")),
        name="mlp",
    )(x, g2, mod, mod, w1, w2, g3, mod)


def _swa_proj_kernel(*refs, rope, qscale):
    if rope:
        x_ref, g_ref, sc_ref, sh_ref, w_ref, cos_ref, sin_ref, q_ref, k2_ref, v2_ref = refs
        cos, sin = cos_ref[...], sin_ref[...]
    else:
        x_ref, g_ref, sc_ref, sh_ref, w_ref, q_ref, k2_ref, v2_ref = refs
    h = (_rms(x_ref[...], g_ref[...]) * (1.0 + sc_ref[0]) + sh_ref[0]).astype(BF16)
    p = jnp.dot(h, w_ref[...], preferred_element_type=F32)
    dq = q_ref.shape[1]
    for t in range(dq // LANES):
        tile = p[:, t * LANES:(t + 1) * LANES] * qscale
        if rope:
            tile = _rot(tile, cos, sin)
        q_ref[:, t * LANES:(t + 1) * LANES] = tile.astype(BF16)
    for c in range(SWA_KV_HEADS):
        tile = p[:, dq + c * LANES:dq + (c + 1) * LANES]
        if rope:
            tile = _rot(tile, cos, sin)
        first = _first_head_lanes(tile.shape)
        k2_ref[:, 2 * c * LANES:(2 * c + 1) * LANES] = jnp.where(first, tile, 0.0).astype(BF16)
        k2_ref[:, (2 * c + 1) * LANES:(2 * c + 2) * LANES] = jnp.where(first, 0.0, tile).astype(BF16)
    v2_ref[...] = p[:, dq + SWA_KV_HEADS * LANES:].astype(BF16)


def _swa_proj(x, mod, grp, g0, w, rope_tabs, tm, n_per_batch):
    t, d = x.shape
    rope = rope_tabs is not None
    row = lambda i: (i, 0)
    dq = SWA_HEADS * SWA_HEAD_DIM
    in_specs = [pl.BlockSpec((tm, d), row), _const_spec((1, d)),
                _mod_spec(grp, 1, d), _mod_spec(grp, 0, d), _const_spec(w.shape)]
    args = [x, g0, mod, mod, w]
    if rope:
        pos = lambda i: (i % n_per_batch, 0)
        in_specs += [pl.BlockSpec((tm, LANES), pos), pl.BlockSpec((tm, LANES), pos)]
        args += list(rope_tabs)
    out_shape = [jax.ShapeDtypeStruct((t, dq), BF16),
                 jax.ShapeDtypeStruct((t, 2 * SWA_KV_HEADS * LANES), BF16),
                 jax.ShapeDtypeStruct((t, SWA_KV_HEADS * LANES), BF16)]
    out_specs = [pl.BlockSpec((tm, dq), row), pl.BlockSpec((tm, 2 * SWA_KV_HEADS * LANES), row),
                 pl.BlockSpec((tm, SWA_KV_HEADS * LANES), row)]
    return pl.pallas_call(
        functools.partial(_swa_proj_kernel, rope=rope, qscale=SWA_HEAD_DIM ** -0.5 * LOG2E),
        grid=(t // tm,), in_specs=in_specs, out_specs=out_specs, out_shape=out_shape,
        compiler_params=_params(("arbitrary",)),
        name="swa_proj_lat" if rope else "swa_proj_ctx",
    )(*args)


def _swa_attn_kernel(sink_ref, q_ref, kc_ref, kp_ref, kcur_ref, kn_ref, vc_ref, vp_ref, vcur_ref,
                     vn_ref, o_ref, *, seq):
    kvh = pl.program_id(1)
    i = pl.program_id(2)
    lc = kc_ref.shape[0]
    kcat = jnp.concatenate([kc_ref[...], kp_ref[...], kcur_ref[...], kn_ref[...]], axis=0)
    vcat = jnp.concatenate([vc_ref[...], vp_ref[...], vcur_ref[...], vn_ref[...]], axis=0)
    nk = kcat.shape[0]
    shape = (SWA_QBLK, nk)
    col = lax.broadcasted_iota(jnp.int32, shape, 1)
    qpos = i * SWA_QBLK + lax.broadcasted_iota(jnp.int32, shape, 0)
    kpos = i * SWA_QBLK - SWA_WINDOW + (col - lc)
    band = (jnp.abs(qpos - kpos) <= SWA_WINDOW) & (kpos >= 0) & (kpos < seq)
    mask = (col < lc) | band
    lane = lax.broadcasted_iota(jnp.int32, (SWA_QBLK, LANES), 1)
    for t in range(SWA_GROUP // 2):
        qt = q_ref[:, t * LANES:(t + 1) * LANES]
        outs = []
        for e in range(2):
            kk = kcat[:, e * LANES:(e + 1) * LANES]
            s = lax.dot_general(qt, kk, NT_DIMS, preferred_element_type=F32)
            s = jnp.where(mask, s, NEG_BIG)
            sk = sink_ref[kvh * SWA_GROUP + 2 * t + e] * LOG2E
            m = jnp.maximum(jnp.max(s, axis=1, keepdims=True), sk)
            p = jnp.exp2(s - m)
            den = jnp.sum(p, axis=1, keepdims=True) + jnp.exp2(sk - m)
            o = jnp.dot(p.astype(BF16), vcat, preferred_element_type=F32)
            outs.append(o * (1.0 / den))
        o_ref[:, t * LANES:(t + 1) * LANES] = jnp.where(lane < HALF_TILE, outs[0],
                                                        outs[1]).astype(BF16)


def _swa_attn(sink, q, k2, v2, k2c, v2c, nb, seq, lc):
    t = q.shape[0]
    nblk = seq // SWA_QBLK
    gq = SWA_GROUP * SWA_HEAD_DIM
    cur = lambda b, h, i: (b * nblk + i, h)
    prv = lambda b, h, i: (b * nblk + jnp.maximum(i - 1, 0), h)
    nxt = lambda b, h, i: (b * nblk + jnp.minimum(i + 1, nblk - 1), h)
    cx = lambda b, h, i: (b, h)
    return pl.pallas_call(
        functools.partial(_swa_attn_kernel, seq=seq),
        grid=(nb, SWA_KV_HEADS, nblk),
        in_specs=[pl.BlockSpec(memory_space=pltpu.SMEM),
                  pl.BlockSpec((SWA_QBLK, gq), cur),
                  pl.BlockSpec((lc, 2 * LANES), cx),
                  pl.BlockSpec((SWA_QBLK, 2 * LANES), prv),
                  pl.BlockSpec((SWA_QBLK, 2 * LANES), cur),
                  pl.BlockSpec((SWA_QBLK, 2 * LANES), nxt),
                  pl.BlockSpec((lc, LANES), cx),
                  pl.BlockSpec((SWA_QBLK, LANES), prv),
                  pl.BlockSpec((SWA_QBLK, LANES), cur),
                  pl.BlockSpec((SWA_QBLK, LANES), nxt)],
        out_specs=pl.BlockSpec((SWA_QBLK, gq), cur),
        out_shape=jax.ShapeDtypeStruct((t, SWA_HEADS * SWA_HEAD_DIM), BF16),
        compiler_params=_params(("arbitrary", "arbitrary", "arbitrary")),
        name="swa_attn",
    )(sink, q, k2c, k2, k2, k2, v2c, v2, v2, v2)


def _rope_tables(seq):
    rows = seq // GRID_W
    row = jnp.repeat(jnp.arange(rows, dtype=F32), GRID_W)
    col = jnp.tile(jnp.arange(GRID_W, dtype=F32), rows)
    n_freq = MLA_ROPE // 4
    freqs = ROPE_BASE ** (-jnp.arange(n_freq, dtype=F32) / n_freq)
    ang = jnp.concatenate([row[:, None] * freqs, col[:, None] * freqs], axis=-1)
    cos, sin = jnp.cos(ang), jnp.sin(ang)
    return (jnp.concatenate([cos, cos, cos, cos], axis=-1),
            jnp.concatenate([-sin, -sin, sin, sin], axis=-1))


def _pair_tiles(w, n_heads, half):
    k = w.shape[0]
    x1 = w[:, :, :half].reshape(k, n_heads // 2, 2 * half)
    x2 = w[:, :, half:].reshape(k, n_heads // 2, 2 * half)
    return jnp.concatenate([x1, x2], axis=2).reshape(k, n_heads * 2 * half)


def _mla_weights(w_in, g_qa, g_kva, w_qb, w_kvb):
    half = MLA_ROPE // 2
    lat = MLA_Q_LORA + MLA_KV_LORA
    k1, k2 = w_in[:, lat:lat + half], w_in[:, lat + half:]
    qb = w_qb.reshape(MLA_Q_LORA, MLA_HEADS, MLA_NOPE + MLA_ROPE)
    kvb = w_kvb.reshape(MLA_KV_LORA, MLA_HEADS, MLA_NOPE + MLA_V)
    return {
        "w_in": jnp.concatenate([w_in[:, :lat], k1, k1, k2, k2], axis=1).astype(BF16),
        "g_qa": g_qa.reshape(1, -1), "g_kva": g_kva.reshape(1, -1),
        "w_qn": qb[:, :, :MLA_NOPE].reshape(MLA_Q_LORA, -1).astype(BF16),
        "w_qr": _pair_tiles(qb[:, :, MLA_NOPE:], MLA_HEADS, half).astype(BF16),
        "w_kn": kvb[:, :, :MLA_NOPE].reshape(MLA_KV_LORA, -1).astype(BF16),
        "w_vt": kvb[:, :, MLA_NOPE:].reshape(MLA_KV_LORA, -1).T.astype(BF16),
    }


def _swa_weights(w_qkv):
    d = w_qkv.shape[0]
    half = SWA_HEAD_DIM // 2
    dq = SWA_HEADS * SWA_HEAD_DIM
    dkv = SWA_KV_HEADS * SWA_HEAD_DIM
    q = _pair_tiles(w_qkv[:, :dq].reshape(d, SWA_HEADS, SWA_HEAD_DIM), SWA_HEADS, half)
    k = w_qkv[:, dq:dq + dkv].reshape(d, SWA_KV_HEADS, SWA_HEAD_DIM)
    k1, k2 = k[:, :, :half], k[:, :, half:]
    k = jnp.concatenate([k1, k1, k2, k2], axis=2).reshape(d, SWA_KV_HEADS * LANES)
    v = w_qkv[:, dq + dkv:].reshape(d, SWA_KV_HEADS, SWA_HEAD_DIM)
    v = jnp.concatenate([v, v], axis=2).reshape(d, SWA_KV_HEADS * LANES)
    return jnp.concatenate([q, k, v], axis=1).astype(BF16)


def kernel(x, c, ctx, c_ctx, w_mod, b_mod, g_norm, w_ff_in, w_ff_out, mla_w_in, mla_g_qa,
           mla_g_kva, mla_w_qb, mla_w_kvb, mla_w_out, swa_w_qkv, swa_sink, swa_w_out):
    nb, seq, d = x.shape
    lc = ctx.shape[1]
    depth = w_mod.shape[0]
    assert nb + 1 <= MOD_ROWS
    tm = min(TOKEN_TILE, seq)
    tq = min(ATTN_TQ, seq)
    n_per_batch = seq // tm
    grp_lat = lambda i: i // n_per_batch
    grp_ctx = lambda i: nb

    cmat = jnp.zeros((MOD_ROWS, d), F32).at[:nb].set(c).at[nb].set(c_ctx)
    mod_all = _modulation(cmat, w_mod, b_mod)
    rope_tabs = _rope_tables(seq)

    xl = x.reshape(nb * seq, d)
    xc = ctx.reshape(nb * lc, d)
    for i in range(depth):
        need_ctx = i < depth - 1
        mod = mod_all[i].reshape(MOD_ROWS, 1, 6 * d)
        g = g_norm[i].reshape(4, 1, d)
        j = i // 2
        if i % 2 == 0:
            w = _mla_weights(mla_w_in[j], mla_g_qa[j], mla_g_kva[j], mla_w_qb[j], mla_w_kvb[j])
            w_out = mla_w_out[j].astype(BF16)
            qn, qr, kn, kr, vt = _mla_proj(xl, mod, grp_lat, g[0], w, rope_tabs, tm, n_per_batch)
            qnc, qrc, knc, krc, vtc = _mla_proj(xc, mod, grp_ctx, g[0], w, None, lc, 1)
            ctx_kv = (knc.reshape(nb, lc, -1), krc.reshape(nb, lc, -1), vtc)
            lat_kv = (kn.reshape(nb, seq, -1), kr.reshape(nb, seq, -1), vt)
            o_l = _mla_attn(qn, qr, ctx_kv, lat_kv, nb, tq)
            o_c = _mla_attn(qnc, qrc, ctx_kv, None, nb, lc) if need_ctx else None
        else:
            w = _swa_weights(swa_w_qkv[j])
            w_out = swa_w_out[j].astype(BF16)
            q, k2, v2 = _swa_proj(xl, mod, grp_lat, g[0], w, rope_tabs, tm, n_per_batch)
            qc, k2c, v2c = _swa_proj(xc, mod, grp_ctx, g[0], w, None, lc, 1)
            o_l = _swa_attn(swa_sink[j], q, k2, v2, k2c, v2c, nb, seq, lc)
            assert not need_ctx
            o_c = None
        w1 = w_ff_in[i].astype(BF16)
        w2 = w_ff_out[i].astype(BF16)
        xl = _outproj(o_l, w_out, xl, g[1], mod, grp_lat, tm)
        xl = _mlp(xl, g[2], g[3], mod, grp_lat, w1, w2, tm)
        if need_ctx:
            xc = _outproj(o_c, w_out, xc, g[1], mod, grp_ctx, lc)
            xc = _mlp(xc, g[2], g[3], mod, grp_ctx, w1, w2, lc)
    return xl.reshape(nb, seq, d)
```

```python
import functools
import math

import jax
import jax.numpy as jnp
from jax import lax
from jax.experimental import pallas as pl
from jax.experimental.pallas import tpu as pltpu

F32 = jnp.float32
BF16 = jnp.bfloat16

GRID_W = 64
ROPE_BASE = 10000.0
NORM_EPS = 1e-6
LOG2E = math.log2(math.e)
NEG_BIG = -1e30

MLA_HEADS = 16
MLA_Q_LORA = 512
MLA_KV_LORA = 512
MLA_NOPE = 128
MLA_ROPE = 64
MLA_V = 128

SWA_HEADS = 32
SWA_KV_HEADS = 4
SWA_HEAD_DIM = 64
SWA_WINDOW = 128
SWA_GROUP = SWA_HEADS // SWA_KV_HEADS
SWA_QBLK = 128

LANES = 128
HALF_TILE = 64

MOD_ROWS = 8
MOD_TN = 1024
TOKEN_TILE = 512
FF_TILE = 1024
ATTN_TQ = 512
KV_UNROLL = 4
VMEM_LIMIT = 56 * 1024 * 1024

NT_DIMS = (((1,), (1,)), ((), ()))


def _rms(xf, g):
    ms = jnp.mean(xf * xf, axis=-1, keepdims=True)
    return xf * lax.rsqrt(ms + NORM_EPS) * g


def _rot(tile, cos, sin):
    return tile * cos + pltpu.roll(tile, HALF_TILE, 1) * sin


def _first_head_lanes(shape):
    lane = lax.broadcasted_iota(jnp.int32, shape, 1)
    return (lane & 32) == 0


def _params(sem):
    return pltpu.CompilerParams(dimension_semantics=sem, vmem_limit_bytes=VMEM_LIMIT)


def _const_spec(shape):
    nd = len(shape)
    return pl.BlockSpec(shape, lambda *_: (0,) * nd, pipeline_mode=pl.Buffered(1))


def _mod_spec(grp, which, d):
    return pl.BlockSpec((1, 1, d), lambda i, *_: (grp(i), 0, which))


def _mod_kernel(c_ref, w_ref, b_ref, o_ref):
    c = c_ref[...]
    a = c / (1.0 + jnp.exp(-c))
    o_ref[0] = jnp.dot(a, w_ref[0], preferred_element_type=F32,
                       precision=lax.Precision.HIGHEST) + b_ref[0]


def _modulation(cmat, w_mod, b_mod):
    depth, d, n = w_mod.shape
    return pl.pallas_call(
        _mod_kernel,
        grid=(depth, n // MOD_TN),
        in_specs=[pl.BlockSpec((MOD_ROWS, d), lambda l, j: (0, 0)),
                  pl.BlockSpec((1, d, MOD_TN), lambda l, j: (l, 0, j)),
                  pl.BlockSpec((1, 1, MOD_TN), lambda l, j: (l, 0, j))],
        out_specs=pl.BlockSpec((1, MOD_ROWS, MOD_TN), lambda l, j: (l, 0, j)),
        out_shape=jax.ShapeDtypeStruct((depth, MOD_ROWS, n), F32),
        compiler_params=_params(("arbitrary", "arbitrary")),
        name="modulation",
    )(cmat, w_mod, b_mod.reshape(depth, 1, n))


def _mla_proj_kernel(*refs, rope, qscale):
    if rope:
        (x_ref, g_ref, sc_ref, sh_ref, win_ref, gqa_ref, gkva_ref, wqn_ref, wqr_ref, wkn_ref,
         wvt_ref, cos_ref, sin_ref, qn_ref, qr_ref, kn_ref, kr_ref, vt_ref) = refs
        cos, sin = cos_ref[...], sin_ref[...]
    else:
        (x_ref, g_ref, sc_ref, sh_ref, win_ref, gqa_ref, gkva_ref, wqn_ref, wqr_ref, wkn_ref,
         wvt_ref, qn_ref, qr_ref, kn_ref, kr_ref, vt_ref) = refs
    h = (_rms(x_ref[...], g_ref[...]) * (1.0 + sc_ref[0]) + sh_ref[0]).astype(BF16)
    p = jnp.dot(h, win_ref[...], preferred_element_type=F32)
    qa = _rms(p[:, :MLA_Q_LORA], gqa_ref[...]).astype(BF16)
    ckv = _rms(p[:, MLA_Q_LORA:MLA_Q_LORA + MLA_KV_LORA], gkva_ref[...]).astype(BF16)
    kr = p[:, MLA_Q_LORA + MLA_KV_LORA:]

    qn = jnp.dot(qa, wqn_ref[...], preferred_element_type=F32) * qscale
    qn_ref[...] = qn.astype(BF16)
    qr = jnp.dot(qa, wqr_ref[...], preferred_element_type=F32) * qscale
    for t in range(qr.shape[1] // LANES):
        tile = qr[:, t * LANES:(t + 1) * LANES]
        if rope:
            tile = _rot(tile, cos, sin)
        qr_ref[:, t * LANES:(t + 1) * LANES] = tile.astype(BF16)

    if rope:
        kr = _rot(kr, cos, sin)
    first = _first_head_lanes(kr.shape)
    kr_ref[:, :LANES] = jnp.where(first, kr, 0.0).astype(BF16)
    kr_ref[:, LANES:] = jnp.where(first, 0.0, kr).astype(BF16)

    kn_ref[...] = jnp.dot(ckv, wkn_ref[...], preferred_element_type=F32).astype(BF16)
    vt = lax.dot_general(wvt_ref[...], ckv, NT_DIMS, preferred_element_type=F32).astype(BF16)
    for hd in range(MLA_HEADS):
        vt_ref[0, hd, 0] = vt[hd * MLA_V:(hd + 1) * MLA_V, :]


def _mla_proj(x, mod, grp, g0, w, rope_tabs, tm, n_per_batch):
    t, d = x.shape
    n_tiles = t // tm
    nb = n_tiles // n_per_batch
    rope = rope_tabs is not None
    qscale = (MLA_NOPE + MLA_ROPE) ** -0.5 * LOG2E
    row = lambda i: (i, 0)
    in_specs = [pl.BlockSpec((tm, d), row), _const_spec((1, d)),
                _mod_spec(grp, 1, d), _mod_spec(grp, 0, d),
                _const_spec(w["w_in"].shape), _const_spec((1, MLA_Q_LORA)),
                _const_spec((1, MLA_KV_LORA)), _const_spec(w["w_qn"].shape),
                _const_spec(w["w_qr"].shape), _const_spec(w["w_kn"].shape),
                _const_spec(w["w_vt"].shape)]
    args = [x, g0, mod, mod, w["w_in"], w["g_qa"], w["g_kva"], w["w_qn"], w["w_qr"], w["w_kn"],
            w["w_vt"]]
    if rope:
        pos = lambda i: (i % n_per_batch, 0)
        in_specs += [pl.BlockSpec((tm, LANES), pos), pl.BlockSpec((tm, LANES), pos)]
        args += list(rope_tabs)
    hn = MLA_HEADS * MLA_NOPE
    hr = MLA_HEADS * MLA_ROPE
    out_shape = [jax.ShapeDtypeStruct((t, hn), BF16), jax.ShapeDtypeStruct((t, hr), BF16),
                 jax.ShapeDtypeStruct((t, hn), BF16), jax.ShapeDtypeStruct((t, 2 * LANES), BF16),
                 jax.ShapeDtypeStruct((nb, MLA_HEADS, n_per_batch, MLA_V, tm), BF16)]
    out_specs = [pl.BlockSpec((tm, hn), row), pl.BlockSpec((tm, hr), row),
                 pl.BlockSpec((tm, hn), row), pl.BlockSpec((tm, 2 * LANES), row),
                 pl.BlockSpec((1, MLA_HEADS, 1, MLA_V, tm),
                              lambda i: (i // n_per_batch, 0, i % n_per_batch, 0, 0))]
    return pl.pallas_call(
        functools.partial(_mla_proj_kernel, rope=rope, qscale=qscale),
        grid=(n_tiles,), in_specs=in_specs, out_specs=out_specs, out_shape=out_shape,
        compiler_params=_params(("arbitrary",)),
        name="mla_proj_lat" if rope else "mla_proj_ctx",
    )(*args)


def _mla_attn_kernel(*refs, n_lat, tk):
    if n_lat:
        (qn_ref, qr_ref, knc_ref, krc_ref, vtc_ref, knl_ref, krl_ref, vtl_ref, o_ref,
         acc_ref, s_ref) = refs
    else:
        qn_ref, qr_ref, knc_ref, krc_ref, vtc_ref, o_ref, acc_ref, s_ref = refs
    q = jnp.concatenate([qn_ref[...], qr_ref[...]], axis=1)
    tq = q.shape[0]
    lc = knc_ref.shape[1]

    def scores(kblk):
        return lax.dot_general(kblk, q, NT_DIMS, preferred_element_type=F32)

    def lat_scores(j):
        off = pl.multiple_of(j * tk, tk)
        return scores(jnp.concatenate([knl_ref[0, pl.ds(off, tk), :],
                                       krl_ref[0, pl.ds(off, tk), :]], axis=1))

    def update(s_view, vt, m, l):
        m_new = jnp.maximum(m, jnp.max(s_view[...], axis=0, keepdims=True))
        alpha = jnp.exp2(m - m_new)
        p = jnp.exp2(s_view[...] - m_new)
        l_new = alpha * l + jnp.sum(p, axis=0, keepdims=True)
        acc_ref[...] = acc_ref[...] * alpha + jnp.dot(vt, p.astype(BF16),
                                                      preferred_element_type=F32)
        return m_new, l_new

    acc_ref[...] = jnp.zeros_like(acc_ref)
    m = jnp.full((1, tq), NEG_BIG, F32)
    l = jnp.zeros((1, tq), F32)
    if n_lat:
        s_ref[0] = lat_scores(0)
    ctx_view = s_ref.at[1, pl.ds(0, lc)]
    ctx_view[...] = scores(jnp.concatenate([knc_ref[0], krc_ref[0]], axis=1))
    m, l = update(ctx_view, vtc_ref[0, 0, 0], m, l)
    if n_lat:
        def body(jj, carry):
            for u in range(KV_UNROLL):
                j = KV_UNROLL * jj + u
                s_ref[(u + 1) % 2] = lat_scores(jnp.minimum(j + 1, n_lat - 1))
                carry = update(s_ref.at[u % 2], vtl_ref[0, 0, j], *carry)
            return carry
        m, l = lax.fori_loop(0, n_lat // KV_UNROLL, body, (m, l))
    o = acc_ref[...] * (1.0 / l)
    o_ref[...] = o.T.astype(BF16)


def _mla_attn(qn, qr, ctx_kv, lat_kv, nb, tq):
    t = qn.shape[0]
    nq = t // nb // tq
    knc, krc, vtc = ctx_kv
    lc = knc.shape[1]
    qrow = lambda b, h, i: (b * nq + i, h)
    in_specs = [pl.BlockSpec((tq, LANES), qrow),
                pl.BlockSpec((tq, LANES), lambda b, h, i: (b * nq + i, h // 2)),
                pl.BlockSpec((1, lc, LANES), lambda b, h, i: (b, 0, h)),
                pl.BlockSpec((1, lc, LANES), lambda b, h, i: (b, 0, h % 2)),
                pl.BlockSpec((1, 1, 1, MLA_V, lc), lambda b, h, i: (b, h, 0, 0, 0))]
    args = [qn, qr, knc, krc, vtc]
    n_lat, tk = 0, 0
    if lat_kv is not None:
        knl, krl, vtl = lat_kv
        s = knl.shape[1]
        n_lat, tk = vtl.shape[2], vtl.shape[4]
        assert n_lat % KV_UNROLL == 0 and tk >= lc
        in_specs += [pl.BlockSpec((1, s, LANES), lambda b, h, i: (b, 0, h)),
                     pl.BlockSpec((1, s, LANES), lambda b, h, i: (b, 0, h % 2)),
                     pl.BlockSpec((1, 1, n_lat, MLA_V, tk), lambda b, h, i: (b, h, 0, 0, 0))]
        args += [knl, krl, vtl]
    scratch = [pltpu.VMEM((MLA_V, tq), F32), pltpu.VMEM((2, max(tk, lc), tq), F32)]
    return pl.pallas_call(
        functools.partial(_mla_attn_kernel, n_lat=n_lat, tk=tk),
        grid=(nb, MLA_HEADS, nq), in_specs=in_specs,
        out_specs=pl.BlockSpec((tq, MLA_V), qrow),
        out_shape=jax.ShapeDtypeStruct((t, MLA_HEADS * MLA_V), BF16),
        scratch_shapes=scratch,
        compiler_params=_params(("arbitrary", "arbitrary", "arbitrary")),
        name="mla_attn_lat" if n_lat else "mla_attn_ctx",
    )(*args)


def _outproj_kernel(o_ref, w_ref, x_ref, g_ref, gt_ref, out_ref):
    y = jnp.dot(o_ref[...], w_ref[...], preferred_element_type=F32)
    out_ref[...] = x_ref[...] + gt_ref[0] * _rms(y, g_ref[...])


def _outproj(o, w_out, x, g1, mod, grp, tm):
    t, d = x.shape
    row = lambda i: (i, 0)
    return pl.pallas_call(
        _outproj_kernel,
        grid=(t // tm,),
        in_specs=[pl.BlockSpec((tm, o.shape[1]), row), _const_spec(w_out.shape),
                  pl.BlockSpec((tm, d), row), _const_spec((1, d)), _mod_spec(grp, 2, d)],
        out_specs=pl.BlockSpec((tm, d), row),
        out_shape=jax.ShapeDtypeStruct((t, d), F32),
        compiler_params=_params(("arbitrary",)),
        name="outproj",
    )(o, w_out, x, g1, mod)


def _mlp_kernel(x_ref, g2_ref, sc_ref, sh_ref, w1_ref, w2_ref, g3_ref, gt_ref, out_ref,
                f_ref, acc_ref):
    k = pl.program_id(1)

    @pl.when(k == 0)
    def _():
        f = _rms(x_ref[...], g2_ref[...]) * (1.0 + sc_ref[0]) + sh_ref[0]
        f_ref[...] = f.astype(BF16)
        acc_ref[...] = jnp.zeros_like(acc_ref)

    u = jnp.maximum(jnp.dot(f_ref[...], w1_ref[...], preferred_element_type=F32), 0.0)
    acc_ref[...] += jnp.dot((u * u).astype(BF16), w2_ref[...], preferred_element_type=F32)

    @pl.when(k == pl.num_programs(1) - 1)
    def _():
        out_ref[...] = x_ref[...] + gt_ref[0] * _rms(acc_ref[...], g3_ref[...])


def _mlp(x, g2, g3, mod, grp, w1, w2, tm):
    t, d = x.shape
    dff = w1.shape[1]
    row = lambda i, k: (i, 0)
    return pl.pallas_call(
        _mlp_kernel,
        grid=(t // tm, dff // FF_TILE),
        in_specs=[pl.BlockSpec((tm, d), row), _const_spec((1, d)),
                  _mod_spec(grp, 4, d), _mod_spec(grp, 3, d),
                  pl.BlockSpec((d, FF_TILE), lambda i, k: (0, k)),
                  pl.BlockSpec((FF_TILE, d), lambda i, k: (k, 0)),
                  _const_spec((1, d)), _mod_spec(grp, 5, d)],
        out_specs=pl.BlockSpec((tm, d), row),
        out_shape=jax.ShapeDtypeStruct((t, d), F32),
        scratch_shapes=[pltpu.VMEM((tm, d), BF16), pltpu.VMEM((tm, d), F32)],
        compiler_params=_params(("arbitrary", "arbitrary")),
        name="mlp",
    )(x, g2, mod, mod, w1, w2, g3, mod)


def _swa_proj_kernel(*refs, rope, qscale):
    if rope:
        x_ref, g_ref, sc_ref, sh_ref, w_ref, cos_ref, sin_ref, q_ref, k2_ref, v2_ref = refs
        cos, sin = cos_ref[...], sin_ref[...]
    else:
        x_ref, g_ref, sc_ref, sh_ref, w_ref, q_ref, k2_ref, v2_ref = refs
    h = (_rms(x_ref[...], g_ref[...]) * (1.0 + sc_ref[0]) + sh_ref[0]).astype(BF16)
    p = jnp.dot(h, w_ref[...], preferred_element_type=F32)
    dq = q_ref.shape[1]
    for t in range(dq // LANES):
        tile = p[:, t * LANES:(t + 1) * LANES] * qscale
        if rope:
            tile = _rot(tile, cos, sin)
        q_ref[:, t * LANES:(t + 1) * LANES] = tile.astype(BF16)
    for c in range(SWA_KV_HEADS):
        tile = p[:, dq + c * LANES:dq + (c + 1) * LANES]
        if rope:
            tile = _rot(tile, cos, sin)
        first = _first_head_lanes(tile.shape)
        k2_ref[:, 2 * c * LANES:(2 * c + 1) * LANES] = jnp.where(first, tile, 0.0).astype(BF16)
        k2_ref[:, (2 * c + 1) * LANES:(2 * c + 2) * LANES] = jnp.where(first, 0.0, tile).astype(BF16)
    v2_ref[...] = p[:, dq + SWA_KV_HEADS * LANES:].astype(BF16)


def _swa_proj(x, mod, grp, g0, w, rope_tabs, tm, n_per_batch):
    t, d = x.shape
    rope = rope_tabs is not None
    row = lambda i: (i, 0)
    dq = SWA_HEADS * SWA_HEAD_DIM
    in_specs = [pl.BlockSpec((tm, d), row), _const_spec((1, d)),
                _mod_spec(grp, 1, d), _mod_spec(grp, 0, d), _const_spec(w.shape)]
    args = [x, g0, mod, mod, w]
    if rope:
        pos = lambda i: (i % n_per_batch, 0)
        in_specs += [pl.BlockSpec((tm, LANES), pos), pl.BlockSpec((tm, LANES), pos)]
        args += list(rope_tabs)
    out_shape = [jax.ShapeDtypeStruct((t, dq), BF16),
                 jax.ShapeDtypeStruct((t, 2 * SWA_KV_HEADS * LANES), BF16),
                 jax.ShapeDtypeStruct((t, SWA_KV_HEADS * LANES), BF16)]
    out_specs = [pl.BlockSpec((tm, dq), row), pl.BlockSpec((tm, 2 * SWA_KV_HEADS * LANES), row),
                 pl.BlockSpec((tm, SWA_KV_HEADS * LANES), row)]
    return pl.pallas_call(
        functools.partial(_swa_proj_kernel, rope=rope, qscale=SWA_HEAD_DIM ** -0.5 * LOG2E),
        grid=(t // tm,), in_specs=in_specs, out_specs=out_specs, out_shape=out_shape,
        compiler_params=_params(("arbitrary",)),
        name="swa_proj_lat" if rope else "swa_proj_ctx",
    )(*args)


def _swa_attn_kernel(sink_ref, q_ref, kc_ref, kp_ref, kcur_ref, kn_ref, vc_ref, vp_ref, vcur_ref,
                     vn_ref, o_ref, *, seq):
    kvh = pl.program_id(1)
    i = pl.program_id(2)
    lc = kc_ref.shape[0]
    kcat = jnp.concatenate([kc_ref[...], kp_ref[...], kcur_ref[...], kn_ref[...]], axis=0)
    vcat = jnp.concatenate([vc_ref[...], vp_ref[...], vcur_ref[...], vn_ref[...]], axis=0)
    nk = kcat.shape[0]
    shape = (SWA_QBLK, nk)
    col = lax.broadcasted_iota(jnp.int32, shape, 1)
    qpos = i * SWA_QBLK + lax.broadcasted_iota(jnp.int32, shape, 0)
    kpos = i * SWA_QBLK - SWA_WINDOW + (col - lc)
    band = (jnp.abs(qpos - kpos) <= SWA_WINDOW) & (kpos >= 0) & (kpos < seq)
    mask = (col < lc) | band
    lane = lax.broadcasted_iota(jnp.int32, (SWA_QBLK, LANES), 1)
    for t in range(SWA_GROUP // 2):
        qt = q_ref[:, t * LANES:(t + 1) * LANES]
        outs = []
        for e in range(2):
            kk = kcat[:, e * LANES:(e + 1) * LANES]
            s = lax.dot_general(qt, kk, NT_DIMS, preferred_element_type=F32)
            s = jnp.where(mask, s, NEG_BIG)
            sk = sink_ref[kvh * SWA_GROUP + 2 * t + e] * LOG2E
            m = jnp.maximum(jnp.max(s, axis=1, keepdims=True), sk)
            p = jnp.exp2(s - m)
            den = jnp.sum(p, axis=1, keepdims=True) + jnp.exp2(sk - m)
            o = jnp.dot(p.astype(BF16), vcat, preferred_element_type=F32)
            outs.append(o * (1.0 / den))
        o_ref[:, t * LANES:(t + 1) * LANES] = jnp.where(lane < HALF_TILE, outs[0],
                                                        outs[1]).astype(BF16)


def _swa_attn(sink, q, k2, v2, k2c, v2c, nb, seq, lc):
    t = q.shape[0]
    nblk = seq // SWA_QBLK
    gq = SWA_GROUP * SWA_HEAD_DIM
    cur = lambda b, h, i: (b * nblk + i, h)
    prv = lambda b, h, i: (b * nblk + jnp.maximum(i - 1, 0), h)
    nxt = lambda b, h, i: (b * nblk + jnp.minimum(i + 1, nblk - 1), h)
    cx = lambda b, h, i: (b, h)
    return pl.pallas_call(
        functools.partial(_swa_attn_kernel, seq=seq),
        grid=(nb, SWA_KV_HEADS, nblk),
        in_specs=[pl.BlockSpec(memory_space=pltpu.SMEM),
                  pl.BlockSpec((SWA_QBLK, gq), cur),
                  pl.BlockSpec((lc, 2 * LANES), cx),
                  pl.BlockSpec((SWA_QBLK, 2 * LANES), prv),
                  pl.BlockSpec((SWA_QBLK, 2 * LANES), cur),
                  pl.BlockSpec((SWA_QBLK, 2 * LANES), nxt),
                  pl.BlockSpec((lc, LANES), cx),
                  pl.BlockSpec((SWA_QBLK, LANES), prv),
                  pl.BlockSpec((SWA_QBLK, LANES), cur),
                  pl.BlockSpec((SWA_QBLK, LANES), nxt)],
        out_specs=pl.BlockSpec((SWA_QBLK, gq), cur),
        out_shape=jax.ShapeDtypeStruct((t, SWA_HEADS * SWA_HEAD_DIM), BF16),
        compiler_params=_params(("arbitrary", "arbitrary", "arbitrary")),
        name="swa_attn",
    )(sink, q, k2c, k2, k2, k2, v2c, v2, v2, v2)


def _rope_tables(seq):
    rows = seq // GRID_W
    row = jnp.repeat(jnp.arange(rows, dtype=F32), GRID_W)
    col = jnp.tile(jnp.arange(GRID_W, dtype=F32), rows)
    n_freq = MLA_ROPE // 4
    freqs = ROPE_BASE ** (-jnp.arange(n_freq, dtype=F32) / n_freq)
    ang = jnp.concatenate([row[:, None] * freqs, col[:, None] * freqs], axis=-1)
    cos, sin = jnp.cos(ang), jnp.sin(ang)
    return (jnp.concatenate([cos, cos, cos, cos], axis=-1),
            jnp.concatenate([-sin, -sin, sin, sin], axis=-1))


def _pair_tiles(w, n_heads, half):
    k = w.shape[0]
    x1 = w[:, :, :half].reshape(k, n_heads // 2, 2 * half)
    x2 = w[:, :, half:].reshape(k, n_heads // 2, 2 * half)
    return jnp.concatenate([x1, x2], axis=2).reshape(k, n_heads * 2 * half)


def _mla_weights(w_in, g_qa, g_kva, w_qb, w_kvb):
    half = MLA_ROPE // 2
    lat = MLA_Q_LORA + MLA_KV_LORA
    k1, k2 = w_in[:, lat:lat + half], w_in[:, lat + half:]
    qb = w_qb.reshape(MLA_Q_LORA, MLA_HEADS, MLA_NOPE + MLA_ROPE)
    kvb = w_kvb.reshape(MLA_KV_LORA, MLA_HEADS, MLA_NOPE + MLA_V)
    return {
        "w_in": jnp.concatenate([w_in[:, :lat], k1, k1, k2, k2], axis=1).astype(BF16),
        "g_qa": g_qa.reshape(1, -1), "g_kva": g_kva.reshape(1, -1),
        "w_qn": qb[:, :, :MLA_NOPE].reshape(MLA_Q_LORA, -1).astype(BF16),
        "w_qr": _pair_tiles(qb[:, :, MLA_NOPE:], MLA_HEADS, half).astype(BF16),
        "w_kn": kvb[:, :, :MLA_NOPE].reshape(MLA_KV_LORA, -1).astype(BF16),
        "w_vt": kvb[:, :, MLA_NOPE:].reshape(MLA_KV_LORA, -1).T.astype(BF16),
    }


def _swa_weights(w_qkv):
    d = w_qkv.shape[0]
    half = SWA_HEAD_DIM // 2
    dq = SWA_HEADS * SWA_HEAD_DIM
    dkv = SWA_KV_HEADS * SWA_HEAD_DIM
    q = _pair_tiles(w_qkv[:, :dq].reshape(d, SWA_HEADS, SWA_HEAD_DIM), SWA_HEADS, half)
    k = w_qkv[:, dq:dq + dkv].reshape(d, SWA_KV_HEADS, SWA_HEAD_DIM)
    k1, k2 = k[:, :, :half], k[:, :, half:]
    k = jnp.concatenate([k1, k1, k2, k2], axis=2).reshape(d, SWA_KV_HEADS * LANES)
    v = w_qkv[:, dq + dkv:].reshape(d, SWA_KV_HEADS, SWA_HEAD_DIM)
    v = jnp.concatenate([v, v], axis=2).reshape(d, SWA_KV_HEADS * LANES)
    return jnp.concatenate([q, k, v], axis=1).astype(BF16)


def kernel(x, c, ctx, c_ctx, w_mod, b_mod, g_norm, w_ff_in, w_ff_out, mla_w_in, mla_g_qa,
           mla_g_kva, mla_w_qb, mla_w_kvb, mla_w_out, swa_w_qkv, swa_sink, swa_w_out):
    nb, seq, d = x.shape
    lc = ctx.shape[1]
    depth = w_mod.shape[0]
    assert nb + 1 <= MOD_ROWS
    tm = min(TOKEN_TILE, seq)
    tq = min(ATTN_TQ, seq)
    n_per_batch = seq // tm
    grp_lat = lambda i: i // n_per_batch
    grp_ctx = lambda i: nb

    cmat = jnp.zeros((MOD_ROWS, d), F32).at[:nb].set(c).at[nb].set(c_ctx)
    mod_all = _modulation(cmat, w_mod, b_mod)
    rope_tabs = _rope_tables(seq)

    xl = x.reshape(nb * seq, d)
    xc = ctx.reshape(nb * lc, d)
    for i in range(depth):
        need_ctx = i < depth - 1
        mod = mod_all[i].reshape(MOD_ROWS, 1, 6 * d)
        g = g_norm[i].reshape(4, 1, d)
        j = i // 2
        if i % 2 == 0:
            w = _mla_weights(mla_w_in[j], mla_g_qa[j], mla_g_kva[j], mla_w_qb[j], mla_w_kvb[j])
            w_out = mla_w_out[j].astype(BF16)
            qn, qr, kn, kr, vt = _mla_proj(xl, mod, grp_lat, g[0], w, rope_tabs, tm, n_per_batch)
            qnc, qrc, knc, krc, vtc = _mla_proj(xc, mod, grp_ctx, g[0], w, None, lc, 1)
            ctx_kv = (knc.reshape(nb, lc, -1), krc.reshape(nb, lc, -1), vtc)
            lat_kv = (kn.reshape(nb, seq, -1), kr.reshape(nb, seq, -1), vt)
            o_l = _mla_attn(qn, qr, ctx_kv, lat_kv, nb, tq)
            o_c = _mla_attn(qnc, qrc, ctx_kv, None, nb, lc) if need_ctx else None
        else:
            w = _swa_weights(swa_w_qkv[j])
            w_out = swa_w_out[j].astype(BF16)
            q, k2, v2 = _swa_proj(xl, mod, grp_lat, g[0], w, rope_tabs, tm, n_per_batch)
            qc, k2c, v2c = _swa_proj(xc, mod, grp_ctx, g[0], w, None, lc, 1)
            o_l = _swa_attn(swa_sink[j], q, k2, v2, k2c, v2c, nb, seq, lc)
            assert not need_ctx
            o_c = None
        w1 = w_ff_in[i].astype(BF16)
        w2 = w_ff_out[i].astype(BF16)
        xl = _outproj(o_l, w_out, xl, g[1], mod, grp_lat, tm)
        xl = _mlp(xl, g[2], g[3], mod, grp_lat, w1, w2, tm)
        if need_ctx:
            xc = _outproj(o_c, w_out, xc, g[1], mod, grp_ctx, lc)
            xc = _mlp(xc, g[2], g[3], mod, grp_ctx, w1, w2, lc)
    return xl.reshape(nb, seq, d)
```

```python
import functools
import math

import jax
import jax.numpy as jnp
from jax import lax
from jax.experimental import pallas as pl
from jax.experimental.pallas import tpu as pltpu

F32 = jnp.float32
BF16 = jnp.bfloat16

GRID_W = 64
ROPE_BASE = 10000.0
NORM_EPS = 1e-6
LOG2E = math.log2(math.e)
NEG_BIG = -1e30
REF_LIMIT = 60.0

MLA_HEADS = 16
MLA_Q_LORA = 512
MLA_KV_LORA = 512
MLA_NOPE = 128
MLA_ROPE = 64
MLA_V = 128

SWA_HEADS = 32
SWA_KV_HEADS = 4
SWA_HEAD_DIM = 64
SWA_WINDOW = 128
SWA_GROUP = SWA_HEADS // SWA_KV_HEADS
SWA_QBLK = 128
SWA_STEP_BLOCKS = 4

LANES = 128
HALF_TILE = 64

MOD_ROWS = 8
MOD_TN = 1024
TOKEN_TILE = 512
FF_TILE = 1024
ATTN_TQ = 512
KV_UNROLL = 8
VMEM_LIMIT = 56 * 1024 * 1024

NT_DIMS = (((1,), (1,)), ((), ()))


def _rms(xf, g):
    ms = jnp.mean(xf * xf, axis=-1, keepdims=True)
    return xf * lax.rsqrt(ms + NORM_EPS) * g


def _rot(tile, cos, sin):
    return tile * cos + pltpu.roll(tile, HALF_TILE, 1) * sin


def _first_head_lanes(shape):
    lane = lax.broadcasted_iota(jnp.int32, shape, 1)
    return (lane & 32) == 0


def _shift_lane(parity):
    return LANES - 1 - 32 * parity


def _params(sem):
    return pltpu.CompilerParams(dimension_semantics=sem, vmem_limit_bytes=VMEM_LIMIT)


def _const_spec(shape):
    nd = len(shape)
    return pl.BlockSpec(shape, lambda *_: (0,) * nd, pipeline_mode=pl.Buffered(1))


def _mod_spec(grp, which, d):
    return pl.BlockSpec((1, 1, d), lambda i, *_: (grp(i), 0, which))


def _mod_kernel(c_ref, w_ref, b_ref, o_ref):
    c = c_ref[...]
    a = c / (1.0 + jnp.exp(-c))
    o_ref[0] = jnp.dot(a, w_ref[0], preferred_element_type=F32,
                       precision=lax.Precision.HIGHEST) + b_ref[0]


def _modulation(cmat, w_mod, b_mod):
    depth, d, n = w_mod.shape
    return pl.pallas_call(
        _mod_kernel,
        grid=(depth, n // MOD_TN),
        in_specs=[pl.BlockSpec((MOD_ROWS, d), lambda l, j: (0, 0)),
                  pl.BlockSpec((1, d, MOD_TN), lambda l, j: (l, 0, j)),
                  pl.BlockSpec((1, 1, MOD_TN), lambda l, j: (l, 0, j))],
        out_specs=pl.BlockSpec((1, MOD_ROWS, MOD_TN), lambda l, j: (l, 0, j)),
        out_shape=jax.ShapeDtypeStruct((depth, MOD_ROWS, n), F32),
        compiler_params=_params(("arbitrary", "arbitrary")),
        name="modulation",
    )(cmat, w_mod, b_mod.reshape(depth, 1, n))


def _mla_proj_kernel(*refs, rope, qscale):
    if rope:
        (x_ref, g_ref, sc_ref, sh_ref, win_ref, gqa_ref, gkva_ref, wqn_ref, wqr_ref, wkn_ref,
         wvt_ref, cos_ref, sin_ref, qn_ref, qr_ref, kn_ref, kr_ref, vt_ref) = refs
        cos, sin = cos_ref[...], sin_ref[...]
    else:
        (x_ref, g_ref, sc_ref, sh_ref, win_ref, gqa_ref, gkva_ref, wqn_ref, wqr_ref, wkn_ref,
         wvt_ref, qn_ref, qr_ref, kn_ref, kr_ref, vt_ref) = refs
    h = (_rms(x_ref[...], g_ref[...]) * (1.0 + sc_ref[0]) + sh_ref[0]).astype(BF16)
    p = jnp.dot(h, win_ref[...], preferred_element_type=F32)
    qa = _rms(p[:, :MLA_Q_LORA], gqa_ref[...]).astype(BF16)
    ckv = _rms(p[:, MLA_Q_LORA:MLA_Q_LORA + MLA_KV_LORA], gkva_ref[...]).astype(BF16)
    kr = p[:, MLA_Q_LORA + MLA_KV_LORA:]

    qn = jnp.dot(qa, wqn_ref[...], preferred_element_type=F32) * qscale
    qn_ref[...] = qn.astype(BF16)
    qr = jnp.dot(qa, wqr_ref[...], preferred_element_type=F32) * qscale
    for t in range(qr.shape[1] // LANES):
        tile = qr[:, t * LANES:(t + 1) * LANES]
        if rope:
            tile = _rot(tile, cos, sin)
        qr_ref[:, t * LANES:(t + 1) * LANES] = tile.astype(BF16)

    if rope:
        kr = _rot(kr, cos, sin)
    first = _first_head_lanes(kr.shape)
    lane = lax.broadcasted_iota(jnp.int32, kr.shape, 1)
    kr_ref[:, :LANES] = jnp.where(lane == _shift_lane(0), 1.0,
                                  jnp.where(first, kr, 0.0)).astype(BF16)
    kr_ref[:, LANES:] = jnp.where(lane == _shift_lane(1), 1.0,
                                  jnp.where(first, 0.0, kr)).astype(BF16)

    kn_ref[...] = jnp.dot(ckv, wkn_ref[...], preferred_element_type=F32).astype(BF16)
    vt = lax.dot_general(wvt_ref[...], ckv, NT_DIMS, preferred_element_type=F32).astype(BF16)
    for hd in range(MLA_HEADS):
        vt_ref[0, hd, 0] = vt[hd * MLA_V:(hd + 1) * MLA_V, :]


def _mla_proj(x, mod, grp, g0, w, rope_tabs, tm, n_per_batch):
    t, d = x.shape
    n_tiles = t // tm
    nb = n_tiles // n_per_batch
    rope = rope_tabs is not None
    qscale = (MLA_NOPE + MLA_ROPE) ** -0.5 * LOG2E
    row = lambda i: (i, 0)
    in_specs = [pl.BlockSpec((tm, d), row), _const_spec((1, d)),
                _mod_spec(grp, 1, d), _mod_spec(grp, 0, d),
                _const_spec(w["w_in"].shape), _const_spec((1, MLA_Q_LORA)),
                _const_spec((1, MLA_KV_LORA)), _const_spec(w["w_qn"].shape),
                _const_spec(w["w_qr"].shape), _const_spec(w["w_kn"].shape),
                _const_spec(w["w_vt"].shape)]
    args = [x, g0, mod, mod, w["w_in"], w["g_qa"], w["g_kva"], w["w_qn"], w["w_qr"], w["w_kn"],
            w["w_vt"]]
    if rope:
        pos = lambda i: (i % n_per_batch, 0)
        in_specs += [pl.BlockSpec((tm, LANES), pos), pl.BlockSpec((tm, LANES), pos)]
        args += list(rope_tabs)
    hn = MLA_HEADS * MLA_NOPE
    hr = MLA_HEADS * MLA_ROPE
    out_shape = [jax.ShapeDtypeStruct((t, hn), BF16), jax.ShapeDtypeStruct((t, hr), BF16),
                 jax.ShapeDtypeStruct((t, hn), BF16), jax.ShapeDtypeStruct((t, 2 * LANES), BF16),
                 jax.ShapeDtypeStruct((nb, MLA_HEADS, n_per_batch, MLA_V, tm), BF16)]
    out_specs = [pl.BlockSpec((tm, hn), row), pl.BlockSpec((tm, hr), row),
                 pl.BlockSpec((tm, hn), row), pl.BlockSpec((tm, 2 * LANES), row),
                 pl.BlockSpec((1, MLA_HEADS, 1, MLA_V, tm),
                              lambda i: (i // n_per_batch, 0, i % n_per_batch, 0, 0))]
    return pl.pallas_call(
        functools.partial(_mla_proj_kernel, rope=rope, qscale=qscale),
        grid=(n_tiles,), in_specs=in_specs, out_specs=out_specs, out_shape=out_shape,
        compiler_params=_params(("arbitrary",)),
        name="mla_proj_lat" if rope else "mla_proj_ctx",
    )(*args)


def _mla_attn_kernel(*refs, n_lat, tk):
    if n_lat:
        (qn_ref, qr_ref, knc_ref, krc_ref, vtc_ref, knl_ref, krl_ref, vtl_ref, o_ref,
         acc_ref, s_ref, qa_ref, k2_ref) = refs
    else:
        qn_ref, qr_ref, knc_ref, krc_ref, vtc_ref, o_ref, acc_ref, s_ref, qa_ref, k2_ref = refs
    tq = qn_ref.shape[0]
    lc = knc_ref.shape[1]
    parity = pl.program_id(1) % 2

    def lat_keys(j):
        off = pl.multiple_of(j * tk, tk)
        return knl_ref[0, pl.ds(off, tk), :], krl_ref[0, pl.ds(off, tk), :]

    @pl.when(pl.program_id(2) == 0)
    def _():
        def sqnorm_max(kn, kr):
            a, b = kn.astype(F32), kr.astype(F32)
            r = jnp.sum(a * a + b * b, axis=1, keepdims=True)
            return jnp.max(r, axis=0, keepdims=True)
        mx = sqnorm_max(knc_ref[0], krc_ref[0])
        if n_lat:
            mx = lax.fori_loop(0, n_lat, lambda j, c: jnp.maximum(c, sqnorm_max(*lat_keys(j))), mx)
        k2_ref[...] = jnp.broadcast_to(mx, k2_ref.shape)

    qn = qn_ref[...]
    qr = qr_ref[...]
    lane = lax.broadcasted_iota(jnp.int32, qr.shape, 1)
    own = ((lane >> 5) & 1) == parity
    qnf, qrf = qn.astype(F32), jnp.where(own, qr.astype(F32), 0.0)
    q2 = jnp.sum(qnf * qnf + qrf * qrf, axis=1, keepdims=True)
    ref = jnp.sqrt(q2 * k2_ref[0:1, 0:1])
    fast = jnp.max(ref) <= REF_LIMIT
    shift = jnp.where(fast, -ref, 0.0).astype(BF16)
    qa_ref[:, :LANES] = qn
    qa_ref[:, LANES:] = jnp.where(lane == LANES - 1 - 32 * parity, shift, qr)

    def scores(kn, kr):
        return lax.dot_general(jnp.concatenate([kn, kr], axis=1), qa_ref[...], NT_DIMS,
                               preferred_element_type=F32)

    def finish(l):
        o_ref[...] = (acc_ref[...] * (1.0 / l)).T.astype(BF16)

    def shifted_softmax():
        def chunk(s, vt, l8):
            p = jnp.exp2(s)
            l8 = l8 + jnp.sum(p.reshape(-1, 8, tq), axis=0)
            return l8, jnp.dot(vt, p.astype(BF16), preferred_element_type=F32)

        if n_lat:
            s_ref[0] = scores(*lat_keys(0))
        l8, pv = chunk(scores(knc_ref[0], krc_ref[0]), vtc_ref[0, 0, 0], jnp.zeros((8, tq), F32))
        acc_ref[...] = pv
        if n_lat:
            def body(jj, l8):
                s = s_ref[0]
                pv = None
                for u in range(KV_UNROLL):
                    j = KV_UNROLL * jj + u
                    s_next = scores(*lat_keys(jnp.minimum(j + 1, n_lat - 1)))
                    l8, d = chunk(s, vtl_ref[0, 0, j], l8)
                    pv = d if pv is None else pv + d
                    s = s_next
                s_ref[0] = s
                acc_ref[...] += pv
                return l8
            l8 = lax.fori_loop(0, n_lat // KV_UNROLL, body, l8)
        finish(jnp.sum(l8, axis=0, keepdims=True))

    def online_softmax():
        def update(s_view, vt, m, l):
            m_new = jnp.maximum(m, jnp.max(s_view[...], axis=0, keepdims=True))
            alpha = jnp.exp2(m - m_new)
            p = jnp.exp2(s_view[...] - m_new)
            l_new = alpha * l + jnp.sum(p, axis=0, keepdims=True)
            acc_ref[...] = acc_ref[...] * alpha + jnp.dot(vt, p.astype(BF16),
                                                          preferred_element_type=F32)
            return m_new, l_new

        acc_ref[...] = jnp.zeros_like(acc_ref)
        m = jnp.full((1, tq), NEG_BIG, F32)
        l = jnp.zeros((1, tq), F32)
        if n_lat:
            s_ref[0] = scores(*lat_keys(0))
        ctx_view = s_ref.at[1, pl.ds(0, lc)]
        ctx_view[...] = scores(knc_ref[0], krc_ref[0])
        m, l = update(ctx_view, vtc_ref[0, 0, 0], m, l)
        if n_lat:
            def body(jj, carry):
                for u in range(KV_UNROLL):
                    j = KV_UNROLL * jj + u
                    s_ref[(u + 1) % 2] = scores(*lat_keys(jnp.minimum(j + 1, n_lat - 1)))
                    carry = update(s_ref.at[u % 2], vtl_ref[0, 0, j], *carry)
                return carry
            m, l = lax.fori_loop(0, n_lat // KV_UNROLL, body, (m, l))
        finish(l)

    lax.cond(fast, shifted_softmax, online_softmax)


def _mla_attn(qn, qr, ctx_kv, lat_kv, nb, tq):
    t = qn.shape[0]
    nq = t // nb // tq
    knc, krc, vtc = ctx_kv
    lc = knc.shape[1]
    qrow = lambda b, h, i: (b * nq + i, h)
    in_specs = [pl.BlockSpec((tq, LANES), qrow),
                pl.BlockSpec((tq, LANES), lambda b, h, i: (b * nq + i, h // 2)),
                pl.BlockSpec((1, lc, LANES), lambda b, h, i: (b, 0, h)),
                pl.BlockSpec((1, lc, LANES), lambda b, h, i: (b, 0, h % 2)),
                pl.BlockSpec((1, 1, 1, MLA_V, lc), lambda b, h, i: (b, h, 0, 0, 0))]
    args = [qn, qr, knc, krc, vtc]
    n_lat, tk = 0, 0
    if lat_kv is not None:
        knl, krl, vtl = lat_kv
        s = knl.shape[1]
        n_lat, tk = vtl.shape[2], vtl.shape[4]
        assert n_lat % KV_UNROLL == 0 and tk >= lc
        in_specs += [pl.BlockSpec((1, s, LANES), lambda b, h, i: (b, 0, h)),
                     pl.BlockSpec((1, s, LANES), lambda b, h, i: (b, 0, h % 2)),
                     pl.BlockSpec((1, 1, n_lat, MLA_V, tk), lambda b, h, i: (b, h, 0, 0, 0))]
        args += [knl, krl, vtl]
    scratch = [pltpu.VMEM((MLA_V, tq), F32), pltpu.VMEM((2, max(tk, lc), tq), F32),
               pltpu.VMEM((tq, 2 * LANES), BF16), pltpu.VMEM((8, LANES), F32)]
    return pl.pallas_call(
        functools.partial(_mla_attn_kernel, n_lat=n_lat, tk=tk),
        grid=(nb, MLA_HEADS, nq), in_specs=in_specs,
        out_specs=pl.BlockSpec((tq, MLA_V), qrow),
        out_shape=jax.ShapeDtypeStruct((t, MLA_HEADS * MLA_V), BF16),
        scratch_shapes=scratch,
        compiler_params=_params(("arbitrary", "arbitrary", "arbitrary")),
        name="mla_attn_lat" if n_lat else "mla_attn_ctx",
    )(*args)


def _outproj_kernel(o_ref, w_ref, x_ref, g_ref, gt_ref, out_ref):
    y = jnp.dot(o_ref[...], w_ref[...], preferred_element_type=F32)
    out_ref[...] = x_ref[...] + gt_ref[0] * _rms(y, g_ref[...])


def _outproj(o, w_out, x, g1, mod, grp, tm):
    t, d = x.shape
    row = lambda i: (i, 0)
    return pl.pallas_call(
        _outproj_kernel,
        grid=(t // tm,),
        in_specs=[pl.BlockSpec((tm, o.shape[1]), row), _const_spec(w_out.shape),
                  pl.BlockSpec((tm, d), row), _const_spec((1, d)), _mod_spec(grp, 2, d)],
        out_specs=pl.BlockSpec((tm, d), row),
        out_shape=jax.ShapeDtypeStruct((t, d), F32),
        compiler_params=_params(("arbitrary",)),
        name="outproj",
    )(o, w_out, x, g1, mod)


def _mlp_kernel(x_ref, g2_ref, sc_ref, sh_ref, w1_ref, w2_ref, g3_ref, gt_ref, out_ref,
                f_ref, acc_ref):
    k = pl.program_id(1)

    @pl.when(k == 0)
    def _():
        f = _rms(x_ref[...], g2_ref[...]) * (1.0 + sc_ref[0]) + sh_ref[0]
        f_ref[...] = f.astype(BF16)
        acc_ref[...] = jnp.zeros_like(acc_ref)

    u = jnp.maximum(jnp.dot(f_ref[...], w1_ref[...], preferred_element_type=F32), 0.0)
    acc_ref[...] += jnp.dot((u * u).astype(BF16), w2_ref[...], preferred_element_type=F32)

    @pl.when(k == pl.num_programs(1) - 1)
    def _():
        out_ref[...] = x_ref[...] + gt_ref[0] * _rms(acc_ref[...], g3_ref[...])


def _mlp(x, g2, g3, mod, grp, w1, w2, tm):
    t, d = x.shape
    dff = w1.shape[1]
    row = lambda i, k: (i, 0)
    return pl.pallas_call(
        _mlp_kernel,
        grid=(t // tm, dff // FF_TILE),
        in_specs=[pl.BlockSpec((tm, d), row), _const_spec((1, d)),
                  _mod_spec(grp, 4, d), _mod_spec(grp, 3, d),
                  pl.BlockSpec((d, FF_TILE), lambda i, k: (0, k)),
                  pl.BlockSpec((FF_TILE, d), lambda i, k: (k, 0)),
                  _const_spec((1, d)), _mod_spec(grp, 5, d)],
        out_specs=pl.BlockSpec((tm, d), row),
        out_shape=jax.ShapeDtypeStruct((t, d), F32),
        scratch_shapes=[pltpu.VMEM((tm, d), BF16), pltpu.VMEM((tm, d), F32)],
        compiler_params=_params(("arbitrary", "arbitrary")),
        name="mlp",
    )(x, g2, mod, mod, w1, w2, g3, mod)


def _swa_proj_kernel(*refs, rope, qscale):
    if rope:
        x_ref, g_ref, sc_ref, sh_ref, w_ref, wvt_ref, cos_ref, sin_ref, q_ref, k2_ref, vt_ref = refs
        cos, sin = cos_ref[...], sin_ref[...]
    else:
        x_ref, g_ref, sc_ref, sh_ref, w_ref, wvt_ref, q_ref, k2_ref, vt_ref = refs
    h = (_rms(x_ref[...], g_ref[...]) * (1.0 + sc_ref[0]) + sh_ref[0]).astype(BF16)
    p = jnp.dot(h, w_ref[...], preferred_element_type=F32)
    dq = q_ref.shape[1]
    for t in range(dq // LANES):
        tile = p[:, t * LANES:(t + 1) * LANES] * qscale
        if rope:
            tile = _rot(tile, cos, sin)
        q_ref[:, t * LANES:(t + 1) * LANES] = tile.astype(BF16)
    for c in range(SWA_KV_HEADS):
        tile = p[:, dq + c * LANES:dq + (c + 1) * LANES]
        if rope:
            tile = _rot(tile, cos, sin)
        first = _first_head_lanes(tile.shape)
        k2_ref[:, 2 * c * LANES:(2 * c + 1) * LANES] = jnp.where(first, tile, 0.0).astype(BF16)
        k2_ref[:, (2 * c + 1) * LANES:(2 * c + 2) * LANES] = jnp.where(first, 0.0, tile).astype(BF16)
    vt_ref[...] = lax.dot_general(wvt_ref[...], h, NT_DIMS,
                                  preferred_element_type=F32).astype(BF16)


def _swa_proj(x, mod, grp, g0, w, rope_tabs, tm, n_per_batch):
    t, d = x.shape
    rope = rope_tabs is not None
    row = lambda i: (i, 0)
    dq = SWA_HEADS * SWA_HEAD_DIM
    dkv = SWA_KV_HEADS * SWA_HEAD_DIM
    w_qk, w_vt = w
    in_specs = [pl.BlockSpec((tm, d), row), _const_spec((1, d)),
                _mod_spec(grp, 1, d), _mod_spec(grp, 0, d), _const_spec(w_qk.shape),
                _const_spec(w_vt.shape)]
    args = [x, g0, mod, mod, w_qk, w_vt]
    if rope:
        pos = lambda i: (i % n_per_batch, 0)
        in_specs += [pl.BlockSpec((tm, LANES), pos), pl.BlockSpec((tm, LANES), pos)]
        args += list(rope_tabs)
    out_shape = [jax.ShapeDtypeStruct((t, dq), BF16),
                 jax.ShapeDtypeStruct((t, 2 * SWA_KV_HEADS * LANES), BF16),
                 jax.ShapeDtypeStruct((dkv, t), BF16)]
    out_specs = [pl.BlockSpec((tm, dq), row), pl.BlockSpec((tm, 2 * SWA_KV_HEADS * LANES), row),
                 pl.BlockSpec((dkv, tm), lambda i: (0, i))]
    return pl.pallas_call(
        functools.partial(_swa_proj_kernel, rope=rope, qscale=SWA_HEAD_DIM ** -0.5 * LOG2E),
        grid=(t // tm,), in_specs=in_specs, out_specs=out_specs, out_shape=out_shape,
        compiler_params=_params(("arbitrary",)),
        name="swa_proj_lat" if rope else "swa_proj_ctx",
    )(*args)


def _swa_attn_kernel(sink_ref, q_ref, kc_ref, kp_ref, kcur_ref, kn_ref, vc_ref, vp_ref, vcur_ref,
                     vn_ref, o_ref, s_ref, *, qb):
    kvh = pl.program_id(1)
    i = pl.program_id(2)
    lc = kc_ref.shape[0]
    span = SWA_QBLK + 2 * SWA_WINDOW
    n_pairs = SWA_GROUP // 2
    kc, vc = kc_ref[...], vc_ref[...]
    kwin = jnp.concatenate([kp_ref[...], kcur_ref[...], kn_ref[...]], axis=0)
    vwin = jnp.concatenate([vp_ref[...], vcur_ref[...], vn_ref[...]], axis=1)
    r = lax.broadcasted_iota(jnp.int32, (lc + span, SWA_QBLK), 0)
    rel = r - lc - lax.broadcasted_iota(jnp.int32, (lc + span, SWA_QBLK), 1)
    bias = jnp.where((r < lc) | ((rel >= 0) & (rel <= 2 * SWA_WINDOW)), 0.0, NEG_BIG)
    bias = jnp.concatenate([bias] * n_pairs, axis=1)
    pair = lax.broadcasted_iota(jnp.int32, (1, n_pairs * SWA_QBLK), 1) // SWA_QBLK
    keep_prev = jnp.where(i == 0, 0.0, 1.0)
    keep_next = jnp.where(i == pl.num_programs(2) - 1, 0.0, 1.0)
    def scores(blk, e):
        qs = jnp.concatenate([q_ref[blk * SWA_QBLK:(blk + 1) * SWA_QBLK, t * LANES:(t + 1) * LANES]
                              for t in range(n_pairs)], axis=0)
        kcat = jnp.concatenate([kc[:, e * LANES:(e + 1) * LANES],
                                kwin[blk * SWA_QBLK:blk * SWA_QBLK + span,
                                     e * LANES:(e + 1) * LANES]], axis=0)
        return lax.dot_general(kcat, qs, NT_DIMS, preferred_element_type=F32)

    s_ref[0] = scores(0, 0)
    for blk in range(qb):
        rows = slice(blk * SWA_QBLK, (blk + 1) * SWA_QBLK)
        vt = jnp.concatenate([vc, vwin[:, blk * SWA_QBLK:blk * SWA_QBLK + span]], axis=1)
        halves = []
        for e in range(2):
            if e == 0:
                s_ref[1] = scores(blk, 1)
            elif blk + 1 < qb:
                s_ref[0] = scores(blk + 1, 0)
            s = s_ref[e] + bias
            sk = jnp.zeros(pair.shape, F32)
            for t in range(n_pairs):
                sk = jnp.where(pair == t, sink_ref[kvh * SWA_GROUP + 2 * t + e] * LOG2E, sk)
            m = jnp.maximum(jnp.max(s, axis=0, keepdims=True), sk)
            p = jnp.exp2(s - m)
            if blk == 0:
                p = jnp.concatenate([p[:lc], p[lc:lc + SWA_WINDOW] * keep_prev,
                                     p[lc + SWA_WINDOW:]], axis=0)
            if blk == qb - 1:
                p = jnp.concatenate([p[:lc + span - SWA_WINDOW],
                                     p[lc + span - SWA_WINDOW:] * keep_next], axis=0)
            den = jnp.sum(p, axis=0, keepdims=True) + jnp.exp2(sk - m)
            o = jnp.dot(vt, p.astype(BF16), preferred_element_type=F32)
            halves.append(o * (1.0 / den))
        both = jnp.concatenate(halves, axis=0)
        for t in range(n_pairs):
            tile = both[:, t * SWA_QBLK:(t + 1) * SWA_QBLK]
            o_ref[rows, t * LANES:(t + 1) * LANES] = tile.T.astype(BF16)


def _swa_attn(sink, q, k2, vt, k2c, vtc, nb, seq, lc):
    t = q.shape[0]
    nblk = seq // SWA_QBLK
    qb = min(SWA_STEP_BLOCKS, nblk)
    nsteps = nblk // qb
    gq = SWA_GROUP * SWA_HEAD_DIM
    hd = SWA_HEAD_DIM
    prev_blk = lambda b, i: b * nblk + jnp.maximum(qb * i - 1, 0)
    next_blk = lambda b, i: b * nblk + jnp.minimum(qb * i + qb, nblk - 1)
    return pl.pallas_call(
        functools.partial(_swa_attn_kernel, qb=qb),
        grid=(nb, SWA_KV_HEADS, nsteps),
        in_specs=[pl.BlockSpec(memory_space=pltpu.SMEM),
                  pl.BlockSpec((qb * SWA_QBLK, gq), lambda b, h, i: (b * nsteps + i, h)),
                  pl.BlockSpec((lc, 2 * LANES), lambda b, h, i: (b, h)),
                  pl.BlockSpec((SWA_QBLK, 2 * LANES), lambda b, h, i: (prev_blk(b, i), h)),
                  pl.BlockSpec((qb * SWA_QBLK, 2 * LANES), lambda b, h, i: (b * nsteps + i, h)),
                  pl.BlockSpec((SWA_QBLK, 2 * LANES), lambda b, h, i: (next_blk(b, i), h)),
                  pl.BlockSpec((hd, lc), lambda b, h, i: (h, b)),
                  pl.BlockSpec((hd, SWA_QBLK), lambda b, h, i: (h, prev_blk(b, i))),
                  pl.BlockSpec((hd, qb * SWA_QBLK), lambda b, h, i: (h, b * nsteps + i)),
                  pl.BlockSpec((hd, SWA_QBLK), lambda b, h, i: (h, next_blk(b, i)))],
        out_specs=pl.BlockSpec((qb * SWA_QBLK, gq), lambda b, h, i: (b * nsteps + i, h)),
        out_shape=jax.ShapeDtypeStruct((t, SWA_HEADS * SWA_HEAD_DIM), BF16),
        scratch_shapes=[pltpu.VMEM((2, lc + SWA_QBLK + 2 * SWA_WINDOW, gq), F32)],
        compiler_params=_params(("arbitrary", "arbitrary", "arbitrary")),
        name="swa_attn",
    )(sink, q, k2c, k2, k2, k2, vtc, vt, vt, vt)


def _rope_tables(seq):
    rows = seq // GRID_W
    row = jnp.repeat(jnp.arange(rows, dtype=F32), GRID_W)
    col = jnp.tile(jnp.arange(GRID_W, dtype=F32), rows)
    n_freq = MLA_ROPE // 4
    freqs = ROPE_BASE ** (-jnp.arange(n_freq, dtype=F32) / n_freq)
    ang = jnp.concatenate([row[:, None] * freqs, col[:, None] * freqs], axis=-1)
    cos, sin = jnp.cos(ang), jnp.sin(ang)
    return (jnp.concatenate([cos, cos, cos, cos], axis=-1),
            jnp.concatenate([-sin, -sin, sin, sin], axis=-1))


def _pair_tiles(w, n_heads, half):
    k = w.shape[0]
    x1 = w[:, :, :half].reshape(k, n_heads // 2, 2 * half)
    x2 = w[:, :, half:].reshape(k, n_heads // 2, 2 * half)
    return jnp.concatenate([x1, x2], axis=2).reshape(k, n_heads * 2 * half)


def _mla_weights(w_in, g_qa, g_kva, w_qb, w_kvb):
    half = MLA_ROPE // 2
    lat = MLA_Q_LORA + MLA_KV_LORA
    k1, k2 = w_in[:, lat:lat + half], w_in[:, lat + half:]
    qb = w_qb.reshape(MLA_Q_LORA, MLA_HEADS, MLA_NOPE + MLA_ROPE)
    kvb = w_kvb.reshape(MLA_KV_LORA, MLA_HEADS, MLA_NOPE + MLA_V)
    return {
        "w_in": jnp.concatenate([w_in[:, :lat], k1, k1, k2, k2], axis=1).astype(BF16),
        "g_qa": g_qa.reshape(1, -1), "g_kva": g_kva.reshape(1, -1),
        "w_qn": qb[:, :, :MLA_NOPE].reshape(MLA_Q_LORA, -1).astype(BF16),
        "w_qr": _pair_tiles(qb[:, :, MLA_NOPE:], MLA_HEADS, half).astype(BF16),
        "w_kn": kvb[:, :, :MLA_NOPE].reshape(MLA_KV_LORA, -1).astype(BF16),
        "w_vt": kvb[:, :, MLA_NOPE:].reshape(MLA_KV_LORA, -1).T.astype(BF16),
    }


def _swa_weights(w_qkv):
    d = w_qkv.shape[0]
    half = SWA_HEAD_DIM // 2
    dq = SWA_HEADS * SWA_HEAD_DIM
    dkv = SWA_KV_HEADS * SWA_HEAD_DIM
    q = _pair_tiles(w_qkv[:, :dq].reshape(d, SWA_HEADS, SWA_HEAD_DIM), SWA_HEADS, half)
    k = w_qkv[:, dq:dq + dkv].reshape(d, SWA_KV_HEADS, SWA_HEAD_DIM)
    k1, k2 = k[:, :, :half], k[:, :, half:]
    k = jnp.concatenate([k1, k1, k2, k2], axis=2).reshape(d, SWA_KV_HEADS * LANES)
    return jnp.concatenate([q, k], axis=1).astype(BF16), w_qkv[:, dq + dkv:].T.astype(BF16)


def kernel(x, c, ctx, c_ctx, w_mod, b_mod, g_norm, w_ff_in, w_ff_out, mla_w_in, mla_g_qa,
           mla_g_kva, mla_w_qb, mla_w_kvb, mla_w_out, swa_w_qkv, swa_sink, swa_w_out):
    nb, seq, d = x.shape
    lc = ctx.shape[1]
    depth = w_mod.shape[0]
    assert nb + 1 <= MOD_ROWS
    tm = min(TOKEN_TILE, seq)
    tq = min(ATTN_TQ, seq)
    n_per_batch = seq // tm
    grp_lat = lambda i: i // n_per_batch
    grp_ctx = lambda i: nb

    cmat = jnp.zeros((MOD_ROWS, d), F32).at[:nb].set(c).at[nb].set(c_ctx)
    mod_all = _modulation(cmat, w_mod, b_mod)
    rope_tabs = _rope_tables(seq)

    xl = x.reshape(nb * seq, d)
    xc = ctx.reshape(nb * lc, d)
    for i in range(depth):
        need_ctx = i < depth - 1
        mod = mod_all[i].reshape(MOD_ROWS, 1, 6 * d)
        g = g_norm[i].reshape(4, 1, d)
        j = i // 2
        if i % 2 == 0:
            w = _mla_weights(mla_w_in[j], mla_g_qa[j], mla_g_kva[j], mla_w_qb[j], mla_w_kvb[j])
            w_out = mla_w_out[j].astype(BF16)
            qn, qr, kn, kr, vt = _mla_proj(xl, mod, grp_lat, g[0], w, rope_tabs, tm, n_per_batch)
            qnc, qrc, knc, krc, vtc = _mla_proj(xc, mod, grp_ctx, g[0], w, None, lc, 1)
            ctx_kv = (knc.reshape(nb, lc, -1), krc.reshape(nb, lc, -1), vtc)
            lat_kv = (kn.reshape(nb, seq, -1), kr.reshape(nb, seq, -1), vt)
            o_l = _mla_attn(qn, qr, ctx_kv, lat_kv, nb, tq)
            o_c = _mla_attn(qnc, qrc, ctx_kv, None, nb, lc) if need_ctx else None
        else:
            w = _swa_weights(swa_w_qkv[j])
            w_out = swa_w_out[j].astype(BF16)
            q, k2, vt = _swa_proj(xl, mod, grp_lat, g[0], w, rope_tabs, tm, n_per_batch)
            qc, k2c, vtc = _swa_proj(xc, mod, grp_ctx, g[0], w, None, lc, 1)
            o_l = _swa_attn(swa_sink[j], q, k2, vt, k2c, vtc, nb, seq, lc)
            assert not need_ctx
            o_c = None
        w1 = w_ff_in[i].astype(BF16)
        w2 = w_ff_out[i].astype(BF16)
        xl = _outproj(o_l, w_out, xl, g[1], mod, grp_lat, tm)
        xl = _mlp(xl, g[2], g[3], mod, grp_lat, w1, w2, tm)
        if need_ctx:
            xc = _outproj(o_c, w_out, xc, g[1], mod, grp_ctx, lc)
            xc = _mlp(xc, g[2], g[3], mod, grp_ctx, w1, w2, lc)
    return xl.reshape(nb, seq, d)
```

```python
import functools
import math

import jax
import jax.numpy as jnp
from jax import lax
from jax.experimental import pallas as pl
from jax.experimental.pallas import tpu as pltpu

F32 = jnp.float32
BF16 = jnp.bfloat16

GRID_W = 64
ROPE_BASE = 10000.0
NORM_EPS = 1e-6
LOG2E = math.log2(math.e)
NEG_BIG = -1e30
REF_LIMIT = 60.0

MLA_HEADS = 16
MLA_Q_LORA = 512
MLA_KV_LORA = 512
MLA_NOPE = 128
MLA_ROPE = 64
MLA_V = 128

SWA_HEADS = 32
SWA_KV_HEADS = 4
SWA_HEAD_DIM = 64
SWA_WINDOW = 128
SWA_GROUP = SWA_HEADS // SWA_KV_HEADS
SWA_QBLK = 128
SWA_STEP_BLOCKS = 4

LANES = 128
HALF_TILE = 64

MOD_ROWS = 8
MOD_TN = 1024
TOKEN_TILE = 512
FF_TILE = 1024
ATTN_TQ = 512
KV_GROUP = 4
KV_UNROLL = 4
VMEM_LIMIT = 56 * 1024 * 1024

NT_DIMS = (((1,), (1,)), ((), ()))


def _rms(xf, g):
    ms = jnp.mean(xf * xf, axis=-1, keepdims=True)
    return xf * lax.rsqrt(ms + NORM_EPS) * g


def _rot(tile, cos, sin):
    return tile * cos + pltpu.roll(tile, HALF_TILE, 1) * sin


def _first_head_lanes(shape):
    lane = lax.broadcasted_iota(jnp.int32, shape, 1)
    return (lane & 32) == 0


def _shift_lane(parity):
    return LANES - 1 - 32 * parity


def _params(sem):
    return pltpu.CompilerParams(dimension_semantics=sem, vmem_limit_bytes=VMEM_LIMIT)


def _const_spec(shape):
    nd = len(shape)
    return pl.BlockSpec(shape, lambda *_: (0,) * nd, pipeline_mode=pl.Buffered(1))


def _mod_spec(grp, which, d):
    return pl.BlockSpec((1, 1, d), lambda i, *_: (grp(i), 0, which))


def _mod_kernel(c_ref, w_ref, b_ref, o_ref):
    c = c_ref[...]
    a = c / (1.0 + jnp.exp(-c))
    o_ref[0] = jnp.dot(a, w_ref[0], preferred_element_type=F32,
                       precision=lax.Precision.HIGHEST) + b_ref[0]


def _modulation(cmat, w_mod, b_mod):
    depth, d, n = w_mod.shape
    return pl.pallas_call(
        _mod_kernel,
        grid=(depth, n // MOD_TN),
        in_specs=[pl.BlockSpec((MOD_ROWS, d), lambda l, j: (0, 0)),
                  pl.BlockSpec((1, d, MOD_TN), lambda l, j: (l, 0, j)),
                  pl.BlockSpec((1, 1, MOD_TN), lambda l, j: (l, 0, j))],
        out_specs=pl.BlockSpec((1, MOD_ROWS, MOD_TN), lambda l, j: (l, 0, j)),
        out_shape=jax.ShapeDtypeStruct((depth, MOD_ROWS, n), F32),
        compiler_params=_params(("arbitrary", "arbitrary")),
        name="modulation",
    )(cmat, w_mod, b_mod.reshape(depth, 1, n))


def _mla_proj_kernel(*refs, rope, qscale):
    if rope:
        (x_ref, g_ref, sc_ref, sh_ref, win_ref, gqa_ref, gkva_ref, wqn_ref, wqr_ref, wkn_ref,
         wvt_ref, cos_ref, sin_ref, qn_ref, qr_ref, kn_ref, kr_ref, vt_ref) = refs
        cos, sin = cos_ref[...], sin_ref[...]
    else:
        (x_ref, g_ref, sc_ref, sh_ref, win_ref, gqa_ref, gkva_ref, wqn_ref, wqr_ref, wkn_ref,
         wvt_ref, qn_ref, qr_ref, kn_ref, kr_ref, vt_ref) = refs
    h = (_rms(x_ref[...], g_ref[...]) * (1.0 + sc_ref[0]) + sh_ref[0]).astype(BF16)
    p = jnp.dot(h, win_ref[...], preferred_element_type=F32)
    qa = _rms(p[:, :MLA_Q_LORA], gqa_ref[...]).astype(BF16)
    ckv = _rms(p[:, MLA_Q_LORA:MLA_Q_LORA + MLA_KV_LORA], gkva_ref[...]).astype(BF16)
    kr = p[:, MLA_Q_LORA + MLA_KV_LORA:]

    qn = jnp.dot(qa, wqn_ref[...], preferred_element_type=F32) * qscale
    qn_ref[...] = qn.astype(BF16)
    qr = jnp.dot(qa, wqr_ref[...], preferred_element_type=F32) * qscale
    for t in range(qr.shape[1] // LANES):
        tile = qr[:, t * LANES:(t + 1) * LANES]
        if rope:
            tile = _rot(tile, cos, sin)
        qr_ref[:, t * LANES:(t + 1) * LANES] = tile.astype(BF16)

    if rope:
        kr = _rot(kr, cos, sin)
    first = _first_head_lanes(kr.shape)
    lane = lax.broadcasted_iota(jnp.int32, kr.shape, 1)
    kr_ref[:, :LANES] = jnp.where(lane == _shift_lane(0), 1.0,
                                  jnp.where(first, kr, 0.0)).astype(BF16)
    kr_ref[:, LANES:] = jnp.where(lane == _shift_lane(1), 1.0,
                                  jnp.where(first, 0.0, kr)).astype(BF16)

    kn_ref[...] = jnp.dot(ckv, wkn_ref[...], preferred_element_type=F32).astype(BF16)
    vt = lax.dot_general(wvt_ref[...], ckv, NT_DIMS, preferred_element_type=F32).astype(BF16)
    for hd in range(MLA_HEADS):
        vt_ref[0, hd, 0] = vt[hd * MLA_V:(hd + 1) * MLA_V, :]


def _mla_proj(x, mod, grp, g0, w, rope_tabs, tm, n_per_batch):
    t, d = x.shape
    n_tiles = t // tm
    nb = n_tiles // n_per_batch
    rope = rope_tabs is not None
    qscale = (MLA_NOPE + MLA_ROPE) ** -0.5 * LOG2E
    row = lambda i: (i, 0)
    in_specs = [pl.BlockSpec((tm, d), row), _const_spec((1, d)),
                _mod_spec(grp, 1, d), _mod_spec(grp, 0, d),
                _const_spec(w["w_in"].shape), _const_spec((1, MLA_Q_LORA)),
                _const_spec((1, MLA_KV_LORA)), _const_spec(w["w_qn"].shape),
                _const_spec(w["w_qr"].shape), _const_spec(w["w_kn"].shape),
                _const_spec(w["w_vt"].shape)]
    args = [x, g0, mod, mod, w["w_in"], w["g_qa"], w["g_kva"], w["w_qn"], w["w_qr"], w["w_kn"],
            w["w_vt"]]
    if rope:
        pos = lambda i: (i % n_per_batch, 0)
        in_specs += [pl.BlockSpec((tm, LANES), pos), pl.BlockSpec((tm, LANES), pos)]
        args += list(rope_tabs)
    hn = MLA_HEADS * MLA_NOPE
    hr = MLA_HEADS * MLA_ROPE
    out_shape = [jax.ShapeDtypeStruct((t, hn), BF16), jax.ShapeDtypeStruct((t, hr), BF16),
                 jax.ShapeDtypeStruct((t, hn), BF16), jax.ShapeDtypeStruct((t, 2 * LANES), BF16),
                 jax.ShapeDtypeStruct((nb, MLA_HEADS, n_per_batch, MLA_V, tm), BF16)]
    out_specs = [pl.BlockSpec((tm, hn), row), pl.BlockSpec((tm, hr), row),
                 pl.BlockSpec((tm, hn), row), pl.BlockSpec((tm, 2 * LANES), row),
                 pl.BlockSpec((1, MLA_HEADS, 1, MLA_V, tm),
                              lambda i: (i // n_per_batch, 0, i % n_per_batch, 0, 0))]
    return pl.pallas_call(
        functools.partial(_mla_proj_kernel, rope=rope, qscale=qscale),
        grid=(n_tiles,), in_specs=in_specs, out_specs=out_specs, out_shape=out_shape,
        compiler_params=_params(("arbitrary",)),
        name="mla_proj_lat" if rope else "mla_proj_ctx",
    )(*args)


def _mla_attn_kernel(*refs, n_lat, tk):
    if n_lat:
        (qn_ref, qr_ref, knc_ref, krc_ref, vtc_ref, knl_ref, krl_ref, vtl_ref, o_ref,
         acc_ref, s_ref, qa_ref, k2_ref) = refs
    else:
        qn_ref, qr_ref, knc_ref, krc_ref, vtc_ref, o_ref, acc_ref, s_ref, qa_ref, k2_ref = refs
    tq = qn_ref.shape[0]
    lc = knc_ref.shape[1]
    parity = pl.program_id(1) % 2

    def lat_keys(j):
        off = pl.multiple_of(j * tk, tk)
        return knl_ref[0, pl.ds(off, tk), :], krl_ref[0, pl.ds(off, tk), :]

    @pl.when(pl.program_id(2) == 0)
    def _():
        def sqnorm_max(kn, kr):
            a, b = kn.astype(F32), kr.astype(F32)
            r = jnp.sum(a * a + b * b, axis=1, keepdims=True)
            return jnp.max(r, axis=0, keepdims=True)
        mx = sqnorm_max(knc_ref[0], krc_ref[0])
        if n_lat:
            mx = lax.fori_loop(0, n_lat, lambda j, c: jnp.maximum(c, sqnorm_max(*lat_keys(j))), mx)
        k2_ref[...] = jnp.broadcast_to(mx, k2_ref.shape)

    qn = qn_ref[...]
    qr = qr_ref[...]
    lane = lax.broadcasted_iota(jnp.int32, qr.shape, 1)
    own = ((lane >> 5) & 1) == parity
    qnf, qrf = qn.astype(F32), jnp.where(own, qr.astype(F32), 0.0)
    q2 = jnp.sum(qnf * qnf + qrf * qrf, axis=1, keepdims=True)
    ref = jnp.sqrt(q2 * k2_ref[0:1, 0:1])
    fast = jnp.max(ref) <= REF_LIMIT
    shift = jnp.where(fast, -ref, 0.0).astype(BF16)
    qa_ref[:, :LANES] = qn
    qa_ref[:, LANES:] = jnp.where(lane == LANES - 1 - 32 * parity, shift, qr)

    def scores(kn, kr):
        return lax.dot_general(jnp.concatenate([kn, kr], axis=1), qa_ref[...], NT_DIMS,
                               preferred_element_type=F32)

    def finish(l):
        o_ref[...] = (acc_ref[...] * (1.0 / l)).T.astype(BF16)

    def shifted_softmax():
        def chunk(s, vt, l8):
            p = jnp.exp2(s)
            l8 = l8 + jnp.sum(p.reshape(-1, 8, tq), axis=0)
            return l8, jnp.dot(vt, p.astype(BF16), preferred_element_type=F32)

        gk = KV_GROUP * tk
        n_groups = n_lat // KV_GROUP

        def group_scores(g):
            off = pl.multiple_of(g * gk, gk)
            return scores(knl_ref[0, pl.ds(off, gk), :], krl_ref[0, pl.ds(off, gk), :])

        def consume(slot, g, l8):
            vt = jnp.concatenate([vtl_ref[0, 0, KV_GROUP * g + u] for u in range(KV_GROUP)], axis=1)
            l8, pv = chunk(s_ref[slot], vt, l8)
            acc_ref[...] += pv
            return l8

        def pair(g, l8, last):
            s_ref[1] = group_scores(g + 1)
            l8 = consume(0, g, l8)
            if not last:
                s_ref[0] = group_scores(g + 2)
            return consume(1, g + 1, l8)

        if n_lat:
            s_ref[0] = group_scores(0)
        l8, pv = chunk(scores(knc_ref[0], krc_ref[0]), vtc_ref[0, 0, 0], jnp.zeros((8, tq), F32))
        acc_ref[...] = pv
        if n_lat:
            l8 = lax.fori_loop(0, n_groups // 2 - 1, lambda jj, c: pair(2 * jj, c, False), l8)
            l8 = pair(n_groups - 2, l8, True)
        finish(jnp.sum(l8, axis=0, keepdims=True))

    def online_softmax():
        def update(s_view, vt, m, l):
            m_new = jnp.maximum(m, jnp.max(s_view[...], axis=0, keepdims=True))
            alpha = jnp.exp2(m - m_new)
            p = jnp.exp2(s_view[...] - m_new)
            l_new = alpha * l + jnp.sum(p, axis=0, keepdims=True)
            acc_ref[...] = acc_ref[...] * alpha + jnp.dot(vt, p.astype(BF16),
                                                          preferred_element_type=F32)
            return m_new, l_new

        acc_ref[...] = jnp.zeros_like(acc_ref)
        m = jnp.full((1, tq), NEG_BIG, F32)
        l = jnp.zeros((1, tq), F32)
        slots = [s_ref.at[0, pl.ds(0, tk)], s_ref.at[1, pl.ds(0, tk)]] if n_lat else None
        if n_lat:
            slots[0][...] = scores(*lat_keys(0))
        ctx_view = s_ref.at[1, pl.ds(0, lc)]
        ctx_view[...] = scores(knc_ref[0], krc_ref[0])
        m, l = update(ctx_view, vtc_ref[0, 0, 0], m, l)
        if n_lat:
            def body(jj, carry):
                for u in range(KV_UNROLL):
                    j = KV_UNROLL * jj + u
                    slots[(u + 1) % 2][...] = scores(*lat_keys(jnp.minimum(j + 1, n_lat - 1)))
                    carry = update(slots[u % 2], vtl_ref[0, 0, j], *carry)
                return carry
            m, l = lax.fori_loop(0, n_lat // KV_UNROLL, body, (m, l))
        finish(l)

    lax.cond(fast, shifted_softmax, online_softmax)


def _mla_attn(qn, qr, ctx_kv, lat_kv, nb, tq):
    t = qn.shape[0]
    nq = t // nb // tq
    knc, krc, vtc = ctx_kv
    lc = knc.shape[1]
    qrow = lambda b, h, i: (b * nq + i, h)
    in_specs = [pl.BlockSpec((tq, LANES), qrow),
                pl.BlockSpec((tq, LANES), lambda b, h, i: (b * nq + i, h // 2)),
                pl.BlockSpec((1, lc, LANES), lambda b, h, i: (b, 0, h)),
                pl.BlockSpec((1, lc, LANES), lambda b, h, i: (b, 0, h % 2)),
                pl.BlockSpec((1, 1, 1, MLA_V, lc), lambda b, h, i: (b, h, 0, 0, 0))]
    args = [qn, qr, knc, krc, vtc]
    n_lat, tk = 0, 0
    if lat_kv is not None:
        knl, krl, vtl = lat_kv
        s = knl.shape[1]
        n_lat, tk = vtl.shape[2], vtl.shape[4]
        assert n_lat % KV_UNROLL == 0 and n_lat % (2 * KV_GROUP) == 0 and tk >= lc
        in_specs += [pl.BlockSpec((1, s, LANES), lambda b, h, i: (b, 0, h)),
                     pl.BlockSpec((1, s, LANES), lambda b, h, i: (b, 0, h % 2)),
                     pl.BlockSpec((1, 1, n_lat, MLA_V, tk), lambda b, h, i: (b, h, 0, 0, 0))]
        args += [knl, krl, vtl]
    scratch = [pltpu.VMEM((MLA_V, tq), F32), pltpu.VMEM((2, max(KV_GROUP * tk, lc), tq), F32),
               pltpu.VMEM((tq, 2 * LANES), BF16), pltpu.VMEM((8, LANES), F32)]
    return pl.pallas_call(
        functools.partial(_mla_attn_kernel, n_lat=n_lat, tk=tk),
        grid=(nb, MLA_HEADS, nq), in_specs=in_specs,
        out_specs=pl.BlockSpec((tq, MLA_V), qrow),
        out_shape=jax.ShapeDtypeStruct((t, MLA_HEADS * MLA_V), BF16),
        scratch_shapes=scratch,
        compiler_params=_params(("arbitrary", "arbitrary", "arbitrary")),
        name="mla_attn_lat" if n_lat else "mla_attn_ctx",
    )(*args)


def _outproj_kernel(o_ref, w_ref, x_ref, g_ref, gt_ref, out_ref):
    y = jnp.dot(o_ref[...], w_ref[...], preferred_element_type=F32)
    out_ref[...] = x_ref[...] + gt_ref[0] * _rms(y, g_ref[...])


def _outproj(o, w_out, x, g1, mod, grp, tm):
    t, d = x.shape
    row = lambda i: (i, 0)
    return pl.pallas_call(
        _outproj_kernel,
        grid=(t // tm,),
        in_specs=[pl.BlockSpec((tm, o.shape[1]), row), _const_spec(w_out.shape),
                  pl.BlockSpec((tm, d), row), _const_spec((1, d)), _mod_spec(grp, 2, d)],
        out_specs=pl.BlockSpec((tm, d), row),
        out_shape=jax.ShapeDtypeStruct((t, d), F32),
        compiler_params=_params(("arbitrary",)),
        name="outproj",
    )(o, w_out, x, g1, mod)


def _mlp_kernel(x_ref, g2_ref, sc_ref, sh_ref, w1_ref, w2_ref, g3_ref, gt_ref, out_ref,
                f_ref, acc_ref):
    k = pl.program_id(1)

    @pl.when(k == 0)
    def _():
        f = _rms(x_ref[...], g2_ref[...]) * (1.0 + sc_ref[0]) + sh_ref[0]
        f_ref[...] = f.astype(BF16)
        acc_ref[...] = jnp.zeros_like(acc_ref)

    u = jnp.maximum(jnp.dot(f_ref[...], w1_ref[...], preferred_element_type=F32), 0.0)
    acc_ref[...] += jnp.dot((u * u).astype(BF16), w2_ref[...], preferred_element_type=F32)

    @pl.when(k == pl.num_programs(1) - 1)
    def _():
        out_ref[...] = x_ref[...] + gt_ref[0] * _rms(acc_ref[...], g3_ref[...])


def _mlp(x, g2, g3, mod, grp, w1, w2, tm):
    t, d = x.shape
    dff = w1.shape[1]
    row = lambda i, k: (i, 0)
    return pl.pallas_call(
        _mlp_kernel,
        grid=(t // tm, dff // FF_TILE),
        in_specs=[pl.BlockSpec((tm, d), row), _const_spec((1, d)),
                  _mod_spec(grp, 4, d), _mod_spec(grp, 3, d),
                  pl.BlockSpec((d, FF_TILE), lambda i, k: (0, k)),
                  pl.BlockSpec((FF_TILE, d), lambda i, k: (k, 0)),
                  _const_spec((1, d)), _mod_spec(grp, 5, d)],
        out_specs=pl.BlockSpec((tm, d), row),
        out_shape=jax.ShapeDtypeStruct((t, d), F32),
        scratch_shapes=[pltpu.VMEM((tm, d), BF16), pltpu.VMEM((tm, d), F32)],
        compiler_params=_params(("arbitrary", "arbitrary")),
        name="mlp",
    )(x, g2, mod, mod, w1, w2, g3, mod)


def _swa_proj_kernel(*refs, rope, qscale):
    if rope:
        x_ref, g_ref, sc_ref, sh_ref, w_ref, wvt_ref, cos_ref, sin_ref, q_ref, k2_ref, vt_ref = refs
        cos, sin = cos_ref[...], sin_ref[...]
    else:
        x_ref, g_ref, sc_ref, sh_ref, w_ref, wvt_ref, q_ref, k2_ref, vt_ref = refs
    h = (_rms(x_ref[...], g_ref[...]) * (1.0 + sc_ref[0]) + sh_ref[0]).astype(BF16)
    p = jnp.dot(h, w_ref[...], preferred_element_type=F32)
    dq = q_ref.shape[1]
    for t in range(dq // LANES):
        tile = p[:, t * LANES:(t + 1) * LANES] * qscale
        if rope:
            tile = _rot(tile, cos, sin)
        q_ref[:, t * LANES:(t + 1) * LANES] = tile.astype(BF16)
    for c in range(SWA_KV_HEADS):
        tile = p[:, dq + c * LANES:dq + (c + 1) * LANES]
        if rope:
            tile = _rot(tile, cos, sin)
        first = _first_head_lanes(tile.shape)
        k2_ref[:, 2 * c * LANES:(2 * c + 1) * LANES] = jnp.where(first, tile, 0.0).astype(BF16)
        k2_ref[:, (2 * c + 1) * LANES:(2 * c + 2) * LANES] = jnp.where(first, 0.0, tile).astype(BF16)
    vt_ref[...] = lax.dot_general(wvt_ref[...], h, NT_DIMS,
                                  preferred_element_type=F32).astype(BF16)


def _swa_proj(x, mod, grp, g0, w, rope_tabs, tm, n_per_batch):
    t, d = x.shape
    rope = rope_tabs is not None
    row = lambda i: (i, 0)
    dq = SWA_HEADS * SWA_HEAD_DIM
    dkv = SWA_KV_HEADS * SWA_HEAD_DIM
    w_qk, w_vt = w
    in_specs = [pl.BlockSpec((tm, d), row), _const_spec((1, d)),
                _mod_spec(grp, 1, d), _mod_spec(grp, 0, d), _const_spec(w_qk.shape),
                _const_spec(w_vt.shape)]
    args = [x, g0, mod, mod, w_qk, w_vt]
    if rope:
        pos = lambda i: (i % n_per_batch, 0)
        in_specs += [pl.BlockSpec((tm, LANES), pos), pl.BlockSpec((tm, LANES), pos)]
        args += list(rope_tabs)
    out_shape = [jax.ShapeDtypeStruct((t, dq), BF16),
                 jax.ShapeDtypeStruct((t, 2 * SWA_KV_HEADS * LANES), BF16),
                 jax.ShapeDtypeStruct((dkv, t), BF16)]
    out_specs = [pl.BlockSpec((tm, dq), row), pl.BlockSpec((tm, 2 * SWA_KV_HEADS * LANES), row),
                 pl.BlockSpec((dkv, tm), lambda i: (0, i))]
    return pl.pallas_call(
        functools.partial(_swa_proj_kernel, rope=rope, qscale=SWA_HEAD_DIM ** -0.5 * LOG2E),
        grid=(t // tm,), in_specs=in_specs, out_specs=out_specs, out_shape=out_shape,
        compiler_params=_params(("arbitrary",)),
        name="swa_proj_lat" if rope else "swa_proj_ctx",
    )(*args)


def _swa_attn_kernel(sink_ref, q_ref, kc_ref, kp_ref, kcur_ref, kn_ref, vc_ref, vp_ref, vcur_ref,
                     vn_ref, o_ref, s_ref, *, qb):
    kvh = pl.program_id(1)
    i = pl.program_id(2)
    lc = kc_ref.shape[0]
    span = SWA_QBLK + 2 * SWA_WINDOW
    n_pairs = SWA_GROUP // 2
    kc, vc = kc_ref[...], vc_ref[...]
    kwin = jnp.concatenate([kp_ref[...], kcur_ref[...], kn_ref[...]], axis=0)
    vwin = jnp.concatenate([vp_ref[...], vcur_ref[...], vn_ref[...]], axis=1)
    r = lax.broadcasted_iota(jnp.int32, (lc + span, SWA_QBLK), 0)
    rel = r - lc - lax.broadcasted_iota(jnp.int32, (lc + span, SWA_QBLK), 1)
    bias = jnp.where((r < lc) | ((rel >= 0) & (rel <= 2 * SWA_WINDOW)), 0.0, NEG_BIG)
    bias = jnp.concatenate([bias] * n_pairs, axis=1)
    pair = lax.broadcasted_iota(jnp.int32, (1, n_pairs * SWA_QBLK), 1) // SWA_QBLK
    keep_prev = jnp.where(i == 0, 0.0, 1.0)
    keep_next = jnp.where(i == pl.num_programs(2) - 1, 0.0, 1.0)
    def scores(blk, e):
        qs = jnp.concatenate([q_ref[blk * SWA_QBLK:(blk + 1) * SWA_QBLK, t * LANES:(t + 1) * LANES]
                              for t in range(n_pairs)], axis=0)
        kcat = jnp.concatenate([kc[:, e * LANES:(e + 1) * LANES],
                                kwin[blk * SWA_QBLK:blk * SWA_QBLK + span,
                                     e * LANES:(e + 1) * LANES]], axis=0)
        return lax.dot_general(kcat, qs, NT_DIMS, preferred_element_type=F32)

    s_ref[0] = scores(0, 0)
    for blk in range(qb):
        rows = slice(blk * SWA_QBLK, (blk + 1) * SWA_QBLK)
        vt = jnp.concatenate([vc, vwin[:, blk * SWA_QBLK:blk * SWA_QBLK + span]], axis=1)
        halves = []
        for e in range(2):
            if e == 0:
                s_ref[1] = scores(blk, 1)
            elif blk + 1 < qb:
                s_ref[0] = scores(blk + 1, 0)
            s = s_ref[e] + bias
            sk = jnp.zeros(pair.shape, F32)
            for t in range(n_pairs):
                sk = jnp.where(pair == t, sink_ref[kvh * SWA_GROUP + 2 * t + e] * LOG2E, sk)
            m = jnp.maximum(jnp.max(s, axis=0, keepdims=True), sk)
            p = jnp.exp2(s - m)
            if blk == 0:
                p = jnp.concatenate([p[:lc], p[lc:lc + SWA_WINDOW] * keep_prev,
                                     p[lc + SWA_WINDOW:]], axis=0)
            if blk == qb - 1:
                p = jnp.concatenate([p[:lc + span - SWA_WINDOW],
                                     p[lc + span - SWA_WINDOW:] * keep_next], axis=0)
            den = jnp.sum(p, axis=0, keepdims=True) + jnp.exp2(sk - m)
            o = jnp.dot(vt, p.astype(BF16), preferred_element_type=F32)
            halves.append(o * (1.0 / den))
        both = jnp.concatenate(halves, axis=0)
        for t in range(n_pairs):
            tile = both[:, t * SWA_QBLK:(t + 1) * SWA_QBLK]
            o_ref[rows, t * LANES:(t + 1) * LANES] = tile.T.astype(BF16)


def _swa_attn(sink, q, k2, vt, k2c, vtc, nb, seq, lc):
    t = q.shape[0]
    nblk = seq // SWA_QBLK
    qb = min(SWA_STEP_BLOCKS, nblk)
    nsteps = nblk // qb
    gq = SWA_GROUP * SWA_HEAD_DIM
    hd = SWA_HEAD_DIM
    prev_blk = lambda b, i: b * nblk + jnp.maximum(qb * i - 1, 0)
    next_blk = lambda b, i: b * nblk + jnp.minimum(qb * i + qb, nblk - 1)
    return pl.pallas_call(
        functools.partial(_swa_attn_kernel, qb=qb),
        grid=(nb, SWA_KV_HEADS, nsteps),
        in_specs=[pl.BlockSpec(memory_space=pltpu.SMEM),
                  pl.BlockSpec((qb * SWA_QBLK, gq), lambda b, h, i: (b * nsteps + i, h)),
                  pl.BlockSpec((lc, 2 * LANES), lambda b, h, i: (b, h)),
                  pl.BlockSpec((SWA_QBLK, 2 * LANES), lambda b, h, i: (prev_blk(b, i), h)),
                  pl.BlockSpec((qb * SWA_QBLK, 2 * LANES), lambda b, h, i: (b * nsteps + i, h)),
                  pl.BlockSpec((SWA_QBLK, 2 * LANES), lambda b, h, i: (next_blk(b, i), h)),
                  pl.BlockSpec((hd, lc), lambda b, h, i: (h, b)),
                  pl.BlockSpec((hd, SWA_QBLK), lambda b, h, i: (h, prev_blk(b, i))),
                  pl.BlockSpec((hd, qb * SWA_QBLK), lambda b, h, i: (h, b * nsteps + i)),
                  pl.BlockSpec((hd, SWA_QBLK), lambda b, h, i: (h, next_blk(b, i)))],
        out_specs=pl.BlockSpec((qb * SWA_QBLK, gq), lambda b, h, i: (b * nsteps + i, h)),
        out_shape=jax.ShapeDtypeStruct((t, SWA_HEADS * SWA_HEAD_DIM), BF16),
        scratch_shapes=[pltpu.VMEM((2, lc + SWA_QBLK + 2 * SWA_WINDOW, gq), F32)],
        compiler_params=_params(("arbitrary", "arbitrary", "arbitrary")),
        name="swa_attn",
    )(sink, q, k2c, k2, k2, k2, vtc, vt, vt, vt)


def _rope_tables(seq):
    rows = seq // GRID_W
    row = jnp.repeat(jnp.arange(rows, dtype=F32), GRID_W)
    col = jnp.tile(jnp.arange(GRID_W, dtype=F32), rows)
    n_freq = MLA_ROPE // 4
    freqs = ROPE_BASE ** (-jnp.arange(n_freq, dtype=F32) / n_freq)
    ang = jnp.concatenate([row[:, None] * freqs, col[:, None] * freqs], axis=-1)
    cos, sin = jnp.cos(ang), jnp.sin(ang)
    return (jnp.concatenate([cos, cos, cos, cos], axis=-1),
            jnp.concatenate([-sin, -sin, sin, sin], axis=-1))


def _pair_tiles(w, n_heads, half):
    k = w.shape[0]
    x1 = w[:, :, :half].reshape(k, n_heads // 2, 2 * half)
    x2 = w[:, :, half:].reshape(k, n_heads // 2, 2 * half)
    return jnp.concatenate([x1, x2], axis=2).reshape(k, n_heads * 2 * half)


def _mla_weights(w_in, g_qa, g_kva, w_qb, w_kvb):
    half = MLA_ROPE // 2
    lat = MLA_Q_LORA + MLA_KV_LORA
    k1, k2 = w_in[:, lat:lat + half], w_in[:, lat + half:]
    qb = w_qb.reshape(MLA_Q_LORA, MLA_HEADS, MLA_NOPE + MLA_ROPE)
    kvb = w_kvb.reshape(MLA_KV_LORA, MLA_HEADS, MLA_NOPE + MLA_V)
    return {
        "w_in": jnp.concatenate([w_in[:, :lat], k1, k1, k2, k2], axis=1).astype(BF16),
        "g_qa": g_qa.reshape(1, -1), "g_kva": g_kva.reshape(1, -1),
        "w_qn": qb[:, :, :MLA_NOPE].reshape(MLA_Q_LORA, -1).astype(BF16),
        "w_qr": _pair_tiles(qb[:, :, MLA_NOPE:], MLA_HEADS, half).astype(BF16),
        "w_kn": kvb[:, :, :MLA_NOPE].reshape(MLA_KV_LORA, -1).astype(BF16),
        "w_vt": kvb[:, :, MLA_NOPE:].reshape(MLA_KV_LORA, -1).T.astype(BF16),
    }


def _swa_weights(w_qkv):
    d = w_qkv.shape[0]
    half = SWA_HEAD_DIM // 2
    dq = SWA_HEADS * SWA_HEAD_DIM
    dkv = SWA_KV_HEADS * SWA_HEAD_DIM
    q = _pair_tiles(w_qkv[:, :dq].reshape(d, SWA_HEADS, SWA_HEAD_DIM), SWA_HEADS, half)
    k = w_qkv[:, dq:dq + dkv].reshape(d, SWA_KV_HEADS, SWA_HEAD_DIM)
    k1, k2 = k[:, :, :half], k[:, :, half:]
    k = jnp.concatenate([k1, k1, k2, k2], axis=2).reshape(d, SWA_KV_HEADS * LANES)
    return jnp.concatenate([q, k], axis=1).astype(BF16), w_qkv[:, dq + dkv:].T.astype(BF16)


def kernel(x, c, ctx, c_ctx, w_mod, b_mod, g_norm, w_ff_in, w_ff_out, mla_w_in, mla_g_qa,
           mla_g_kva, mla_w_qb, mla_w_kvb, mla_w_out, swa_w_qkv, swa_sink, swa_w_out):
    nb, seq, d = x.shape
    lc = ctx.shape[1]
    depth = w_mod.shape[0]
    assert nb + 1 <= MOD_ROWS
    tm = min(TOKEN_TILE, seq)
    tq = min(ATTN_TQ, seq)
    n_per_batch = seq // tm
    grp_lat = lambda i: i // n_per_batch
    grp_ctx = lambda i: nb

    cmat = jnp.zeros((MOD_ROWS, d), F32).at[:nb].set(c).at[nb].set(c_ctx)
    mod_all = _modulation(cmat, w_mod, b_mod)
    rope_tabs = _rope_tables(seq)

    xl = x.reshape(nb * seq, d)
    xc = ctx.reshape(nb * lc, d)
    for i in range(depth):
        need_ctx = i < depth - 1
        mod = mod_all[i].reshape(MOD_ROWS, 1, 6 * d)
        g = g_norm[i].reshape(4, 1, d)
        j = i // 2
        if i % 2 == 0:
            w = _mla_weights(mla_w_in[j], mla_g_qa[j], mla_g_kva[j], mla_w_qb[j], mla_w_kvb[j])
            w_out = mla_w_out[j].astype(BF16)
            qn, qr, kn, kr, vt = _mla_proj(xl, mod, grp_lat, g[0], w, rope_tabs, tm, n_per_batch)
            qnc, qrc, knc, krc, vtc = _mla_proj(xc, mod, grp_ctx, g[0], w, None, lc, 1)
            ctx_kv = (knc.reshape(nb, lc, -1), krc.reshape(nb, lc, -1), vtc)
            lat_kv = (kn.reshape(nb, seq, -1), kr.reshape(nb, seq, -1), vt)
            o_l = _mla_attn(qn, qr, ctx_kv, lat_kv, nb, tq)
            o_c = _mla_attn(qnc, qrc, ctx_kv, None, nb, lc) if need_ctx else None
        else:
            w = _swa_weights(swa_w_qkv[j])
            w_out = swa_w_out[j].astype(BF16)
            q, k2, vt = _swa_proj(xl, mod, grp_lat, g[0], w, rope_tabs, tm, n_per_batch)
            qc, k2c, vtc = _swa_proj(xc, mod, grp_ctx, g[0], w, None, lc, 1)
            o_l = _swa_attn(swa_sink[j], q, k2, vt, k2c, vtc, nb, seq, lc)
            assert not need_ctx
            o_c = None
        w1 = w_ff_in[i].astype(BF16)
        w2 = w_ff_out[i].astype(BF16)
        xl = _outproj(o_l, w_out, xl, g[1], mod, grp_lat, tm)
        xl = _mlp(xl, g[2], g[3], mod, grp_lat, w1, w2, tm)
        if need_ctx:
            xc = _outproj(o_c, w_out, xc, g[1], mod, grp_ctx, lc)
            xc = _mlp(xc, g[2], g[3], mod, grp_ctx, w1, w2, lc)
    return xl.reshape(nb, seq, d)
```

```python
import functools
import math

import jax
import jax.numpy as jnp
from jax import lax
from jax.experimental import pallas as pl
from jax.experimental.pallas import tpu as pltpu

F32 = jnp.float32
BF16 = jnp.bfloat16

GRID_W = 64
ROPE_BASE = 10000.0
NORM_EPS = 1e-6
LOG2E = math.log2(math.e)
NEG_BIG = -1e30
REF_LIMIT = 60.0

MLA_HEADS = 16
MLA_Q_LORA = 512
MLA_KV_LORA = 512
MLA_NOPE = 128
MLA_ROPE = 64
MLA_V = 128

SWA_HEADS = 32
SWA_KV_HEADS = 4
SWA_HEAD_DIM = 64
SWA_WINDOW = 128
SWA_GROUP = SWA_HEADS // SWA_KV_HEADS
SWA_QBLK = 128
SWA_STEP_BLOCKS = 4

LANES = 128
HALF_TILE = 64

MOD_ROWS = 8
MOD_TN = 1024
TOKEN_TILE = 512
FF_TILE = 1024
ATTN_TQ = 512
KV_GROUP = 8
KV_UNROLL = 4
VMEM_LIMIT = 56 * 1024 * 1024

NT_DIMS = (((1,), (1,)), ((), ()))


def _rms(xf, g):
    ms = jnp.mean(xf * xf, axis=-1, keepdims=True)
    return xf * lax.rsqrt(ms + NORM_EPS) * g


def _rot(tile, cos, sin):
    return tile * cos + pltpu.roll(tile, HALF_TILE, 1) * sin


def _first_head_lanes(shape):
    lane = lax.broadcasted_iota(jnp.int32, shape, 1)
    return (lane & 32) == 0


def _shift_lane(parity):
    return LANES - 1 - 32 * parity


def _params(sem):
    return pltpu.CompilerParams(dimension_semantics=sem, vmem_limit_bytes=VMEM_LIMIT)


def _const_spec(shape):
    nd = len(shape)
    return pl.BlockSpec(shape, lambda *_: (0,) * nd, pipeline_mode=pl.Buffered(1))


def _mod_spec(grp, which, d):
    return pl.BlockSpec((1, 1, d), lambda i, *_: (grp(i), 0, which))


def _mod_kernel(c_ref, w_ref, b_ref, o_ref):
    c = c_ref[...]
    a = c / (1.0 + jnp.exp(-c))
    o_ref[0] = jnp.dot(a, w_ref[0], preferred_element_type=F32,
                       precision=lax.Precision.HIGHEST) + b_ref[0]


def _modulation(cmat, w_mod, b_mod):
    depth, d, n = w_mod.shape
    return pl.pallas_call(
        _mod_kernel,
        grid=(depth, n // MOD_TN),
        in_specs=[pl.BlockSpec((MOD_ROWS, d), lambda l, j: (0, 0)),
                  pl.BlockSpec((1, d, MOD_TN), lambda l, j: (l, 0, j)),
                  pl.BlockSpec((1, 1, MOD_TN), lambda l, j: (l, 0, j))],
        out_specs=pl.BlockSpec((1, MOD_ROWS, MOD_TN), lambda l, j: (l, 0, j)),
        out_shape=jax.ShapeDtypeStruct((depth, MOD_ROWS, n), F32),
        compiler_params=_params(("arbitrary", "arbitrary")),
        name="modulation",
    )(cmat, w_mod, b_mod.reshape(depth, 1, n))


def _mla_proj_kernel(*refs, rope, qscale):
    if rope:
        (x_ref, g_ref, sc_ref, sh_ref, win_ref, gqa_ref, gkva_ref, wqn_ref, wqr_ref, wkn_ref,
         wvt_ref, cos_ref, sin_ref, qn_ref, qr_ref, kn_ref, kr_ref, vt_ref) = refs
        cos, sin = cos_ref[...], sin_ref[...]
    else:
        (x_ref, g_ref, sc_ref, sh_ref, win_ref, gqa_ref, gkva_ref, wqn_ref, wqr_ref, wkn_ref,
         wvt_ref, qn_ref, qr_ref, kn_ref, kr_ref, vt_ref) = refs
    h = (_rms(x_ref[...], g_ref[...]) * (1.0 + sc_ref[0]) + sh_ref[0]).astype(BF16)
    p = jnp.dot(h, win_ref[...], preferred_element_type=F32)
    qa = _rms(p[:, :MLA_Q_LORA], gqa_ref[...]).astype(BF16)
    ckv = _rms(p[:, MLA_Q_LORA:MLA_Q_LORA + MLA_KV_LORA], gkva_ref[...]).astype(BF16)
    kr = p[:, MLA_Q_LORA + MLA_KV_LORA:]

    qn = jnp.dot(qa, wqn_ref[...], preferred_element_type=F32) * qscale
    qn_ref[...] = qn.astype(BF16)
    qr = jnp.dot(qa, wqr_ref[...], preferred_element_type=F32) * qscale
    for t in range(qr.shape[1] // LANES):
        tile = qr[:, t * LANES:(t + 1) * LANES]
        if rope:
            tile = _rot(tile, cos, sin)
        qr_ref[:, t * LANES:(t + 1) * LANES] = tile.astype(BF16)

    if rope:
        kr = _rot(kr, cos, sin)
    first = _first_head_lanes(kr.shape)
    lane = lax.broadcasted_iota(jnp.int32, kr.shape, 1)
    kr_ref[:, :LANES] = jnp.where(lane == _shift_lane(0), 1.0,
                                  jnp.where(first, kr, 0.0)).astype(BF16)
    kr_ref[:, LANES:] = jnp.where(lane == _shift_lane(1), 1.0,
                                  jnp.where(first, 0.0, kr)).astype(BF16)

    kn_ref[...] = jnp.dot(ckv, wkn_ref[...], preferred_element_type=F32).astype(BF16)
    vt = lax.dot_general(wvt_ref[...], ckv, NT_DIMS, preferred_element_type=F32).astype(BF16)
    for hd in range(MLA_HEADS):
        vt_ref[0, hd, 0] = vt[hd * MLA_V:(hd + 1) * MLA_V, :]


def _mla_proj(x, mod, grp, g0, w, rope_tabs, tm, n_per_batch):
    t, d = x.shape
    n_tiles = t // tm
    nb = n_tiles // n_per_batch
    rope = rope_tabs is not None
    qscale = (MLA_NOPE + MLA_ROPE) ** -0.5 * LOG2E
    row = lambda i: (i, 0)
    in_specs = [pl.BlockSpec((tm, d), row), _const_spec((1, d)),
                _mod_spec(grp, 1, d), _mod_spec(grp, 0, d),
                _const_spec(w["w_in"].shape), _const_spec((1, MLA_Q_LORA)),
                _const_spec((1, MLA_KV_LORA)), _const_spec(w["w_qn"].shape),
                _const_spec(w["w_qr"].shape), _const_spec(w["w_kn"].shape),
                _const_spec(w["w_vt"].shape)]
    args = [x, g0, mod, mod, w["w_in"], w["g_qa"], w["g_kva"], w["w_qn"], w["w_qr"], w["w_kn"],
            w["w_vt"]]
    if rope:
        pos = lambda i: (i % n_per_batch, 0)
        in_specs += [pl.BlockSpec((tm, LANES), pos), pl.BlockSpec((tm, LANES), pos)]
        args += list(rope_tabs)
    hn = MLA_HEADS * MLA_NOPE
    hr = MLA_HEADS * MLA_ROPE
    out_shape = [jax.ShapeDtypeStruct((t, hn), BF16), jax.ShapeDtypeStruct((t, hr), BF16),
                 jax.ShapeDtypeStruct((t, hn), BF16), jax.ShapeDtypeStruct((t, 2 * LANES), BF16),
                 jax.ShapeDtypeStruct((nb, MLA_HEADS, n_per_batch, MLA_V, tm), BF16)]
    out_specs = [pl.BlockSpec((tm, hn), row), pl.BlockSpec((tm, hr), row),
                 pl.BlockSpec((tm, hn), row), pl.BlockSpec((tm, 2 * LANES), row),
                 pl.BlockSpec((1, MLA_HEADS, 1, MLA_V, tm),
                              lambda i: (i // n_per_batch, 0, i % n_per_batch, 0, 0))]
    return pl.pallas_call(
        functools.partial(_mla_proj_kernel, rope=rope, qscale=qscale),
        grid=(n_tiles,), in_specs=in_specs, out_specs=out_specs, out_shape=out_shape,
        compiler_params=_params(("arbitrary",)),
        name="mla_proj_lat" if rope else "mla_proj_ctx",
    )(*args)


def _mla_attn_kernel(*refs, n_lat, tk):
    if n_lat:
        (qn_ref, qr_ref, knc_ref, krc_ref, vtc_ref, knl_ref, krl_ref, vtl_ref, o_ref,
         acc_ref, s_ref, qa_ref, k2_ref) = refs
    else:
        qn_ref, qr_ref, knc_ref, krc_ref, vtc_ref, o_ref, acc_ref, s_ref, qa_ref, k2_ref = refs
    tq = qn_ref.shape[0]
    lc = knc_ref.shape[1]
    parity = pl.program_id(1) % 2

    def lat_keys(j):
        off = pl.multiple_of(j * tk, tk)
        return knl_ref[0, pl.ds(off, tk), :], krl_ref[0, pl.ds(off, tk), :]

    @pl.when(pl.program_id(2) == 0)
    def _():
        def sqnorm_max(kn, kr):
            a, b = kn.astype(F32), kr.astype(F32)
            r = jnp.sum(a * a + b * b, axis=1, keepdims=True)
            return jnp.max(r, axis=0, keepdims=True)
        mx = sqnorm_max(knc_ref[0], krc_ref[0])
        if n_lat:
            mx = lax.fori_loop(0, n_lat, lambda j, c: jnp.maximum(c, sqnorm_max(*lat_keys(j))), mx)
        k2_ref[...] = jnp.broadcast_to(mx, k2_ref.shape)

    qn = qn_ref[...]
    qr = qr_ref[...]
    lane = lax.broadcasted_iota(jnp.int32, qr.shape, 1)
    own = ((lane >> 5) & 1) == parity
    qnf, qrf = qn.astype(F32), jnp.where(own, qr.astype(F32), 0.0)
    q2 = jnp.sum(qnf * qnf + qrf * qrf, axis=1, keepdims=True)
    ref = jnp.sqrt(q2 * k2_ref[0:1, 0:1])
    fast = jnp.max(ref) <= REF_LIMIT
    shift = jnp.where(fast, -ref, 0.0).astype(BF16)
    qr = jnp.where(lane == LANES - 1 - 32 * parity, shift, qr)
    qa_ref[:LANES, :] = qn.astype(F32).T.astype(BF16)
    qa_ref[LANES:, :] = qr.astype(F32).T.astype(BF16)

    def scores(kn, kr):
        return jnp.dot(jnp.concatenate([kn, kr], axis=1), qa_ref[...],
                       preferred_element_type=F32)

    def finish(l):
        o_ref[...] = (acc_ref[...] * (1.0 / l)).T.astype(BF16)

    def shifted_softmax():
        def chunk(s, vt, l8):
            p = jnp.exp2(s)
            l8 = l8 + jnp.sum(p.reshape(-1, 8, tq), axis=0)
            return l8, jnp.dot(vt, p.astype(BF16), preferred_element_type=F32)

        gk = KV_GROUP * tk
        n_groups = n_lat // KV_GROUP

        def group_scores(g):
            off = pl.multiple_of(g * gk, gk)
            return scores(knl_ref[0, pl.ds(off, gk), :], krl_ref[0, pl.ds(off, gk), :])

        def consume(slot, g, l8):
            vt = jnp.concatenate([vtl_ref[0, 0, KV_GROUP * g + u] for u in range(KV_GROUP)], axis=1)
            l8, pv = chunk(s_ref[slot], vt, l8)
            acc_ref[...] += pv
            return l8

        def pair(g, l8, last):
            s_ref[1] = group_scores(g + 1)
            l8 = consume(0, g, l8)
            if not last:
                s_ref[0] = group_scores(g + 2)
            return consume(1, g + 1, l8)

        if n_lat:
            s_ref[0] = group_scores(0)
        l8, pv = chunk(scores(knc_ref[0], krc_ref[0]), vtc_ref[0, 0, 0], jnp.zeros((8, tq), F32))
        acc_ref[...] = pv
        if n_lat:
            l8 = lax.fori_loop(0, n_groups // 2 - 1, lambda jj, c: pair(2 * jj, c, False), l8)
            l8 = pair(n_groups - 2, l8, True)
        finish(jnp.sum(l8, axis=0, keepdims=True))

    def online_softmax():
        def update(s_view, vt, m, l):
            m_new = jnp.maximum(m, jnp.max(s_view[...], axis=0, keepdims=True))
            alpha = jnp.exp2(m - m_new)
            p = jnp.exp2(s_view[...] - m_new)
            l_new = alpha * l + jnp.sum(p, axis=0, keepdims=True)
            acc_ref[...] = acc_ref[...] * alpha + jnp.dot(vt, p.astype(BF16),
                                                          preferred_element_type=F32)
            return m_new, l_new

        acc_ref[...] = jnp.zeros_like(acc_ref)
        m = jnp.full((1, tq), NEG_BIG, F32)
        l = jnp.zeros((1, tq), F32)
        slots = [s_ref.at[0, pl.ds(0, tk)], s_ref.at[1, pl.ds(0, tk)]] if n_lat else None
        if n_lat:
            slots[0][...] = scores(*lat_keys(0))
        ctx_view = s_ref.at[1, pl.ds(0, lc)]
        ctx_view[...] = scores(knc_ref[0], krc_ref[0])
        m, l = update(ctx_view, vtc_ref[0, 0, 0], m, l)
        if n_lat:
            def body(jj, carry):
                for u in range(KV_UNROLL):
                    j = KV_UNROLL * jj + u
                    slots[(u + 1) % 2][...] = scores(*lat_keys(jnp.minimum(j + 1, n_lat - 1)))
                    carry = update(slots[u % 2], vtl_ref[0, 0, j], *carry)
                return carry
            m, l = lax.fori_loop(0, n_lat // KV_UNROLL, body, (m, l))
        finish(l)

    lax.cond(fast, shifted_softmax, online_softmax)


def _mla_attn(qn, qr, ctx_kv, lat_kv, nb, tq):
    t = qn.shape[0]
    nq = t // nb // tq
    knc, krc, vtc = ctx_kv
    lc = knc.shape[1]
    qrow = lambda b, h, i: (b * nq + i, h)
    in_specs = [pl.BlockSpec((tq, LANES), qrow),
                pl.BlockSpec((tq, LANES), lambda b, h, i: (b * nq + i, h // 2)),
                pl.BlockSpec((1, lc, LANES), lambda b, h, i: (b, 0, h)),
                pl.BlockSpec((1, lc, LANES), lambda b, h, i: (b, 0, h % 2)),
                pl.BlockSpec((1, 1, 1, MLA_V, lc), lambda b, h, i: (b, h, 0, 0, 0))]
    args = [qn, qr, knc, krc, vtc]
    n_lat, tk = 0, 0
    if lat_kv is not None:
        knl, krl, vtl = lat_kv
        s = knl.shape[1]
        n_lat, tk = vtl.shape[2], vtl.shape[4]
        assert n_lat % KV_UNROLL == 0 and n_lat % (2 * KV_GROUP) == 0 and tk >= lc
        in_specs += [pl.BlockSpec((1, s, LANES), lambda b, h, i: (b, 0, h)),
                     pl.BlockSpec((1, s, LANES), lambda b, h, i: (b, 0, h % 2)),
                     pl.BlockSpec((1, 1, n_lat, MLA_V, tk), lambda b, h, i: (b, h, 0, 0, 0))]
        args += [knl, krl, vtl]
    scratch = [pltpu.VMEM((MLA_V, tq), F32), pltpu.VMEM((2, max(KV_GROUP * tk, lc), tq), F32),
               pltpu.VMEM((2 * LANES, tq), BF16), pltpu.VMEM((8, LANES), F32)]
    return pl.pallas_call(
        functools.partial(_mla_attn_kernel, n_lat=n_lat, tk=tk),
        grid=(nb, MLA_HEADS, nq), in_specs=in_specs,
        out_specs=pl.BlockSpec((tq, MLA_V), qrow),
        out_shape=jax.ShapeDtypeStruct((t, MLA_HEADS * MLA_V), BF16),
        scratch_shapes=scratch,
        compiler_params=_params(("arbitrary", "arbitrary", "arbitrary")),
        name="mla_attn_lat" if n_lat else "mla_attn_ctx",
    )(*args)


def _outproj_kernel(o_ref, w_ref, x_ref, g_ref, gt_ref, out_ref):
    y = jnp.dot(o_ref[...], w_ref[...], preferred_element_type=F32)
    out_ref[...] = x_ref[...] + gt_ref[0] * _rms(y, g_ref[...])


def _outproj(o, w_out, x, g1, mod, grp, tm):
    t, d = x.shape
    row = lambda i: (i, 0)
    return pl.pallas_call(
        _outproj_kernel,
        grid=(t // tm,),
        in_specs=[pl.BlockSpec((tm, o.shape[1]), row), _const_spec(w_out.shape),
                  pl.BlockSpec((tm, d), row), _const_spec((1, d)), _mod_spec(grp, 2, d)],
        out_specs=pl.BlockSpec((tm, d), row),
        out_shape=jax.ShapeDtypeStruct((t, d), F32),
        compiler_params=_params(("arbitrary",)),
        name="outproj",
    )(o, w_out, x, g1, mod)


def _mlp_kernel(x_ref, g2_ref, sc_ref, sh_ref, w1_ref, w2_ref, g3_ref, gt_ref, out_ref,
                f_ref, acc_ref):
    k = pl.program_id(1)

    @pl.when(k == 0)
    def _():
        f = _rms(x_ref[...], g2_ref[...]) * (1.0 + sc_ref[0]) + sh_ref[0]
        f_ref[...] = f.astype(BF16)
        acc_ref[...] = jnp.zeros_like(acc_ref)

    u = jnp.maximum(jnp.dot(f_ref[...], w1_ref[...], preferred_element_type=F32), 0.0)
    acc_ref[...] += jnp.dot((u * u).astype(BF16), w2_ref[...], preferred_element_type=F32)

    @pl.when(k == pl.num_programs(1) - 1)
    def _():
        out_ref[...] = x_ref[...] + gt_ref[0] * _rms(acc_ref[...], g3_ref[...])


def _mlp(x, g2, g3, mod, grp, w1, w2, tm):
    t, d = x.shape
    dff = w1.shape[1]
    row = lambda i, k: (i, 0)
    return pl.pallas_call(
        _mlp_kernel,
        grid=(t // tm, dff // FF_TILE),
        in_specs=[pl.BlockSpec((tm, d), row), _const_spec((1, d)),
                  _mod_spec(grp, 4, d), _mod_spec(grp, 3, d),
                  pl.BlockSpec((d, FF_TILE), lambda i, k: (0, k)),
                  pl.BlockSpec((FF_TILE, d), lambda i, k: (k, 0)),
                  _const_spec((1, d)), _mod_spec(grp, 5, d)],
        out_specs=pl.BlockSpec((tm, d), row),
        out_shape=jax.ShapeDtypeStruct((t, d), F32),
        scratch_shapes=[pltpu.VMEM((tm, d), BF16), pltpu.VMEM((tm, d), F32)],
        compiler_params=_params(("arbitrary", "arbitrary")),
        name="mlp",
    )(x, g2, mod, mod, w1, w2, g3, mod)


def _swa_proj_kernel(*refs, rope, qscale):
    if rope:
        x_ref, g_ref, sc_ref, sh_ref, w_ref, wvt_ref, cos_ref, sin_ref, q_ref, k2_ref, vt_ref = refs
        cos, sin = cos_ref[...], sin_ref[...]
    else:
        x_ref, g_ref, sc_ref, sh_ref, w_ref, wvt_ref, q_ref, k2_ref, vt_ref = refs
    h = (_rms(x_ref[...], g_ref[...]) * (1.0 + sc_ref[0]) + sh_ref[0]).astype(BF16)
    p = jnp.dot(h, w_ref[...], preferred_element_type=F32)
    dq = q_ref.shape[1]
    for t in range(dq // LANES):
        tile = p[:, t * LANES:(t + 1) * LANES] * qscale
        if rope:
            tile = _rot(tile, cos, sin)
        q_ref[:, t * LANES:(t + 1) * LANES] = tile.astype(BF16)
    for c in range(SWA_KV_HEADS):
        tile = p[:, dq + c * LANES:dq + (c + 1) * LANES]
        if rope:
            tile = _rot(tile, cos, sin)
        first = _first_head_lanes(tile.shape)
        k2_ref[:, 2 * c * LANES:(2 * c + 1) * LANES] = jnp.where(first, tile, 0.0).astype(BF16)
        k2_ref[:, (2 * c + 1) * LANES:(2 * c + 2) * LANES] = jnp.where(first, 0.0, tile).astype(BF16)
    vt_ref[...] = lax.dot_general(wvt_ref[...], h, NT_DIMS,
                                  preferred_element_type=F32).astype(BF16)


def _swa_proj(x, mod, grp, g0, w, rope_tabs, tm, n_per_batch):
    t, d = x.shape
    rope = rope_tabs is not None
    row = lambda i: (i, 0)
    dq = SWA_HEADS * SWA_HEAD_DIM
    dkv = SWA_KV_HEADS * SWA_HEAD_DIM
    w_qk, w_vt = w
    in_specs = [pl.BlockSpec((tm, d), row), _const_spec((1, d)),
                _mod_spec(grp, 1, d), _mod_spec(grp, 0, d), _const_spec(w_qk.shape),
                _const_spec(w_vt.shape)]
    args = [x, g0, mod, mod, w_qk, w_vt]
    if rope:
        pos = lambda i: (i % n_per_batch, 0)
        in_specs += [pl.BlockSpec((tm, LANES), pos), pl.BlockSpec((tm, LANES), pos)]
        args += list(rope_tabs)
    out_shape = [jax.ShapeDtypeStruct((t, dq), BF16),
                 jax.ShapeDtypeStruct((t, 2 * SWA_KV_HEADS * LANES), BF16),
                 jax.ShapeDtypeStruct((dkv, t), BF16)]
    out_specs = [pl.BlockSpec((tm, dq), row), pl.BlockSpec((tm, 2 * SWA_KV_HEADS * LANES), row),
                 pl.BlockSpec((dkv, tm), lambda i: (0, i))]
    return pl.pallas_call(
        functools.partial(_swa_proj_kernel, rope=rope, qscale=SWA_HEAD_DIM ** -0.5 * LOG2E),
        grid=(t // tm,), in_specs=in_specs, out_specs=out_specs, out_shape=out_shape,
        compiler_params=_params(("arbitrary",)),
        name="swa_proj_lat" if rope else "swa_proj_ctx",
    )(*args)


def _swa_attn_kernel(sink_ref, q_ref, kc_ref, kp_ref, kcur_ref, kn_ref, vc_ref, vp_ref, vcur_ref,
                     vn_ref, o_ref, s_ref, *, qb):
    kvh = pl.program_id(1)
    i = pl.program_id(2)
    lc = kc_ref.shape[0]
    span = SWA_QBLK + 2 * SWA_WINDOW
    n_pairs = SWA_GROUP // 2
    kc, vc = kc_ref[...], vc_ref[...]
    kwin = jnp.concatenate([kp_ref[...], kcur_ref[...], kn_ref[...]], axis=0)
    vwin = jnp.concatenate([vp_ref[...], vcur_ref[...], vn_ref[...]], axis=1)
    r = lax.broadcasted_iota(jnp.int32, (lc + span, SWA_QBLK), 0)
    rel = r - lc - lax.broadcasted_iota(jnp.int32, (lc + span, SWA_QBLK), 1)
    bias = jnp.where((r < lc) | ((rel >= 0) & (rel <= 2 * SWA_WINDOW)), 0.0, NEG_BIG)
    bias = jnp.concatenate([bias] * n_pairs, axis=1)
    pair = lax.broadcasted_iota(jnp.int32, (1, n_pairs * SWA_QBLK), 1) // SWA_QBLK
    keep_prev = jnp.where(i == 0, 0.0, 1.0)
    keep_next = jnp.where(i == pl.num_programs(2) - 1, 0.0, 1.0)
    def scores(blk, e):
        qs = jnp.concatenate([q_ref[blk * SWA_QBLK:(blk + 1) * SWA_QBLK, t * LANES:(t + 1) * LANES]
                              for t in range(n_pairs)], axis=0)
        kcat = jnp.concatenate([kc[:, e * LANES:(e + 1) * LANES],
                                kwin[blk * SWA_QBLK:blk * SWA_QBLK + span,
                                     e * LANES:(e + 1) * LANES]], axis=0)
        return lax.dot_general(kcat, qs, NT_DIMS, preferred_element_type=F32)

    s_ref[0] = scores(0, 0)
    for blk in range(qb):
        rows = slice(blk * SWA_QBLK, (blk + 1) * SWA_QBLK)
        vt = jnp.concatenate([vc, vwin[:, blk * SWA_QBLK:blk * SWA_QBLK + span]], axis=1)
        halves = []
        for e in range(2):
            if e == 0:
                s_ref[1] = scores(blk, 1)
            elif blk + 1 < qb:
                s_ref[0] = scores(blk + 1, 0)
            s = s_ref[e] + bias
            sk = jnp.zeros(pair.shape, F32)
            for t in range(n_pairs):
                sk = jnp.where(pair == t, sink_ref[kvh * SWA_GROUP + 2 * t + e] * LOG2E, sk)
            m = jnp.maximum(jnp.max(s, axis=0, keepdims=True), sk)
            p = jnp.exp2(s - m)
            if blk == 0:
                p = jnp.concatenate([p[:lc], p[lc:lc + SWA_WINDOW] * keep_prev,
                                     p[lc + SWA_WINDOW:]], axis=0)
            if blk == qb - 1:
                p = jnp.concatenate([p[:lc + span - SWA_WINDOW],
                                     p[lc + span - SWA_WINDOW:] * keep_next], axis=0)
            den = jnp.sum(p, axis=0, keepdims=True) + jnp.exp2(sk - m)
            o = jnp.dot(vt, p.astype(BF16), preferred_element_type=F32)
            halves.append(o * (1.0 / den))
        both = jnp.concatenate(halves, axis=0)
        for t in range(n_pairs):
            tile = both[:, t * SWA_QBLK:(t + 1) * SWA_QBLK]
            o_ref[rows, t * LANES:(t + 1) * LANES] = tile.T.astype(BF16)


def _swa_attn(sink, q, k2, vt, k2c, vtc, nb, seq, lc):
    t = q.shape[0]
    nblk = seq // SWA_QBLK
    qb = min(SWA_STEP_BLOCKS, nblk)
    nsteps = nblk // qb
    gq = SWA_GROUP * SWA_HEAD_DIM
    hd = SWA_HEAD_DIM
    prev_blk = lambda b, i: b * nblk + jnp.maximum(qb * i - 1, 0)
    next_blk = lambda b, i: b * nblk + jnp.minimum(qb * i + qb, nblk - 1)
    return pl.pallas_call(
        functools.partial(_swa_attn_kernel, qb=qb),
        grid=(nb, SWA_KV_HEADS, nsteps),
        in_specs=[pl.BlockSpec(memory_space=pltpu.SMEM),
                  pl.BlockSpec((qb * SWA_QBLK, gq), lambda b, h, i: (b * nsteps + i, h)),
                  pl.BlockSpec((lc, 2 * LANES), lambda b, h, i: (b, h)),
                  pl.BlockSpec((SWA_QBLK, 2 * LANES), lambda b, h, i: (prev_blk(b, i), h)),
                  pl.BlockSpec((qb * SWA_QBLK, 2 * LANES), lambda b, h, i: (b * nsteps + i, h)),
                  pl.BlockSpec((SWA_QBLK, 2 * LANES), lambda b, h, i: (next_blk(b, i), h)),
                  pl.BlockSpec((hd, lc), lambda b, h, i: (h, b)),
                  pl.BlockSpec((hd, SWA_QBLK), lambda b, h, i: (h, prev_blk(b, i))),
                  pl.BlockSpec((hd, qb * SWA_QBLK), lambda b, h, i: (h, b * nsteps + i)),
                  pl.BlockSpec((hd, SWA_QBLK), lambda b, h, i: (h, next_blk(b, i)))],
        out_specs=pl.BlockSpec((qb * SWA_QBLK, gq), lambda b, h, i: (b * nsteps + i, h)),
        out_shape=jax.ShapeDtypeStruct((t, SWA_HEADS * SWA_HEAD_DIM), BF16),
        scratch_shapes=[pltpu.VMEM((2, lc + SWA_QBLK + 2 * SWA_WINDOW, gq), F32)],
        compiler_params=_params(("arbitrary", "arbitrary", "arbitrary")),
        name="swa_attn",
    )(sink, q, k2c, k2, k2, k2, vtc, vt, vt, vt)


def _rope_tables(seq):
    rows = seq // GRID_W
    row = jnp.repeat(jnp.arange(rows, dtype=F32), GRID_W)
    col = jnp.tile(jnp.arange(GRID_W, dtype=F32), rows)
    n_freq = MLA_ROPE // 4
    freqs = ROPE_BASE ** (-jnp.arange(n_freq, dtype=F32) / n_freq)
    ang = jnp.concatenate([row[:, None] * freqs, col[:, None] * freqs], axis=-1)
    cos, sin = jnp.cos(ang), jnp.sin(ang)
    return (jnp.concatenate([cos, cos, cos, cos], axis=-1),
            jnp.concatenate([-sin, -sin, sin, sin], axis=-1))


def _pair_tiles(w, n_heads, half):
    k = w.shape[0]
    x1 = w[:, :, :half].reshape(k, n_heads // 2, 2 * half)
    x2 = w[:, :, half:].reshape(k, n_heads // 2, 2 * half)
    return jnp.concatenate([x1, x2], axis=2).reshape(k, n_heads * 2 * half)


def _mla_weights(w_in, g_qa, g_kva, w_qb, w_kvb):
    half = MLA_ROPE // 2
    lat = MLA_Q_LORA + MLA_KV_LORA
    k1, k2 = w_in[:, lat:lat + half], w_in[:, lat + half:]
    qb = w_qb.reshape(MLA_Q_LORA, MLA_HEADS, MLA_NOPE + MLA_ROPE)
    kvb = w_kvb.reshape(MLA_KV_LORA, MLA_HEADS, MLA_NOPE + MLA_V)
    return {
        "w_in": jnp.concatenate([w_in[:, :lat], k1, k1, k2, k2], axis=1).astype(BF16),
        "g_qa": g_qa.reshape(1, -1), "g_kva": g_kva.reshape(1, -1),
        "w_qn": qb[:, :, :MLA_NOPE].reshape(MLA_Q_LORA, -1).astype(BF16),
        "w_qr": _pair_tiles(qb[:, :, MLA_NOPE:], MLA_HEADS, half).astype(BF16),
        "w_kn": kvb[:, :, :MLA_NOPE].reshape(MLA_KV_LORA, -1).astype(BF16),
        "w_vt": kvb[:, :, MLA_NOPE:].reshape(MLA_KV_LORA, -1).T.astype(BF16),
    }


def _swa_weights(w_qkv):
    d = w_qkv.shape[0]
    half = SWA_HEAD_DIM // 2
    dq = SWA_HEADS * SWA_HEAD_DIM
    dkv = SWA_KV_HEADS * SWA_HEAD_DIM
    q = _pair_tiles(w_qkv[:, :dq].reshape(d, SWA_HEADS, SWA_HEAD_DIM), SWA_HEADS, half)
    k = w_qkv[:, dq:dq + dkv].reshape(d, SWA_KV_HEADS, SWA_HEAD_DIM)
    k1, k2 = k[:, :, :half], k[:, :, half:]
    k = jnp.concatenate([k1, k1, k2, k2], axis=2).reshape(d, SWA_KV_HEADS * LANES)
    return jnp.concatenate([q, k], axis=1).astype(BF16), w_qkv[:, dq + dkv:].T.astype(BF16)


def kernel(x, c, ctx, c_ctx, w_mod, b_mod, g_norm, w_ff_in, w_ff_out, mla_w_in, mla_g_qa,
           mla_g_kva, mla_w_qb, mla_w_kvb, mla_w_out, swa_w_qkv, swa_sink, swa_w_out):
    nb, seq, d = x.shape
    lc = ctx.shape[1]
    depth = w_mod.shape[0]
    assert nb + 1 <= MOD_ROWS
    tm = min(TOKEN_TILE, seq)
    tq = min(ATTN_TQ, seq)
    n_per_batch = seq // tm
    grp_lat = lambda i: i // n_per_batch
    grp_ctx = lambda i: nb

    cmat = jnp.zeros((MOD_ROWS, d), F32).at[:nb].set(c).at[nb].set(c_ctx)
    mod_all = _modulation(cmat, w_mod, b_mod)
    rope_tabs = _rope_tables(seq)

    xl = x.reshape(nb * seq, d)
    xc = ctx.reshape(nb * lc, d)
    for i in range(depth):
        need_ctx = i < depth - 1
        mod = mod_all[i].reshape(MOD_ROWS, 1, 6 * d)
        g = g_norm[i].reshape(4, 1, d)
        j = i // 2
        if i % 2 == 0:
            w = _mla_weights(mla_w_in[j], mla_g_qa[j], mla_g_kva[j], mla_w_qb[j], mla_w_kvb[j])
            w_out = mla_w_out[j].astype(BF16)
            qn, qr, kn, kr, vt = _mla_proj(xl, mod, grp_lat, g[0], w, rope_tabs, tm, n_per_batch)
            qnc, qrc, knc, krc, vtc = _mla_proj(xc, mod, grp_ctx, g[0], w, None, lc, 1)
            ctx_kv = (knc.reshape(nb, lc, -1), krc.reshape(nb, lc, -1), vtc)
            lat_kv = (kn.reshape(nb, seq, -1), kr.reshape(nb, seq, -1), vt)
            o_l = _mla_attn(qn, qr, ctx_kv, lat_kv, nb, tq)
            o_c = _mla_attn(qnc, qrc, ctx_kv, None, nb, lc) if need_ctx else None
        else:
            w = _swa_weights(swa_w_qkv[j])
            w_out = swa_w_out[j].astype(BF16)
            q, k2, vt = _swa_proj(xl, mod, grp_lat, g[0], w, rope_tabs, tm, n_per_batch)
            qc, k2c, vtc = _swa_proj(xc, mod, grp_ctx, g[0], w, None, lc, 1)
            o_l = _swa_attn(swa_sink[j], q, k2, vt, k2c, vtc, nb, seq, lc)
            assert not need_ctx
            o_c = None
        w1 = w_ff_in[i].astype(BF16)
        w2 = w_ff_out[i].astype(BF16)
        xl = _outproj(o_l, w_out, xl, g[1], mod, grp_lat, tm)
        xl = _mlp(xl, g[2], g[3], mod, grp_lat, w1, w2, tm)
        if need_ctx:
            xc = _outproj(o_c, w_out, xc, g[1], mod, grp_ctx, lc)
            xc = _mlp(xc, g[2], g[3], mod, grp_ctx, w1, w2, lc)
    return xl.reshape(nb, seq, d)
```

```python
import functools
import math

import jax
import jax.numpy as jnp
from jax import lax
from jax.experimental import pallas as pl
from jax.experimental.pallas import tpu as pltpu

F32 = jnp.float32
BF16 = jnp.bfloat16

GRID_W = 64
ROPE_BASE = 10000.0
NORM_EPS = 1e-6
LOG2E = math.log2(math.e)
NEG_BIG = -1e30
REF_LIMIT = 60.0

MLA_HEADS = 16
MLA_Q_LORA = 512
MLA_KV_LORA = 512
MLA_NOPE = 128
MLA_ROPE = 64
MLA_V = 128

SWA_HEADS = 32
SWA_KV_HEADS = 4
SWA_HEAD_DIM = 64
SWA_WINDOW = 128
SWA_GROUP = SWA_HEADS // SWA_KV_HEADS
SWA_QBLK = 128
SWA_STEP_BLOCKS = 4

LANES = 128
HALF_TILE = 64

MOD_ROWS = 8
MOD_TN = 1024
TOKEN_TILE = 512
ROW_CHUNK = 32
ROW_UNROLL = 4
FF_TILE = 1024
ATTN_TQ = 512
KV_GROUP = 8
KV_UNROLL = 4
VMEM_LIMIT = 56 * 1024 * 1024

NT_DIMS = (((1,), (1,)), ((), ()))


def _rms(xf, g):
    ms = jnp.mean(xf * xf, axis=-1, keepdims=True)
    return xf * lax.rsqrt(ms + NORM_EPS) * g


def _inv_rms(xf):
    return lax.rsqrt(jnp.mean(xf * xf, axis=-1, keepdims=True) + NORM_EPS)


def _row_chunks(n_rows, fn):
    def body(i, carry):
        fn(pl.ds(pl.multiple_of(i * ROW_CHUNK, ROW_CHUNK), ROW_CHUNK))
        return carry
    lax.fori_loop(0, n_rows // ROW_CHUNK, body, 0, unroll=ROW_UNROLL)


def _modulated_norm(x_ref, g_ref, sc_ref, sh_ref, h_ref):
    gain = g_ref[...] * (1.0 + sc_ref[0])
    shift = sh_ref[0]

    def rows(rs):
        xf = x_ref[rs, :]
        h_ref[rs, :] = (xf * _inv_rms(xf) * gain + shift).astype(h_ref.dtype)
    _row_chunks(h_ref.shape[0], rows)


def _rot(tile, cos, sin):
    return tile * cos + pltpu.roll(tile, HALF_TILE, 1) * sin


def _first_head_lanes(shape):
    lane = lax.broadcasted_iota(jnp.int32, shape, 1)
    return (lane & 32) == 0


def _shift_lane(parity):
    return LANES - 1 - 32 * parity


def _params(sem):
    return pltpu.CompilerParams(dimension_semantics=sem, vmem_limit_bytes=VMEM_LIMIT)


def _const_spec(shape):
    nd = len(shape)
    return pl.BlockSpec(shape, lambda *_: (0,) * nd, pipeline_mode=pl.Buffered(1))


def _mod_spec(grp, which, d):
    return pl.BlockSpec((1, 1, d), lambda i, *_: (grp(i), 0, which))


def _mod_kernel(c_ref, w_ref, b_ref, o_ref):
    c = c_ref[...]
    a = c / (1.0 + jnp.exp(-c))
    o_ref[0] = jnp.dot(a, w_ref[0], preferred_element_type=F32,
                       precision=lax.Precision.HIGHEST) + b_ref[0]


def _modulation(cmat, w_mod, b_mod):
    depth, d, n = w_mod.shape
    return pl.pallas_call(
        _mod_kernel,
        grid=(depth, n // MOD_TN),
        in_specs=[pl.BlockSpec((MOD_ROWS, d), lambda l, j: (0, 0)),
                  pl.BlockSpec((1, d, MOD_TN), lambda l, j: (l, 0, j)),
                  pl.BlockSpec((1, 1, MOD_TN), lambda l, j: (l, 0, j))],
        out_specs=pl.BlockSpec((1, MOD_ROWS, MOD_TN), lambda l, j: (l, 0, j)),
        out_shape=jax.ShapeDtypeStruct((depth, MOD_ROWS, n), F32),
        compiler_params=_params(("arbitrary", "arbitrary")),
        name="modulation",
    )(cmat, w_mod, b_mod.reshape(depth, 1, n))


def _mla_proj_kernel(*refs, rope, qscale):
    if rope:
        (x_ref, g_ref, sc_ref, sh_ref, win_ref, gqa_ref, gkva_ref, wqn_ref, wqr_ref, wkn_ref,
         wvt_ref, cos_ref, sin_ref, cost_ref, sint_ref, qn_ref, qr_ref, kn_ref, kr_ref, vt_ref,
         h_ref) = refs
        cos, sin = cos_ref[...], sin_ref[...]
    else:
        (x_ref, g_ref, sc_ref, sh_ref, win_ref, gqa_ref, gkva_ref, wqn_ref, wqr_ref, wkn_ref,
         wvt_ref, qn_ref, qr_ref, kn_ref, kr_ref, vt_ref, h_ref) = refs
    _modulated_norm(x_ref, g_ref, sc_ref, sh_ref, h_ref)
    p = jnp.dot(h_ref[...], win_ref[...], preferred_element_type=F32)
    qa = _rms(p[:, :MLA_Q_LORA], gqa_ref[...]).astype(BF16)
    ckv = _rms(p[:, MLA_Q_LORA:MLA_Q_LORA + MLA_KV_LORA], gkva_ref[...]).astype(BF16)
    kr = p[:, MLA_Q_LORA + MLA_KV_LORA:]

    qn = lax.dot_general(wqn_ref[...], qa, NT_DIMS, preferred_element_type=F32) * qscale
    qn_ref[...] = qn.astype(BF16)
    qr = lax.dot_general(wqr_ref[...], qa, NT_DIMS, preferred_element_type=F32) * qscale
    for t in range(qr.shape[0] // LANES):
        tile = qr[t * LANES:(t + 1) * LANES, :]
        if rope:
            swapped = jnp.concatenate([tile[HALF_TILE:], tile[:HALF_TILE]], axis=0)
            tile = tile * cost_ref[...] + swapped * sint_ref[...]
        qr_ref[t * LANES:(t + 1) * LANES, :] = tile.astype(BF16)

    if rope:
        kr = _rot(kr, cos, sin)
    first = _first_head_lanes(kr.shape)
    lane = lax.broadcasted_iota(jnp.int32, kr.shape, 1)
    kr_ref[:, :LANES] = jnp.where(lane == _shift_lane(0), 1.0,
                                  jnp.where(first, kr, 0.0)).astype(BF16)
    kr_ref[:, LANES:] = jnp.where(lane == _shift_lane(1), 1.0,
                                  jnp.where(first, 0.0, kr)).astype(BF16)

    kn_ref[...] = jnp.dot(ckv, wkn_ref[...], preferred_element_type=F32).astype(BF16)
    vt = lax.dot_general(wvt_ref[...], ckv, NT_DIMS, preferred_element_type=F32).astype(BF16)
    for hd in range(MLA_HEADS):
        vt_ref[0, hd, 0] = vt[hd * MLA_V:(hd + 1) * MLA_V, :]


def _mla_proj(x, mod, grp, g0, w, rope_tabs, tm, n_per_batch):
    t, d = x.shape
    n_tiles = t // tm
    nb = n_tiles // n_per_batch
    rope = rope_tabs is not None
    qscale = (MLA_NOPE + MLA_ROPE) ** -0.5 * LOG2E
    row = lambda i: (i, 0)
    in_specs = [pl.BlockSpec((tm, d), row), _const_spec((1, d)),
                _mod_spec(grp, 1, d), _mod_spec(grp, 0, d),
                _const_spec(w["w_in"].shape), _const_spec((1, MLA_Q_LORA)),
                _const_spec((1, MLA_KV_LORA)), _const_spec(w["w_qn"].shape),
                _const_spec(w["w_qr"].shape), _const_spec(w["w_kn"].shape),
                _const_spec(w["w_vt"].shape)]
    args = [x, g0, mod, mod, w["w_in"], w["g_qa"], w["g_kva"], w["w_qn"], w["w_qr"], w["w_kn"],
            w["w_vt"]]
    if rope:
        pos = lambda i: (i % n_per_batch, 0)
        pos_t = lambda i: (0, i % n_per_batch)
        in_specs += [pl.BlockSpec((tm, LANES), pos), pl.BlockSpec((tm, LANES), pos),
                     pl.BlockSpec((LANES, tm), pos_t), pl.BlockSpec((LANES, tm), pos_t)]
        args += [rope_tabs[0], rope_tabs[1], rope_tabs[0].T, rope_tabs[1].T]
    hn = MLA_HEADS * MLA_NOPE
    hr = MLA_HEADS * MLA_ROPE
    col = lambda i: (0, i)
    out_shape = [jax.ShapeDtypeStruct((hn, t), BF16), jax.ShapeDtypeStruct((hr, t), BF16),
                 jax.ShapeDtypeStruct((t, hn), BF16), jax.ShapeDtypeStruct((t, 2 * LANES), BF16),
                 jax.ShapeDtypeStruct((nb, MLA_HEADS, n_per_batch, MLA_V, tm), BF16)]
    out_specs = [pl.BlockSpec((hn, tm), col), pl.BlockSpec((hr, tm), col),
                 pl.BlockSpec((tm, hn), row), pl.BlockSpec((tm, 2 * LANES), row),
                 pl.BlockSpec((1, MLA_HEADS, 1, MLA_V, tm),
                              lambda i: (i // n_per_batch, 0, i % n_per_batch, 0, 0))]
    return pl.pallas_call(
        functools.partial(_mla_proj_kernel, rope=rope, qscale=qscale),
        grid=(n_tiles,), in_specs=in_specs, out_specs=out_specs, out_shape=out_shape,
        scratch_shapes=[pltpu.VMEM((tm, d), BF16)],
        compiler_params=_params(("arbitrary",)),
        name="mla_proj_lat" if rope else "mla_proj_ctx",
    )(*args)


def _kv_group_sizes(n_chunks):
    if n_chunks <= 2:
        return [n_chunks] if n_chunks else []
    tail = [min(KV_GROUP, n_chunks) - 2, 2]
    body = n_chunks - sum(tail)
    assert body % KV_GROUP == 0
    return [KV_GROUP] * (body // KV_GROUP) + tail


def _mla_attn_kernel(*refs, n_lat, tk):
    if n_lat:
        (qn_ref, qr_ref, knc_ref, krc_ref, vtc_ref, knl_ref, krl_ref, vtl_ref, o_ref,
         acc_ref, s_ref, qa_ref, k2_ref) = refs
    else:
        qn_ref, qr_ref, knc_ref, krc_ref, vtc_ref, o_ref, acc_ref, s_ref, qa_ref, k2_ref = refs
    tq = qn_ref.shape[1]
    lc = knc_ref.shape[1]
    parity = pl.program_id(1) % 2

    def lat_keys(j):
        off = pl.multiple_of(j * tk, tk)
        return knl_ref[0, pl.ds(off, tk), :], krl_ref[0, pl.ds(off, tk), :]

    @pl.when(pl.program_id(2) == 0)
    def _():
        def sqnorm_max(kn, kr):
            a, b = kn.astype(F32), kr.astype(F32)
            r = jnp.sum(a * a + b * b, axis=1, keepdims=True)
            return jnp.max(r, axis=0, keepdims=True)
        mx = sqnorm_max(knc_ref[0], krc_ref[0])
        if n_lat:
            mx = lax.fori_loop(0, n_lat, lambda j, c: jnp.maximum(c, sqnorm_max(*lat_keys(j))), mx)
        k2_ref[...] = jnp.broadcast_to(mx, k2_ref.shape)

    qn = qn_ref[...]
    qr = qr_ref[...]
    feat = lax.broadcasted_iota(jnp.int32, qr.shape, 0)
    own = ((feat >> 5) & 1) == parity
    qnf, qrf = qn.astype(F32), jnp.where(own, qr.astype(F32), 0.0)
    q2 = jnp.sum(qnf * qnf + qrf * qrf, axis=0, keepdims=True)
    ref = jnp.sqrt(q2 * k2_ref[0:1, 0:1])
    fast = jnp.max(ref) <= REF_LIMIT
    shift = jnp.where(fast, -ref, 0.0).astype(BF16)
    qa_ref[:LANES, :] = qn
    qa_ref[LANES:, :] = jnp.where(feat == LANES - 1 - 32 * parity, shift, qr)

    def scores(kn, kr):
        return jnp.dot(jnp.concatenate([kn, kr], axis=1), qa_ref[...],
                       preferred_element_type=F32)

    def finish(l):
        o_ref[...] = (acc_ref[...] * (1.0 / l)).T.astype(BF16)

    def shifted_softmax():
        def chunk(s, vt, l8):
            p = jnp.exp2(s)
            l8 = l8 + jnp.sum(p.reshape(-1, 8, tq), axis=0)
            return l8, jnp.dot(vt, p.astype(BF16), preferred_element_type=F32)

        sizes = _kv_group_sizes(n_lat)
        starts = [sum(sizes[:g]) for g in range(len(sizes))]

        def slot(g):
            return s_ref.at[g % 2, pl.ds(0, sizes[g] * tk)]

        def group_scores(g):
            rows = pl.ds(starts[g] * tk, sizes[g] * tk)
            return scores(knl_ref[0, rows, :], krl_ref[0, rows, :])

        if sizes:
            slot(0)[...] = group_scores(0)
        l8, pv = chunk(scores(knc_ref[0], krc_ref[0]), vtc_ref[0, 0, 0], jnp.zeros((8, tq), F32))
        acc_ref[...] = pv
        for g in range(len(sizes)):
            if g + 1 < len(sizes):
                slot(g + 1)[...] = group_scores(g + 1)
            vt = jnp.concatenate([vtl_ref[0, 0, starts[g] + u] for u in range(sizes[g])], axis=1)
            l8, pv = chunk(slot(g)[...], vt, l8)
            acc_ref[...] += pv
        finish(jnp.sum(l8, axis=0, keepdims=True))

    def online_softmax():
        def update(s_view, vt, m, l):
            m_new = jnp.maximum(m, jnp.max(s_view[...], axis=0, keepdims=True))
            alpha = jnp.exp2(m - m_new)
            p = jnp.exp2(s_view[...] - m_new)
            l_new = alpha * l + jnp.sum(p, axis=0, keepdims=True)
            acc_ref[...] = acc_ref[...] * alpha + jnp.dot(vt, p.astype(BF16),
                                                          preferred_element_type=F32)
            return m_new, l_new

        acc_ref[...] = jnp.zeros_like(acc_ref)
        m = jnp.full((1, tq), NEG_BIG, F32)
        l = jnp.zeros((1, tq), F32)
        slots = [s_ref.at[0, pl.ds(0, tk)], s_ref.at[1, pl.ds(0, tk)]] if n_lat else None
        if n_lat:
            slots[0][...] = scores(*lat_keys(0))
        ctx_view = s_ref.at[1, pl.ds(0, lc)]
        ctx_view[...] = scores(knc_ref[0], krc_ref[0])
        m, l = update(ctx_view, vtc_ref[0, 0, 0], m, l)
        if n_lat:
            def body(jj, carry):
                for u in range(KV_UNROLL):
                    j = KV_UNROLL * jj + u
                    slots[(u + 1) % 2][...] = scores(*lat_keys(jnp.minimum(j + 1, n_lat - 1)))
                    carry = update(slots[u % 2], vtl_ref[0, 0, j], *carry)
                return carry
            m, l = lax.fori_loop(0, n_lat // KV_UNROLL, body, (m, l))
        finish(l)

    lax.cond(fast, shifted_softmax, online_softmax)


def _mla_attn(qn, qr, ctx_kv, lat_kv, nb, tq):
    t = qn.shape[1]
    nq = t // nb // tq
    knc, krc, vtc = ctx_kv
    lc = knc.shape[1]
    qrow = lambda b, h, i: (b * nq + i, h)
    in_specs = [pl.BlockSpec((LANES, tq), lambda b, h, i: (h, b * nq + i)),
                pl.BlockSpec((LANES, tq), lambda b, h, i: (h // 2, b * nq + i)),
                pl.BlockSpec((1, lc, LANES), lambda b, h, i: (b, 0, h)),
                pl.BlockSpec((1, lc, LANES), lambda b, h, i: (b, 0, h % 2)),
                pl.BlockSpec((1, 1, 1, MLA_V, lc), lambda b, h, i: (b, h, 0, 0, 0))]
    args = [qn, qr, knc, krc, vtc]
    n_lat, tk = 0, 0
    if lat_kv is not None:
        knl, krl, vtl = lat_kv
        s = knl.shape[1]
        n_lat, tk = vtl.shape[2], vtl.shape[4]
        assert n_lat % KV_UNROLL == 0 and tk >= lc
        in_specs += [pl.BlockSpec((1, s, LANES), lambda b, h, i: (b, 0, h)),
                     pl.BlockSpec((1, s, LANES), lambda b, h, i: (b, 0, h % 2)),
                     pl.BlockSpec((1, 1, n_lat, MLA_V, tk), lambda b, h, i: (b, h, 0, 0, 0))]
        args += [knl, krl, vtl]
    scratch = [pltpu.VMEM((MLA_V, tq), F32), pltpu.VMEM((2, max(KV_GROUP * tk, lc), tq), F32),
               pltpu.VMEM((2 * LANES, tq), BF16), pltpu.VMEM((8, LANES), F32)]
    return pl.pallas_call(
        functools.partial(_mla_attn_kernel, n_lat=n_lat, tk=tk),
        grid=(nb, MLA_HEADS, nq), in_specs=in_specs,
        out_specs=pl.BlockSpec((tq, MLA_V), qrow),
        out_shape=jax.ShapeDtypeStruct((t, MLA_HEADS * MLA_V), BF16),
        scratch_shapes=scratch,
        compiler_params=_params(("arbitrary", "arbitrary", "arbitrary")),
        name="mla_attn_lat" if n_lat else "mla_attn_ctx",
    )(*args)


def _outproj_kernel(o_ref, w_ref, x_ref, g_ref, gt_ref, out_ref, y_ref):
    y_ref[...] = jnp.dot(o_ref[...], w_ref[...], preferred_element_type=F32)
    gain = gt_ref[0] * g_ref[...]

    def rows(rs):
        y = y_ref[rs, :]
        out_ref[rs, :] = x_ref[rs, :] + y * _inv_rms(y) * gain
    _row_chunks(y_ref.shape[0], rows)


def _outproj(o, w_out, x, g1, mod, grp, tm):
    t, d = x.shape
    row = lambda i: (i, 0)
    return pl.pallas_call(
        _outproj_kernel,
        grid=(t // tm,),
        in_specs=[pl.BlockSpec((tm, o.shape[1]), row), _const_spec(w_out.shape),
                  pl.BlockSpec((tm, d), row), _const_spec((1, d)), _mod_spec(grp, 2, d)],
        out_specs=pl.BlockSpec((tm, d), row),
        out_shape=jax.ShapeDtypeStruct((t, d), F32),
        scratch_shapes=[pltpu.VMEM((tm, d), F32)],
        compiler_params=_params(("arbitrary",)),
        name="outproj",
    )(o, w_out, x, g1, mod)


def _mlp_kernel(x_ref, g2_ref, sc_ref, sh_ref, w1_ref, w2_ref, g3_ref, gt_ref, out_ref,
                f_ref, acc_ref):
    k = pl.program_id(1)

    @pl.when(k == 0)
    def _():
        _modulated_norm(x_ref, g2_ref, sc_ref, sh_ref, f_ref)
        acc_ref[...] = jnp.zeros_like(acc_ref)

    u = jnp.maximum(jnp.dot(f_ref[...], w1_ref[...], preferred_element_type=F32), 0.0)
    acc_ref[...] += jnp.dot((u * u).astype(BF16), w2_ref[...], preferred_element_type=F32)

    @pl.when(k == pl.num_programs(1) - 1)
    def _():
        gain = gt_ref[0] * g3_ref[...]

        def rows(rs):
            y = acc_ref[rs, :]
            out_ref[rs, :] = x_ref[rs, :] + y * _inv_rms(y) * gain
        _row_chunks(acc_ref.shape[0], rows)


def _mlp(x, g2, g3, mod, grp, w1, w2, tm):
    t, d = x.shape
    dff = w1.shape[1]
    row = lambda i, k: (i, 0)
    return pl.pallas_call(
        _mlp_kernel,
        grid=(t // tm, dff // FF_TILE),
        in_specs=[pl.BlockSpec((tm, d), row), _const_spec((1, d)),
                  _mod_spec(grp, 4, d), _mod_spec(grp, 3, d),
                  pl.BlockSpec((d, FF_TILE), lambda i, k: (0, k)),
                  pl.BlockSpec((FF_TILE, d), lambda i, k: (k, 0)),
                  _const_spec((1, d)), _mod_spec(grp, 5, d)],
        out_specs=pl.BlockSpec((tm, d), row),
        out_shape=jax.ShapeDtypeStruct((t, d), F32),
        scratch_shapes=[pltpu.VMEM((tm, d), BF16), pltpu.VMEM((tm, d), F32)],
        compiler_params=_params(("arbitrary", "arbitrary")),
        name="mlp",
    )(x, g2, mod, mod, w1, w2, g3, mod)


def _swa_proj_kernel(*refs, rope, qscale):
    if rope:
        (x_ref, g_ref, sc_ref, sh_ref, w_ref, wvt_ref, cos_ref, sin_ref, q_ref, k2_ref, vt_ref,
         h_ref) = refs
        cos, sin = cos_ref[...], sin_ref[...]
    else:
        x_ref, g_ref, sc_ref, sh_ref, w_ref, wvt_ref, q_ref, k2_ref, vt_ref, h_ref = refs
    _modulated_norm(x_ref, g_ref, sc_ref, sh_ref, h_ref)
    h = h_ref[...]
    p = jnp.dot(h, w_ref[...], preferred_element_type=F32)
    dq = q_ref.shape[1]
    for t in range(dq // LANES):
        tile = p[:, t * LANES:(t + 1) * LANES] * qscale
        if rope:
            tile = _rot(tile, cos, sin)
        q_ref[:, t * LANES:(t + 1) * LANES] = tile.astype(BF16)
    for c in range(SWA_KV_HEADS):
        tile = p[:, dq + c * LANES:dq + (c + 1) * LANES]
        if rope:
            tile = _rot(tile, cos, sin)
        first = _first_head_lanes(tile.shape)
        k2_ref[:, 2 * c * LANES:(2 * c + 1) * LANES] = jnp.where(first, tile, 0.0).astype(BF16)
        k2_ref[:, (2 * c + 1) * LANES:(2 * c + 2) * LANES] = jnp.where(first, 0.0, tile).astype(BF16)
    vt_ref[...] = lax.dot_general(wvt_ref[...], h, NT_DIMS,
                                  preferred_element_type=F32).astype(BF16)


def _swa_proj(x, mod, grp, g0, w, rope_tabs, tm, n_per_batch):
    t, d = x.shape
    rope = rope_tabs is not None
    row = lambda i: (i, 0)
    dq = SWA_HEADS * SWA_HEAD_DIM
    dkv = SWA_KV_HEADS * SWA_HEAD_DIM
    w_qk, w_vt = w
    in_specs = [pl.BlockSpec((tm, d), row), _const_spec((1, d)),
                _mod_spec(grp, 1, d), _mod_spec(grp, 0, d), _const_spec(w_qk.shape),
                _const_spec(w_vt.shape)]
    args = [x, g0, mod, mod, w_qk, w_vt]
    if rope:
        pos = lambda i: (i % n_per_batch, 0)
        in_specs += [pl.BlockSpec((tm, LANES), pos), pl.BlockSpec((tm, LANES), pos)]
        args += list(rope_tabs)
    out_shape = [jax.ShapeDtypeStruct((t, dq), BF16),
                 jax.ShapeDtypeStruct((t, 2 * SWA_KV_HEADS * LANES), BF16),
                 jax.ShapeDtypeStruct((dkv, t), BF16)]
    out_specs = [pl.BlockSpec((tm, dq), row), pl.BlockSpec((tm, 2 * SWA_KV_HEADS * LANES), row),
                 pl.BlockSpec((dkv, tm), lambda i: (0, i))]
    return pl.pallas_call(
        functools.partial(_swa_proj_kernel, rope=rope, qscale=SWA_HEAD_DIM ** -0.5 * LOG2E),
        grid=(t // tm,), in_specs=in_specs, out_specs=out_specs, out_shape=out_shape,
        scratch_shapes=[pltpu.VMEM((tm, d), BF16)],
        compiler_params=_params(("arbitrary",)),
        name="swa_proj_lat" if rope else "swa_proj_ctx",
    )(*args)


def _swa_attn_kernel(sink_ref, q_ref, kc_ref, kp_ref, kcur_ref, kn_ref, vc_ref, vp_ref, vcur_ref,
                     vn_ref, o_ref, s_ref, *, qb):
    kvh = pl.program_id(1)
    i = pl.program_id(2)
    lc = kc_ref.shape[0]
    span = SWA_QBLK + 2 * SWA_WINDOW
    n_pairs = SWA_GROUP // 2
    kc, vc = kc_ref[...], vc_ref[...]
    kwin = jnp.concatenate([kp_ref[...], kcur_ref[...], kn_ref[...]], axis=0)
    vwin = jnp.concatenate([vp_ref[...], vcur_ref[...], vn_ref[...]], axis=1)
    r = lax.broadcasted_iota(jnp.int32, (lc + span, SWA_QBLK), 0)
    rel = r - lc - lax.broadcasted_iota(jnp.int32, (lc + span, SWA_QBLK), 1)
    bias = jnp.where((r < lc) | ((rel >= 0) & (rel <= 2 * SWA_WINDOW)), 0.0, NEG_BIG)
    bias = jnp.concatenate([bias] * n_pairs, axis=1)
    pair = lax.broadcasted_iota(jnp.int32, (1, n_pairs * SWA_QBLK), 1) // SWA_QBLK
    keep_prev = jnp.where(i == 0, 0.0, 1.0)
    keep_next = jnp.where(i == pl.num_programs(2) - 1, 0.0, 1.0)
    def scores(blk, e):
        qs = jnp.concatenate([q_ref[blk * SWA_QBLK:(blk + 1) * SWA_QBLK, t * LANES:(t + 1) * LANES]
                              for t in range(n_pairs)], axis=0)
        kcat = jnp.concatenate([kc[:, e * LANES:(e + 1) * LANES],
                                kwin[blk * SWA_QBLK:blk * SWA_QBLK + span,
                                     e * LANES:(e + 1) * LANES]], axis=0)
        return lax.dot_general(kcat, qs, NT_DIMS, preferred_element_type=F32)

    s_ref[0] = scores(0, 0)
    for blk in range(qb):
        rows = slice(blk * SWA_QBLK, (blk + 1) * SWA_QBLK)
        vt = jnp.concatenate([vc, vwin[:, blk * SWA_QBLK:blk * SWA_QBLK + span]], axis=1)
        halves = []
        for e in range(2):
            if e == 0:
                s_ref[1] = scores(blk, 1)
            elif blk + 1 < qb:
                s_ref[0] = scores(blk + 1, 0)
            s = s_ref[e] + bias
            sk = jnp.zeros(pair.shape, F32)
            for t in range(n_pairs):
                sk = jnp.where(pair == t, sink_ref[kvh * SWA_GROUP + 2 * t + e] * LOG2E, sk)
            m = jnp.maximum(jnp.max(s, axis=0, keepdims=True), sk)
            p = jnp.exp2(s - m)
            if blk == 0:
                p = jnp.concatenate([p[:lc], p[lc:lc + SWA_WINDOW] * keep_prev,
                                     p[lc + SWA_WINDOW:]], axis=0)
            if blk == qb - 1:
                p = jnp.concatenate([p[:lc + span - SWA_WINDOW],
                                     p[lc + span - SWA_WINDOW:] * keep_next], axis=0)
            den = jnp.sum(p, axis=0, keepdims=True) + jnp.exp2(sk - m)
            o = jnp.dot(vt, p.astype(BF16), preferred_element_type=F32)
            halves.append(o * (1.0 / den))
        both = jnp.concatenate(halves, axis=0)
        for t in range(n_pairs):
            tile = both[:, t * SWA_QBLK:(t + 1) * SWA_QBLK]
            o_ref[rows, t * LANES:(t + 1) * LANES] = tile.T.astype(BF16)


def _swa_attn(sink, q, k2, vt, k2c, vtc, nb, seq, lc):
    t = q.shape[0]
    nblk = seq // SWA_QBLK
    qb = min(SWA_STEP_BLOCKS, nblk)
    nsteps = nblk // qb
    gq = SWA_GROUP * SWA_HEAD_DIM
    hd = SWA_HEAD_DIM
    prev_blk = lambda b, i: b * nblk + jnp.maximum(qb * i - 1, 0)
    next_blk = lambda b, i: b * nblk + jnp.minimum(qb * i + qb, nblk - 1)
    return pl.pallas_call(
        functools.partial(_swa_attn_kernel, qb=qb),
        grid=(nb, SWA_KV_HEADS, nsteps),
        in_specs=[pl.BlockSpec(memory_space=pltpu.SMEM),
                  pl.BlockSpec((qb * SWA_QBLK, gq), lambda b, h, i: (b * nsteps + i, h)),
                  pl.BlockSpec((lc, 2 * LANES), lambda b, h, i: (b, h)),
                  pl.BlockSpec((SWA_QBLK, 2 * LANES), lambda b, h, i: (prev_blk(b, i), h)),
                  pl.BlockSpec((qb * SWA_QBLK, 2 * LANES), lambda b, h, i: (b * nsteps + i, h)),
                  pl.BlockSpec((SWA_QBLK, 2 * LANES), lambda b, h, i: (next_blk(b, i), h)),
                  pl.BlockSpec((hd, lc), lambda b, h, i: (h, b)),
                  pl.BlockSpec((hd, SWA_QBLK), lambda b, h, i: (h, prev_blk(b, i))),
                  pl.BlockSpec((hd, qb * SWA_QBLK), lambda b, h, i: (h, b * nsteps + i)),
                  pl.BlockSpec((hd, SWA_QBLK), lambda b, h, i: (h, next_blk(b, i)))],
        out_specs=pl.BlockSpec((qb * SWA_QBLK, gq), lambda b, h, i: (b * nsteps + i, h)),
        out_shape=jax.ShapeDtypeStruct((t, SWA_HEADS * SWA_HEAD_DIM), BF16),
        scratch_shapes=[pltpu.VMEM((2, lc + SWA_QBLK + 2 * SWA_WINDOW, gq), F32)],
        compiler_params=_params(("arbitrary", "arbitrary", "arbitrary")),
        name="swa_attn",
    )(sink, q, k2c, k2, k2, k2, vtc, vt, vt, vt)


def _rope_tables(seq):
    rows = seq // GRID_W
    row = jnp.repeat(jnp.arange(rows, dtype=F32), GRID_W)
    col = jnp.tile(jnp.arange(GRID_W, dtype=F32), rows)
    n_freq = MLA_ROPE // 4
    freqs = ROPE_BASE ** (-jnp.arange(n_freq, dtype=F32) / n_freq)
    ang = jnp.concatenate([row[:, None] * freqs, col[:, None] * freqs], axis=-1)
    cos, sin = jnp.cos(ang), jnp.sin(ang)
    return (jnp.concatenate([cos, cos, cos, cos], axis=-1),
            jnp.concatenate([-sin, -sin, sin, sin], axis=-1))


def _pair_tiles(w, n_heads, half):
    k = w.shape[0]
    x1 = w[:, :, :half].reshape(k, n_heads // 2, 2 * half)
    x2 = w[:, :, half:].reshape(k, n_heads // 2, 2 * half)
    return jnp.concatenate([x1, x2], axis=2).reshape(k, n_heads * 2 * half)


def _mla_weights(w_in, g_qa, g_kva, w_qb, w_kvb):
    half = MLA_ROPE // 2
    lat = MLA_Q_LORA + MLA_KV_LORA
    k1, k2 = w_in[:, lat:lat + half], w_in[:, lat + half:]
    qb = w_qb.reshape(MLA_Q_LORA, MLA_HEADS, MLA_NOPE + MLA_ROPE)
    kvb = w_kvb.reshape(MLA_KV_LORA, MLA_HEADS, MLA_NOPE + MLA_V)
    return {
        "w_in": jnp.concatenate([w_in[:, :lat], k1, k1, k2, k2], axis=1).astype(BF16),
        "g_qa": g_qa.reshape(1, -1), "g_kva": g_kva.reshape(1, -1),
        "w_qn": qb[:, :, :MLA_NOPE].reshape(MLA_Q_LORA, -1).T.astype(BF16),
        "w_qr": _pair_tiles(qb[:, :, MLA_NOPE:], MLA_HEADS, half).T.astype(BF16),
        "w_kn": kvb[:, :, :MLA_NOPE].reshape(MLA_KV_LORA, -1).astype(BF16),
        "w_vt": kvb[:, :, MLA_NOPE:].reshape(MLA_KV_LORA, -1).T.astype(BF16),
    }


def _swa_weights(w_qkv):
    d = w_qkv.shape[0]
    half = SWA_HEAD_DIM // 2
    dq = SWA_HEADS * SWA_HEAD_DIM
    dkv = SWA_KV_HEADS * SWA_HEAD_DIM
    q = _pair_tiles(w_qkv[:, :dq].reshape(d, SWA_HEADS, SWA_HEAD_DIM), SWA_HEADS, half)
    k = w_qkv[:, dq:dq + dkv].reshape(d, SWA_KV_HEADS, SWA_HEAD_DIM)
    k1, k2 = k[:, :, :half], k[:, :, half:]
    k = jnp.concatenate([k1, k1, k2, k2], axis=2).reshape(d, SWA_KV_HEADS * LANES)
    return jnp.concatenate([q, k], axis=1).astype(BF16), w_qkv[:, dq + dkv:].T.astype(BF16)


def kernel(x, c, ctx, c_ctx, w_mod, b_mod, g_norm, w_ff_in, w_ff_out, mla_w_in, mla_g_qa,
           mla_g_kva, mla_w_qb, mla_w_kvb, mla_w_out, swa_w_qkv, swa_sink, swa_w_out):
    nb, seq, d = x.shape
    lc = ctx.shape[1]
    depth = w_mod.shape[0]
    assert nb + 1 <= MOD_ROWS
    tm = min(TOKEN_TILE, seq)
    tq = min(ATTN_TQ, seq)
    n_per_batch = seq // tm
    grp_lat = lambda i: i // n_per_batch
    grp_ctx = lambda i: nb

    cmat = jnp.zeros((MOD_ROWS, d), F32).at[:nb].set(c).at[nb].set(c_ctx)
    mod_all = _modulation(cmat, w_mod, b_mod)
    rope_tabs = _rope_tables(seq)

    xl = x.reshape(nb * seq, d)
    xc = ctx.reshape(nb * lc, d)
    for i in range(depth):
        need_ctx = i < depth - 1
        mod = mod_all[i].reshape(MOD_ROWS, 1, 6 * d)
        g = g_norm[i].reshape(4, 1, d)
        j = i // 2
        if i % 2 == 0:
            w = _mla_weights(mla_w_in[j], mla_g_qa[j], mla_g_kva[j], mla_w_qb[j], mla_w_kvb[j])
            w_out = mla_w_out[j].astype(BF16)
            qn, qr, kn, kr, vt = _mla_proj(xl, mod, grp_lat, g[0], w, rope_tabs, tm, n_per_batch)
            qnc, qrc, knc, krc, vtc = _mla_proj(xc, mod, grp_ctx, g[0], w, None, lc, 1)
            ctx_kv = (knc.reshape(nb, lc, -1), krc.reshape(nb, lc, -1), vtc)
            lat_kv = (kn.reshape(nb, seq, -1), kr.reshape(nb, seq, -1), vt)
            o_l = _mla_attn(qn, qr, ctx_kv, lat_kv, nb, tq)
            o_c = _mla_attn(qnc, qrc, ctx_kv, None, nb, lc) if need_ctx else None
        else:
            w = _swa_weights(swa_w_qkv[j])
            w_out = swa_w_out[j].astype(BF16)
            q, k2, vt = _swa_proj(xl, mod, grp_lat, g[0], w, rope_tabs, tm, n_per_batch)
            qc, k2c, vtc = _swa_proj(xc, mod, grp_ctx, g[0], w, None, lc, 1)
            o_l = _swa_attn(swa_sink[j], q, k2, vt, k2c, vtc, nb, seq, lc)
            assert not need_ctx
            o_c = None
        w1 = w_ff_in[i].astype(BF16)
        w2 = w_ff_out[i].astype(BF16)
        xl = _outproj(o_l, w_out, xl, g[1], mod, grp_lat, tm)
        xl = _mlp(xl, g[2], g[3], mod, grp_lat, w1, w2, tm)
        if need_ctx:
            xc = _outproj(o_c, w_out, xc, g[1], mod, grp_ctx, lc)
            xc = _mlp(xc, g[2], g[3], mod, grp_ctx, w1, w2, lc)
    return xl.reshape(nb, seq, d)
```

```python
import functools
import math

import jax
import jax.numpy as jnp
from jax import lax
from jax.experimental import pallas as pl
from jax.experimental.pallas import tpu as pltpu

F32 = jnp.float32
BF16 = jnp.bfloat16

GRID_W = 64
ROPE_BASE = 10000.0
NORM_EPS = 1e-6
LOG2E = math.log2(math.e)
NEG_BIG = -1e30
REF_LIMIT = 60.0
NORM_SLACK = 1.03

MLA_HEADS = 16
MLA_Q_LORA = 512
MLA_KV_LORA = 512
MLA_NOPE = 128
MLA_ROPE = 64
MLA_V = 128

SWA_HEADS = 32
SWA_KV_HEADS = 4
SWA_HEAD_DIM = 64
SWA_WINDOW = 128
SWA_GROUP = SWA_HEADS // SWA_KV_HEADS
SWA_QBLK = 128
SWA_STEP_BLOCKS = 4

LANES = 128
HALF_TILE = 64

MOD_ROWS = 8
MOD_TN = 1024
TOKEN_TILE = 512
ROW_CHUNK = 32
ROW_UNROLL = 4
FF_TILE = 1024
ATTN_TQ = 512
KV_GROUP = 8
KV_UNROLL = 4
VMEM_LIMIT = 56 * 1024 * 1024

NT_DIMS = (((1,), (1,)), ((), ()))


def _rms(xf, g):
    ms = jnp.mean(xf * xf, axis=-1, keepdims=True)
    return xf * lax.rsqrt(ms + NORM_EPS) * g


def _inv_rms(xf):
    return lax.rsqrt(jnp.mean(xf * xf, axis=-1, keepdims=True) + NORM_EPS)


def _row_chunks(n_rows, fn):
    def body(i, carry):
        fn(pl.ds(pl.multiple_of(i * ROW_CHUNK, ROW_CHUNK), ROW_CHUNK))
        return carry
    lax.fori_loop(0, n_rows // ROW_CHUNK, body, 0, unroll=ROW_UNROLL)


def _modulated_norm(x_ref, g_ref, sc_ref, sh_ref, h_ref):
    gain = g_ref[...] * (1.0 + sc_ref[0])
    shift = sh_ref[0]

    def rows(rs):
        xf = x_ref[rs, :]
        h_ref[rs, :] = (xf * _inv_rms(xf) * gain + shift).astype(h_ref.dtype)
    _row_chunks(h_ref.shape[0], rows)


def _rot(tile, cos, sin):
    return tile * cos + pltpu.roll(tile, HALF_TILE, 1) * sin


def _first_head_lanes(shape):
    lane = lax.broadcasted_iota(jnp.int32, shape, 1)
    return (lane & 32) == 0


def _shift_lane(parity):
    return LANES - 1 - 32 * parity


def _params(sem):
    return pltpu.CompilerParams(dimension_semantics=sem, vmem_limit_bytes=VMEM_LIMIT)


def _const_spec(shape):
    nd = len(shape)
    return pl.BlockSpec(shape, lambda *_: (0,) * nd, pipeline_mode=pl.Buffered(1))


def _mod_spec(grp, which, d):
    return pl.BlockSpec((1, 1, d), lambda i, *_: (grp(i), 0, which))


def _mod_kernel(c_ref, w_ref, b_ref, o_ref):
    c = c_ref[...]
    a = c / (1.0 + jnp.exp(-c))
    o_ref[0] = jnp.dot(a, w_ref[0], preferred_element_type=F32,
                       precision=lax.Precision.HIGHEST) + b_ref[0]


def _modulation(cmat, w_mod, b_mod):
    depth, d, n = w_mod.shape
    return pl.pallas_call(
        _mod_kernel,
        grid=(depth, n // MOD_TN),
        in_specs=[pl.BlockSpec((MOD_ROWS, d), lambda l, j: (0, 0)),
                  pl.BlockSpec((1, d, MOD_TN), lambda l, j: (l, 0, j)),
                  pl.BlockSpec((1, 1, MOD_TN), lambda l, j: (l, 0, j))],
        out_specs=pl.BlockSpec((1, MOD_ROWS, MOD_TN), lambda l, j: (l, 0, j)),
        out_shape=jax.ShapeDtypeStruct((depth, MOD_ROWS, n), F32),
        compiler_params=_params(("arbitrary", "arbitrary")),
        name="modulation",
    )(cmat, w_mod, b_mod.reshape(depth, 1, n))


def _mla_proj_kernel(*refs, rope, qscale):
    if rope:
        (x_ref, g_ref, sc_ref, sh_ref, win_ref, gqa_ref, gkva_ref, wqn_ref, wqr_ref, wkn_ref,
         wvt_ref, cos_ref, sin_ref, cost_ref, sint_ref, qn_ref, qr_ref, kn_ref, kr_ref,
         vt_ref) = refs
        cos, sin = cos_ref[...], sin_ref[...]
    else:
        (x_ref, g_ref, sc_ref, sh_ref, win_ref, gqa_ref, gkva_ref, wqn_ref, wqr_ref, wkn_ref,
         wvt_ref, qn_ref, qr_ref, kn_ref, kr_ref, vt_ref) = refs
    h = (_rms(x_ref[...], g_ref[...]) * (1.0 + sc_ref[0]) + sh_ref[0]).astype(BF16)
    p = jnp.dot(h, win_ref[...], preferred_element_type=F32)
    qa = _rms(p[:, :MLA_Q_LORA], gqa_ref[...]).astype(BF16)
    ckv = _rms(p[:, MLA_Q_LORA:MLA_Q_LORA + MLA_KV_LORA], gkva_ref[...]).astype(BF16)
    kr = p[:, MLA_Q_LORA + MLA_KV_LORA:]

    qn = lax.dot_general(wqn_ref[...], qa, NT_DIMS, preferred_element_type=F32) * qscale
    qn_ref[...] = qn.astype(BF16)
    qr = lax.dot_general(wqr_ref[...], qa, NT_DIMS, preferred_element_type=F32) * qscale
    for t in range(qr.shape[0] // LANES):
        tile = qr[t * LANES:(t + 1) * LANES, :]
        if rope:
            swapped = jnp.concatenate([tile[HALF_TILE:], tile[:HALF_TILE]], axis=0)
            tile = tile * cost_ref[...] + swapped * sint_ref[...]
        qr_ref[t * LANES:(t + 1) * LANES, :] = tile.astype(BF16)

    if rope:
        kr = _rot(kr, cos, sin)
    first = _first_head_lanes(kr.shape)
    lane = lax.broadcasted_iota(jnp.int32, kr.shape, 1)
    kr_ref[:, :LANES] = jnp.where(lane == _shift_lane(0), 1.0,
                                  jnp.where(first, kr, 0.0)).astype(BF16)
    kr_ref[:, LANES:] = jnp.where(lane == _shift_lane(1), 1.0,
                                  jnp.where(first, 0.0, kr)).astype(BF16)

    kn_ref[...] = jnp.dot(ckv, wkn_ref[...], preferred_element_type=F32).astype(BF16)
    vt = lax.dot_general(wvt_ref[...], ckv, NT_DIMS, preferred_element_type=F32).astype(BF16)
    for hd in range(MLA_HEADS):
        vt_ref[0, hd, 0] = vt[hd * MLA_V:(hd + 1) * MLA_V, :]


def _mla_proj(x, mod, grp, g0, w, rope_tabs, tm, n_per_batch):
    t, d = x.shape
    n_tiles = t // tm
    nb = n_tiles // n_per_batch
    rope = rope_tabs is not None
    qscale = (MLA_NOPE + MLA_ROPE) ** -0.5 * LOG2E
    row = lambda i: (i, 0)
    in_specs = [pl.BlockSpec((tm, d), row), _const_spec((1, d)),
                _mod_spec(grp, 1, d), _mod_spec(grp, 0, d),
                _const_spec(w["w_in"].shape), _const_spec((1, MLA_Q_LORA)),
                _const_spec((1, MLA_KV_LORA)), _const_spec(w["w_qn"].shape),
                _const_spec(w["w_qr"].shape), _const_spec(w["w_kn"].shape),
                _const_spec(w["w_vt"].shape)]
    args = [x, g0, mod, mod, w["w_in"], w["g_qa"], w["g_kva"], w["w_qn"], w["w_qr"], w["w_kn"],
            w["w_vt"]]
    if rope:
        pos = lambda i: (i % n_per_batch, 0)
        pos_t = lambda i: (0, i % n_per_batch)
        in_specs += [pl.BlockSpec((tm, LANES), pos), pl.BlockSpec((tm, LANES), pos),
                     pl.BlockSpec((LANES, tm), pos_t), pl.BlockSpec((LANES, tm), pos_t)]
        args += [rope_tabs[0], rope_tabs[1], rope_tabs[0].T, rope_tabs[1].T]
    hn = MLA_HEADS * MLA_NOPE
    hr = MLA_HEADS * MLA_ROPE
    col = lambda i: (0, i)
    out_shape = [jax.ShapeDtypeStruct((hn, t), BF16), jax.ShapeDtypeStruct((hr, t), BF16),
                 jax.ShapeDtypeStruct((t, hn), BF16), jax.ShapeDtypeStruct((t, 2 * LANES), BF16),
                 jax.ShapeDtypeStruct((nb, MLA_HEADS, n_per_batch, MLA_V, tm), BF16)]
    out_specs = [pl.BlockSpec((hn, tm), col), pl.BlockSpec((hr, tm), col),
                 pl.BlockSpec((tm, hn), row), pl.BlockSpec((tm, 2 * LANES), row),
                 pl.BlockSpec((1, MLA_HEADS, 1, MLA_V, tm),
                              lambda i: (i // n_per_batch, 0, i % n_per_batch, 0, 0))]
    return pl.pallas_call(
        functools.partial(_mla_proj_kernel, rope=rope, qscale=qscale),
        grid=(n_tiles,), in_specs=in_specs, out_specs=out_specs, out_shape=out_shape,
        compiler_params=_params(("arbitrary",)),
        name="mla_proj_lat" if rope else "mla_proj_ctx",
    )(*args)


def _kv_group_sizes(n_chunks):
    if n_chunks <= 2:
        return [n_chunks] if n_chunks else []
    tail = [min(KV_GROUP, n_chunks) - 2, 2]
    body = n_chunks - sum(tail)
    assert body % KV_GROUP == 0
    return [KV_GROUP] * (body // KV_GROUP) + tail


def _mla_attn_kernel(*refs, n_lat, tk):
    if n_lat:
        (qn_ref, qr_ref, knc_ref, krc_ref, vtc_ref, knl_ref, krl_ref, vtl_ref, o_ref,
         acc_ref, s_ref, qa_ref, k2_ref) = refs
    else:
        qn_ref, qr_ref, knc_ref, krc_ref, vtc_ref, o_ref, acc_ref, s_ref, qa_ref, k2_ref = refs
    tq = qn_ref.shape[1]
    lc = knc_ref.shape[1]
    parity = pl.program_id(1) % 2

    def lat_keys(j):
        off = pl.multiple_of(j * tk, tk)
        return knl_ref[0, pl.ds(off, tk), :], krl_ref[0, pl.ds(off, tk), :]

    @pl.when(pl.program_id(2) == 0)
    def _():
        ones = jnp.ones((LANES, LANES), BF16)

        def sqnorm_max(kn, kr):
            r = jnp.dot(kn * kn + kr * kr, ones, preferred_element_type=F32)
            return jnp.max(r, axis=0, keepdims=True)
        mx = sqnorm_max(knc_ref[0], krc_ref[0])
        if n_lat:
            mx = lax.fori_loop(0, n_lat, lambda j, c: jnp.maximum(c, sqnorm_max(*lat_keys(j))), mx,
                               unroll=KV_UNROLL)
        k2_ref[...] = jnp.broadcast_to(mx * NORM_SLACK, k2_ref.shape)

    qn = qn_ref[...]
    qr = qr_ref[...]
    feat = lax.broadcasted_iota(jnp.int32, qr.shape, 0)
    own = ((feat >> 5) & 1) == parity
    qnf, qrf = qn.astype(F32), jnp.where(own, qr.astype(F32), 0.0)
    q2 = jnp.sum(qnf * qnf + qrf * qrf, axis=0, keepdims=True)
    ref = jnp.sqrt(q2 * k2_ref[0:1, 0:1])
    fast = jnp.max(ref) <= REF_LIMIT
    shift = jnp.where(fast, -ref, 0.0).astype(BF16)
    qa_ref[:LANES, :] = qn
    qa_ref[LANES:, :] = jnp.where(feat == LANES - 1 - 32 * parity, shift, qr)

    def scores(kn, kr):
        return jnp.dot(jnp.concatenate([kn, kr], axis=1), qa_ref[...],
                       preferred_element_type=F32)

    def finish(l):
        o_ref[...] = (acc_ref[...] * (1.0 / l)).T.astype(BF16)

    def shifted_softmax():
        def chunk(s, vt, l8):
            p = jnp.exp2(s)
            l8 = l8 + jnp.sum(p.reshape(-1, 8, tq), axis=0)
            return l8, jnp.dot(vt, p.astype(BF16), preferred_element_type=F32)

        sizes = _kv_group_sizes(n_lat)
        starts = [sum(sizes[:g]) for g in range(len(sizes))]
        last = max(len(sizes) - 1, 0)

        def n_rows(g):
            return (sizes[g] * tk if sizes else 0) + (lc if g == last else 0)

        def fill_scores(g):
            view = s_ref.at[g % 2]
            lat = sizes[g] * tk if sizes else 0
            if lat:
                rows = pl.ds(starts[g] * tk, lat)
                view[pl.ds(0, lat), :] = scores(knl_ref[0, rows, :], krl_ref[0, rows, :])
            if g == last:
                view[pl.ds(lat, lc), :] = scores(knc_ref[0], krc_ref[0])

        fill_scores(0)
        l8 = jnp.zeros((8, tq), F32)
        for g in range(last + 1):
            if g < last:
                fill_scores(g + 1)
            vts = [vtl_ref[0, 0, starts[g] + u] for u in range(sizes[g])] if sizes else []
            if g == last:
                vts.append(vtc_ref[0, 0, 0])
            l8, pv = chunk(s_ref[g % 2, pl.ds(0, n_rows(g)), :], jnp.concatenate(vts, axis=1), l8)
            if g == 0:
                acc_ref[...] = pv
            else:
                acc_ref[...] += pv
        finish(jnp.sum(l8, axis=0, keepdims=True))

    def online_softmax():
        def update(s_view, vt, m, l):
            m_new = jnp.maximum(m, jnp.max(s_view[...], axis=0, keepdims=True))
            alpha = jnp.exp2(m - m_new)
            p = jnp.exp2(s_view[...] - m_new)
            l_new = alpha * l + jnp.sum(p, axis=0, keepdims=True)
            acc_ref[...] = acc_ref[...] * alpha + jnp.dot(vt, p.astype(BF16),
                                                          preferred_element_type=F32)
            return m_new, l_new

        acc_ref[...] = jnp.zeros_like(acc_ref)
        m = jnp.full((1, tq), NEG_BIG, F32)
        l = jnp.zeros((1, tq), F32)
        slots = [s_ref.at[0, pl.ds(0, tk)], s_ref.at[1, pl.ds(0, tk)]] if n_lat else None
        if n_lat:
            slots[0][...] = scores(*lat_keys(0))
        ctx_view = s_ref.at[1, pl.ds(0, lc)]
        ctx_view[...] = scores(knc_ref[0], krc_ref[0])
        m, l = update(ctx_view, vtc_ref[0, 0, 0], m, l)
        if n_lat:
            def body(jj, carry):
                for u in range(KV_UNROLL):
                    j = KV_UNROLL * jj + u
                    slots[(u + 1) % 2][...] = scores(*lat_keys(jnp.minimum(j + 1, n_lat - 1)))
                    carry = update(slots[u % 2], vtl_ref[0, 0, j], *carry)
                return carry
            m, l = lax.fori_loop(0, n_lat // KV_UNROLL, body, (m, l))
        finish(l)

    lax.cond(fast, shifted_softmax, online_softmax)


def _mla_attn(qn, qr, ctx_kv, lat_kv, nb, tq):
    t = qn.shape[1]
    nq = t // nb // tq
    knc, krc, vtc = ctx_kv
    lc = knc.shape[1]
    qrow = lambda b, h, i: (b * nq + i, h)
    in_specs = [pl.BlockSpec((LANES, tq), lambda b, h, i: (h, b * nq + i)),
                pl.BlockSpec((LANES, tq), lambda b, h, i: (h // 2, b * nq + i)),
                pl.BlockSpec((1, lc, LANES), lambda b, h, i: (b, 0, h)),
                pl.BlockSpec((1, lc, LANES), lambda b, h, i: (b, 0, h % 2)),
                pl.BlockSpec((1, 1, 1, MLA_V, lc), lambda b, h, i: (b, h, 0, 0, 0))]
    args = [qn, qr, knc, krc, vtc]
    n_lat, tk = 0, 0
    if lat_kv is not None:
        knl, krl, vtl = lat_kv
        s = knl.shape[1]
        n_lat, tk = vtl.shape[2], vtl.shape[4]
        assert n_lat % KV_UNROLL == 0 and tk >= lc
        in_specs += [pl.BlockSpec((1, s, LANES), lambda b, h, i: (b, 0, h)),
                     pl.BlockSpec((1, s, LANES), lambda b, h, i: (b, 0, h % 2)),
                     pl.BlockSpec((1, 1, n_lat, MLA_V, tk), lambda b, h, i: (b, h, 0, 0, 0))]
        args += [knl, krl, vtl]
    scratch = [pltpu.VMEM((MLA_V, tq), F32), pltpu.VMEM((2, max(KV_GROUP * tk, lc), tq), F32),
               pltpu.VMEM((2 * LANES, tq), BF16), pltpu.VMEM((8, LANES), F32)]
    return pl.pallas_call(
        functools.partial(_mla_attn_kernel, n_lat=n_lat, tk=tk),
        grid=(nb, MLA_HEADS, nq), in_specs=in_specs,
        out_specs=pl.BlockSpec((tq, MLA_V), qrow),
        out_shape=jax.ShapeDtypeStruct((t, MLA_HEADS * MLA_V), BF16),
        scratch_shapes=scratch,
        compiler_params=_params(("arbitrary", "arbitrary", "arbitrary")),
        name="mla_attn_lat" if n_lat else "mla_attn_ctx",
    )(*args)


def _outproj_kernel(o_ref, w_ref, x_ref, g_ref, gt_ref, out_ref):
    y = jnp.dot(o_ref[...], w_ref[...], preferred_element_type=F32)
    out_ref[...] = x_ref[...] + gt_ref[0] * _rms(y, g_ref[...])


def _outproj(o, w_out, x, g1, mod, grp, tm):
    t, d = x.shape
    row = lambda i: (i, 0)
    return pl.pallas_call(
        _outproj_kernel,
        grid=(t // tm,),
        in_specs=[pl.BlockSpec((tm, o.shape[1]), row), _const_spec(w_out.shape),
                  pl.BlockSpec((tm, d), row), _const_spec((1, d)), _mod_spec(grp, 2, d)],
        out_specs=pl.BlockSpec((tm, d), row),
        out_shape=jax.ShapeDtypeStruct((t, d), F32),
        compiler_params=_params(("arbitrary",)),
        name="outproj",
    )(o, w_out, x, g1, mod)


def _mlp_kernel(x_ref, g2_ref, sc_ref, sh_ref, w1_ref, w2_ref, g3_ref, gt_ref, out_ref,
                f_ref, acc_ref):
    k = pl.program_id(1)

    @pl.when(k == 0)
    def _():
        _modulated_norm(x_ref, g2_ref, sc_ref, sh_ref, f_ref)
        acc_ref[...] = jnp.zeros_like(acc_ref)

    u = jnp.maximum(jnp.dot(f_ref[...], w1_ref[...], preferred_element_type=F32), 0.0)
    acc_ref[...] += jnp.dot((u * u).astype(BF16), w2_ref[...], preferred_element_type=F32)

    @pl.when(k == pl.num_programs(1) - 1)
    def _():
        gain = gt_ref[0] * g3_ref[...]

        def rows(rs):
            y = acc_ref[rs, :]
            out_ref[rs, :] = x_ref[rs, :] + y * _inv_rms(y) * gain
        _row_chunks(acc_ref.shape[0], rows)


def _mlp(x, g2, g3, mod, grp, w1, w2, tm):
    t, d = x.shape
    dff = w1.shape[1]
    row = lambda i, k: (i, 0)
    return pl.pallas_call(
        _mlp_kernel,
        grid=(t // tm, dff // FF_TILE),
        in_specs=[pl.BlockSpec((tm, d), row), _const_spec((1, d)),
                  _mod_spec(grp, 4, d), _mod_spec(grp, 3, d),
                  pl.BlockSpec((d, FF_TILE), lambda i, k: (0, k)),
                  pl.BlockSpec((FF_TILE, d), lambda i, k: (k, 0)),
                  _const_spec((1, d)), _mod_spec(grp, 5, d)],
        out_specs=pl.BlockSpec((tm, d), row),
        out_shape=jax.ShapeDtypeStruct((t, d), F32),
        scratch_shapes=[pltpu.VMEM((tm, d), BF16), pltpu.VMEM((tm, d), F32)],
        compiler_params=_params(("arbitrary", "arbitrary")),
        name="mlp",
    )(x, g2, mod, mod, w1, w2, g3, mod)


def _swa_proj_kernel(*refs, rope, qscale):
    if rope:
        x_ref, g_ref, sc_ref, sh_ref, w_ref, wvt_ref, cos_ref, sin_ref, q_ref, k2_ref, vt_ref = refs
        cos, sin = cos_ref[...], sin_ref[...]
    else:
        x_ref, g_ref, sc_ref, sh_ref, w_ref, wvt_ref, q_ref, k2_ref, vt_ref = refs
    h = (_rms(x_ref[...], g_ref[...]) * (1.0 + sc_ref[0]) + sh_ref[0]).astype(BF16)
    p = jnp.dot(h, w_ref[...], preferred_element_type=F32)
    dq = q_ref.shape[1]
    for t in range(dq // LANES):
        tile = p[:, t * LANES:(t + 1) * LANES] * qscale
        if rope:
            tile = _rot(tile, cos, sin)
        q_ref[:, t * LANES:(t + 1) * LANES] = tile.astype(BF16)
    for c in range(SWA_KV_HEADS):
        tile = p[:, dq + c * LANES:dq + (c + 1) * LANES]
        if rope:
            tile = _rot(tile, cos, sin)
        first = _first_head_lanes(tile.shape)
        k2_ref[:, 2 * c * LANES:(2 * c + 1) * LANES] = jnp.where(first, tile, 0.0).astype(BF16)
        k2_ref[:, (2 * c + 1) * LANES:(2 * c + 2) * LANES] = jnp.where(first, 0.0, tile).astype(BF16)
    vt_ref[...] = lax.dot_general(wvt_ref[...], h, NT_DIMS,
                                  preferred_element_type=F32).astype(BF16)


def _swa_proj(x, mod, grp, g0, w, rope_tabs, tm, n_per_batch):
    t, d = x.shape
    rope = rope_tabs is not None
    row = lambda i: (i, 0)
    dq = SWA_HEADS * SWA_HEAD_DIM
    dkv = SWA_KV_HEADS * SWA_HEAD_DIM
    w_qk, w_vt = w
    in_specs = [pl.BlockSpec((tm, d), row), _const_spec((1, d)),
                _mod_spec(grp, 1, d), _mod_spec(grp, 0, d), _const_spec(w_qk.shape),
                _const_spec(w_vt.shape)]
    args = [x, g0, mod, mod, w_qk, w_vt]
    if rope:
        pos = lambda i: (i % n_per_batch, 0)
        in_specs += [pl.BlockSpec((tm, LANES), pos), pl.BlockSpec((tm, LANES), pos)]
        args += list(rope_tabs)
    out_shape = [jax.ShapeDtypeStruct((t, dq), BF16),
                 jax.ShapeDtypeStruct((t, 2 * SWA_KV_HEADS * LANES), BF16),
                 jax.ShapeDtypeStruct((dkv, t), BF16)]
    out_specs = [pl.BlockSpec((tm, dq), row), pl.BlockSpec((tm, 2 * SWA_KV_HEADS * LANES), row),
                 pl.BlockSpec((dkv, tm), lambda i: (0, i))]
    return pl.pallas_call(
        functools.partial(_swa_proj_kernel, rope=rope, qscale=SWA_HEAD_DIM ** -0.5 * LOG2E),
        grid=(t // tm,), in_specs=in_specs, out_specs=out_specs, out_shape=out_shape,
        compiler_params=_params(("arbitrary",)),
        name="swa_proj_lat" if rope else "swa_proj_ctx",
    )(*args)


def _swa_attn_kernel(sink_ref, q_ref, kc_ref, kp_ref, kcur_ref, kn_ref, vc_ref, vp_ref, vcur_ref,
                     vn_ref, o_ref, s_ref, *, qb):
    kvh = pl.program_id(1)
    i = pl.program_id(2)
    lc = kc_ref.shape[0]
    span = SWA_QBLK + 2 * SWA_WINDOW
    n_pairs = SWA_GROUP // 2
    kc, vc = kc_ref[...], vc_ref[...]
    kwin = jnp.concatenate([kp_ref[...], kcur_ref[...], kn_ref[...]], axis=0)
    vwin = jnp.concatenate([vp_ref[...], vcur_ref[...], vn_ref[...]], axis=1)
    r = lax.broadcasted_iota(jnp.int32, (lc + span, SWA_QBLK), 0)
    rel = r - lc - lax.broadcasted_iota(jnp.int32, (lc + span, SWA_QBLK), 1)
    bias = jnp.where((r < lc) | ((rel >= 0) & (rel <= 2 * SWA_WINDOW)), 0.0, NEG_BIG)
    bias = jnp.concatenate([bias] * n_pairs, axis=1)
    pair = lax.broadcasted_iota(jnp.int32, (1, n_pairs * SWA_QBLK), 1) // SWA_QBLK
    keep_prev = jnp.where(i == 0, 0.0, 1.0)
    keep_next = jnp.where(i == pl.num_programs(2) - 1, 0.0, 1.0)
    def scores(blk, e):
        qs = jnp.concatenate([q_ref[blk * SWA_QBLK:(blk + 1) * SWA_QBLK, t * LANES:(t + 1) * LANES]
                              for t in range(n_pairs)], axis=0)
        kcat = jnp.concatenate([kc[:, e * LANES:(e + 1) * LANES],
                                kwin[blk * SWA_QBLK:blk * SWA_QBLK + span,
                                     e * LANES:(e + 1) * LANES]], axis=0)
        return lax.dot_general(kcat, qs, NT_DIMS, preferred_element_type=F32)

    s_ref[0] = scores(0, 0)
    for blk in range(qb):
        rows = slice(blk * SWA_QBLK, (blk + 1) * SWA_QBLK)
        vt = jnp.concatenate([vc, vwin[:, blk * SWA_QBLK:blk * SWA_QBLK + span]], axis=1)
        halves = []
        for e in range(2):
            if e == 0:
                s_ref[1] = scores(blk, 1)
            elif blk + 1 < qb:
                s_ref[0] = scores(blk + 1, 0)
            s = s_ref[e] + bias
            sk = jnp.zeros(pair.shape, F32)
            for t in range(n_pairs):
                sk = jnp.where(pair == t, sink_ref[kvh * SWA_GROUP + 2 * t + e] * LOG2E, sk)
            m = jnp.maximum(jnp.max(s, axis=0, keepdims=True), sk)
            p = jnp.exp2(s - m)
            if blk == 0:
                p = jnp.concatenate([p[:lc], p[lc:lc + SWA_WINDOW] * keep_prev,
                                     p[lc + SWA_WINDOW:]], axis=0)
            if blk == qb - 1:
                p = jnp.concatenate([p[:lc + span - SWA_WINDOW],
                                     p[lc + span - SWA_WINDOW:] * keep_next], axis=0)
            den = jnp.sum(p, axis=0, keepdims=True) + jnp.exp2(sk - m)
            o = jnp.dot(vt, p.astype(BF16), preferred_element_type=F32)
            halves.append(o * (1.0 / den))
        both = jnp.concatenate(halves, axis=0)
        for t in range(n_pairs):
            tile = both[:, t * SWA_QBLK:(t + 1) * SWA_QBLK]
            o_ref[rows, t * LANES:(t + 1) * LANES] = tile.T.astype(BF16)


def _swa_attn(sink, q, k2, vt, k2c, vtc, nb, seq, lc):
    t = q.shape[0]
    nblk = seq // SWA_QBLK
    qb = min(SWA_STEP_BLOCKS, nblk)
    nsteps = nblk // qb
    gq = SWA_GROUP * SWA_HEAD_DIM
    hd = SWA_HEAD_DIM
    prev_blk = lambda b, i: b * nblk + jnp.maximum(qb * i - 1, 0)
    next_blk = lambda b, i: b * nblk + jnp.minimum(qb * i + qb, nblk - 1)
    return pl.pallas_call(
        functools.partial(_swa_attn_kernel, qb=qb),
        grid=(nb, SWA_KV_HEADS, nsteps),
        in_specs=[pl.BlockSpec(memory_space=pltpu.SMEM),
                  pl.BlockSpec((qb * SWA_QBLK, gq), lambda b, h, i: (b * nsteps + i, h)),
                  pl.BlockSpec((lc, 2 * LANES), lambda b, h, i: (b, h)),
                  pl.BlockSpec((SWA_QBLK, 2 * LANES), lambda b, h, i: (prev_blk(b, i), h)),
                  pl.BlockSpec((qb * SWA_QBLK, 2 * LANES), lambda b, h, i: (b * nsteps + i, h)),
                  pl.BlockSpec((SWA_QBLK, 2 * LANES), lambda b, h, i: (next_blk(b, i), h)),
                  pl.BlockSpec((hd, lc), lambda b, h, i: (h, b)),
                  pl.BlockSpec((hd, SWA_QBLK), lambda b, h, i: (h, prev_blk(b, i))),
                  pl.BlockSpec((hd, qb * SWA_QBLK), lambda b, h, i: (h, b * nsteps + i)),
                  pl.BlockSpec((hd, SWA_QBLK), lambda b, h, i: (h, next_blk(b, i)))],
        out_specs=pl.BlockSpec((qb * SWA_QBLK, gq), lambda b, h, i: (b * nsteps + i, h)),
        out_shape=jax.ShapeDtypeStruct((t, SWA_HEADS * SWA_HEAD_DIM), BF16),
        scratch_shapes=[pltpu.VMEM((2, lc + SWA_QBLK + 2 * SWA_WINDOW, gq), F32)],
        compiler_params=_params(("arbitrary", "arbitrary", "arbitrary")),
        name="swa_attn",
    )(sink, q, k2c, k2, k2, k2, vtc, vt, vt, vt)


def _rope_tables(seq):
    rows = seq // GRID_W
    row = jnp.repeat(jnp.arange(rows, dtype=F32), GRID_W)
    col = jnp.tile(jnp.arange(GRID_W, dtype=F32), rows)
    n_freq = MLA_ROPE // 4
    freqs = ROPE_BASE ** (-jnp.arange(n_freq, dtype=F32) / n_freq)
    ang = jnp.concatenate([row[:, None] * freqs, col[:, None] * freqs], axis=-1)
    cos, sin = jnp.cos(ang), jnp.sin(ang)
    return (jnp.concatenate([cos, cos, cos, cos], axis=-1),
            jnp.concatenate([-sin, -sin, sin, sin], axis=-1))


def _pair_tiles(w, n_heads, half):
    k = w.shape[0]
    x1 = w[:, :, :half].reshape(k, n_heads // 2, 2 * half)
    x2 = w[:, :, half:].reshape(k, n_heads // 2, 2 * half)
    return jnp.concatenate([x1, x2], axis=2).reshape(k, n_heads * 2 * half)


def _mla_weights(w_in, g_qa, g_kva, w_qb, w_kvb):
    half = MLA_ROPE // 2
    lat = MLA_Q_LORA + MLA_KV_LORA
    k1, k2 = w_in[:, lat:lat + half], w_in[:, lat + half:]
    qb = w_qb.reshape(MLA_Q_LORA, MLA_HEADS, MLA_NOPE + MLA_ROPE)
    kvb = w_kvb.reshape(MLA_KV_LORA, MLA_HEADS, MLA_NOPE + MLA_V)
    return {
        "w_in": jnp.concatenate([w_in[:, :lat], k1, k1, k2, k2], axis=1).astype(BF16),
        "g_qa": g_qa.reshape(1, -1), "g_kva": g_kva.reshape(1, -1),
        "w_qn": qb[:, :, :MLA_NOPE].reshape(MLA_Q_LORA, -1).T.astype(BF16),
        "w_qr": _pair_tiles(qb[:, :, MLA_NOPE:], MLA_HEADS, half).T.astype(BF16),
        "w_kn": kvb[:, :, :MLA_NOPE].reshape(MLA_KV_LORA, -1).astype(BF16),
        "w_vt": kvb[:, :, MLA_NOPE:].reshape(MLA_KV_LORA, -1).T.astype(BF16),
    }


def _swa_weights(w_qkv):
    d = w_qkv.shape[0]
    half = SWA_HEAD_DIM // 2
    dq = SWA_HEADS * SWA_HEAD_DIM
    dkv = SWA_KV_HEADS * SWA_HEAD_DIM
    q = _pair_tiles(w_qkv[:, :dq].reshape(d, SWA_HEADS, SWA_HEAD_DIM), SWA_HEADS, half)
    k = w_qkv[:, dq:dq + dkv].reshape(d, SWA_KV_HEADS, SWA_HEAD_DIM)
    k1, k2 = k[:, :, :half], k[:, :, half:]
    k = jnp.concatenate([k1, k1, k2, k2], axis=2).reshape(d, SWA_KV_HEADS * LANES)
    return jnp.concatenate([q, k], axis=1).astype(BF16), w_qkv[:, dq + dkv:].T.astype(BF16)


def kernel(x, c, ctx, c_ctx, w_mod, b_mod, g_norm, w_ff_in, w_ff_out, mla_w_in, mla_g_qa,
           mla_g_kva, mla_w_qb, mla_w_kvb, mla_w_out, swa_w_qkv, swa_sink, swa_w_out):
    nb, seq, d = x.shape
    lc = ctx.shape[1]
    depth = w_mod.shape[0]
    assert nb + 1 <= MOD_ROWS
    tm = min(TOKEN_TILE, seq)
    tq = min(ATTN_TQ, seq)
    n_per_batch = seq // tm
    grp_lat = lambda i: i // n_per_batch
    grp_ctx = lambda i: nb

    cmat = jnp.zeros((MOD_ROWS, d), F32).at[:nb].set(c).at[nb].set(c_ctx)
    mod_all = _modulation(cmat, w_mod, b_mod)
    rope_tabs = _rope_tables(seq)

    xl = x.reshape(nb * seq, d)
    xc = ctx.reshape(nb * lc, d)
    for i in range(depth):
        need_ctx = i < depth - 1
        mod = mod_all[i].reshape(MOD_ROWS, 1, 6 * d)
        g = g_norm[i].reshape(4, 1, d)
        j = i // 2
        if i % 2 == 0:
            w = _mla_weights(mla_w_in[j], mla_g_qa[j], mla_g_kva[j], mla_w_qb[j], mla_w_kvb[j])
            w_out = mla_w_out[j].astype(BF16)
            qn, qr, kn, kr, vt = _mla_proj(xl, mod, grp_lat, g[0], w, rope_tabs, tm, n_per_batch)
            qnc, qrc, knc, krc, vtc = _mla_proj(xc, mod, grp_ctx, g[0], w, None, lc, 1)
            ctx_kv = (knc.reshape(nb, lc, -1), krc.reshape(nb, lc, -1), vtc)
            lat_kv = (kn.reshape(nb, seq, -1), kr.reshape(nb, seq, -1), vt)
            o_l = _mla_attn(qn, qr, ctx_kv, lat_kv, nb, tq)
            o_c = _mla_attn(qnc, qrc, ctx_kv, None, nb, lc) if need_ctx else None
        else:
            w = _swa_weights(swa_w_qkv[j])
            w_out = swa_w_out[j].astype(BF16)
            q, k2, vt = _swa_proj(xl, mod, grp_lat, g[0], w, rope_tabs, tm, n_per_batch)
            qc, k2c, vtc = _swa_proj(xc, mod, grp_ctx, g[0], w, None, lc, 1)
            o_l = _swa_attn(swa_sink[j], q, k2, vt, k2c, vtc, nb, seq, lc)
            assert not need_ctx
            o_c = None
        w1 = w_ff_in[i].astype(BF16)
        w2 = w_ff_out[i].astype(BF16)
        xl = _outproj(o_l, w_out, xl, g[1], mod, grp_lat, tm)
        xl = _mlp(xl, g[2], g[3], mod, grp_lat, w1, w2, tm)
        if need_ctx:
            xc = _outproj(o_c, w_out, xc, g[1], mod, grp_ctx, lc)
            xc = _mlp(xc, g[2], g[3], mod, grp_ctx, w1, w2, lc)
    return xl.reshape(nb, seq, d)
```

```python
import functools
import math

import jax
import jax.numpy as jnp
from jax import lax
from jax.experimental import pallas as pl
from jax.experimental.pallas import tpu as pltpu

F32 = jnp.float32
BF16 = jnp.bfloat16

GRID_W = 64
ROPE_BASE = 10000.0
NORM_EPS = 1e-6
LOG2E = math.log2(math.e)
NEG_BIG = -1e30
REF_LIMIT = 60.0
NORM_SLACK = 1.03

MLA_HEADS = 16
MLA_Q_LORA = 512
MLA_KV_LORA = 512
MLA_NOPE = 128
MLA_ROPE = 64
MLA_V = 128

SWA_HEADS = 32
SWA_KV_HEADS = 4
SWA_HEAD_DIM = 64
SWA_WINDOW = 128
SWA_GROUP = SWA_HEADS // SWA_KV_HEADS
SWA_QBLK = 128
SWA_STEP_BLOCKS = 4

LANES = 128
HALF_TILE = 64

MOD_ROWS = 8
MOD_TN = 1024
TOKEN_TILE = 512
ROW_CHUNK = 32
ROW_UNROLL = 4
FF_TILE = 1024
ATTN_TQ = 512
KV_GROUP = 8
KV_UNROLL = 4
VMEM_LIMIT = 56 * 1024 * 1024

NT_DIMS = (((1,), (1,)), ((), ()))


def _rms(xf, g):
    ms = jnp.mean(xf * xf, axis=-1, keepdims=True)
    return xf * lax.rsqrt(ms + NORM_EPS) * g


def _inv_rms(xf):
    return lax.rsqrt(jnp.mean(xf * xf, axis=-1, keepdims=True) + NORM_EPS)


def _row_chunks(n_rows, fn):
    def body(i, carry):
        fn(pl.ds(pl.multiple_of(i * ROW_CHUNK, ROW_CHUNK), ROW_CHUNK))
        return carry
    lax.fori_loop(0, n_rows // ROW_CHUNK, body, 0, unroll=ROW_UNROLL)


def _modulated_norm(x_ref, g_ref, sc_ref, sh_ref, h_ref):
    gain = g_ref[...] * (1.0 + sc_ref[0])
    shift = sh_ref[0]

    def rows(rs):
        xf = x_ref[rs, :]
        h_ref[rs, :] = (xf * _inv_rms(xf) * gain + shift).astype(h_ref.dtype)
    _row_chunks(h_ref.shape[0], rows)


def _rot(tile, cos, sin):
    return tile * cos + pltpu.roll(tile, HALF_TILE, 1) * sin


def _first_head_lanes(shape):
    lane = lax.broadcasted_iota(jnp.int32, shape, 1)
    return (lane & 32) == 0


def _shift_lane(parity):
    return LANES - 1 - 32 * parity


def _params(sem):
    return pltpu.CompilerParams(dimension_semantics=sem, vmem_limit_bytes=VMEM_LIMIT)


def _const_spec(shape):
    nd = len(shape)
    return pl.BlockSpec(shape, lambda *_: (0,) * nd, pipeline_mode=pl.Buffered(1))


def _mod_spec(grp, which, d):
    return pl.BlockSpec((1, 1, d), lambda i, *_: (grp(i), 0, which))


def _mod_kernel(c_ref, w_ref, b_ref, o_ref):
    c = c_ref[...]
    a = c / (1.0 + jnp.exp(-c))
    o_ref[0] = jnp.dot(a, w_ref[0], preferred_element_type=F32,
                       precision=lax.Precision.HIGHEST) + b_ref[0]


def _modulation(cmat, w_mod, b_mod):
    depth, d, n = w_mod.shape
    return pl.pallas_call(
        _mod_kernel,
        grid=(depth, n // MOD_TN),
        in_specs=[pl.BlockSpec((MOD_ROWS, d), lambda l, j: (0, 0)),
                  pl.BlockSpec((1, d, MOD_TN), lambda l, j: (l, 0, j)),
                  pl.BlockSpec((1, 1, MOD_TN), lambda l, j: (l, 0, j))],
        out_specs=pl.BlockSpec((1, MOD_ROWS, MOD_TN), lambda l, j: (l, 0, j)),
        out_shape=jax.ShapeDtypeStruct((depth, MOD_ROWS, n), F32),
        compiler_params=_params(("arbitrary", "arbitrary")),
        name="modulation",
    )(cmat, w_mod, b_mod.reshape(depth, 1, n))


def _mla_proj_kernel(*refs, rope, qscale):
    if rope:
        (x_ref, g_ref, sc_ref, sh_ref, win_ref, gqa_ref, gkva_ref, wqn_ref, wqr_ref, wkn_ref,
         wvt_ref, cos_ref, sin_ref, cost_ref, sint_ref, qn_ref, qr_ref, kn_ref, kr_ref,
         vt_ref) = refs
        cos, sin = cos_ref[...], sin_ref[...]
    else:
        (x_ref, g_ref, sc_ref, sh_ref, win_ref, gqa_ref, gkva_ref, wqn_ref, wqr_ref, wkn_ref,
         wvt_ref, qn_ref, qr_ref, kn_ref, kr_ref, vt_ref) = refs
    h = (_rms(x_ref[...], g_ref[...]) * (1.0 + sc_ref[0]) + sh_ref[0]).astype(BF16)
    p = jnp.dot(h, win_ref[...], preferred_element_type=F32)
    qa = _rms(p[:, :MLA_Q_LORA], gqa_ref[...]).astype(BF16)
    ckv = _rms(p[:, MLA_Q_LORA:MLA_Q_LORA + MLA_KV_LORA], gkva_ref[...]).astype(BF16)
    kr = p[:, MLA_Q_LORA + MLA_KV_LORA:]

    qn = lax.dot_general(wqn_ref[...], qa, NT_DIMS, preferred_element_type=F32) * qscale
    qn_ref[...] = qn.astype(BF16)
    qr = lax.dot_general(wqr_ref[...], qa, NT_DIMS, preferred_element_type=F32) * qscale
    for t in range(qr.shape[0] // LANES):
        tile = qr[t * LANES:(t + 1) * LANES, :]
        if rope:
            swapped = jnp.concatenate([tile[HALF_TILE:], tile[:HALF_TILE]], axis=0)
            tile = tile * cost_ref[...] + swapped * sint_ref[...]
        qr_ref[t * LANES:(t + 1) * LANES, :] = tile.astype(BF16)

    if rope:
        kr = _rot(kr, cos, sin)
    first = _first_head_lanes(kr.shape)
    lane = lax.broadcasted_iota(jnp.int32, kr.shape, 1)
    kr_ref[:, :LANES] = jnp.where(lane == _shift_lane(0), 1.0,
                                  jnp.where(first, kr, 0.0)).astype(BF16)
    kr_ref[:, LANES:] = jnp.where(lane == _shift_lane(1), 1.0,
                                  jnp.where(first, 0.0, kr)).astype(BF16)

    kn_ref[...] = jnp.dot(ckv, wkn_ref[...], preferred_element_type=F32).astype(BF16)
    vt = lax.dot_general(wvt_ref[...], ckv, NT_DIMS, preferred_element_type=F32).astype(BF16)
    for hd in range(MLA_HEADS):
        vt_ref[0, hd, 0] = vt[hd * MLA_V:(hd + 1) * MLA_V, :]


def _mla_proj(x, mod, grp, g0, w, rope_tabs, tm, n_per_batch):
    t, d = x.shape
    n_tiles = t // tm
    nb = n_tiles // n_per_batch
    rope = rope_tabs is not None
    qscale = (MLA_NOPE + MLA_ROPE) ** -0.5 * LOG2E
    row = lambda i: (i, 0)
    in_specs = [pl.BlockSpec((tm, d), row), _const_spec((1, d)),
                _mod_spec(grp, 1, d), _mod_spec(grp, 0, d),
                _const_spec(w["w_in"].shape), _const_spec((1, MLA_Q_LORA)),
                _const_spec((1, MLA_KV_LORA)), _const_spec(w["w_qn"].shape),
                _const_spec(w["w_qr"].shape), _const_spec(w["w_kn"].shape),
                _const_spec(w["w_vt"].shape)]
    args = [x, g0, mod, mod, w["w_in"], w["g_qa"], w["g_kva"], w["w_qn"], w["w_qr"], w["w_kn"],
            w["w_vt"]]
    if rope:
        pos = lambda i: (i % n_per_batch, 0)
        pos_t = lambda i: (0, i % n_per_batch)
        in_specs += [pl.BlockSpec((tm, LANES), pos), pl.BlockSpec((tm, LANES), pos),
                     pl.BlockSpec((LANES, tm), pos_t), pl.BlockSpec((LANES, tm), pos_t)]
        args += list(rope_tabs)
    hn = MLA_HEADS * MLA_NOPE
    hr = MLA_HEADS * MLA_ROPE
    col = lambda i: (0, i)
    out_shape = [jax.ShapeDtypeStruct((hn, t), BF16), jax.ShapeDtypeStruct((hr, t), BF16),
                 jax.ShapeDtypeStruct((t, hn), BF16), jax.ShapeDtypeStruct((t, 2 * LANES), BF16),
                 jax.ShapeDtypeStruct((nb, MLA_HEADS, n_per_batch, MLA_V, tm), BF16)]
    out_specs = [pl.BlockSpec((hn, tm), col), pl.BlockSpec((hr, tm), col),
                 pl.BlockSpec((tm, hn), row), pl.BlockSpec((tm, 2 * LANES), row),
                 pl.BlockSpec((1, MLA_HEADS, 1, MLA_V, tm),
                              lambda i: (i // n_per_batch, 0, i % n_per_batch, 0, 0))]
    return pl.pallas_call(
        functools.partial(_mla_proj_kernel, rope=rope, qscale=qscale),
        grid=(n_tiles,), in_specs=in_specs, out_specs=out_specs, out_shape=out_shape,
        compiler_params=_params(("arbitrary",)),
        name="mla_proj_lat" if rope else "mla_proj_ctx",
    )(*args)


def _kv_group_sizes(n_chunks):
    if n_chunks <= 2:
        return [n_chunks] if n_chunks else []
    tail = [min(KV_GROUP, n_chunks) - 2, 2]
    body = n_chunks - sum(tail)
    assert body % KV_GROUP == 0
    return [KV_GROUP] * (body // KV_GROUP) + tail


def _mla_attn_kernel(*refs, n_lat, tk):
    if n_lat:
        (qn_ref, qr_ref, knc_ref, krc_ref, vtc_ref, knl_ref, krl_ref, vtl_ref, o_ref,
         acc_ref, s_ref, qa_ref, k2_ref) = refs
    else:
        qn_ref, qr_ref, knc_ref, krc_ref, vtc_ref, o_ref, acc_ref, s_ref, qa_ref, k2_ref = refs
    tq = qn_ref.shape[1]
    lc = knc_ref.shape[1]
    parity = pl.program_id(1) % 2

    def lat_keys(j):
        off = pl.multiple_of(j * tk, tk)
        return knl_ref[0, pl.ds(off, tk), :], krl_ref[0, pl.ds(off, tk), :]

    @pl.when(pl.program_id(2) == 0)
    def _():
        ones = jnp.ones((LANES, LANES), BF16)

        def sqnorm_max(kn, kr):
            r = jnp.dot(kn * kn + kr * kr, ones, preferred_element_type=F32)
            return jnp.max(r, axis=0, keepdims=True)
        mx = sqnorm_max(knc_ref[0], krc_ref[0])
        if n_lat:
            mx = lax.fori_loop(0, n_lat, lambda j, c: jnp.maximum(c, sqnorm_max(*lat_keys(j))), mx,
                               unroll=KV_UNROLL)
        k2_ref[...] = jnp.broadcast_to(mx * NORM_SLACK, k2_ref.shape)

    qn = qn_ref[...]
    qr = qr_ref[...]
    feat = lax.broadcasted_iota(jnp.int32, qr.shape, 0)
    own = ((feat >> 5) & 1) == parity
    qnf, qrf = qn.astype(F32), jnp.where(own, qr.astype(F32), 0.0)
    q2 = jnp.sum(qnf * qnf + qrf * qrf, axis=0, keepdims=True)
    ref = jnp.sqrt(q2 * k2_ref[0:1, 0:1])
    fast = jnp.max(ref) <= REF_LIMIT
    shift_row = feat == LANES - 1 - 32 * parity
    qa_ref[:LANES, :] = qn
    qa_ref[LANES:, :] = jnp.where(shift_row, (-ref).astype(BF16), qr)

    def scores(kn, kr):
        return jnp.dot(jnp.concatenate([kn, kr], axis=1), qa_ref[...],
                       preferred_element_type=F32)

    def finish(l):
        o_ref[...] = (acc_ref[...] * (1.0 / l)).T.astype(BF16)

    def shifted_softmax():
        def chunk(s, vt, l8):
            p = jnp.exp2(s)
            l8 = l8 + jnp.sum(p.reshape(-1, 8, tq), axis=0)
            return l8, jnp.dot(vt, p.astype(BF16), preferred_element_type=F32)

        sizes = _kv_group_sizes(n_lat)
        starts = [sum(sizes[:g]) for g in range(len(sizes))]
        last = max(len(sizes) - 1, 0)

        def n_rows(g):
            return (sizes[g] * tk if sizes else 0) + (lc if g == last else 0)

        def fill_scores(g):
            view = s_ref.at[g % 2]
            lat = sizes[g] * tk if sizes else 0
            if lat:
                rows = pl.ds(starts[g] * tk, lat)
                view[pl.ds(0, lat), :] = scores(knl_ref[0, rows, :], krl_ref[0, rows, :])
            if g == last:
                view[pl.ds(lat, lc), :] = scores(knc_ref[0], krc_ref[0])

        fill_scores(0)
        l8 = jnp.zeros((8, tq), F32)
        for g in range(last + 1):
            if g < last:
                fill_scores(g + 1)
            vts = [vtl_ref[0, 0, starts[g] + u] for u in range(sizes[g])] if sizes else []
            if g == last:
                vts.append(vtc_ref[0, 0, 0])
            l8, pv = chunk(s_ref[g % 2, pl.ds(0, n_rows(g)), :], jnp.concatenate(vts, axis=1), l8)
            if g == 0:
                acc_ref[...] = pv
            else:
                acc_ref[...] += pv
        finish(jnp.sum(l8, axis=0, keepdims=True))

    def online_softmax():
        def update(s_view, vt, m, l):
            m_new = jnp.maximum(m, jnp.max(s_view[...], axis=0, keepdims=True))
            alpha = jnp.exp2(m - m_new)
            p = jnp.exp2(s_view[...] - m_new)
            l_new = alpha * l + jnp.sum(p, axis=0, keepdims=True)
            acc_ref[...] = acc_ref[...] * alpha + jnp.dot(vt, p.astype(BF16),
                                                          preferred_element_type=F32)
            return m_new, l_new

        acc_ref[...] = jnp.zeros_like(acc_ref)
        m = jnp.full((1, tq), NEG_BIG, F32)
        l = jnp.zeros((1, tq), F32)
        slots = [s_ref.at[0, pl.ds(0, tk)], s_ref.at[1, pl.ds(0, tk)]] if n_lat else None
        if n_lat:
            slots[0][...] = scores(*lat_keys(0))
        ctx_view = s_ref.at[1, pl.ds(0, lc)]
        ctx_view[...] = scores(knc_ref[0], krc_ref[0])
        m, l = update(ctx_view, vtc_ref[0, 0, 0], m, l)
        if n_lat:
            def body(jj, carry):
                for u in range(KV_UNROLL):
                    j = KV_UNROLL * jj + u
                    slots[(u + 1) % 2][...] = scores(*lat_keys(jnp.minimum(j + 1, n_lat - 1)))
                    carry = update(slots[u % 2], vtl_ref[0, 0, j], *carry)
                return carry
            m, l = lax.fori_loop(0, n_lat // KV_UNROLL, body, (m, l))
        finish(l)

    shifted_softmax()

    @pl.when(jnp.logical_not(fast))
    def _():
        qa_ref[LANES:, :] = jnp.where(shift_row, jnp.zeros_like(qr), qr)
        online_softmax()


def _mla_attn(qn, qr, ctx_kv, lat_kv, nb, tq):
    t = qn.shape[1]
    nq = t // nb // tq
    knc, krc, vtc = ctx_kv
    lc = knc.shape[1]
    qrow = lambda b, h, i: (b * nq + i, h)
    in_specs = [pl.BlockSpec((LANES, tq), lambda b, h, i: (h, b * nq + i)),
                pl.BlockSpec((LANES, tq), lambda b, h, i: (h // 2, b * nq + i)),
                pl.BlockSpec((1, lc, LANES), lambda b, h, i: (b, 0, h)),
                pl.BlockSpec((1, lc, LANES), lambda b, h, i: (b, 0, h % 2)),
                pl.BlockSpec((1, 1, 1, MLA_V, lc), lambda b, h, i: (b, h, 0, 0, 0))]
    args = [qn, qr, knc, krc, vtc]
    n_lat, tk = 0, 0
    if lat_kv is not None:
        knl, krl, vtl = lat_kv
        s = knl.shape[1]
        n_lat, tk = vtl.shape[2], vtl.shape[4]
        assert n_lat % KV_UNROLL == 0 and tk >= lc
        in_specs += [pl.BlockSpec((1, s, LANES), lambda b, h, i: (b, 0, h)),
                     pl.BlockSpec((1, s, LANES), lambda b, h, i: (b, 0, h % 2)),
                     pl.BlockSpec((1, 1, n_lat, MLA_V, tk), lambda b, h, i: (b, h, 0, 0, 0))]
        args += [knl, krl, vtl]
    scratch = [pltpu.VMEM((MLA_V, tq), F32), pltpu.VMEM((2, max(KV_GROUP * tk, lc), tq), F32),
               pltpu.VMEM((2 * LANES, tq), BF16), pltpu.VMEM((8, LANES), F32)]
    return pl.pallas_call(
        functools.partial(_mla_attn_kernel, n_lat=n_lat, tk=tk),
        grid=(nb, MLA_HEADS, nq), in_specs=in_specs,
        out_specs=pl.BlockSpec((tq, MLA_V), qrow),
        out_shape=jax.ShapeDtypeStruct((t, MLA_HEADS * MLA_V), BF16),
        scratch_shapes=scratch,
        compiler_params=_params(("arbitrary", "arbitrary", "arbitrary")),
        name="mla_attn_lat" if n_lat else "mla_attn_ctx",
    )(*args)


def _outproj_kernel(o_ref, w_ref, x_ref, g_ref, gt_ref, out_ref):
    y = jnp.dot(o_ref[...], w_ref[...], preferred_element_type=F32)
    out_ref[...] = x_ref[...] + gt_ref[0] * _rms(y, g_ref[...])


def _outproj(o, w_out, x, g1, mod, grp, tm):
    t, d = x.shape
    row = lambda i: (i, 0)
    return pl.pallas_call(
        _outproj_kernel,
        grid=(t // tm,),
        in_specs=[pl.BlockSpec((tm, o.shape[1]), row), _const_spec(w_out.shape),
                  pl.BlockSpec((tm, d), row), _const_spec((1, d)), _mod_spec(grp, 2, d)],
        out_specs=pl.BlockSpec((tm, d), row),
        out_shape=jax.ShapeDtypeStruct((t, d), F32),
        compiler_params=_params(("arbitrary",)),
        name="outproj",
    )(o, w_out, x, g1, mod)


def _mlp_kernel(x_ref, g2_ref, sc_ref, sh_ref, w1_ref, w2_ref, g3_ref, gt_ref, out_ref,
                f_ref, acc_ref):
    k = pl.program_id(1)

    @pl.when(k == 0)
    def _():
        _modulated_norm(x_ref, g2_ref, sc_ref, sh_ref, f_ref)
        acc_ref[...] = jnp.zeros_like(acc_ref)

    u = jnp.maximum(jnp.dot(f_ref[...], w1_ref[...], preferred_element_type=F32), 0.0)
    acc_ref[...] += jnp.dot((u * u).astype(BF16), w2_ref[...], preferred_element_type=F32)

    @pl.when(k == pl.num_programs(1) - 1)
    def _():
        gain = gt_ref[0] * g3_ref[...]

        def rows(rs):
            y = acc_ref[rs, :]
            out_ref[rs, :] = x_ref[rs, :] + y * _inv_rms(y) * gain
        _row_chunks(acc_ref.shape[0], rows)


def _mlp(x, g2, g3, mod, grp, w1, w2, tm):
    t, d = x.shape
    dff = w1.shape[1]
    row = lambda i, k: (i, 0)
    return pl.pallas_call(
        _mlp_kernel,
        grid=(t // tm, dff // FF_TILE),
        in_specs=[pl.BlockSpec((tm, d), row), _const_spec((1, d)),
                  _mod_spec(grp, 4, d), _mod_spec(grp, 3, d),
                  pl.BlockSpec((d, FF_TILE), lambda i, k: (0, k)),
                  pl.BlockSpec((FF_TILE, d), lambda i, k: (k, 0)),
                  _const_spec((1, d)), _mod_spec(grp, 5, d)],
        out_specs=pl.BlockSpec((tm, d), row),
        out_shape=jax.ShapeDtypeStruct((t, d), F32),
        scratch_shapes=[pltpu.VMEM((tm, d), BF16), pltpu.VMEM((tm, d), F32)],
        compiler_params=_params(("arbitrary", "arbitrary")),
        name="mlp",
    )(x, g2, mod, mod, w1, w2, g3, mod)


def _swa_proj_kernel(*refs, rope, qscale):
    if rope:
        (x_ref, g_ref, sc_ref, sh_ref, wqt_ref, wk_ref, wvt_ref, cos_ref, sin_ref, cost_ref,
         sint_ref, q_ref, k2_ref, vt_ref) = refs
        cos, sin = cos_ref[...], sin_ref[...]
    else:
        x_ref, g_ref, sc_ref, sh_ref, wqt_ref, wk_ref, wvt_ref, q_ref, k2_ref, vt_ref = refs
    h = (_rms(x_ref[...], g_ref[...]) * (1.0 + sc_ref[0]) + sh_ref[0]).astype(BF16)
    qt = lax.dot_general(wqt_ref[...], h, NT_DIMS, preferred_element_type=F32) * qscale
    for t in range(qt.shape[0] // LANES):
        tile = qt[t * LANES:(t + 1) * LANES, :]
        if rope:
            swapped = jnp.concatenate([tile[HALF_TILE:], tile[:HALF_TILE]], axis=0)
            tile = tile * cost_ref[...] + swapped * sint_ref[...]
        q_ref[t * LANES:(t + 1) * LANES, :] = tile.astype(BF16)
    k = jnp.dot(h, wk_ref[...], preferred_element_type=F32)
    for c in range(SWA_KV_HEADS):
        tile = k[:, c * LANES:(c + 1) * LANES]
        if rope:
            tile = _rot(tile, cos, sin)
        first = _first_head_lanes(tile.shape)
        lane = lax.broadcasted_iota(jnp.int32, tile.shape, 1)
        k2_ref[:, 2 * c * LANES:(2 * c + 1) * LANES] = jnp.where(
            lane == _shift_lane(0), 1.0, jnp.where(first, tile, 0.0)).astype(BF16)
        k2_ref[:, (2 * c + 1) * LANES:(2 * c + 2) * LANES] = jnp.where(
            lane == _shift_lane(1), 1.0, jnp.where(first, 0.0, tile)).astype(BF16)
    vt_ref[...] = lax.dot_general(wvt_ref[...], h, NT_DIMS,
                                  preferred_element_type=F32).astype(BF16)


def _swa_proj(x, mod, grp, g0, w, rope_tabs, tm, n_per_batch):
    t, d = x.shape
    rope = rope_tabs is not None
    row = lambda i: (i, 0)
    dq = SWA_HEADS * SWA_HEAD_DIM
    dkv = SWA_KV_HEADS * SWA_HEAD_DIM
    w_qt, w_k, w_vt = w
    col = lambda i: (0, i)
    in_specs = [pl.BlockSpec((tm, d), row), _const_spec((1, d)),
                _mod_spec(grp, 1, d), _mod_spec(grp, 0, d), _const_spec(w_qt.shape),
                _const_spec(w_k.shape), _const_spec(w_vt.shape)]
    args = [x, g0, mod, mod, w_qt, w_k, w_vt]
    if rope:
        pos = lambda i: (i % n_per_batch, 0)
        pos_t = lambda i: (0, i % n_per_batch)
        in_specs += [pl.BlockSpec((tm, LANES), pos), pl.BlockSpec((tm, LANES), pos),
                     pl.BlockSpec((LANES, tm), pos_t), pl.BlockSpec((LANES, tm), pos_t)]
        args += list(rope_tabs)
    out_shape = [jax.ShapeDtypeStruct((dq, t), BF16),
                 jax.ShapeDtypeStruct((t, 2 * SWA_KV_HEADS * LANES), BF16),
                 jax.ShapeDtypeStruct((dkv, t), BF16)]
    out_specs = [pl.BlockSpec((dq, tm), col), pl.BlockSpec((tm, 2 * SWA_KV_HEADS * LANES), row),
                 pl.BlockSpec((dkv, tm), col)]
    return pl.pallas_call(
        functools.partial(_swa_proj_kernel, rope=rope, qscale=SWA_HEAD_DIM ** -0.5 * LOG2E),
        grid=(t // tm,), in_specs=in_specs, out_specs=out_specs, out_shape=out_shape,
        compiler_params=_params(("arbitrary",)),
        name="swa_proj_lat" if rope else "swa_proj_ctx",
    )(*args)


def _swa_attn_kernel(sink_ref, q_ref, kc_ref, kp_ref, kcur_ref, kn_ref, vc_ref, vp_ref, vcur_ref,
                     vn_ref, o_ref, s_ref, qa_ref, *, qb):
    kvh = pl.program_id(1)
    i = pl.program_id(2)
    lc = kc_ref.shape[0]
    span = SWA_QBLK + 2 * SWA_WINDOW
    n_pairs = SWA_GROUP // 2
    kc, vc = kc_ref[...], vc_ref[...]
    kwin = jnp.concatenate([kp_ref[...], kcur_ref[...], kn_ref[...]], axis=0)
    vwin = jnp.concatenate([vp_ref[...], vcur_ref[...], vn_ref[...]], axis=1)
    r = lax.broadcasted_iota(jnp.int32, (lc + span, SWA_QBLK), 0)
    rel = r - lc - lax.broadcasted_iota(jnp.int32, (lc + span, SWA_QBLK), 1)
    bias = jnp.where((r < lc) | ((rel >= 0) & (rel <= 2 * SWA_WINDOW)), 0.0, NEG_BIG)
    bias = jnp.concatenate([bias] * n_pairs, axis=1)
    pair = lax.broadcasted_iota(jnp.int32, (1, n_pairs * SWA_QBLK), 1) // SWA_QBLK
    keep_prev = jnp.where(i == 0, 0.0, 1.0)
    keep_next = jnp.where(i == pl.num_programs(2) - 1, 0.0, 1.0)

    ones = jnp.ones((LANES, LANES), BF16)
    kall = jnp.concatenate([kc[:, :LANES], kwin[:, :LANES]], axis=0)
    k2 = jnp.max(jnp.dot(kall * kall, ones, preferred_element_type=F32), axis=0, keepdims=True)
    k2 = k2[:, 0:1] * NORM_SLACK
    feat = lax.broadcasted_iota(jnp.int32, (LANES, n_pairs * SWA_QBLK), 0)
    first_rows = (feat & 32) == 0
    q_t, refs = [], []
    for blk in range(qb):
        qt = jnp.concatenate([q_ref[t * LANES:(t + 1) * LANES, blk * SWA_QBLK:(blk + 1) * SWA_QBLK]
                              for t in range(n_pairs)], axis=1)
        sq = qt.astype(F32)
        sq = sq * sq
        q2_first = jnp.sum(jnp.where(first_rows, sq, 0.0), axis=0, keepdims=True)
        q2_second = jnp.sum(sq, axis=0, keepdims=True) - q2_first
        q_t.append(qt)
        refs.append([jnp.sqrt(q2 * k2).astype(BF16) for q2 in (q2_first, q2_second)])
    ref_max = functools.reduce(jnp.maximum, [jnp.max(r.astype(F32)) for pr in refs for r in pr])
    fast = ref_max <= REF_LIMIT
    def set_queries(shifted):
        for blk in range(qb):
            for e in range(2):
                shift = -refs[blk][e] if shifted else jnp.zeros_like(refs[blk][e])
                qa_ref[blk, e] = jnp.where(feat == _shift_lane(e), shift, q_t[blk])

    def scores(blk, e):
        kcat = jnp.concatenate([kc[:, e * LANES:(e + 1) * LANES],
                                kwin[blk * SWA_QBLK:blk * SWA_QBLK + span,
                                     e * LANES:(e + 1) * LANES]], axis=0)
        return jnp.dot(kcat, qa_ref[blk, e], preferred_element_type=F32)

    def run(shifted):
        s_ref[0] = scores(0, 0)
        for blk in range(qb):
            rows = slice(blk * SWA_QBLK, (blk + 1) * SWA_QBLK)
            vt = jnp.concatenate([vc, vwin[:, blk * SWA_QBLK:blk * SWA_QBLK + span]], axis=1)
            halves = []
            for e in range(2):
                if e == 0:
                    s_ref[1] = scores(blk, 1)
                elif blk + 1 < qb:
                    s_ref[0] = scores(blk + 1, 0)
                s = s_ref[e] + bias
                sk = jnp.zeros(pair.shape, F32)
                for t in range(n_pairs):
                    sk = jnp.where(pair == t, sink_ref[kvh * SWA_GROUP + 2 * t + e] * LOG2E, sk)
                if shifted:
                    m = refs[blk][e].astype(F32)
                    p = jnp.exp2(s)
                else:
                    m = jnp.maximum(jnp.max(s, axis=0, keepdims=True), sk)
                    p = jnp.exp2(s - m)
                if blk == 0:
                    p = jnp.concatenate([p[:lc], p[lc:lc + SWA_WINDOW] * keep_prev,
                                         p[lc + SWA_WINDOW:]], axis=0)
                if blk == qb - 1:
                    p = jnp.concatenate([p[:lc + span - SWA_WINDOW],
                                         p[lc + span - SWA_WINDOW:] * keep_next], axis=0)
                den = jnp.sum(p, axis=0, keepdims=True) + jnp.exp2(sk - m)
                o = jnp.dot(vt, p.astype(BF16), preferred_element_type=F32)
                halves.append(o * (1.0 / den))
            both = jnp.concatenate(halves, axis=0)
            for t in range(n_pairs):
                tile = both[:, t * SWA_QBLK:(t + 1) * SWA_QBLK]
                o_ref[rows, t * LANES:(t + 1) * LANES] = tile.T.astype(BF16)

    set_queries(True)
    run(True)

    @pl.when(jnp.logical_not(fast))
    def _():
        set_queries(False)
        run(False)


def _swa_attn(sink, q, k2, vt, k2c, vtc, nb, seq, lc):
    t = q.shape[1]
    nblk = seq // SWA_QBLK
    qb = min(SWA_STEP_BLOCKS, nblk)
    nsteps = nblk // qb
    gq = SWA_GROUP * SWA_HEAD_DIM
    hd = SWA_HEAD_DIM
    prev_blk = lambda b, i: b * nblk + jnp.maximum(qb * i - 1, 0)
    next_blk = lambda b, i: b * nblk + jnp.minimum(qb * i + qb, nblk - 1)
    return pl.pallas_call(
        functools.partial(_swa_attn_kernel, qb=qb),
        grid=(nb, SWA_KV_HEADS, nsteps),
        in_specs=[pl.BlockSpec(memory_space=pltpu.SMEM),
                  pl.BlockSpec((gq, qb * SWA_QBLK), lambda b, h, i: (h, b * nsteps + i)),
                  pl.BlockSpec((lc, 2 * LANES), lambda b, h, i: (b, h)),
                  pl.BlockSpec((SWA_QBLK, 2 * LANES), lambda b, h, i: (prev_blk(b, i), h)),
                  pl.BlockSpec((qb * SWA_QBLK, 2 * LANES), lambda b, h, i: (b * nsteps + i, h)),
                  pl.BlockSpec((SWA_QBLK, 2 * LANES), lambda b, h, i: (next_blk(b, i), h)),
                  pl.BlockSpec((hd, lc), lambda b, h, i: (h, b)),
                  pl.BlockSpec((hd, SWA_QBLK), lambda b, h, i: (h, prev_blk(b, i))),
                  pl.BlockSpec((hd, qb * SWA_QBLK), lambda b, h, i: (h, b * nsteps + i)),
                  pl.BlockSpec((hd, SWA_QBLK), lambda b, h, i: (h, next_blk(b, i)))],
        out_specs=pl.BlockSpec((qb * SWA_QBLK, gq), lambda b, h, i: (b * nsteps + i, h)),
        out_shape=jax.ShapeDtypeStruct((t, SWA_HEADS * SWA_HEAD_DIM), BF16),
        scratch_shapes=[pltpu.VMEM((2, lc + SWA_QBLK + 2 * SWA_WINDOW, gq), F32),
                        pltpu.VMEM((qb, 2, LANES, gq), BF16)],
        compiler_params=_params(("arbitrary", "arbitrary", "arbitrary")),
        name="swa_attn",
    )(sink, q, k2c, k2, k2, k2, vtc, vt, vt, vt)


def _rope_tables(seq):
    rows = seq // GRID_W
    row = jnp.repeat(jnp.arange(rows, dtype=F32), GRID_W)
    col = jnp.tile(jnp.arange(GRID_W, dtype=F32), rows)
    n_freq = MLA_ROPE // 4
    freqs = ROPE_BASE ** (-jnp.arange(n_freq, dtype=F32) / n_freq)
    ang = jnp.concatenate([row[:, None] * freqs, col[:, None] * freqs], axis=-1)
    cos, sin = jnp.cos(ang), jnp.sin(ang)
    cos_t = jnp.concatenate([cos, cos, cos, cos], axis=-1)
    sin_t = jnp.concatenate([-sin, -sin, sin, sin], axis=-1)
    return cos_t, sin_t, cos_t.T, sin_t.T


def _pair_tiles(w, n_heads, half):
    k = w.shape[0]
    x1 = w[:, :, :half].reshape(k, n_heads // 2, 2 * half)
    x2 = w[:, :, half:].reshape(k, n_heads // 2, 2 * half)
    return jnp.concatenate([x1, x2], axis=2).reshape(k, n_heads * 2 * half)


def _mla_weights(w_in, g_qa, g_kva, w_qb, w_kvb):
    half = MLA_ROPE // 2
    lat = MLA_Q_LORA + MLA_KV_LORA
    k1, k2 = w_in[:, lat:lat + half], w_in[:, lat + half:]
    qb = w_qb.reshape(MLA_Q_LORA, MLA_HEADS, MLA_NOPE + MLA_ROPE)
    kvb = w_kvb.reshape(MLA_KV_LORA, MLA_HEADS, MLA_NOPE + MLA_V)
    return {
        "w_in": jnp.concatenate([w_in[:, :lat], k1, k1, k2, k2], axis=1).astype(BF16),
        "g_qa": g_qa.reshape(1, -1), "g_kva": g_kva.reshape(1, -1),
        "w_qn": qb[:, :, :MLA_NOPE].reshape(MLA_Q_LORA, -1).T.astype(BF16),
        "w_qr": _pair_tiles(qb[:, :, MLA_NOPE:], MLA_HEADS, half).T.astype(BF16),
        "w_kn": kvb[:, :, :MLA_NOPE].reshape(MLA_KV_LORA, -1).astype(BF16),
        "w_vt": kvb[:, :, MLA_NOPE:].reshape(MLA_KV_LORA, -1).T.astype(BF16),
    }


def _swa_weights(w_qkv):
    d = w_qkv.shape[0]
    half = SWA_HEAD_DIM // 2
    dq = SWA_HEADS * SWA_HEAD_DIM
    dkv = SWA_KV_HEADS * SWA_HEAD_DIM
    q = _pair_tiles(w_qkv[:, :dq].reshape(d, SWA_HEADS, SWA_HEAD_DIM), SWA_HEADS, half)
    k = w_qkv[:, dq:dq + dkv].reshape(d, SWA_KV_HEADS, SWA_HEAD_DIM)
    k1, k2 = k[:, :, :half], k[:, :, half:]
    k = jnp.concatenate([k1, k1, k2, k2], axis=2).reshape(d, SWA_KV_HEADS * LANES)
    return q.T.astype(BF16), k.astype(BF16), w_qkv[:, dq + dkv:].T.astype(BF16)


def kernel(x, c, ctx, c_ctx, w_mod, b_mod, g_norm, w_ff_in, w_ff_out, mla_w_in, mla_g_qa,
           mla_g_kva, mla_w_qb, mla_w_kvb, mla_w_out, swa_w_qkv, swa_sink, swa_w_out):
    nb, seq, d = x.shape
    lc = ctx.shape[1]
    depth = w_mod.shape[0]
    assert nb + 1 <= MOD_ROWS
    tm = min(TOKEN_TILE, seq)
    tq = min(ATTN_TQ, seq)
    n_per_batch = seq // tm
    grp_lat = lambda i: i // n_per_batch
    grp_ctx = lambda i: nb

    cmat = jnp.zeros((MOD_ROWS, d), F32).at[:nb].set(c).at[nb].set(c_ctx)
    mod_all = _modulation(cmat, w_mod, b_mod)
    rope_tabs = _rope_tables(seq)

    xl = x.reshape(nb * seq, d)
    xc = ctx.reshape(nb * lc, d)
    for i in range(depth):
        need_ctx = i < depth - 1
        mod = mod_all[i].reshape(MOD_ROWS, 1, 6 * d)
        g = g_norm[i].reshape(4, 1, d)
        j = i // 2
        if i % 2 == 0:
            w = _mla_weights(mla_w_in[j], mla_g_qa[j], mla_g_kva[j], mla_w_qb[j], mla_w_kvb[j])
            w_out = mla_w_out[j].astype(BF16)
            qn, qr, kn, kr, vt = _mla_proj(xl, mod, grp_lat, g[0], w, rope_tabs, tm, n_per_batch)
            qnc, qrc, knc, krc, vtc = _mla_proj(xc, mod, grp_ctx, g[0], w, None, lc, 1)
            ctx_kv = (knc.reshape(nb, lc, -1), krc.reshape(nb, lc, -1), vtc)
            lat_kv = (kn.reshape(nb, seq, -1), kr.reshape(nb, seq, -1), vt)
            o_l = _mla_attn(qn, qr, ctx_kv, lat_kv, nb, tq)
            o_c = _mla_attn(qnc, qrc, ctx_kv, None, nb, lc) if need_ctx else None
        else:
            w = _swa_weights(swa_w_qkv[j])
            w_out = swa_w_out[j].astype(BF16)
            q, k2, vt = _swa_proj(xl, mod, grp_lat, g[0], w, rope_tabs, tm, n_per_batch)
            qc, k2c, vtc = _swa_proj(xc, mod, grp_ctx, g[0], w, None, lc, 1)
            o_l = _swa_attn(swa_sink[j], q, k2, vt, k2c, vtc, nb, seq, lc)
            assert not need_ctx
            o_c = None
        w1 = w_ff_in[i].astype(BF16)
        w2 = w_ff_out[i].astype(BF16)
        xl = _outproj(o_l, w_out, xl, g[1], mod, grp_lat, tm)
        xl = _mlp(xl, g[2], g[3], mod, grp_lat, w1, w2, tm)
        if need_ctx:
            xc = _outproj(o_c, w_out, xc, g[1], mod, grp_ctx, lc)
            xc = _mlp(xc, g[2], g[3], mod, grp_ctx, w1, w2, lc)
    return xl.reshape(nb, seq, d)
```

```python
import functools
import math

import jax
import jax.numpy as jnp
from jax import lax
from jax.experimental import pallas as pl
from jax.experimental.pallas import tpu as pltpu

F32 = jnp.float32
BF16 = jnp.bfloat16

GRID_W = 64
ROPE_BASE = 10000.0
NORM_EPS = 1e-6
LOG2E = math.log2(math.e)
NEG_BIG = -1e30
REF_LIMIT = 60.0
NORM_SLACK = 1.03

MLA_HEADS = 16
MLA_Q_LORA = 512
MLA_KV_LORA = 512
MLA_NOPE = 128
MLA_ROPE = 64
MLA_V = 128

SWA_HEADS = 32
SWA_KV_HEADS = 4
SWA_HEAD_DIM = 64
SWA_WINDOW = 128
SWA_GROUP = SWA_HEADS // SWA_KV_HEADS
SWA_QBLK = 128
SWA_STEP_BLOCKS = 4

LANES = 128
HALF_TILE = 64

MOD_ROWS = 8
MOD_TN = 1024
TOKEN_TILE = 512
ROW_CHUNK = 32
ROW_UNROLL = 4
FF_TILE = 1024
ATTN_TQ = 512
KV_GROUP = 8
KV_UNROLL = 4
VMEM_LIMIT = 56 * 1024 * 1024

NT_DIMS = (((1,), (1,)), ((), ()))


def _rms(xf, g):
    ms = jnp.mean(xf * xf, axis=-1, keepdims=True)
    return xf * lax.rsqrt(ms + NORM_EPS) * g


def _inv_rms(xf):
    return lax.rsqrt(jnp.mean(xf * xf, axis=-1, keepdims=True) + NORM_EPS)


def _row_chunks(n_rows, fn):
    def body(i, carry):
        fn(pl.ds(pl.multiple_of(i * ROW_CHUNK, ROW_CHUNK), ROW_CHUNK))
        return carry
    lax.fori_loop(0, n_rows // ROW_CHUNK, body, 0, unroll=ROW_UNROLL)


def _modulated_norm(x_ref, g_ref, sc_ref, sh_ref, h_ref):
    gain = g_ref[...] * (1.0 + sc_ref[0])
    shift = sh_ref[0]

    def rows(rs):
        xf = x_ref[rs, :]
        h_ref[rs, :] = (xf * _inv_rms(xf) * gain + shift).astype(h_ref.dtype)
    _row_chunks(h_ref.shape[0], rows)


def _rot(tile, cos, sin):
    return tile * cos + pltpu.roll(tile, HALF_TILE, 1) * sin


def _first_head_lanes(shape):
    lane = lax.broadcasted_iota(jnp.int32, shape, 1)
    return (lane & 32) == 0


def _shift_lane(parity):
    return LANES - 1 - 32 * parity


def _params(sem):
    return pltpu.CompilerParams(dimension_semantics=sem, vmem_limit_bytes=VMEM_LIMIT)


def _const_spec(shape):
    nd = len(shape)
    return pl.BlockSpec(shape, lambda *_: (0,) * nd, pipeline_mode=pl.Buffered(1))


def _mod_spec(grp, which, d):
    return pl.BlockSpec((1, 1, d), lambda i, *_: (grp(i), 0, which))


def _mod_kernel(c_ref, w_ref, b_ref, o_ref):
    c = c_ref[...]
    a = c / (1.0 + jnp.exp(-c))
    o_ref[0] = jnp.dot(a, w_ref[0], preferred_element_type=F32,
                       precision=lax.Precision.HIGHEST) + b_ref[0]


def _modulation(cmat, w_mod, b_mod):
    depth, d, n = w_mod.shape
    return pl.pallas_call(
        _mod_kernel,
        grid=(depth, n // MOD_TN),
        in_specs=[pl.BlockSpec((MOD_ROWS, d), lambda l, j: (0, 0)),
                  pl.BlockSpec((1, d, MOD_TN), lambda l, j: (l, 0, j)),
                  pl.BlockSpec((1, 1, MOD_TN), lambda l, j: (l, 0, j))],
        out_specs=pl.BlockSpec((1, MOD_ROWS, MOD_TN), lambda l, j: (l, 0, j)),
        out_shape=jax.ShapeDtypeStruct((depth, MOD_ROWS, n), F32),
        compiler_params=_params(("arbitrary", "arbitrary")),
        name="modulation",
    )(cmat, w_mod, b_mod.reshape(depth, 1, n))


def _mla_proj_kernel(*refs, rope, qscale):
    if rope:
        (x_ref, g_ref, sc_ref, sh_ref, win_ref, gqa_ref, gkva_ref, wqn_ref, wqr_ref, wkn_ref,
         wvt_ref, cos_ref, sin_ref, cost_ref, sint_ref, qn_ref, qr_ref, kn_ref, kr_ref,
         vt_ref) = refs
        cos, sin = cos_ref[...], sin_ref[...]
    else:
        (x_ref, g_ref, sc_ref, sh_ref, win_ref, gqa_ref, gkva_ref, wqn_ref, wqr_ref, wkn_ref,
         wvt_ref, qn_ref, qr_ref, kn_ref, kr_ref, vt_ref) = refs
    h = (_rms(x_ref[...], g_ref[...]) * (1.0 + sc_ref[0]) + sh_ref[0]).astype(BF16)
    p = jnp.dot(h, win_ref[...], preferred_element_type=F32)
    qa = _rms(p[:, :MLA_Q_LORA], gqa_ref[...]).astype(BF16)
    ckv = _rms(p[:, MLA_Q_LORA:MLA_Q_LORA + MLA_KV_LORA], gkva_ref[...]).astype(BF16)
    kr = p[:, MLA_Q_LORA + MLA_KV_LORA:]

    qn = lax.dot_general(wqn_ref[...], qa, NT_DIMS, preferred_element_type=F32) * qscale
    qn_ref[...] = qn.astype(BF16)
    qr = lax.dot_general(wqr_ref[...], qa, NT_DIMS, preferred_element_type=F32) * qscale
    for t in range(qr.shape[0] // LANES):
        tile = qr[t * LANES:(t + 1) * LANES, :]
        if rope:
            swapped = jnp.concatenate([tile[HALF_TILE:], tile[:HALF_TILE]], axis=0)
            tile = tile * cost_ref[...] + swapped * sint_ref[...]
        qr_ref[t * LANES:(t + 1) * LANES, :] = tile.astype(BF16)

    if rope:
        kr = _rot(kr, cos, sin)
    first = _first_head_lanes(kr.shape)
    lane = lax.broadcasted_iota(jnp.int32, kr.shape, 1)
    kr_ref[:, :LANES] = jnp.where(lane == _shift_lane(0), 1.0,
                                  jnp.where(first, kr, 0.0)).astype(BF16)
    kr_ref[:, LANES:] = jnp.where(lane == _shift_lane(1), 1.0,
                                  jnp.where(first, 0.0, kr)).astype(BF16)

    kn_ref[...] = jnp.dot(ckv, wkn_ref[...], preferred_element_type=F32).astype(BF16)
    vt = lax.dot_general(wvt_ref[...], ckv, NT_DIMS, preferred_element_type=F32).astype(BF16)
    for hd in range(MLA_HEADS):
        vt_ref[0, hd, 0] = vt[hd * MLA_V:(hd + 1) * MLA_V, :]


def _mla_proj(x, mod, grp, g0, w, rope_tabs, tm, n_per_batch):
    t, d = x.shape
    n_tiles = t // tm
    nb = n_tiles // n_per_batch
    rope = rope_tabs is not None
    qscale = (MLA_NOPE + MLA_ROPE) ** -0.5 * LOG2E
    row = lambda i: (i, 0)
    in_specs = [pl.BlockSpec((tm, d), row), _const_spec((1, d)),
                _mod_spec(grp, 1, d), _mod_spec(grp, 0, d),
                _const_spec(w["w_in"].shape), _const_spec((1, MLA_Q_LORA)),
                _const_spec((1, MLA_KV_LORA)), _const_spec(w["w_qn"].shape),
                _const_spec(w["w_qr"].shape), _const_spec(w["w_kn"].shape),
                _const_spec(w["w_vt"].shape)]
    args = [x, g0, mod, mod, w["w_in"], w["g_qa"], w["g_kva"], w["w_qn"], w["w_qr"], w["w_kn"],
            w["w_vt"]]
    if rope:
        pos = lambda i: (i % n_per_batch, 0)
        pos_t = lambda i: (0, i % n_per_batch)
        in_specs += [pl.BlockSpec((tm, LANES), pos), pl.BlockSpec((tm, LANES), pos),
                     pl.BlockSpec((LANES, tm), pos_t), pl.BlockSpec((LANES, tm), pos_t)]
        args += list(rope_tabs)
    hn = MLA_HEADS * MLA_NOPE
    hr = MLA_HEADS * MLA_ROPE
    col = lambda i: (0, i)
    out_shape = [jax.ShapeDtypeStruct((hn, t), BF16), jax.ShapeDtypeStruct((hr, t), BF16),
                 jax.ShapeDtypeStruct((t, hn), BF16), jax.ShapeDtypeStruct((t, 2 * LANES), BF16),
                 jax.ShapeDtypeStruct((nb, MLA_HEADS, n_per_batch, MLA_V, tm), BF16)]
    out_specs = [pl.BlockSpec((hn, tm), col), pl.BlockSpec((hr, tm), col),
                 pl.BlockSpec((tm, hn), row), pl.BlockSpec((tm, 2 * LANES), row),
                 pl.BlockSpec((1, MLA_HEADS, 1, MLA_V, tm),
                              lambda i: (i // n_per_batch, 0, i % n_per_batch, 0, 0))]
    return pl.pallas_call(
        functools.partial(_mla_proj_kernel, rope=rope, qscale=qscale),
        grid=(n_tiles,), in_specs=in_specs, out_specs=out_specs, out_shape=out_shape,
        compiler_params=_params(("arbitrary",)),
        name="mla_proj_lat" if rope else "mla_proj_ctx",
    )(*args)


def _kv_group_sizes(n_chunks):
    if n_chunks <= 2:
        return [n_chunks] if n_chunks else []
    tail = [min(KV_GROUP, n_chunks) - 2, 2]
    body = n_chunks - sum(tail)
    assert body % KV_GROUP == 0
    return [KV_GROUP] * (body // KV_GROUP) + tail


def _mla_attn_kernel(*refs, n_lat, tk):
    if n_lat:
        (qn_ref, qr_ref, knc_ref, krc_ref, vtc_ref, knl_ref, krl_ref, vtl_ref, o_ref,
         acc_ref, s_ref, qa_ref, k2_ref) = refs
    else:
        qn_ref, qr_ref, knc_ref, krc_ref, vtc_ref, o_ref, acc_ref, s_ref, qa_ref, k2_ref = refs
    tq = qn_ref.shape[1]
    lc = knc_ref.shape[1]
    parity = pl.program_id(1) % 2

    def lat_keys(j):
        off = pl.multiple_of(j * tk, tk)
        return knl_ref[0, pl.ds(off, tk), :], krl_ref[0, pl.ds(off, tk), :]

    @pl.when(pl.program_id(2) == 0)
    def _():
        ones = jnp.ones((LANES, LANES), BF16)

        def sqnorm_max(kn, kr):
            r = jnp.dot(kn * kn + kr * kr, ones, preferred_element_type=F32)
            return jnp.max(r, axis=0, keepdims=True)
        mx = sqnorm_max(knc_ref[0], krc_ref[0])
        if n_lat:
            mx = lax.fori_loop(0, n_lat, lambda j, c: jnp.maximum(c, sqnorm_max(*lat_keys(j))), mx,
                               unroll=KV_UNROLL)
        k2_ref[...] = jnp.broadcast_to(mx * NORM_SLACK, k2_ref.shape)

    qn = qn_ref[...]
    qr = qr_ref[...]
    feat = lax.broadcasted_iota(jnp.int32, qr.shape, 0)
    own = ((feat >> 5) & 1) == parity
    qnf, qrf = qn.astype(F32), jnp.where(own, qr.astype(F32), 0.0)
    q2 = jnp.sum(qnf * qnf + qrf * qrf, axis=0, keepdims=True)
    ref = jnp.sqrt(q2 * k2_ref[0:1, 0:1])
    fast = jnp.max(ref) <= REF_LIMIT
    shift = jnp.where(fast, -ref, 0.0).astype(BF16)
    qa_ref[:LANES, :] = qn
    qa_ref[LANES:, :] = jnp.where(feat == LANES - 1 - 32 * parity, shift, qr)

    def scores(kn, kr):
        return jnp.dot(jnp.concatenate([kn, kr], axis=1), qa_ref[...],
                       preferred_element_type=F32)

    def finish(l):
        o_ref[...] = (acc_ref[...] * (1.0 / l)).T.astype(BF16)

    def shifted_softmax():
        def chunk(s, vt, l8):
            p = jnp.exp2(s)
            l8 = l8 + jnp.sum(p.reshape(-1, 8, tq), axis=0)
            return l8, jnp.dot(vt, p.astype(BF16), preferred_element_type=F32)

        sizes = _kv_group_sizes(n_lat)
        starts = [sum(sizes[:g]) for g in range(len(sizes))]
        last = max(len(sizes) - 1, 0)

        def n_rows(g):
            return (sizes[g] * tk if sizes else 0) + (lc if g == last else 0)

        def fill_scores(g):
            view = s_ref.at[g % 2]
            lat = sizes[g] * tk if sizes else 0
            if lat:
                rows = pl.ds(starts[g] * tk, lat)
                view[pl.ds(0, lat), :] = scores(knl_ref[0, rows, :], krl_ref[0, rows, :])
            if g == last:
                view[pl.ds(lat, lc), :] = scores(knc_ref[0], krc_ref[0])

        fill_scores(0)
        l8 = jnp.zeros((8, tq), F32)
        for g in range(last + 1):
            if g < last:
                fill_scores(g + 1)
            vts = [vtl_ref[0, 0, starts[g] + u] for u in range(sizes[g])] if sizes else []
            if g == last:
                vts.append(vtc_ref[0, 0, 0])
            l8, pv = chunk(s_ref[g % 2, pl.ds(0, n_rows(g)), :], jnp.concatenate(vts, axis=1), l8)
            if g == 0:
                acc_ref[...] = pv
            else:
                acc_ref[...] += pv
        finish(jnp.sum(l8, axis=0, keepdims=True))

    def online_softmax():
        def update(s_view, vt, m, l):
            m_new = jnp.maximum(m, jnp.max(s_view[...], axis=0, keepdims=True))
            alpha = jnp.exp2(m - m_new)
            p = jnp.exp2(s_view[...] - m_new)
            l_new = alpha * l + jnp.sum(p, axis=0, keepdims=True)
            acc_ref[...] = acc_ref[...] * alpha + jnp.dot(vt, p.astype(BF16),
                                                          preferred_element_type=F32)
            return m_new, l_new

        acc_ref[...] = jnp.zeros_like(acc_ref)
        m = jnp.full((1, tq), NEG_BIG, F32)
        l = jnp.zeros((1, tq), F32)
        slots = [s_ref.at[0, pl.ds(0, tk)], s_ref.at[1, pl.ds(0, tk)]] if n_lat else None
        if n_lat:
            slots[0][...] = scores(*lat_keys(0))
        ctx_view = s_ref.at[1, pl.ds(0, lc)]
        ctx_view[...] = scores(knc_ref[0], krc_ref[0])
        m, l = update(ctx_view, vtc_ref[0, 0, 0], m, l)
        if n_lat:
            def body(jj, carry):
                for u in range(KV_UNROLL):
                    j = KV_UNROLL * jj + u
                    slots[(u + 1) % 2][...] = scores(*lat_keys(jnp.minimum(j + 1, n_lat - 1)))
                    carry = update(slots[u % 2], vtl_ref[0, 0, j], *carry)
                return carry
            m, l = lax.fori_loop(0, n_lat // KV_UNROLL, body, (m, l))
        finish(l)

    lax.cond(fast, shifted_softmax, online_softmax)


def _mla_attn(qn, qr, ctx_kv, lat_kv, nb, tq):
    t = qn.shape[1]
    nq = t // nb // tq
    knc, krc, vtc = ctx_kv
    lc = knc.shape[1]
    qrow = lambda b, h, i: (b * nq + i, h)
    in_specs = [pl.BlockSpec((LANES, tq), lambda b, h, i: (h, b * nq + i)),
                pl.BlockSpec((LANES, tq), lambda b, h, i: (h // 2, b * nq + i)),
                pl.BlockSpec((1, lc, LANES), lambda b, h, i: (b, 0, h)),
                pl.BlockSpec((1, lc, LANES), lambda b, h, i: (b, 0, h % 2)),
                pl.BlockSpec((1, 1, 1, MLA_V, lc), lambda b, h, i: (b, h, 0, 0, 0))]
    args = [qn, qr, knc, krc, vtc]
    n_lat, tk = 0, 0
    if lat_kv is not None:
        knl, krl, vtl = lat_kv
        s = knl.shape[1]
        n_lat, tk = vtl.shape[2], vtl.shape[4]
        assert n_lat % KV_UNROLL == 0 and tk >= lc
        in_specs += [pl.BlockSpec((1, s, LANES), lambda b, h, i: (b, 0, h)),
                     pl.BlockSpec((1, s, LANES), lambda b, h, i: (b, 0, h % 2)),
                     pl.BlockSpec((1, 1, n_lat, MLA_V, tk), lambda b, h, i: (b, h, 0, 0, 0))]
        args += [knl, krl, vtl]
    scratch = [pltpu.VMEM((MLA_V, tq), F32), pltpu.VMEM((2, max(KV_GROUP * tk, lc), tq), F32),
               pltpu.VMEM((2 * LANES, tq), BF16), pltpu.VMEM((8, LANES), F32)]
    return pl.pallas_call(
        functools.partial(_mla_attn_kernel, n_lat=n_lat, tk=tk),
        grid=(nb, MLA_HEADS, nq), in_specs=in_specs,
        out_specs=pl.BlockSpec((tq, MLA_V), qrow),
        out_shape=jax.ShapeDtypeStruct((t, MLA_HEADS * MLA_V), BF16),
        scratch_shapes=scratch,
        compiler_params=_params(("arbitrary", "arbitrary", "arbitrary")),
        name="mla_attn_lat" if n_lat else "mla_attn_ctx",
    )(*args)


def _outproj_kernel(o_ref, w_ref, x_ref, g_ref, gt_ref, out_ref):
    y = jnp.dot(o_ref[...], w_ref[...], preferred_element_type=F32)
    out_ref[...] = x_ref[...] + gt_ref[0] * _rms(y, g_ref[...])


def _outproj(o, w_out, x, g1, mod, grp, tm):
    t, d = x.shape
    row = lambda i: (i, 0)
    return pl.pallas_call(
        _outproj_kernel,
        grid=(t // tm,),
        in_specs=[pl.BlockSpec((tm, o.shape[1]), row), _const_spec(w_out.shape),
                  pl.BlockSpec((tm, d), row), _const_spec((1, d)), _mod_spec(grp, 2, d)],
        out_specs=pl.BlockSpec((tm, d), row),
        out_shape=jax.ShapeDtypeStruct((t, d), F32),
        compiler_params=_params(("arbitrary",)),
        name="outproj",
    )(o, w_out, x, g1, mod)


def _mlp_kernel(x_ref, g2_ref, sc_ref, sh_ref, w1_ref, w2_ref, g3_ref, gt_ref, out_ref,
                f_ref, acc_ref):
    k = pl.program_id(1)

    def ff_chunk():
        u = jnp.maximum(jnp.dot(f_ref[...], w1_ref[...], preferred_element_type=F32), 0.0)
        return jnp.dot((u * u).astype(BF16), w2_ref[...], preferred_element_type=F32)

    @pl.when(k == 0)
    def _():
        _modulated_norm(x_ref, g2_ref, sc_ref, sh_ref, f_ref)
        acc_ref[...] = ff_chunk()

    @pl.when(k > 0)
    def _():
        acc_ref[...] += ff_chunk()

    @pl.when(k == pl.num_programs(1) - 1)
    def _():
        gain = gt_ref[0] * g3_ref[...]

        def rows(rs):
            y = acc_ref[rs, :]
            out_ref[rs, :] = x_ref[rs, :] + y * _inv_rms(y) * gain
        _row_chunks(acc_ref.shape[0], rows)


def _mlp(x, g2, g3, mod, grp, w1, w2, tm):
    t, d = x.shape
    dff = w1.shape[1]
    row = lambda i, k: (i, 0)
    return pl.pallas_call(
        _mlp_kernel,
        grid=(t // tm, dff // FF_TILE),
        in_specs=[pl.BlockSpec((tm, d), row), _const_spec((1, d)),
                  _mod_spec(grp, 4, d), _mod_spec(grp, 3, d),
                  pl.BlockSpec((d, FF_TILE), lambda i, k: (0, k)),
                  pl.BlockSpec((FF_TILE, d), lambda i, k: (k, 0)),
                  _const_spec((1, d)), _mod_spec(grp, 5, d)],
        out_specs=pl.BlockSpec((tm, d), row),
        out_shape=jax.ShapeDtypeStruct((t, d), F32),
        scratch_shapes=[pltpu.VMEM((tm, d), BF16), pltpu.VMEM((tm, d), F32)],
        compiler_params=_params(("arbitrary", "arbitrary")),
        name="mlp",
    )(x, g2, mod, mod, w1, w2, g3, mod)


def _swa_proj_kernel(*refs, rope, qscale):
    if rope:
        (x_ref, g_ref, sc_ref, sh_ref, wqt_ref, wk_ref, wvt_ref, cos_ref, sin_ref, cost_ref,
         sint_ref, q_ref, k2_ref, vt_ref) = refs
        cos, sin = cos_ref[...], sin_ref[...]
    else:
        x_ref, g_ref, sc_ref, sh_ref, wqt_ref, wk_ref, wvt_ref, q_ref, k2_ref, vt_ref = refs
    h = (_rms(x_ref[...], g_ref[...]) * (1.0 + sc_ref[0]) + sh_ref[0]).astype(BF16)
    qt = lax.dot_general(wqt_ref[...], h, NT_DIMS, preferred_element_type=F32) * qscale
    for t in range(qt.shape[0] // LANES):
        tile = qt[t * LANES:(t + 1) * LANES, :]
        if rope:
            swapped = jnp.concatenate([tile[HALF_TILE:], tile[:HALF_TILE]], axis=0)
            tile = tile * cost_ref[...] + swapped * sint_ref[...]
        q_ref[t * LANES:(t + 1) * LANES, :] = tile.astype(BF16)
    k = jnp.dot(h, wk_ref[...], preferred_element_type=F32)
    for c in range(SWA_KV_HEADS):
        tile = k[:, c * LANES:(c + 1) * LANES]
        if rope:
            tile = _rot(tile, cos, sin)
        first = _first_head_lanes(tile.shape)
        lane = lax.broadcasted_iota(jnp.int32, tile.shape, 1)
        k2_ref[:, 2 * c * LANES:(2 * c + 1) * LANES] = jnp.where(
            lane == _shift_lane(0), 1.0, jnp.where(first, tile, 0.0)).astype(BF16)
        k2_ref[:, (2 * c + 1) * LANES:(2 * c + 2) * LANES] = jnp.where(
            lane == _shift_lane(1), 1.0, jnp.where(first, 0.0, tile)).astype(BF16)
    vt_ref[...] = lax.dot_general(wvt_ref[...], h, NT_DIMS,
                                  preferred_element_type=F32).astype(BF16)


def _swa_proj(x, mod, grp, g0, w, rope_tabs, tm, n_per_batch):
    t, d = x.shape
    rope = rope_tabs is not None
    row = lambda i: (i, 0)
    dq = SWA_HEADS * SWA_HEAD_DIM
    dkv = SWA_KV_HEADS * SWA_HEAD_DIM
    w_qt, w_k, w_vt = w
    col = lambda i: (0, i)
    in_specs = [pl.BlockSpec((tm, d), row), _const_spec((1, d)),
                _mod_spec(grp, 1, d), _mod_spec(grp, 0, d), _const_spec(w_qt.shape),
                _const_spec(w_k.shape), _const_spec(w_vt.shape)]
    args = [x, g0, mod, mod, w_qt, w_k, w_vt]
    if rope:
        pos = lambda i: (i % n_per_batch, 0)
        pos_t = lambda i: (0, i % n_per_batch)
        in_specs += [pl.BlockSpec((tm, LANES), pos), pl.BlockSpec((tm, LANES), pos),
                     pl.BlockSpec((LANES, tm), pos_t), pl.BlockSpec((LANES, tm), pos_t)]
        args += list(rope_tabs)
    out_shape = [jax.ShapeDtypeStruct((dq, t), BF16),
                 jax.ShapeDtypeStruct((t, 2 * SWA_KV_HEADS * LANES), BF16),
                 jax.ShapeDtypeStruct((dkv, t), BF16)]
    out_specs = [pl.BlockSpec((dq, tm), col), pl.BlockSpec((tm, 2 * SWA_KV_HEADS * LANES), row),
                 pl.BlockSpec((dkv, tm), col)]
    return pl.pallas_call(
        functools.partial(_swa_proj_kernel, rope=rope, qscale=SWA_HEAD_DIM ** -0.5 * LOG2E),
        grid=(t // tm,), in_specs=in_specs, out_specs=out_specs, out_shape=out_shape,
        compiler_params=_params(("arbitrary",)),
        name="swa_proj_lat" if rope else "swa_proj_ctx",
    )(*args)


def _swa_attn_kernel(sink_ref, q_ref, kc_ref, kp_ref, kcur_ref, kn_ref, vc_ref, vp_ref, vcur_ref,
                     vn_ref, o_ref, s_ref, qa_ref, *, qb):
    kvh = pl.program_id(1)
    i = pl.program_id(2)
    lc = kc_ref.shape[0]
    span = SWA_QBLK + 2 * SWA_WINDOW
    n_pairs = SWA_GROUP // 2
    kc, vc = kc_ref[...], vc_ref[...]
    kwin = jnp.concatenate([kp_ref[...], kcur_ref[...], kn_ref[...]], axis=0)
    vwin = jnp.concatenate([vp_ref[...], vcur_ref[...], vn_ref[...]], axis=1)
    r = lax.broadcasted_iota(jnp.int32, (lc + span, SWA_QBLK), 0)
    rel = r - lc - lax.broadcasted_iota(jnp.int32, (lc + span, SWA_QBLK), 1)
    bias = jnp.where((r < lc) | ((rel >= 0) & (rel <= 2 * SWA_WINDOW)), 0.0, NEG_BIG)
    bias = jnp.concatenate([bias] * n_pairs, axis=1)
    pair = lax.broadcasted_iota(jnp.int32, (1, n_pairs * SWA_QBLK), 1) // SWA_QBLK
    keep_prev = jnp.where(i == 0, 0.0, 1.0)
    keep_next = jnp.where(i == pl.num_programs(2) - 1, 0.0, 1.0)

    ones = jnp.ones((LANES, LANES), BF16)
    kall = jnp.concatenate([kc[:, :LANES], kwin[:, :LANES]], axis=0)
    k2 = jnp.max(jnp.dot(kall * kall, ones, preferred_element_type=F32), axis=0, keepdims=True)
    k2 = k2[:, 0:1] * NORM_SLACK
    feat = lax.broadcasted_iota(jnp.int32, (LANES, n_pairs * SWA_QBLK), 0)
    first_rows = (feat & 32) == 0
    q_t, refs = [], []
    for blk in range(qb):
        qt = jnp.concatenate([q_ref[t * LANES:(t + 1) * LANES, blk * SWA_QBLK:(blk + 1) * SWA_QBLK]
                              for t in range(n_pairs)], axis=1)
        sq = qt.astype(F32)
        sq = sq * sq
        q2_first = jnp.sum(jnp.where(first_rows, sq, 0.0), axis=0, keepdims=True)
        q2_second = jnp.sum(sq, axis=0, keepdims=True) - q2_first
        q_t.append(qt)
        refs.append([jnp.sqrt(q2 * k2).astype(BF16) for q2 in (q2_first, q2_second)])
    ref_max = functools.reduce(jnp.maximum, [jnp.max(r.astype(F32)) for pr in refs for r in pr])
    fast = ref_max <= REF_LIMIT
    def set_queries(shifted):
        for blk in range(qb):
            for e in range(2):
                shift = -refs[blk][e] if shifted else jnp.zeros_like(refs[blk][e])
                qa_ref[blk, e] = jnp.where(feat == _shift_lane(e), shift, q_t[blk])

    def scores(blk, e):
        kcat = jnp.concatenate([kc[:, e * LANES:(e + 1) * LANES],
                                kwin[blk * SWA_QBLK:blk * SWA_QBLK + span,
                                     e * LANES:(e + 1) * LANES]], axis=0)
        return jnp.dot(kcat, qa_ref[blk, e], preferred_element_type=F32)

    def run(shifted):
        s_ref[0] = scores(0, 0)
        for blk in range(qb):
            rows = slice(blk * SWA_QBLK, (blk + 1) * SWA_QBLK)
            vt = jnp.concatenate([vc, vwin[:, blk * SWA_QBLK:blk * SWA_QBLK + span]], axis=1)
            halves = []
            for e in range(2):
                if e == 0:
                    s_ref[1] = scores(blk, 1)
                elif blk + 1 < qb:
                    s_ref[0] = scores(blk + 1, 0)
                s = s_ref[e] + bias
                sk = jnp.zeros(pair.shape, F32)
                for t in range(n_pairs):
                    sk = jnp.where(pair == t, sink_ref[kvh * SWA_GROUP + 2 * t + e] * LOG2E, sk)
                if shifted:
                    m = refs[blk][e].astype(F32)
                    p = jnp.exp2(s)
                else:
                    m = jnp.maximum(jnp.max(s, axis=0, keepdims=True), sk)
                    p = jnp.exp2(s - m)
                if blk == 0:
                    p = jnp.concatenate([p[:lc], p[lc:lc + SWA_WINDOW] * keep_prev,
                                         p[lc + SWA_WINDOW:]], axis=0)
                if blk == qb - 1:
                    p = jnp.concatenate([p[:lc + span - SWA_WINDOW],
                                         p[lc + span - SWA_WINDOW:] * keep_next], axis=0)
                den = jnp.sum(p, axis=0, keepdims=True) + jnp.exp2(sk - m)
                o = jnp.dot(vt, p.astype(BF16), preferred_element_type=F32)
                halves.append(o * (1.0 / den))
            both = jnp.concatenate(halves, axis=0)
            for t in range(n_pairs):
                tile = both[:, t * SWA_QBLK:(t + 1) * SWA_QBLK]
                o_ref[rows, t * LANES:(t + 1) * LANES] = tile.T.astype(BF16)

    set_queries(True)
    run(True)

    @pl.when(jnp.logical_not(fast))
    def _():
        set_queries(False)
        run(False)


def _swa_attn(sink, q, k2, vt, k2c, vtc, nb, seq, lc):
    t = q.shape[1]
    nblk = seq // SWA_QBLK
    qb = min(SWA_STEP_BLOCKS, nblk)
    nsteps = nblk // qb
    gq = SWA_GROUP * SWA_HEAD_DIM
    hd = SWA_HEAD_DIM
    prev_blk = lambda b, i: b * nblk + jnp.maximum(qb * i - 1, 0)
    next_blk = lambda b, i: b * nblk + jnp.minimum(qb * i + qb, nblk - 1)
    return pl.pallas_call(
        functools.partial(_swa_attn_kernel, qb=qb),
        grid=(nb, SWA_KV_HEADS, nsteps),
        in_specs=[pl.BlockSpec(memory_space=pltpu.SMEM),
                  pl.BlockSpec((gq, qb * SWA_QBLK), lambda b, h, i: (h, b * nsteps + i)),
                  pl.BlockSpec((lc, 2 * LANES), lambda b, h, i: (b, h)),
                  pl.BlockSpec((SWA_QBLK, 2 * LANES), lambda b, h, i: (prev_blk(b, i), h)),
                  pl.BlockSpec((qb * SWA_QBLK, 2 * LANES), lambda b, h, i: (b * nsteps + i, h)),
                  pl.BlockSpec((SWA_QBLK, 2 * LANES), lambda b, h, i: (next_blk(b, i), h)),
                  pl.BlockSpec((hd, lc), lambda b, h, i: (h, b)),
                  pl.BlockSpec((hd, SWA_QBLK), lambda b, h, i: (h, prev_blk(b, i))),
                  pl.BlockSpec((hd, qb * SWA_QBLK), lambda b, h, i: (h, b * nsteps + i)),
                  pl.BlockSpec((hd, SWA_QBLK), lambda b, h, i: (h, next_blk(b, i)))],
        out_specs=pl.BlockSpec((qb * SWA_QBLK, gq), lambda b, h, i: (b * nsteps + i, h)),
        out_shape=jax.ShapeDtypeStruct((t, SWA_HEADS * SWA_HEAD_DIM), BF16),
        scratch_shapes=[pltpu.VMEM((2, lc + SWA_QBLK + 2 * SWA_WINDOW, gq), F32),
                        pltpu.VMEM((qb, 2, LANES, gq), BF16)],
        compiler_params=_params(("arbitrary", "arbitrary", "arbitrary")),
        name="swa_attn",
    )(sink, q, k2c, k2, k2, k2, vtc, vt, vt, vt)


def _rope_tables(seq):
    rows = seq // GRID_W
    row = jnp.repeat(jnp.arange(rows, dtype=F32), GRID_W)
    col = jnp.tile(jnp.arange(GRID_W, dtype=F32), rows)
    n_freq = MLA_ROPE // 4
    freqs = ROPE_BASE ** (-jnp.arange(n_freq, dtype=F32) / n_freq)
    ang = jnp.concatenate([row[:, None] * freqs, col[:, None] * freqs], axis=-1)
    cos, sin = jnp.cos(ang), jnp.sin(ang)
    cos_t = jnp.concatenate([cos, cos, cos, cos], axis=-1)
    sin_t = jnp.concatenate([-sin, -sin, sin, sin], axis=-1)
    return cos_t, sin_t, cos_t.T, sin_t.T


def _pair_tiles(w, n_heads, half):
    k = w.shape[0]
    x1 = w[:, :, :half].reshape(k, n_heads // 2, 2 * half)
    x2 = w[:, :, half:].reshape(k, n_heads // 2, 2 * half)
    return jnp.concatenate([x1, x2], axis=2).reshape(k, n_heads * 2 * half)


def _mla_weights(w_in, g_qa, g_kva, w_qb, w_kvb):
    half = MLA_ROPE // 2
    lat = MLA_Q_LORA + MLA_KV_LORA
    k1, k2 = w_in[:, lat:lat + half], w_in[:, lat + half:]
    qb = w_qb.reshape(MLA_Q_LORA, MLA_HEADS, MLA_NOPE + MLA_ROPE)
    kvb = w_kvb.reshape(MLA_KV_LORA, MLA_HEADS, MLA_NOPE + MLA_V)
    return {
        "w_in": jnp.concatenate([w_in[:, :lat], k1, k1, k2, k2], axis=1).astype(BF16),
        "g_qa": g_qa.reshape(1, -1), "g_kva": g_kva.reshape(1, -1),
        "w_qn": qb[:, :, :MLA_NOPE].reshape(MLA_Q_LORA, -1).T.astype(BF16),
        "w_qr": _pair_tiles(qb[:, :, MLA_NOPE:], MLA_HEADS, half).T.astype(BF16),
        "w_kn": kvb[:, :, :MLA_NOPE].reshape(MLA_KV_LORA, -1).astype(BF16),
        "w_vt": kvb[:, :, MLA_NOPE:].reshape(MLA_KV_LORA, -1).T.astype(BF16),
    }


def _swa_weights(w_qkv):
    d = w_qkv.shape[0]
    half = SWA_HEAD_DIM // 2
    dq = SWA_HEADS * SWA_HEAD_DIM
    dkv = SWA_KV_HEADS * SWA_HEAD_DIM
    q = _pair_tiles(w_qkv[:, :dq].reshape(d, SWA_HEADS, SWA_HEAD_DIM), SWA_HEADS, half)
    k = w_qkv[:, dq:dq + dkv].reshape(d, SWA_KV_HEADS, SWA_HEAD_DIM)
    k1, k2 = k[:, :, :half], k[:, :, half:]
    k = jnp.concatenate([k1, k1, k2, k2], axis=2).reshape(d, SWA_KV_HEADS * LANES)
    return q.T.astype(BF16), k.astype(BF16), w_qkv[:, dq + dkv:].T.astype(BF16)


def kernel(x, c, ctx, c_ctx, w_mod, b_mod, g_norm, w_ff_in, w_ff_out, mla_w_in, mla_g_qa,
           mla_g_kva, mla_w_qb, mla_w_kvb, mla_w_out, swa_w_qkv, swa_sink, swa_w_out):
    nb, seq, d = x.shape
    lc = ctx.shape[1]
    depth = w_mod.shape[0]
    assert nb + 1 <= MOD_ROWS
    tm = min(TOKEN_TILE, seq)
    tq = min(ATTN_TQ, seq)
    n_per_batch = seq // tm
    grp_lat = lambda i: i // n_per_batch
    grp_ctx = lambda i: nb

    cmat = jnp.zeros((MOD_ROWS, d), F32).at[:nb].set(c).at[nb].set(c_ctx)
    mod_all = _modulation(cmat, w_mod, b_mod)
    rope_tabs = _rope_tables(seq)

    xl = x.reshape(nb * seq, d)
    xc = ctx.reshape(nb * lc, d)
    for i in range(depth):
        need_ctx = i < depth - 1
        mod = mod_all[i].reshape(MOD_ROWS, 1, 6 * d)
        g = g_norm[i].reshape(4, 1, d)
        j = i // 2
        if i % 2 == 0:
            w = _mla_weights(mla_w_in[j], mla_g_qa[j], mla_g_kva[j], mla_w_qb[j], mla_w_kvb[j])
            w_out = mla_w_out[j].astype(BF16)
            qn, qr, kn, kr, vt = _mla_proj(xl, mod, grp_lat, g[0], w, rope_tabs, tm, n_per_batch)
            qnc, qrc, knc, krc, vtc = _mla_proj(xc, mod, grp_ctx, g[0], w, None, lc, 1)
            ctx_kv = (knc.reshape(nb, lc, -1), krc.reshape(nb, lc, -1), vtc)
            lat_kv = (kn.reshape(nb, seq, -1), kr.reshape(nb, seq, -1), vt)
            o_l = _mla_attn(qn, qr, ctx_kv, lat_kv, nb, tq)
            o_c = _mla_attn(qnc, qrc, ctx_kv, None, nb, lc) if need_ctx else None
        else:
            w = _swa_weights(swa_w_qkv[j])
            w_out = swa_w_out[j].astype(BF16)
            q, k2, vt = _swa_proj(xl, mod, grp_lat, g[0], w, rope_tabs, tm, n_per_batch)
            qc, k2c, vtc = _swa_proj(xc, mod, grp_ctx, g[0], w, None, lc, 1)
            o_l = _swa_attn(swa_sink[j], q, k2, vt, k2c, vtc, nb, seq, lc)
            assert not need_ctx
            o_c = None
        w1 = w_ff_in[i].astype(BF16)
        w2 = w_ff_out[i].astype(BF16)
        xl = _outproj(o_l, w_out, xl, g[1], mod, grp_lat, tm)
        xl = _mlp(xl, g[2], g[3], mod, grp_lat, w1, w2, tm)
        if need_ctx:
            xc = _outproj(o_c, w_out, xc, g[1], mod, grp_ctx, lc)
            xc = _mlp(xc, g[2], g[3], mod, grp_ctx, w1, w2, lc)
    return xl.reshape(nb, seq, d)
```

```python
import functools
import math

import jax
import jax.numpy as jnp
from jax import lax
from jax.experimental import pallas as pl
from jax.experimental.pallas import tpu as pltpu

F32 = jnp.float32
BF16 = jnp.bfloat16

GRID_W = 64
ROPE_BASE = 10000.0
NORM_EPS = 1e-6
LOG2E = math.log2(math.e)
NEG_BIG = -1e30
REF_LIMIT = 60.0
NORM_SLACK = 1.03

MLA_HEADS = 16
MLA_Q_LORA = 512
MLA_KV_LORA = 512
MLA_NOPE = 128
MLA_ROPE = 64
MLA_V = 128

SWA_HEADS = 32
SWA_KV_HEADS = 4
SWA_HEAD_DIM = 64
SWA_WINDOW = 128
SWA_GROUP = SWA_HEADS // SWA_KV_HEADS
SWA_QBLK = 128
SWA_STEP_BLOCKS = 4

LANES = 128
HALF_TILE = 64

MOD_ROWS = 8
MOD_TN = 1024
TOKEN_TILE = 512
ROW_CHUNK = 32
ROW_UNROLL = 4
MLP_TILE = 1024
FF_TILE = 512
ATTN_TQ = 512
KV_GROUP = 8
KV_UNROLL = 4
VMEM_LIMIT = 56 * 1024 * 1024

NT_DIMS = (((1,), (1,)), ((), ()))


def _rms(xf, g):
    ms = jnp.mean(xf * xf, axis=-1, keepdims=True)
    return xf * lax.rsqrt(ms + NORM_EPS) * g


def _inv_rms(xf):
    return lax.rsqrt(jnp.mean(xf * xf, axis=-1, keepdims=True) + NORM_EPS)


def _row_chunks(n_rows, fn):
    def body(i, carry):
        fn(pl.ds(pl.multiple_of(i * ROW_CHUNK, ROW_CHUNK), ROW_CHUNK))
        return carry
    lax.fori_loop(0, n_rows // ROW_CHUNK, body, 0, unroll=ROW_UNROLL)


def _modulated_norm(x_ref, g_ref, sc_ref, sh_ref, h_ref):
    gain = g_ref[...] * (1.0 + sc_ref[0])
    shift = sh_ref[0]

    def rows(rs):
        xf = x_ref[rs, :]
        h_ref[rs, :] = (xf * _inv_rms(xf) * gain + shift).astype(h_ref.dtype)
    _row_chunks(h_ref.shape[0], rows)


def _rot(tile, cos, sin):
    return tile * cos + pltpu.roll(tile, HALF_TILE, 1) * sin


def _first_head_lanes(shape):
    lane = lax.broadcasted_iota(jnp.int32, shape, 1)
    return (lane & 32) == 0


def _shift_lane(parity):
    return LANES - 1 - 32 * parity


def _params(sem):
    return pltpu.CompilerParams(dimension_semantics=sem, vmem_limit_bytes=VMEM_LIMIT)


def _const_spec(shape):
    nd = len(shape)
    return pl.BlockSpec(shape, lambda *_: (0,) * nd, pipeline_mode=pl.Buffered(1))


def _mod_spec(grp, which, d):
    return pl.BlockSpec((1, 1, d), lambda i, *_: (grp(i), 0, which))


def _mod_kernel(c_ref, w_ref, b_ref, o_ref):
    c = c_ref[...]
    a = c / (1.0 + jnp.exp(-c))
    o_ref[0] = jnp.dot(a, w_ref[0], preferred_element_type=F32,
                       precision=lax.Precision.HIGHEST) + b_ref[0]


def _modulation(cmat, w_mod, b_mod):
    depth, d, n = w_mod.shape
    return pl.pallas_call(
        _mod_kernel,
        grid=(depth, n // MOD_TN),
        in_specs=[pl.BlockSpec((MOD_ROWS, d), lambda l, j: (0, 0)),
                  pl.BlockSpec((1, d, MOD_TN), lambda l, j: (l, 0, j)),
                  pl.BlockSpec((1, 1, MOD_TN), lambda l, j: (l, 0, j))],
        out_specs=pl.BlockSpec((1, MOD_ROWS, MOD_TN), lambda l, j: (l, 0, j)),
        out_shape=jax.ShapeDtypeStruct((depth, MOD_ROWS, n), F32),
        compiler_params=_params(("arbitrary", "arbitrary")),
        name="modulation",
    )(cmat, w_mod, b_mod.reshape(depth, 1, n))


def _mla_proj_kernel(*refs, rope, qscale):
    if rope:
        (x_ref, g_ref, sc_ref, sh_ref, win_ref, gqa_ref, gkva_ref, wqn_ref, wqr_ref, wkn_ref,
         wvt_ref, cos_ref, sin_ref, cost_ref, sint_ref, qn_ref, qr_ref, kn_ref, kr_ref,
         vt_ref) = refs
        cos, sin = cos_ref[...], sin_ref[...]
    else:
        (x_ref, g_ref, sc_ref, sh_ref, win_ref, gqa_ref, gkva_ref, wqn_ref, wqr_ref, wkn_ref,
         wvt_ref, qn_ref, qr_ref, kn_ref, kr_ref, vt_ref) = refs
    h = (_rms(x_ref[...], g_ref[...]) * (1.0 + sc_ref[0]) + sh_ref[0]).astype(BF16)
    p = jnp.dot(h, win_ref[...], preferred_element_type=F32)
    qa = _rms(p[:, :MLA_Q_LORA], gqa_ref[...]).astype(BF16)
    ckv = _rms(p[:, MLA_Q_LORA:MLA_Q_LORA + MLA_KV_LORA], gkva_ref[...]).astype(BF16)
    kr = p[:, MLA_Q_LORA + MLA_KV_LORA:]

    qn = lax.dot_general(wqn_ref[...], qa, NT_DIMS, preferred_element_type=F32) * qscale
    qn_ref[...] = qn.astype(BF16)
    qr = lax.dot_general(wqr_ref[...], qa, NT_DIMS, preferred_element_type=F32) * qscale
    for t in range(qr.shape[0] // LANES):
        tile = qr[t * LANES:(t + 1) * LANES, :]
        if rope:
            swapped = jnp.concatenate([tile[HALF_TILE:], tile[:HALF_TILE]], axis=0)
            tile = tile * cost_ref[...] + swapped * sint_ref[...]
        qr_ref[t * LANES:(t + 1) * LANES, :] = tile.astype(BF16)

    if rope:
        kr = _rot(kr, cos, sin)
    first = _first_head_lanes(kr.shape)
    lane = lax.broadcasted_iota(jnp.int32, kr.shape, 1)
    kr_ref[:, :LANES] = jnp.where(lane == _shift_lane(0), 1.0,
                                  jnp.where(first, kr, 0.0)).astype(BF16)
    kr_ref[:, LANES:] = jnp.where(lane == _shift_lane(1), 1.0,
                                  jnp.where(first, 0.0, kr)).astype(BF16)

    kn_ref[...] = jnp.dot(ckv, wkn_ref[...], preferred_element_type=F32).astype(BF16)
    vt = lax.dot_general(wvt_ref[...], ckv, NT_DIMS, preferred_element_type=F32).astype(BF16)
    for hd in range(MLA_HEADS):
        vt_ref[0, hd, 0] = vt[hd * MLA_V:(hd + 1) * MLA_V, :]


def _mla_proj(x, mod, grp, g0, w, rope_tabs, tm, n_per_batch):
    t, d = x.shape
    n_tiles = t // tm
    nb = n_tiles // n_per_batch
    rope = rope_tabs is not None
    qscale = (MLA_NOPE + MLA_ROPE) ** -0.5 * LOG2E
    row = lambda i: (i, 0)
    in_specs = [pl.BlockSpec((tm, d), row), _const_spec((1, d)),
                _mod_spec(grp, 1, d), _mod_spec(grp, 0, d),
                _const_spec(w["w_in"].shape), _const_spec((1, MLA_Q_LORA)),
                _const_spec((1, MLA_KV_LORA)), _const_spec(w["w_qn"].shape),
                _const_spec(w["w_qr"].shape), _const_spec(w["w_kn"].shape),
                _const_spec(w["w_vt"].shape)]
    args = [x, g0, mod, mod, w["w_in"], w["g_qa"], w["g_kva"], w["w_qn"], w["w_qr"], w["w_kn"],
            w["w_vt"]]
    if rope:
        pos = lambda i: (i % n_per_batch, 0)
        pos_t = lambda i: (0, i % n_per_batch)
        in_specs += [pl.BlockSpec((tm, LANES), pos), pl.BlockSpec((tm, LANES), pos),
                     pl.BlockSpec((LANES, tm), pos_t), pl.BlockSpec((LANES, tm), pos_t)]
        args += list(rope_tabs)
    hn = MLA_HEADS * MLA_NOPE
    hr = MLA_HEADS * MLA_ROPE
    col = lambda i: (0, i)
    out_shape = [jax.ShapeDtypeStruct((hn, t), BF16), jax.ShapeDtypeStruct((hr, t), BF16),
                 jax.ShapeDtypeStruct((t, hn), BF16), jax.ShapeDtypeStruct((t, 2 * LANES), BF16),
                 jax.ShapeDtypeStruct((nb, MLA_HEADS, n_per_batch, MLA_V, tm), BF16)]
    out_specs = [pl.BlockSpec((hn, tm), col), pl.BlockSpec((hr, tm), col),
                 pl.BlockSpec((tm, hn), row), pl.BlockSpec((tm, 2 * LANES), row),
                 pl.BlockSpec((1, MLA_HEADS, 1, MLA_V, tm),
                              lambda i: (i // n_per_batch, 0, i % n_per_batch, 0, 0))]
    return pl.pallas_call(
        functools.partial(_mla_proj_kernel, rope=rope, qscale=qscale),
        grid=(n_tiles,), in_specs=in_specs, out_specs=out_specs, out_shape=out_shape,
        compiler_params=_params(("arbitrary",)),
        name="mla_proj_lat" if rope else "mla_proj_ctx",
    )(*args)


def _kv_group_sizes(n_chunks):
    if n_chunks <= 2:
        return [n_chunks] if n_chunks else []
    tail = [min(KV_GROUP, n_chunks) - 2, 2]
    body = n_chunks - sum(tail)
    assert body % KV_GROUP == 0
    return [KV_GROUP] * (body // KV_GROUP) + tail


def _mla_attn_kernel(*refs, n_lat, tk):
    if n_lat:
        (qn_ref, qr_ref, knc_ref, krc_ref, vtc_ref, knl_ref, krl_ref, vtl_ref, o_ref,
         acc_ref, s_ref, qa_ref, k2_ref) = refs
    else:
        qn_ref, qr_ref, knc_ref, krc_ref, vtc_ref, o_ref, acc_ref, s_ref, qa_ref, k2_ref = refs
    tq = qn_ref.shape[1]
    lc = knc_ref.shape[1]
    parity = pl.program_id(1) % 2

    def lat_keys(j):
        off = pl.multiple_of(j * tk, tk)
        return knl_ref[0, pl.ds(off, tk), :], krl_ref[0, pl.ds(off, tk), :]

    @pl.when(pl.program_id(2) == 0)
    def _():
        ones = jnp.ones((LANES, LANES), BF16)

        def sqnorm_max(kn, kr):
            r = jnp.dot(kn * kn + kr * kr, ones, preferred_element_type=F32)
            return jnp.max(r, axis=0, keepdims=True)
        mx = sqnorm_max(knc_ref[0], krc_ref[0])
        if n_lat:
            mx = lax.fori_loop(0, n_lat, lambda j, c: jnp.maximum(c, sqnorm_max(*lat_keys(j))), mx,
                               unroll=KV_UNROLL)
        k2_ref[...] = jnp.broadcast_to(mx * NORM_SLACK, k2_ref.shape)

    qn = qn_ref[...]
    qr = qr_ref[...]
    feat = lax.broadcasted_iota(jnp.int32, qr.shape, 0)
    own = ((feat >> 5) & 1) == parity
    qnf, qrf = qn.astype(F32), jnp.where(own, qr.astype(F32), 0.0)
    q2 = jnp.sum(qnf * qnf + qrf * qrf, axis=0, keepdims=True)
    ref = jnp.sqrt(q2 * k2_ref[0:1, 0:1])
    fast = jnp.max(ref) <= REF_LIMIT
    shift = jnp.where(fast, -ref, 0.0).astype(BF16)
    qa_ref[:LANES, :] = qn
    qa_ref[LANES:, :] = jnp.where(feat == LANES - 1 - 32 * parity, shift, qr)

    def scores(kn, kr):
        return jnp.dot(jnp.concatenate([kn, kr], axis=1), qa_ref[...],
                       preferred_element_type=F32)

    def finish(l):
        o_ref[...] = (acc_ref[...] * (1.0 / l)).T.astype(BF16)

    def shifted_softmax():
        def chunk(s, vt, l8):
            p = jnp.exp2(s)
            l8 = l8 + jnp.sum(p.reshape(-1, 8, tq), axis=0)
            return l8, jnp.dot(vt, p.astype(BF16), preferred_element_type=F32)

        sizes = _kv_group_sizes(n_lat)
        starts = [sum(sizes[:g]) for g in range(len(sizes))]
        last = max(len(sizes) - 1, 0)

        def n_rows(g):
            return (sizes[g] * tk if sizes else 0) + (lc if g == last else 0)

        def fill_scores(g):
            view = s_ref.at[g % 2]
            lat = sizes[g] * tk if sizes else 0
            if lat:
                rows = pl.ds(starts[g] * tk, lat)
                view[pl.ds(0, lat), :] = scores(knl_ref[0, rows, :], krl_ref[0, rows, :])
            if g == last:
                view[pl.ds(lat, lc), :] = scores(knc_ref[0], krc_ref[0])

        fill_scores(0)
        l8 = jnp.zeros((8, tq), F32)
        for g in range(last + 1):
            if g < last:
                fill_scores(g + 1)
            vts = [vtl_ref[0, 0, starts[g] + u] for u in range(sizes[g])] if sizes else []
            if g == last:
                vts.append(vtc_ref[0, 0, 0])
            l8, pv = chunk(s_ref[g % 2, pl.ds(0, n_rows(g)), :], jnp.concatenate(vts, axis=1), l8)
            if g == 0:
                acc_ref[...] = pv
            else:
                acc_ref[...] += pv
        finish(jnp.sum(l8, axis=0, keepdims=True))

    def online_softmax():
        def update(s_view, vt, m, l):
            m_new = jnp.maximum(m, jnp.max(s_view[...], axis=0, keepdims=True))
            alpha = jnp.exp2(m - m_new)
            p = jnp.exp2(s_view[...] - m_new)
            l_new = alpha * l + jnp.sum(p, axis=0, keepdims=True)
            acc_ref[...] = acc_ref[...] * alpha + jnp.dot(vt, p.astype(BF16),
                                                          preferred_element_type=F32)
            return m_new, l_new

        acc_ref[...] = jnp.zeros_like(acc_ref)
        m = jnp.full((1, tq), NEG_BIG, F32)
        l = jnp.zeros((1, tq), F32)
        slots = [s_ref.at[0, pl.ds(0, tk)], s_ref.at[1, pl.ds(0, tk)]] if n_lat else None
        if n_lat:
            slots[0][...] = scores(*lat_keys(0))
        ctx_view = s_ref.at[1, pl.ds(0, lc)]
        ctx_view[...] = scores(knc_ref[0], krc_ref[0])
        m, l = update(ctx_view, vtc_ref[0, 0, 0], m, l)
        if n_lat:
            def body(jj, carry):
                for u in range(KV_UNROLL):
                    j = KV_UNROLL * jj + u
                    slots[(u + 1) % 2][...] = scores(*lat_keys(jnp.minimum(j + 1, n_lat - 1)))
                    carry = update(slots[u % 2], vtl_ref[0, 0, j], *carry)
                return carry
            m, l = lax.fori_loop(0, n_lat // KV_UNROLL, body, (m, l))
        finish(l)

    lax.cond(fast, shifted_softmax, online_softmax)


def _mla_attn(qn, qr, ctx_kv, lat_kv, nb, tq):
    t = qn.shape[1]
    nq = t // nb // tq
    knc, krc, vtc = ctx_kv
    lc = knc.shape[1]
    qrow = lambda b, h, i: (b * nq + i, h)
    in_specs = [pl.BlockSpec((LANES, tq), lambda b, h, i: (h, b * nq + i)),
                pl.BlockSpec((LANES, tq), lambda b, h, i: (h // 2, b * nq + i)),
                pl.BlockSpec((1, lc, LANES), lambda b, h, i: (b, 0, h)),
                pl.BlockSpec((1, lc, LANES), lambda b, h, i: (b, 0, h % 2)),
                pl.BlockSpec((1, 1, 1, MLA_V, lc), lambda b, h, i: (b, h, 0, 0, 0))]
    args = [qn, qr, knc, krc, vtc]
    n_lat, tk = 0, 0
    if lat_kv is not None:
        knl, krl, vtl = lat_kv
        s = knl.shape[1]
        n_lat, tk = vtl.shape[2], vtl.shape[4]
        assert n_lat % KV_UNROLL == 0 and tk >= lc
        in_specs += [pl.BlockSpec((1, s, LANES), lambda b, h, i: (b, 0, h)),
                     pl.BlockSpec((1, s, LANES), lambda b, h, i: (b, 0, h % 2)),
                     pl.BlockSpec((1, 1, n_lat, MLA_V, tk), lambda b, h, i: (b, h, 0, 0, 0))]
        args += [knl, krl, vtl]
    scratch = [pltpu.VMEM((MLA_V, tq), F32), pltpu.VMEM((2, max(KV_GROUP * tk, lc), tq), F32),
               pltpu.VMEM((2 * LANES, tq), BF16), pltpu.VMEM((8, LANES), F32)]
    return pl.pallas_call(
        functools.partial(_mla_attn_kernel, n_lat=n_lat, tk=tk),
        grid=(nb, MLA_HEADS, nq), in_specs=in_specs,
        out_specs=pl.BlockSpec((tq, MLA_V), qrow),
        out_shape=jax.ShapeDtypeStruct((t, MLA_HEADS * MLA_V), BF16),
        scratch_shapes=scratch,
        compiler_params=_params(("arbitrary", "arbitrary", "arbitrary")),
        name="mla_attn_lat" if n_lat else "mla_attn_ctx",
    )(*args)


def _outproj_kernel(o_ref, w_ref, x_ref, g_ref, gt_ref, out_ref):
    y = jnp.dot(o_ref[...], w_ref[...], preferred_element_type=F32)
    out_ref[...] = x_ref[...] + gt_ref[0] * _rms(y, g_ref[...])


def _outproj(o, w_out, x, g1, mod, grp, tm):
    t, d = x.shape
    row = lambda i: (i, 0)
    return pl.pallas_call(
        _outproj_kernel,
        grid=(t // tm,),
        in_specs=[pl.BlockSpec((tm, o.shape[1]), row), _const_spec(w_out.shape),
                  pl.BlockSpec((tm, d), row), _const_spec((1, d)), _mod_spec(grp, 2, d)],
        out_specs=pl.BlockSpec((tm, d), row),
        out_shape=jax.ShapeDtypeStruct((t, d), F32),
        compiler_params=_params(("arbitrary",)),
        name="outproj",
    )(o, w_out, x, g1, mod)


def _mlp_kernel(x_ref, g2_ref, sc_ref, sh_ref, w1_ref, w2_ref, g3_ref, gt_ref, out_ref,
                f_ref, acc_ref):
    k = pl.program_id(1)

    def ff_chunk():
        u = jnp.maximum(jnp.dot(f_ref[...], w1_ref[...], preferred_element_type=F32), 0.0)
        return jnp.dot((u * u).astype(BF16), w2_ref[...], preferred_element_type=F32)

    @pl.when(k == 0)
    def _():
        _modulated_norm(x_ref, g2_ref, sc_ref, sh_ref, f_ref)
        acc_ref[...] = ff_chunk()

    @pl.when(k > 0)
    def _():
        acc_ref[...] += ff_chunk()

    @pl.when(k == pl.num_programs(1) - 1)
    def _():
        gain = gt_ref[0] * g3_ref[...]

        def rows(rs):
            y = acc_ref[rs, :]
            out_ref[rs, :] = x_ref[rs, :] + y * _inv_rms(y) * gain
        _row_chunks(acc_ref.shape[0], rows)


def _mlp(x, g2, g3, mod, grp, w1, w2, tm):
    t, d = x.shape
    dff = w1.shape[1]
    row = lambda i, k: (i, 0)
    return pl.pallas_call(
        _mlp_kernel,
        grid=(t // tm, dff // FF_TILE),
        in_specs=[pl.BlockSpec((tm, d), row, pipeline_mode=pl.Buffered(1)), _const_spec((1, d)),
                  _mod_spec(grp, 4, d), _mod_spec(grp, 3, d),
                  pl.BlockSpec((d, FF_TILE), lambda i, k: (0, k)),
                  pl.BlockSpec((FF_TILE, d), lambda i, k: (k, 0)),
                  _const_spec((1, d)), _mod_spec(grp, 5, d)],
        out_specs=pl.BlockSpec((tm, d), row),
        out_shape=jax.ShapeDtypeStruct((t, d), F32),
        scratch_shapes=[pltpu.VMEM((tm, d), BF16), pltpu.VMEM((tm, d), F32)],
        compiler_params=_params(("arbitrary", "arbitrary")),
        name="mlp",
    )(x, g2, mod, mod, w1, w2, g3, mod)


def _swa_proj_kernel(*refs, rope, qscale):
    if rope:
        (x_ref, g_ref, sc_ref, sh_ref, wqt_ref, wk_ref, wvt_ref, cos_ref, sin_ref, cost_ref,
         sint_ref, q_ref, k2_ref, vt_ref) = refs
        cos, sin = cos_ref[...], sin_ref[...]
    else:
        x_ref, g_ref, sc_ref, sh_ref, wqt_ref, wk_ref, wvt_ref, q_ref, k2_ref, vt_ref = refs
    h = (_rms(x_ref[...], g_ref[...]) * (1.0 + sc_ref[0]) + sh_ref[0]).astype(BF16)
    qt = lax.dot_general(wqt_ref[...], h, NT_DIMS, preferred_element_type=F32) * qscale
    for t in range(qt.shape[0] // LANES):
        tile = qt[t * LANES:(t + 1) * LANES, :]
        if rope:
            swapped = jnp.concatenate([tile[HALF_TILE:], tile[:HALF_TILE]], axis=0)
            tile = tile * cost_ref[...] + swapped * sint_ref[...]
        q_ref[t * LANES:(t + 1) * LANES, :] = tile.astype(BF16)
    k = jnp.dot(h, wk_ref[...], preferred_element_type=F32)
    for c in range(SWA_KV_HEADS):
        tile = k[:, c * LANES:(c + 1) * LANES]
        if rope:
            tile = _rot(tile, cos, sin)
        first = _first_head_lanes(tile.shape)
        lane = lax.broadcasted_iota(jnp.int32, tile.shape, 1)
        k2_ref[:, 2 * c * LANES:(2 * c + 1) * LANES] = jnp.where(
            lane == _shift_lane(0), 1.0, jnp.where(first, tile, 0.0)).astype(BF16)
        k2_ref[:, (2 * c + 1) * LANES:(2 * c + 2) * LANES] = jnp.where(
            lane == _shift_lane(1), 1.0, jnp.where(first, 0.0, tile)).astype(BF16)
    vt_ref[...] = lax.dot_general(wvt_ref[...], h, NT_DIMS,
                                  preferred_element_type=F32).astype(BF16)


def _swa_proj(x, mod, grp, g0, w, rope_tabs, tm, n_per_batch):
    t, d = x.shape
    rope = rope_tabs is not None
    row = lambda i: (i, 0)
    dq = SWA_HEADS * SWA_HEAD_DIM
    dkv = SWA_KV_HEADS * SWA_HEAD_DIM
    w_qt, w_k, w_vt = w
    col = lambda i: (0, i)
    in_specs = [pl.BlockSpec((tm, d), row), _const_spec((1, d)),
                _mod_spec(grp, 1, d), _mod_spec(grp, 0, d), _const_spec(w_qt.shape),
                _const_spec(w_k.shape), _const_spec(w_vt.shape)]
    args = [x, g0, mod, mod, w_qt, w_k, w_vt]
    if rope:
        pos = lambda i: (i % n_per_batch, 0)
        pos_t = lambda i: (0, i % n_per_batch)
        in_specs += [pl.BlockSpec((tm, LANES), pos), pl.BlockSpec((tm, LANES), pos),
                     pl.BlockSpec((LANES, tm), pos_t), pl.BlockSpec((LANES, tm), pos_t)]
        args += list(rope_tabs)
    out_shape = [jax.ShapeDtypeStruct((dq, t), BF16),
                 jax.ShapeDtypeStruct((t, 2 * SWA_KV_HEADS * LANES), BF16),
                 jax.ShapeDtypeStruct((dkv, t), BF16)]
    out_specs = [pl.BlockSpec((dq, tm), col), pl.BlockSpec((tm, 2 * SWA_KV_HEADS * LANES), row),
                 pl.BlockSpec((dkv, tm), col)]
    return pl.pallas_call(
        functools.partial(_swa_proj_kernel, rope=rope, qscale=SWA_HEAD_DIM ** -0.5 * LOG2E),
        grid=(t // tm,), in_specs=in_specs, out_specs=out_specs, out_shape=out_shape,
        compiler_params=_params(("arbitrary",)),
        name="swa_proj_lat" if rope else "swa_proj_ctx",
    )(*args)


def _swa_attn_kernel(sink_ref, q_ref, kc_ref, kp_ref, kcur_ref, kn_ref, vc_ref, vp_ref, vcur_ref,
                     vn_ref, o_ref, s_ref, qa_ref, *, qb):
    kvh = pl.program_id(1)
    i = pl.program_id(2)
    lc = kc_ref.shape[0]
    span = SWA_QBLK + 2 * SWA_WINDOW
    n_pairs = SWA_GROUP // 2
    kc, vc = kc_ref[...], vc_ref[...]
    kwin = jnp.concatenate([kp_ref[...], kcur_ref[...], kn_ref[...]], axis=0)
    vwin = jnp.concatenate([vp_ref[...], vcur_ref[...], vn_ref[...]], axis=1)
    r = lax.broadcasted_iota(jnp.int32, (lc + span, SWA_QBLK), 0)
    rel = r - lc - lax.broadcasted_iota(jnp.int32, (lc + span, SWA_QBLK), 1)
    bias = jnp.where((r < lc) | ((rel >= 0) & (rel <= 2 * SWA_WINDOW)), 0.0, NEG_BIG)
    bias = jnp.concatenate([bias] * n_pairs, axis=1)
    pair = lax.broadcasted_iota(jnp.int32, (1, n_pairs * SWA_QBLK), 1) // SWA_QBLK
    keep_prev = jnp.where(i == 0, 0.0, 1.0)
    keep_next = jnp.where(i == pl.num_programs(2) - 1, 0.0, 1.0)

    ones = jnp.ones((LANES, LANES), BF16)
    kall = jnp.concatenate([kc[:, :LANES], kwin[:, :LANES]], axis=0)
    k2 = jnp.max(jnp.dot(kall * kall, ones, preferred_element_type=F32), axis=0, keepdims=True)
    k2 = k2[:, 0:1] * NORM_SLACK
    feat = lax.broadcasted_iota(jnp.int32, (LANES, n_pairs * SWA_QBLK), 0)
    first_rows = (feat & 32) == 0
    q_t, refs = [], []
    for blk in range(qb):
        qt = jnp.concatenate([q_ref[t * LANES:(t + 1) * LANES, blk * SWA_QBLK:(blk + 1) * SWA_QBLK]
                              for t in range(n_pairs)], axis=1)
        sq = qt.astype(F32)
        sq = sq * sq
        q2_first = jnp.sum(jnp.where(first_rows, sq, 0.0), axis=0, keepdims=True)
        q2_second = jnp.sum(sq, axis=0, keepdims=True) - q2_first
        q_t.append(qt)
        refs.append([jnp.sqrt(q2 * k2).astype(BF16) for q2 in (q2_first, q2_second)])
    ref_max = functools.reduce(jnp.maximum, [jnp.max(r.astype(F32)) for pr in refs for r in pr])
    fast = ref_max <= REF_LIMIT
    def set_queries(shifted):
        for blk in range(qb):
            for e in range(2):
                shift = -refs[blk][e] if shifted else jnp.zeros_like(refs[blk][e])
                qa_ref[blk, e] = jnp.where(feat == _shift_lane(e), shift, q_t[blk])

    def scores(blk, e):
        kcat = jnp.concatenate([kc[:, e * LANES:(e + 1) * LANES],
                                kwin[blk * SWA_QBLK:blk * SWA_QBLK + span,
                                     e * LANES:(e + 1) * LANES]], axis=0)
        return jnp.dot(kcat, qa_ref[blk, e], preferred_element_type=F32)

    def run(shifted):
        s_ref[0] = scores(0, 0)
        for blk in range(qb):
            rows = slice(blk * SWA_QBLK, (blk + 1) * SWA_QBLK)
            vt = jnp.concatenate([vc, vwin[:, blk * SWA_QBLK:blk * SWA_QBLK + span]], axis=1)
            halves = []
            for e in range(2):
                if e == 0:
                    s_ref[1] = scores(blk, 1)
                elif blk + 1 < qb:
                    s_ref[0] = scores(blk + 1, 0)
                s = s_ref[e] + bias
                sk = jnp.zeros(pair.shape, F32)
                for t in range(n_pairs):
                    sk = jnp.where(pair == t, sink_ref[kvh * SWA_GROUP + 2 * t + e] * LOG2E, sk)
                if shifted:
                    m = refs[blk][e].astype(F32)
                    p = jnp.exp2(s)
                else:
                    m = jnp.maximum(jnp.max(s, axis=0, keepdims=True), sk)
                    p = jnp.exp2(s - m)
                if blk == 0:
                    p = jnp.concatenate([p[:lc], p[lc:lc + SWA_WINDOW] * keep_prev,
                                         p[lc + SWA_WINDOW:]], axis=0)
                if blk == qb - 1:
                    p = jnp.concatenate([p[:lc + span - SWA_WINDOW],
                                         p[lc + span - SWA_WINDOW:] * keep_next], axis=0)
                den = jnp.sum(p, axis=0, keepdims=True) + jnp.exp2(sk - m)
                o = jnp.dot(vt, p.astype(BF16), preferred_element_type=F32)
                halves.append(o * (1.0 / den))
            both = jnp.concatenate(halves, axis=0)
            for t in range(n_pairs):
                tile = both[:, t * SWA_QBLK:(t + 1) * SWA_QBLK]
                o_ref[rows, t * LANES:(t + 1) * LANES] = tile.T.astype(BF16)

    set_queries(True)
    run(True)

    @pl.when(jnp.logical_not(fast))
    def _():
        set_queries(False)
        run(False)


def _swa_attn(sink, q, k2, vt, k2c, vtc, nb, seq, lc):
    t = q.shape[1]
    nblk = seq // SWA_QBLK
    qb = min(SWA_STEP_BLOCKS, nblk)
    nsteps = nblk // qb
    gq = SWA_GROUP * SWA_HEAD_DIM
    hd = SWA_HEAD_DIM
    prev_blk = lambda b, i: b * nblk + jnp.maximum(qb * i - 1, 0)
    next_blk = lambda b, i: b * nblk + jnp.minimum(qb * i + qb, nblk - 1)
    return pl.pallas_call(
        functools.partial(_swa_attn_kernel, qb=qb),
        grid=(nb, SWA_KV_HEADS, nsteps),
        in_specs=[pl.BlockSpec(memory_space=pltpu.SMEM),
                  pl.BlockSpec((gq, qb * SWA_QBLK), lambda b, h, i: (h, b * nsteps + i)),
                  pl.BlockSpec((lc, 2 * LANES), lambda b, h, i: (b, h)),
                  pl.BlockSpec((SWA_QBLK, 2 * LANES), lambda b, h, i: (prev_blk(b, i), h)),
                  pl.BlockSpec((qb * SWA_QBLK, 2 * LANES), lambda b, h, i: (b * nsteps + i, h)),
                  pl.BlockSpec((SWA_QBLK, 2 * LANES), lambda b, h, i: (next_blk(b, i), h)),
                  pl.BlockSpec((hd, lc), lambda b, h, i: (h, b)),
                  pl.BlockSpec((hd, SWA_QBLK), lambda b, h, i: (h, prev_blk(b, i))),
                  pl.BlockSpec((hd, qb * SWA_QBLK), lambda b, h, i: (h, b * nsteps + i)),
                  pl.BlockSpec((hd, SWA_QBLK), lambda b, h, i: (h, next_blk(b, i)))],
        out_specs=pl.BlockSpec((qb * SWA_QBLK, gq), lambda b, h, i: (b * nsteps + i, h)),
        out_shape=jax.ShapeDtypeStruct((t, SWA_HEADS * SWA_HEAD_DIM), BF16),
        scratch_shapes=[pltpu.VMEM((2, lc + SWA_QBLK + 2 * SWA_WINDOW, gq), F32),
                        pltpu.VMEM((qb, 2, LANES, gq), BF16)],
        compiler_params=_params(("arbitrary", "arbitrary", "arbitrary")),
        name="swa_attn",
    )(sink, q, k2c, k2, k2, k2, vtc, vt, vt, vt)


def _rope_tables(seq):
    rows = seq // GRID_W
    row = jnp.repeat(jnp.arange(rows, dtype=F32), GRID_W)
    col = jnp.tile(jnp.arange(GRID_W, dtype=F32), rows)
    n_freq = MLA_ROPE // 4
    freqs = ROPE_BASE ** (-jnp.arange(n_freq, dtype=F32) / n_freq)
    ang = jnp.concatenate([row[:, None] * freqs, col[:, None] * freqs], axis=-1)
    cos, sin = jnp.cos(ang), jnp.sin(ang)
    cos_t = jnp.concatenate([cos, cos, cos, cos], axis=-1)
    sin_t = jnp.concatenate([-sin, -sin, sin, sin], axis=-1)
    return cos_t, sin_t, cos_t.T, sin_t.T


def _pair_tiles(w, n_heads, half):
    k = w.shape[0]
    x1 = w[:, :, :half].reshape(k, n_heads // 2, 2 * half)
    x2 = w[:, :, half:].reshape(k, n_heads // 2, 2 * half)
    return jnp.concatenate([x1, x2], axis=2).reshape(k, n_heads * 2 * half)


def _mla_weights(w_in, g_qa, g_kva, w_qb, w_kvb):
    half = MLA_ROPE // 2
    lat = MLA_Q_LORA + MLA_KV_LORA
    k1, k2 = w_in[:, lat:lat + half], w_in[:, lat + half:]
    qb = w_qb.reshape(MLA_Q_LORA, MLA_HEADS, MLA_NOPE + MLA_ROPE)
    kvb = w_kvb.reshape(MLA_KV_LORA, MLA_HEADS, MLA_NOPE + MLA_V)
    return {
        "w_in": jnp.concatenate([w_in[:, :lat], k1, k1, k2, k2], axis=1).astype(BF16),
        "g_qa": g_qa.reshape(1, -1), "g_kva": g_kva.reshape(1, -1),
        "w_qn": qb[:, :, :MLA_NOPE].reshape(MLA_Q_LORA, -1).T.astype(BF16),
        "w_qr": _pair_tiles(qb[:, :, MLA_NOPE:], MLA_HEADS, half).T.astype(BF16),
        "w_kn": kvb[:, :, :MLA_NOPE].reshape(MLA_KV_LORA, -1).astype(BF16),
        "w_vt": kvb[:, :, MLA_NOPE:].reshape(MLA_KV_LORA, -1).T.astype(BF16),
    }


def _swa_weights(w_qkv):
    d = w_qkv.shape[0]
    half = SWA_HEAD_DIM // 2
    dq = SWA_HEADS * SWA_HEAD_DIM
    dkv = SWA_KV_HEADS * SWA_HEAD_DIM
    q = _pair_tiles(w_qkv[:, :dq].reshape(d, SWA_HEADS, SWA_HEAD_DIM), SWA_HEADS, half)
    k = w_qkv[:, dq:dq + dkv].reshape(d, SWA_KV_HEADS, SWA_HEAD_DIM)
    k1, k2 = k[:, :, :half], k[:, :, half:]
    k = jnp.concatenate([k1, k1, k2, k2], axis=2).reshape(d, SWA_KV_HEADS * LANES)
    return q.T.astype(BF16), k.astype(BF16), w_qkv[:, dq + dkv:].T.astype(BF16)


def kernel(x, c, ctx, c_ctx, w_mod, b_mod, g_norm, w_ff_in, w_ff_out, mla_w_in, mla_g_qa,
           mla_g_kva, mla_w_qb, mla_w_kvb, mla_w_out, swa_w_qkv, swa_sink, swa_w_out):
    nb, seq, d = x.shape
    lc = ctx.shape[1]
    depth = w_mod.shape[0]
    assert nb + 1 <= MOD_ROWS
    tm = min(TOKEN_TILE, seq)
    tq = min(ATTN_TQ, seq)
    n_per_batch = seq // tm
    grp_lat = lambda i: i // n_per_batch
    grp_ctx = lambda i: nb
    mlp_tm = min(MLP_TILE, seq)
    grp_mlp = lambda i: i // (seq // mlp_tm)

    cmat = jnp.zeros((MOD_ROWS, d), F32).at[:nb].set(c).at[nb].set(c_ctx)
    mod_all = _modulation(cmat, w_mod, b_mod)
    rope_tabs = _rope_tables(seq)

    xl = x.reshape(nb * seq, d)
    xc = ctx.reshape(nb * lc, d)
    for i in range(depth):
        need_ctx = i < depth - 1
        mod = mod_all[i].reshape(MOD_ROWS, 1, 6 * d)
        g = g_norm[i].reshape(4, 1, d)
        j = i // 2
        if i % 2 == 0:
            w = _mla_weights(mla_w_in[j], mla_g_qa[j], mla_g_kva[j], mla_w_qb[j], mla_w_kvb[j])
            w_out = mla_w_out[j].astype(BF16)
            qn, qr, kn, kr, vt = _mla_proj(xl, mod, grp_lat, g[0], w, rope_tabs, tm, n_per_batch)
            qnc, qrc, knc, krc, vtc = _mla_proj(xc, mod, grp_ctx, g[0], w, None, lc, 1)
            ctx_kv = (knc.reshape(nb, lc, -1), krc.reshape(nb, lc, -1), vtc)
            lat_kv = (kn.reshape(nb, seq, -1), kr.reshape(nb, seq, -1), vt)
            o_l = _mla_attn(qn, qr, ctx_kv, lat_kv, nb, tq)
            o_c = _mla_attn(qnc, qrc, ctx_kv, None, nb, lc) if need_ctx else None
        else:
            w = _swa_weights(swa_w_qkv[j])
            w_out = swa_w_out[j].astype(BF16)
            q, k2, vt = _swa_proj(xl, mod, grp_lat, g[0], w, rope_tabs, tm, n_per_batch)
            qc, k2c, vtc = _swa_proj(xc, mod, grp_ctx, g[0], w, None, lc, 1)
            o_l = _swa_attn(swa_sink[j], q, k2, vt, k2c, vtc, nb, seq, lc)
            assert not need_ctx
            o_c = None
        w1 = w_ff_in[i].astype(BF16)
        w2 = w_ff_out[i].astype(BF16)
        xl = _outproj(o_l, w_out, xl, g[1], mod, grp_lat, tm)
        xl = _mlp(xl, g[2], g[3], mod, grp_mlp, w1, w2, mlp_tm)
        if need_ctx:
            xc = _outproj(o_c, w_out, xc, g[1], mod, grp_ctx, lc)
            xc = _mlp(xc, g[2], g[3], mod, grp_ctx, w1, w2, lc)
    return xl.reshape(nb, seq, d)
```

```python
import functools
import math

import jax
import jax.numpy as jnp
from jax import lax
from jax.experimental import pallas as pl
from jax.experimental.pallas import tpu as pltpu

F32 = jnp.float32
BF16 = jnp.bfloat16

GRID_W = 64
ROPE_BASE = 10000.0
NORM_EPS = 1e-6
LOG2E = math.log2(math.e)
NEG_BIG = -1e30
REF_LIMIT = 60.0
NORM_SLACK = 1.03

MLA_HEADS = 16
MLA_Q_LORA = 512
MLA_KV_LORA = 512
MLA_NOPE = 128
MLA_ROPE = 64
MLA_V = 128

SWA_HEADS = 32
SWA_KV_HEADS = 4
SWA_HEAD_DIM = 64
SWA_WINDOW = 128
SWA_GROUP = SWA_HEADS // SWA_KV_HEADS
SWA_QBLK = 128
SWA_STEP_BLOCKS = 4

LANES = 128
HALF_TILE = 64

MOD_ROWS = 8
MOD_TN = 1024
TOKEN_TILE = 512
ROW_CHUNK = 32
ROW_UNROLL = 4
FF_TILE = 1024
ATTN_TQ = 512
KV_GROUP = 8
KV_UNROLL = 4
CAST_BLOCK_BYTES = 8 * 1024 * 1024
VMEM_LIMIT = 56 * 1024 * 1024

NT_DIMS = (((1,), (1,)), ((), ()))


def _rms(xf, g):
    ms = jnp.mean(xf * xf, axis=-1, keepdims=True)
    return xf * lax.rsqrt(ms + NORM_EPS) * g


def _inv_rms(xf):
    return lax.rsqrt(jnp.mean(xf * xf, axis=-1, keepdims=True) + NORM_EPS)


def _row_chunks(n_rows, fn):
    def body(i, carry):
        fn(pl.ds(pl.multiple_of(i * ROW_CHUNK, ROW_CHUNK), ROW_CHUNK))
        return carry
    lax.fori_loop(0, n_rows // ROW_CHUNK, body, 0, unroll=ROW_UNROLL)


def _modulated_norm(x_ref, g_ref, sc_ref, sh_ref, h_ref):
    gain = g_ref[...] * (1.0 + sc_ref[0])
    shift = sh_ref[0]

    def rows(rs):
        xf = x_ref[rs, :]
        h_ref[rs, :] = (xf * _inv_rms(xf) * gain + shift).astype(h_ref.dtype)
    _row_chunks(h_ref.shape[0], rows)


def _rot(tile, cos, sin):
    return tile * cos + pltpu.roll(tile, HALF_TILE, 1) * sin


def _first_head_lanes(shape):
    lane = lax.broadcasted_iota(jnp.int32, shape, 1)
    return (lane & 32) == 0


def _shift_lane(parity):
    return LANES - 1 - 32 * parity


def _params(sem):
    return pltpu.CompilerParams(dimension_semantics=sem, vmem_limit_bytes=VMEM_LIMIT)


def _const_spec(shape):
    nd = len(shape)
    return pl.BlockSpec(shape, lambda *_: (0,) * nd, pipeline_mode=pl.Buffered(1))


def _mod_spec(grp, which, d):
    return pl.BlockSpec((1, 1, d), lambda i, *_: (grp(i), 0, which))


def _mod_kernel(c_ref, w_ref, b_ref, o_ref):
    c = c_ref[...]
    a = c / (1.0 + jnp.exp(-c))
    o_ref[0] = jnp.dot(a, w_ref[0], preferred_element_type=F32,
                       precision=lax.Precision.HIGHEST) + b_ref[0]


def _modulation(cmat, w_mod, b_mod):
    depth, d, n = w_mod.shape
    return pl.pallas_call(
        _mod_kernel,
        grid=(depth, n // MOD_TN),
        in_specs=[pl.BlockSpec((MOD_ROWS, d), lambda l, j: (0, 0)),
                  pl.BlockSpec((1, d, MOD_TN), lambda l, j: (l, 0, j)),
                  pl.BlockSpec((1, 1, MOD_TN), lambda l, j: (l, 0, j))],
        out_specs=pl.BlockSpec((1, MOD_ROWS, MOD_TN), lambda l, j: (l, 0, j)),
        out_shape=jax.ShapeDtypeStruct((depth, MOD_ROWS, n), F32),
        compiler_params=_params(("arbitrary", "arbitrary")),
        name="modulation",
    )(cmat, w_mod, b_mod.reshape(depth, 1, n))


def _cast_kernel(w_ref, o_ref):
    o_ref[...] = w_ref[0].astype(o_ref.dtype)


def _layer_to_bf16(w_stack, layer):
    _, r, c = w_stack.shape
    br = max(8, min(r, CAST_BLOCK_BYTES // (4 * c)))
    assert r % br == 0
    return pl.pallas_call(
        _cast_kernel,
        grid=(r // br,),
        in_specs=[pl.BlockSpec((1, br, c), lambda j: (layer, j, 0))],
        out_specs=pl.BlockSpec((br, c), lambda j: (j, 0)),
        out_shape=jax.ShapeDtypeStruct((r, c), BF16),
        compiler_params=_params(("arbitrary",)),
        name="cast_bf16",
    )(w_stack)


def _mla_proj_kernel(*refs, rope, qscale):
    if rope:
        (x_ref, g_ref, sc_ref, sh_ref, win_ref, gqa_ref, gkva_ref, wqn_ref, wqr_ref, wkn_ref,
         wvt_ref, cos_ref, sin_ref, cost_ref, sint_ref, qn_ref, qr_ref, kn_ref, kr_ref,
         vt_ref) = refs
        cos, sin = cos_ref[...], sin_ref[...]
    else:
        (x_ref, g_ref, sc_ref, sh_ref, win_ref, gqa_ref, gkva_ref, wqn_ref, wqr_ref, wkn_ref,
         wvt_ref, qn_ref, qr_ref, kn_ref, kr_ref, vt_ref) = refs
    h = (_rms(x_ref[...], g_ref[...]) * (1.0 + sc_ref[0]) + sh_ref[0]).astype(BF16)
    p = jnp.dot(h, win_ref[...], preferred_element_type=F32)
    qa = _rms(p[:, :MLA_Q_LORA], gqa_ref[...]).astype(BF16)
    ckv = _rms(p[:, MLA_Q_LORA:MLA_Q_LORA + MLA_KV_LORA], gkva_ref[...]).astype(BF16)
    kr = p[:, MLA_Q_LORA + MLA_KV_LORA:]

    qn = lax.dot_general(wqn_ref[...], qa, NT_DIMS, preferred_element_type=F32) * qscale
    qn_ref[...] = qn.astype(BF16)
    qr = lax.dot_general(wqr_ref[...], qa, NT_DIMS, preferred_element_type=F32) * qscale
    for t in range(qr.shape[0] // LANES):
        tile = qr[t * LANES:(t + 1) * LANES, :]
        if rope:
            swapped = jnp.concatenate([tile[HALF_TILE:], tile[:HALF_TILE]], axis=0)
            tile = tile * cost_ref[...] + swapped * sint_ref[...]
        qr_ref[t * LANES:(t + 1) * LANES, :] = tile.astype(BF16)

    if rope:
        kr = _rot(kr, cos, sin)
    first = _first_head_lanes(kr.shape)
    lane = lax.broadcasted_iota(jnp.int32, kr.shape, 1)
    kr_ref[:, :LANES] = jnp.where(lane == _shift_lane(0), 1.0,
                                  jnp.where(first, kr, 0.0)).astype(BF16)
    kr_ref[:, LANES:] = jnp.where(lane == _shift_lane(1), 1.0,
                                  jnp.where(first, 0.0, kr)).astype(BF16)

    kn_ref[...] = jnp.dot(ckv, wkn_ref[...], preferred_element_type=F32).astype(BF16)
    vt = lax.dot_general(wvt_ref[...], ckv, NT_DIMS, preferred_element_type=F32).astype(BF16)
    for hd in range(MLA_HEADS):
        vt_ref[0, hd, 0] = vt[hd * MLA_V:(hd + 1) * MLA_V, :]


def _mla_proj(x, mod, grp, g0, w, rope_tabs, tm, n_per_batch):
    t, d = x.shape
    n_tiles = t // tm
    nb = n_tiles // n_per_batch
    rope = rope_tabs is not None
    qscale = (MLA_NOPE + MLA_ROPE) ** -0.5 * LOG2E
    row = lambda i: (i, 0)
    in_specs = [pl.BlockSpec((tm, d), row), _const_spec((1, d)),
                _mod_spec(grp, 1, d), _mod_spec(grp, 0, d),
                _const_spec(w["w_in"].shape), _const_spec((1, MLA_Q_LORA)),
                _const_spec((1, MLA_KV_LORA)), _const_spec(w["w_qn"].shape),
                _const_spec(w["w_qr"].shape), _const_spec(w["w_kn"].shape),
                _const_spec(w["w_vt"].shape)]
    args = [x, g0, mod, mod, w["w_in"], w["g_qa"], w["g_kva"], w["w_qn"], w["w_qr"], w["w_kn"],
            w["w_vt"]]
    if rope:
        pos = lambda i: (i % n_per_batch, 0)
        pos_t = lambda i: (0, i % n_per_batch)
        in_specs += [pl.BlockSpec((tm, LANES), pos), pl.BlockSpec((tm, LANES), pos),
                     pl.BlockSpec((LANES, tm), pos_t), pl.BlockSpec((LANES, tm), pos_t)]
        args += list(rope_tabs)
    hn = MLA_HEADS * MLA_NOPE
    hr = MLA_HEADS * MLA_ROPE
    col = lambda i: (0, i)
    out_shape = [jax.ShapeDtypeStruct((hn, t), BF16), jax.ShapeDtypeStruct((hr, t), BF16),
                 jax.ShapeDtypeStruct((t, hn), BF16), jax.ShapeDtypeStruct((t, 2 * LANES), BF16),
                 jax.ShapeDtypeStruct((nb, MLA_HEADS, n_per_batch, MLA_V, tm), BF16)]
    out_specs = [pl.BlockSpec((hn, tm), col), pl.BlockSpec((hr, tm), col),
                 pl.BlockSpec((tm, hn), row), pl.BlockSpec((tm, 2 * LANES), row),
                 pl.BlockSpec((1, MLA_HEADS, 1, MLA_V, tm),
                              lambda i: (i // n_per_batch, 0, i % n_per_batch, 0, 0))]
    return pl.pallas_call(
        functools.partial(_mla_proj_kernel, rope=rope, qscale=qscale),
        grid=(n_tiles,), in_specs=in_specs, out_specs=out_specs, out_shape=out_shape,
        compiler_params=_params(("arbitrary",)),
        name="mla_proj_lat" if rope else "mla_proj_ctx",
    )(*args)


def _kv_group_sizes(n_chunks):
    if n_chunks <= 2:
        return [n_chunks] if n_chunks else []
    tail = [min(KV_GROUP, n_chunks) - 2, 2]
    body = n_chunks - sum(tail)
    assert body % KV_GROUP == 0
    return [KV_GROUP] * (body // KV_GROUP) + tail


def _mla_attn_kernel(*refs, n_lat, tk):
    if n_lat:
        (qn_ref, qr_ref, knc_ref, krc_ref, vtc_ref, knl_ref, krl_ref, vtl_ref, o_ref,
         acc_ref, s_ref, qa_ref, k2_ref) = refs
    else:
        qn_ref, qr_ref, knc_ref, krc_ref, vtc_ref, o_ref, acc_ref, s_ref, qa_ref, k2_ref = refs
    tq = qn_ref.shape[1]
    lc = knc_ref.shape[1]
    parity = pl.program_id(1) % 2

    def lat_keys(j):
        off = pl.multiple_of(j * tk, tk)
        return knl_ref[0, pl.ds(off, tk), :], krl_ref[0, pl.ds(off, tk), :]

    @pl.when(pl.program_id(2) == 0)
    def _():
        ones = jnp.ones((LANES, LANES), BF16)

        def sqnorm_max(kn, kr):
            r = jnp.dot(kn * kn + kr * kr, ones, preferred_element_type=F32)
            return jnp.max(r, axis=0, keepdims=True)
        mx = sqnorm_max(knc_ref[0], krc_ref[0])
        if n_lat:
            mx = lax.fori_loop(0, n_lat, lambda j, c: jnp.maximum(c, sqnorm_max(*lat_keys(j))), mx,
                               unroll=KV_UNROLL)
        k2_ref[...] = jnp.broadcast_to(mx * NORM_SLACK, k2_ref.shape)

    qn = qn_ref[...]
    qr = qr_ref[...]
    feat = lax.broadcasted_iota(jnp.int32, qr.shape, 0)
    own = ((feat >> 5) & 1) == parity
    qnf, qrf = qn.astype(F32), jnp.where(own, qr.astype(F32), 0.0)
    q2 = jnp.sum(qnf * qnf + qrf * qrf, axis=0, keepdims=True)
    ref = jnp.sqrt(q2 * k2_ref[0:1, 0:1])
    fast = jnp.max(ref) <= REF_LIMIT
    shift = jnp.where(fast, -ref, 0.0).astype(BF16)
    qa_ref[:LANES, :] = qn
    qa_ref[LANES:, :] = jnp.where(feat == LANES - 1 - 32 * parity, shift, qr)

    def scores(kn, kr):
        return jnp.dot(jnp.concatenate([kn, kr], axis=1), qa_ref[...],
                       preferred_element_type=F32)

    def finish(l):
        o_ref[...] = (acc_ref[...] * (1.0 / l)).T.astype(BF16)

    def shifted_softmax():
        def chunk(s, vt, l8):
            p = jnp.exp2(s)
            l8 = l8 + jnp.sum(p.reshape(-1, 8, tq), axis=0)
            return l8, jnp.dot(vt, p.astype(BF16), preferred_element_type=F32)

        sizes = _kv_group_sizes(n_lat)
        starts = [sum(sizes[:g]) for g in range(len(sizes))]
        last = max(len(sizes) - 1, 0)

        def n_rows(g):
            return (sizes[g] * tk if sizes else 0) + (lc if g == last else 0)

        def fill_scores(g):
            view = s_ref.at[g % 2]
            lat = sizes[g] * tk if sizes else 0
            if lat:
                rows = pl.ds(starts[g] * tk, lat)
                view[pl.ds(0, lat), :] = scores(knl_ref[0, rows, :], krl_ref[0, rows, :])
            if g == last:
                view[pl.ds(lat, lc), :] = scores(knc_ref[0], krc_ref[0])

        fill_scores(0)
        l8 = jnp.zeros((8, tq), F32)
        for g in range(last + 1):
            if g < last:
                fill_scores(g + 1)
            vts = [vtl_ref[0, 0, starts[g] + u] for u in range(sizes[g])] if sizes else []
            if g == last:
                vts.append(vtc_ref[0, 0, 0])
            l8, pv = chunk(s_ref[g % 2, pl.ds(0, n_rows(g)), :], jnp.concatenate(vts, axis=1), l8)
            if g == 0:
                acc_ref[...] = pv
            else:
                acc_ref[...] += pv
        finish(jnp.sum(l8, axis=0, keepdims=True))

    def online_softmax():
        def update(s_view, vt, m, l):
            m_new = jnp.maximum(m, jnp.max(s_view[...], axis=0, keepdims=True))
            alpha = jnp.exp2(m - m_new)
            p = jnp.exp2(s_view[...] - m_new)
            l_new = alpha * l + jnp.sum(p, axis=0, keepdims=True)
            acc_ref[...] = acc_ref[...] * alpha + jnp.dot(vt, p.astype(BF16),
                                                          preferred_element_type=F32)
            return m_new, l_new

        acc_ref[...] = jnp.zeros_like(acc_ref)
        m = jnp.full((1, tq), NEG_BIG, F32)
        l = jnp.zeros((1, tq), F32)
        slots = [s_ref.at[0, pl.ds(0, tk)], s_ref.at[1, pl.ds(0, tk)]] if n_lat else None
        if n_lat:
            slots[0][...] = scores(*lat_keys(0))
        ctx_view = s_ref.at[1, pl.ds(0, lc)]
        ctx_view[...] = scores(knc_ref[0], krc_ref[0])
        m, l = update(ctx_view, vtc_ref[0, 0, 0], m, l)
        if n_lat:
            def body(jj, carry):
                for u in range(KV_UNROLL):
                    j = KV_UNROLL * jj + u
                    slots[(u + 1) % 2][...] = scores(*lat_keys(jnp.minimum(j + 1, n_lat - 1)))
                    carry = update(slots[u % 2], vtl_ref[0, 0, j], *carry)
                return carry
            m, l = lax.fori_loop(0, n_lat // KV_UNROLL, body, (m, l))
        finish(l)

    lax.cond(fast, shifted_softmax, online_softmax)


def _mla_attn(qn, qr, ctx_kv, lat_kv, nb, tq):
    t = qn.shape[1]
    nq = t // nb // tq
    knc, krc, vtc = ctx_kv
    lc = knc.shape[1]
    qrow = lambda b, h, i: (b * nq + i, h)
    in_specs = [pl.BlockSpec((LANES, tq), lambda b, h, i: (h, b * nq + i)),
                pl.BlockSpec((LANES, tq), lambda b, h, i: (h // 2, b * nq + i)),
                pl.BlockSpec((1, lc, LANES), lambda b, h, i: (b, 0, h)),
                pl.BlockSpec((1, lc, LANES), lambda b, h, i: (b, 0, h % 2)),
                pl.BlockSpec((1, 1, 1, MLA_V, lc), lambda b, h, i: (b, h, 0, 0, 0))]
    args = [qn, qr, knc, krc, vtc]
    n_lat, tk = 0, 0
    if lat_kv is not None:
        knl, krl, vtl = lat_kv
        s = knl.shape[1]
        n_lat, tk = vtl.shape[2], vtl.shape[4]
        assert n_lat % KV_UNROLL == 0 and tk >= lc
        in_specs += [pl.BlockSpec((1, s, LANES), lambda b, h, i: (b, 0, h)),
                     pl.BlockSpec((1, s, LANES), lambda b, h, i: (b, 0, h % 2)),
                     pl.BlockSpec((1, 1, n_lat, MLA_V, tk), lambda b, h, i: (b, h, 0, 0, 0))]
        args += [knl, krl, vtl]
    scratch = [pltpu.VMEM((MLA_V, tq), F32), pltpu.VMEM((2, max(KV_GROUP * tk, lc), tq), F32),
               pltpu.VMEM((2 * LANES, tq), BF16), pltpu.VMEM((8, LANES), F32)]
    return pl.pallas_call(
        functools.partial(_mla_attn_kernel, n_lat=n_lat, tk=tk),
        grid=(nb, MLA_HEADS, nq), in_specs=in_specs,
        out_specs=pl.BlockSpec((tq, MLA_V), qrow),
        out_shape=jax.ShapeDtypeStruct((t, MLA_HEADS * MLA_V), BF16),
        scratch_shapes=scratch,
        compiler_params=_params(("arbitrary", "arbitrary", "arbitrary")),
        name="mla_attn_lat" if n_lat else "mla_attn_ctx",
    )(*args)


def _outproj_kernel(o_ref, w_ref, x_ref, g_ref, gt_ref, out_ref):
    y = jnp.dot(o_ref[...], w_ref[...], preferred_element_type=F32)
    out_ref[...] = x_ref[...] + gt_ref[0] * _rms(y, g_ref[...])


def _outproj(o, w_out, x, g1, mod, grp, tm):
    t, d = x.shape
    row = lambda i: (i, 0)
    return pl.pallas_call(
        _outproj_kernel,
        grid=(t // tm,),
        in_specs=[pl.BlockSpec((tm, o.shape[1]), row), _const_spec(w_out.shape),
                  pl.BlockSpec((tm, d), row), _const_spec((1, d)), _mod_spec(grp, 2, d)],
        out_specs=pl.BlockSpec((tm, d), row),
        out_shape=jax.ShapeDtypeStruct((t, d), F32),
        compiler_params=_params(("arbitrary",)),
        name="outproj",
    )(o, w_out, x, g1, mod)


def _mlp_kernel(x_ref, g2_ref, sc_ref, sh_ref, w1_ref, w2_ref, g3_ref, gt_ref, out_ref,
                f_ref, acc_ref):
    k = pl.program_id(1)

    def ff_chunk():
        u = jnp.maximum(jnp.dot(f_ref[...], w1_ref[...], preferred_element_type=F32), 0.0)
        return jnp.dot((u * u).astype(BF16), w2_ref[...], preferred_element_type=F32)

    @pl.when(k == 0)
    def _():
        _modulated_norm(x_ref, g2_ref, sc_ref, sh_ref, f_ref)
        acc_ref[...] = ff_chunk()

    @pl.when(k > 0)
    def _():
        acc_ref[...] += ff_chunk()

    @pl.when(k == pl.num_programs(1) - 1)
    def _():
        gain = gt_ref[0] * g3_ref[...]

        def rows(rs):
            y = acc_ref[rs, :]
            out_ref[rs, :] = x_ref[rs, :] + y * _inv_rms(y) * gain
        _row_chunks(acc_ref.shape[0], rows)


def _mlp(x, g2, g3, mod, grp, w1, w2, tm):
    t, d = x.shape
    dff = w1.shape[1]
    row = lambda i, k: (i, 0)
    return pl.pallas_call(
        _mlp_kernel,
        grid=(t // tm, dff // FF_TILE),
        in_specs=[pl.BlockSpec((tm, d), row), _const_spec((1, d)),
                  _mod_spec(grp, 4, d), _mod_spec(grp, 3, d),
                  pl.BlockSpec((d, FF_TILE), lambda i, k: (0, k)),
                  pl.BlockSpec((FF_TILE, d), lambda i, k: (k, 0)),
                  _const_spec((1, d)), _mod_spec(grp, 5, d)],
        out_specs=pl.BlockSpec((tm, d), row),
        out_shape=jax.ShapeDtypeStruct((t, d), F32),
        scratch_shapes=[pltpu.VMEM((tm, d), BF16), pltpu.VMEM((tm, d), F32)],
        compiler_params=_params(("arbitrary", "arbitrary")),
        name="mlp",
    )(x, g2, mod, mod, w1, w2, g3, mod)


def _swa_proj_kernel(*refs, rope, qscale):
    if rope:
        (x_ref, g_ref, sc_ref, sh_ref, wqt_ref, wk_ref, wvt_ref, cos_ref, sin_ref, cost_ref,
         sint_ref, q_ref, k2_ref, vt_ref) = refs
        cos, sin = cos_ref[...], sin_ref[...]
    else:
        x_ref, g_ref, sc_ref, sh_ref, wqt_ref, wk_ref, wvt_ref, q_ref, k2_ref, vt_ref = refs
    h = (_rms(x_ref[...], g_ref[...]) * (1.0 + sc_ref[0]) + sh_ref[0]).astype(BF16)
    qt = lax.dot_general(wqt_ref[...], h, NT_DIMS, preferred_element_type=F32) * qscale
    for t in range(qt.shape[0] // LANES):
        tile = qt[t * LANES:(t + 1) * LANES, :]
        if rope:
            swapped = jnp.concatenate([tile[HALF_TILE:], tile[:HALF_TILE]], axis=0)
            tile = tile * cost_ref[...] + swapped * sint_ref[...]
        q_ref[t * LANES:(t + 1) * LANES, :] = tile.astype(BF16)
    k = jnp.dot(h, wk_ref[...], preferred_element_type=F32)
    for c in range(SWA_KV_HEADS):
        tile = k[:, c * LANES:(c + 1) * LANES]
        if rope:
            tile = _rot(tile, cos, sin)
        first = _first_head_lanes(tile.shape)
        lane = lax.broadcasted_iota(jnp.int32, tile.shape, 1)
        k2_ref[:, 2 * c * LANES:(2 * c + 1) * LANES] = jnp.where(
            lane == _shift_lane(0), 1.0, jnp.where(first, tile, 0.0)).astype(BF16)
        k2_ref[:, (2 * c + 1) * LANES:(2 * c + 2) * LANES] = jnp.where(
            lane == _shift_lane(1), 1.0, jnp.where(first, 0.0, tile)).astype(BF16)
    vt_ref[...] = lax.dot_general(wvt_ref[...], h, NT_DIMS,
                                  preferred_element_type=F32).astype(BF16)


def _swa_proj(x, mod, grp, g0, w, rope_tabs, tm, n_per_batch):
    t, d = x.shape
    rope = rope_tabs is not None
    row = lambda i: (i, 0)
    dq = SWA_HEADS * SWA_HEAD_DIM
    dkv = SWA_KV_HEADS * SWA_HEAD_DIM
    w_qt, w_k, w_vt = w
    col = lambda i: (0, i)
    in_specs = [pl.BlockSpec((tm, d), row), _const_spec((1, d)),
                _mod_spec(grp, 1, d), _mod_spec(grp, 0, d), _const_spec(w_qt.shape),
                _const_spec(w_k.shape), _const_spec(w_vt.shape)]
    args = [x, g0, mod, mod, w_qt, w_k, w_vt]
    if rope:
        pos = lambda i: (i % n_per_batch, 0)
        pos_t = lambda i: (0, i % n_per_batch)
        in_specs += [pl.BlockSpec((tm, LANES), pos), pl.BlockSpec((tm, LANES), pos),
                     pl.BlockSpec((LANES, tm), pos_t), pl.BlockSpec((LANES, tm), pos_t)]
        args += list(rope_tabs)
    out_shape = [jax.ShapeDtypeStruct((dq, t), BF16),
                 jax.ShapeDtypeStruct((t, 2 * SWA_KV_HEADS * LANES), BF16),
                 jax.ShapeDtypeStruct((dkv, t), BF16)]
    out_specs = [pl.BlockSpec((dq, tm), col), pl.BlockSpec((tm, 2 * SWA_KV_HEADS * LANES), row),
                 pl.BlockSpec((dkv, tm), col)]
    return pl.pallas_call(
        functools.partial(_swa_proj_kernel, rope=rope, qscale=SWA_HEAD_DIM ** -0.5 * LOG2E),
        grid=(t // tm,), in_specs=in_specs, out_specs=out_specs, out_shape=out_shape,
        compiler_params=_params(("arbitrary",)),
        name="swa_proj_lat" if rope else "swa_proj_ctx",
    )(*args)


def _swa_attn_kernel(sink_ref, q_ref, kc_ref, kp_ref, kcur_ref, kn_ref, vc_ref, vp_ref, vcur_ref,
                     vn_ref, o_ref, s_ref, qa_ref, *, qb):
    kvh = pl.program_id(1)
    i = pl.program_id(2)
    lc = kc_ref.shape[0]
    span = SWA_QBLK + 2 * SWA_WINDOW
    n_pairs = SWA_GROUP // 2
    kc, vc = kc_ref[...], vc_ref[...]
    kwin = jnp.concatenate([kp_ref[...], kcur_ref[...], kn_ref[...]], axis=0)
    vwin = jnp.concatenate([vp_ref[...], vcur_ref[...], vn_ref[...]], axis=1)
    r = lax.broadcasted_iota(jnp.int32, (lc + span, SWA_QBLK), 0)
    rel = r - lc - lax.broadcasted_iota(jnp.int32, (lc + span, SWA_QBLK), 1)
    bias = jnp.where((r < lc) | ((rel >= 0) & (rel <= 2 * SWA_WINDOW)), 0.0, NEG_BIG)
    bias = jnp.concatenate([bias] * n_pairs, axis=1)
    pair = lax.broadcasted_iota(jnp.int32, (1, n_pairs * SWA_QBLK), 1) // SWA_QBLK
    keep_prev = jnp.where(i == 0, 0.0, 1.0)
    keep_next = jnp.where(i == pl.num_programs(2) - 1, 0.0, 1.0)

    ones = jnp.ones((LANES, LANES), BF16)
    kall = jnp.concatenate([kc[:, :LANES], kwin[:, :LANES]], axis=0)
    k2 = jnp.max(jnp.dot(kall * kall, ones, preferred_element_type=F32), axis=0, keepdims=True)
    k2 = k2[:, 0:1] * NORM_SLACK
    feat = lax.broadcasted_iota(jnp.int32, (LANES, n_pairs * SWA_QBLK), 0)
    first_rows = (feat & 32) == 0
    q_t, refs = [], []
    for blk in range(qb):
        qt = jnp.concatenate([q_ref[t * LANES:(t + 1) * LANES, blk * SWA_QBLK:(blk + 1) * SWA_QBLK]
                              for t in range(n_pairs)], axis=1)
        sq = qt.astype(F32)
        sq = sq * sq
        q2_first = jnp.sum(jnp.where(first_rows, sq, 0.0), axis=0, keepdims=True)
        q2_second = jnp.sum(sq, axis=0, keepdims=True) - q2_first
        q_t.append(qt)
        refs.append([jnp.sqrt(q2 * k2).astype(BF16) for q2 in (q2_first, q2_second)])
    ref_max = functools.reduce(jnp.maximum, [jnp.max(r.astype(F32)) for pr in refs for r in pr])
    fast = ref_max <= REF_LIMIT
    def set_queries(shifted):
        for blk in range(qb):
            for e in range(2):
                shift = -refs[blk][e] if shifted else jnp.zeros_like(refs[blk][e])
                qa_ref[blk, e] = jnp.where(feat == _shift_lane(e), shift, q_t[blk])

    def scores(blk, e):
        kcat = jnp.concatenate([kc[:, e * LANES:(e + 1) * LANES],
                                kwin[blk * SWA_QBLK:blk * SWA_QBLK + span,
                                     e * LANES:(e + 1) * LANES]], axis=0)
        return jnp.dot(kcat, qa_ref[blk, e], preferred_element_type=F32)

    def run(shifted):
        s_ref[0] = scores(0, 0)
        for blk in range(qb):
            rows = slice(blk * SWA_QBLK, (blk + 1) * SWA_QBLK)
            vt = jnp.concatenate([vc, vwin[:, blk * SWA_QBLK:blk * SWA_QBLK + span]], axis=1)
            halves = []
            for e in range(2):
                if e == 0:
                    s_ref[1] = scores(blk, 1)
                elif blk + 1 < qb:
                    s_ref[0] = scores(blk + 1, 0)
                s = s_ref[e] + bias
                sk = jnp.zeros(pair.shape, F32)
                for t in range(n_pairs):
                    sk = jnp.where(pair == t, sink_ref[kvh * SWA_GROUP + 2 * t + e] * LOG2E, sk)
                if shifted:
                    m = refs[blk][e].astype(F32)
                    p = jnp.exp2(s)
                else:
                    m = jnp.maximum(jnp.max(s, axis=0, keepdims=True), sk)
                    p = jnp.exp2(s - m)
                if blk == 0:
                    p = jnp.concatenate([p[:lc], p[lc:lc + SWA_WINDOW] * keep_prev,
                                         p[lc + SWA_WINDOW:]], axis=0)
                if blk == qb - 1:
                    p = jnp.concatenate([p[:lc + span - SWA_WINDOW],
                                         p[lc + span - SWA_WINDOW:] * keep_next], axis=0)
                den = jnp.sum(p, axis=0, keepdims=True) + jnp.exp2(sk - m)
                o = jnp.dot(vt, p.astype(BF16), preferred_element_type=F32)
                halves.append(o * (1.0 / den))
            both = jnp.concatenate(halves, axis=0)
            for t in range(n_pairs):
                tile = both[:, t * SWA_QBLK:(t + 1) * SWA_QBLK]
                o_ref[rows, t * LANES:(t + 1) * LANES] = tile.T.astype(BF16)

    set_queries(True)
    run(True)

    @pl.when(jnp.logical_not(fast))
    def _():
        set_queries(False)
        run(False)


def _swa_attn(sink, q, k2, vt, k2c, vtc, nb, seq, lc):
    t = q.shape[1]
    nblk = seq // SWA_QBLK
    qb = min(SWA_STEP_BLOCKS, nblk)
    nsteps = nblk // qb
    gq = SWA_GROUP * SWA_HEAD_DIM
    hd = SWA_HEAD_DIM
    prev_blk = lambda b, i: b * nblk + jnp.maximum(qb * i - 1, 0)
    next_blk = lambda b, i: b * nblk + jnp.minimum(qb * i + qb, nblk - 1)
    return pl.pallas_call(
        functools.partial(_swa_attn_kernel, qb=qb),
        grid=(nb, SWA_KV_HEADS, nsteps),
        in_specs=[pl.BlockSpec(memory_space=pltpu.SMEM),
                  pl.BlockSpec((gq, qb * SWA_QBLK), lambda b, h, i: (h, b * nsteps + i)),
                  pl.BlockSpec((lc, 2 * LANES), lambda b, h, i: (b, h)),
                  pl.BlockSpec((SWA_QBLK, 2 * LANES), lambda b, h, i: (prev_blk(b, i), h)),
                  pl.BlockSpec((qb * SWA_QBLK, 2 * LANES), lambda b, h, i: (b * nsteps + i, h)),
                  pl.BlockSpec((SWA_QBLK, 2 * LANES), lambda b, h, i: (next_blk(b, i), h)),
                  pl.BlockSpec((hd, lc), lambda b, h, i: (h, b)),
                  pl.BlockSpec((hd, SWA_QBLK), lambda b, h, i: (h, prev_blk(b, i))),
                  pl.BlockSpec((hd, qb * SWA_QBLK), lambda b, h, i: (h, b * nsteps + i)),
                  pl.BlockSpec((hd, SWA_QBLK), lambda b, h, i: (h, next_blk(b, i)))],
        out_specs=pl.BlockSpec((qb * SWA_QBLK, gq), lambda b, h, i: (b * nsteps + i, h)),
        out_shape=jax.ShapeDtypeStruct((t, SWA_HEADS * SWA_HEAD_DIM), BF16),
        scratch_shapes=[pltpu.VMEM((2, lc + SWA_QBLK + 2 * SWA_WINDOW, gq), F32),
                        pltpu.VMEM((qb, 2, LANES, gq), BF16)],
        compiler_params=_params(("arbitrary", "arbitrary", "arbitrary")),
        name="swa_attn",
    )(sink, q, k2c, k2, k2, k2, vtc, vt, vt, vt)


def _rope_tables(seq):
    rows = seq // GRID_W
    row = jnp.repeat(jnp.arange(rows, dtype=F32), GRID_W)
    col = jnp.tile(jnp.arange(GRID_W, dtype=F32), rows)
    n_freq = MLA_ROPE // 4
    freqs = ROPE_BASE ** (-jnp.arange(n_freq, dtype=F32) / n_freq)
    ang = jnp.concatenate([row[:, None] * freqs, col[:, None] * freqs], axis=-1)
    cos, sin = jnp.cos(ang), jnp.sin(ang)
    cos_t = jnp.concatenate([cos, cos, cos, cos], axis=-1)
    sin_t = jnp.concatenate([-sin, -sin, sin, sin], axis=-1)
    return cos_t, sin_t, cos_t.T, sin_t.T


def _pair_tiles(w, n_heads, half):
    k = w.shape[0]
    x1 = w[:, :, :half].reshape(k, n_heads // 2, 2 * half)
    x2 = w[:, :, half:].reshape(k, n_heads // 2, 2 * half)
    return jnp.concatenate([x1, x2], axis=2).reshape(k, n_heads * 2 * half)


def _mla_weights(w_in, g_qa, g_kva, w_qb, w_kvb):
    half = MLA_ROPE // 2
    lat = MLA_Q_LORA + MLA_KV_LORA
    k1, k2 = w_in[:, lat:lat + half], w_in[:, lat + half:]
    qb = w_qb.reshape(MLA_Q_LORA, MLA_HEADS, MLA_NOPE + MLA_ROPE)
    kvb = w_kvb.reshape(MLA_KV_LORA, MLA_HEADS, MLA_NOPE + MLA_V)
    return {
        "w_in": jnp.concatenate([w_in[:, :lat], k1, k1, k2, k2], axis=1).astype(BF16),
        "g_qa": g_qa.reshape(1, -1), "g_kva": g_kva.reshape(1, -1),
        "w_qn": qb[:, :, :MLA_NOPE].reshape(MLA_Q_LORA, -1).T.astype(BF16),
        "w_qr": _pair_tiles(qb[:, :, MLA_NOPE:], MLA_HEADS, half).T.astype(BF16),
        "w_kn": kvb[:, :, :MLA_NOPE].reshape(MLA_KV_LORA, -1).astype(BF16),
        "w_vt": kvb[:, :, MLA_NOPE:].reshape(MLA_KV_LORA, -1).T.astype(BF16),
    }


def _swa_weights(w_qkv):
    d = w_qkv.shape[0]
    half = SWA_HEAD_DIM // 2
    dq = SWA_HEADS * SWA_HEAD_DIM
    dkv = SWA_KV_HEADS * SWA_HEAD_DIM
    q = _pair_tiles(w_qkv[:, :dq].reshape(d, SWA_HEADS, SWA_HEAD_DIM), SWA_HEADS, half)
    k = w_qkv[:, dq:dq + dkv].reshape(d, SWA_KV_HEADS, SWA_HEAD_DIM)
    k1, k2 = k[:, :, :half], k[:, :, half:]
    k = jnp.concatenate([k1, k1, k2, k2], axis=2).reshape(d, SWA_KV_HEADS * LANES)
    return q.T.astype(BF16), k.astype(BF16), w_qkv[:, dq + dkv:].T.astype(BF16)


def kernel(x, c, ctx, c_ctx, w_mod, b_mod, g_norm, w_ff_in, w_ff_out, mla_w_in, mla_g_qa,
           mla_g_kva, mla_w_qb, mla_w_kvb, mla_w_out, swa_w_qkv, swa_sink, swa_w_out):
    nb, seq, d = x.shape
    lc = ctx.shape[1]
    depth = w_mod.shape[0]
    assert nb + 1 <= MOD_ROWS
    tm = min(TOKEN_TILE, seq)
    tq = min(ATTN_TQ, seq)
    n_per_batch = seq // tm
    grp_lat = lambda i: i // n_per_batch
    grp_ctx = lambda i: nb

    cmat = jnp.zeros((MOD_ROWS, d), F32).at[:nb].set(c).at[nb].set(c_ctx)
    mod_all = _modulation(cmat, w_mod, b_mod)
    rope_tabs = _rope_tables(seq)

    xl = x.reshape(nb * seq, d)
    xc = ctx.reshape(nb * lc, d)
    for i in range(depth):
        need_ctx = i < depth - 1
        mod = mod_all[i].reshape(MOD_ROWS, 1, 6 * d)
        g = g_norm[i].reshape(4, 1, d)
        j = i // 2
        if i % 2 == 0:
            w = _mla_weights(mla_w_in[j], mla_g_qa[j], mla_g_kva[j], mla_w_qb[j], mla_w_kvb[j])
            w_out = mla_w_out[j].astype(BF16)
            qn, qr, kn, kr, vt = _mla_proj(xl, mod, grp_lat, g[0], w, rope_tabs, tm, n_per_batch)
            qnc, qrc, knc, krc, vtc = _mla_proj(xc, mod, grp_ctx, g[0], w, None, lc, 1)
            ctx_kv = (knc.reshape(nb, lc, -1), krc.reshape(nb, lc, -1), vtc)
            lat_kv = (kn.reshape(nb, seq, -1), kr.reshape(nb, seq, -1), vt)
            o_l = _mla_attn(qn, qr, ctx_kv, lat_kv, nb, tq)
            o_c = _mla_attn(qnc, qrc, ctx_kv, None, nb, lc) if need_ctx else None
        else:
            w = _swa_weights(swa_w_qkv[j])
            w_out = swa_w_out[j].astype(BF16)
            q, k2, vt = _swa_proj(xl, mod, grp_lat, g[0], w, rope_tabs, tm, n_per_batch)
            qc, k2c, vtc = _swa_proj(xc, mod, grp_ctx, g[0], w, None, lc, 1)
            o_l = _swa_attn(swa_sink[j], q, k2, vt, k2c, vtc, nb, seq, lc)
            assert not need_ctx
            o_c = None
        w1 = _layer_to_bf16(w_ff_in, i)
        w2 = _layer_to_bf16(w_ff_out, i)
        xl = _outproj(o_l, w_out, xl, g[1], mod, grp_lat, tm)
        xl = _mlp(xl, g[2], g[3], mod, grp_lat, w1, w2, tm)
        if need_ctx:
            xc = _outproj(o_c, w_out, xc, g[1], mod, grp_ctx, lc)
            xc = _mlp(xc, g[2], g[3], mod, grp_ctx, w1, w2, lc)
    return xl.reshape(nb, seq, d)
```

```python
import functools
import math

import jax
import jax.numpy as jnp
from jax import lax
from jax.experimental import pallas as pl
from jax.experimental.pallas import tpu as pltpu

F32 = jnp.float32
BF16 = jnp.bfloat16

GRID_W = 64
ROPE_BASE = 10000.0
NORM_EPS = 1e-6
LOG2E = math.log2(math.e)
NEG_BIG = -1e30
REF_LIMIT = 60.0
NORM_SLACK = 1.03

MLA_HEADS = 16
MLA_Q_LORA = 512
MLA_KV_LORA = 512
MLA_NOPE = 128
MLA_ROPE = 64
MLA_V = 128

SWA_HEADS = 32
SWA_KV_HEADS = 4
SWA_HEAD_DIM = 64
SWA_WINDOW = 128
SWA_GROUP = SWA_HEADS // SWA_KV_HEADS
SWA_QBLK = 128
SWA_STEP_BLOCKS = 8

LANES = 128
HALF_TILE = 64

MOD_ROWS = 8
MOD_TN = 1024
TOKEN_TILE = 512
ROW_CHUNK = 32
ROW_UNROLL = 4
FF_TILE = 1024
ATTN_TQ = 512
KV_GROUP = 8
KV_UNROLL = 4
CAST_BLOCK_BYTES = 8 * 1024 * 1024
VMEM_LIMIT = 56 * 1024 * 1024

NT_DIMS = (((1,), (1,)), ((), ()))


def _rms(xf, g):
    ms = jnp.mean(xf * xf, axis=-1, keepdims=True)
    return xf * lax.rsqrt(ms + NORM_EPS) * g


def _inv_rms(xf):
    return lax.rsqrt(jnp.mean(xf * xf, axis=-1, keepdims=True) + NORM_EPS)


def _row_chunks(n_rows, fn):
    def body(i, carry):
        fn(pl.ds(pl.multiple_of(i * ROW_CHUNK, ROW_CHUNK), ROW_CHUNK))
        return carry
    lax.fori_loop(0, n_rows // ROW_CHUNK, body, 0, unroll=ROW_UNROLL)


def _modulated_norm(x_ref, g_ref, sc_ref, sh_ref, h_ref):
    gain = g_ref[...] * (1.0 + sc_ref[0])
    shift = sh_ref[0]

    def rows(rs):
        xf = x_ref[rs, :]
        h_ref[rs, :] = (xf * _inv_rms(xf) * gain + shift).astype(h_ref.dtype)
    _row_chunks(h_ref.shape[0], rows)


def _rot(tile, cos, sin):
    return tile * cos + pltpu.roll(tile, HALF_TILE, 1) * sin


def _first_head_lanes(shape):
    lane = lax.broadcasted_iota(jnp.int32, shape, 1)
    return (lane & 32) == 0


def _shift_lane(parity):
    return LANES - 1 - 32 * parity


def _params(sem):
    return pltpu.CompilerParams(dimension_semantics=sem, vmem_limit_bytes=VMEM_LIMIT)


def _const_spec(shape):
    nd = len(shape)
    return pl.BlockSpec(shape, lambda *_: (0,) * nd, pipeline_mode=pl.Buffered(1))


def _mod_spec(grp, which, d):
    return pl.BlockSpec((1, 1, d), lambda i, *_: (grp(i), 0, which))


def _mod_kernel(c_ref, w_ref, b_ref, o_ref):
    c = c_ref[...]
    a = c / (1.0 + jnp.exp(-c))
    o_ref[0] = jnp.dot(a, w_ref[0], preferred_element_type=F32,
                       precision=lax.Precision.HIGHEST) + b_ref[0]


def _modulation(cmat, w_mod, b_mod):
    depth, d, n = w_mod.shape
    return pl.pallas_call(
        _mod_kernel,
        grid=(depth, n // MOD_TN),
        in_specs=[pl.BlockSpec((MOD_ROWS, d), lambda l, j: (0, 0)),
                  pl.BlockSpec((1, d, MOD_TN), lambda l, j: (l, 0, j)),
                  pl.BlockSpec((1, 1, MOD_TN), lambda l, j: (l, 0, j))],
        out_specs=pl.BlockSpec((1, MOD_ROWS, MOD_TN), lambda l, j: (l, 0, j)),
        out_shape=jax.ShapeDtypeStruct((depth, MOD_ROWS, n), F32),
        compiler_params=_params(("arbitrary", "arbitrary")),
        name="modulation",
    )(cmat, w_mod, b_mod.reshape(depth, 1, n))


def _cast_kernel(w_ref, o_ref):
    o_ref[...] = w_ref[0].astype(o_ref.dtype)


def _layer_to_bf16(w_stack, layer):
    _, r, c = w_stack.shape
    br = max(8, min(r, CAST_BLOCK_BYTES // (4 * c)))
    assert r % br == 0
    return pl.pallas_call(
        _cast_kernel,
        grid=(r // br,),
        in_specs=[pl.BlockSpec((1, br, c), lambda j: (layer, j, 0))],
        out_specs=pl.BlockSpec((br, c), lambda j: (j, 0)),
        out_shape=jax.ShapeDtypeStruct((r, c), BF16),
        compiler_params=_params(("arbitrary",)),
        name="cast_bf16",
    )(w_stack)


def _mla_proj_kernel(*refs, rope, qscale):
    if rope:
        (x_ref, g_ref, sc_ref, sh_ref, win_ref, gqa_ref, gkva_ref, wqn_ref, wqr_ref, wkn_ref,
         wvt_ref, cos_ref, sin_ref, cost_ref, sint_ref, qn_ref, qr_ref, kn_ref, kr_ref,
         vt_ref) = refs
        cos, sin = cos_ref[...], sin_ref[...]
    else:
        (x_ref, g_ref, sc_ref, sh_ref, win_ref, gqa_ref, gkva_ref, wqn_ref, wqr_ref, wkn_ref,
         wvt_ref, qn_ref, qr_ref, kn_ref, kr_ref, vt_ref) = refs
    h = (_rms(x_ref[...], g_ref[...]) * (1.0 + sc_ref[0]) + sh_ref[0]).astype(BF16)
    p = jnp.dot(h, win_ref[...], preferred_element_type=F32)
    qa = _rms(p[:, :MLA_Q_LORA], gqa_ref[...]).astype(BF16)
    ckv = _rms(p[:, MLA_Q_LORA:MLA_Q_LORA + MLA_KV_LORA], gkva_ref[...]).astype(BF16)
    kr = p[:, MLA_Q_LORA + MLA_KV_LORA:]

    qn = lax.dot_general(wqn_ref[...], qa, NT_DIMS, preferred_element_type=F32) * qscale
    qn_ref[...] = qn.astype(BF16)
    qr = lax.dot_general(wqr_ref[...], qa, NT_DIMS, preferred_element_type=F32) * qscale
    for t in range(qr.shape[0] // LANES):
        tile = qr[t * LANES:(t + 1) * LANES, :]
        if rope:
            swapped = jnp.concatenate([tile[HALF_TILE:], tile[:HALF_TILE]], axis=0)
            tile = tile * cost_ref[...] + swapped * sint_ref[...]
        qr_ref[t * LANES:(t + 1) * LANES, :] = tile.astype(BF16)

    if rope:
        kr = _rot(kr, cos, sin)
    first = _first_head_lanes(kr.shape)
    lane = lax.broadcasted_iota(jnp.int32, kr.shape, 1)
    kr_ref[:, :LANES] = jnp.where(lane == _shift_lane(0), 1.0,
                                  jnp.where(first, kr, 0.0)).astype(BF16)
    kr_ref[:, LANES:] = jnp.where(lane == _shift_lane(1), 1.0,
                                  jnp.where(first, 0.0, kr)).astype(BF16)

    kn_ref[...] = jnp.dot(ckv, wkn_ref[...], preferred_element_type=F32).astype(BF16)
    vt = lax.dot_general(wvt_ref[...], ckv, NT_DIMS, preferred_element_type=F32).astype(BF16)
    for hd in range(MLA_HEADS):
        vt_ref[0, hd, 0] = vt[hd * MLA_V:(hd + 1) * MLA_V, :]


def _mla_proj(x, mod, grp, g0, w, rope_tabs, tm, n_per_batch):
    t, d = x.shape
    n_tiles = t // tm
    nb = n_tiles // n_per_batch
    rope = rope_tabs is not None
    qscale = (MLA_NOPE + MLA_ROPE) ** -0.5 * LOG2E
    row = lambda i: (i, 0)
    in_specs = [pl.BlockSpec((tm, d), row), _const_spec((1, d)),
                _mod_spec(grp, 1, d), _mod_spec(grp, 0, d),
                _const_spec(w["w_in"].shape), _const_spec((1, MLA_Q_LORA)),
                _const_spec((1, MLA_KV_LORA)), _const_spec(w["w_qn"].shape),
                _const_spec(w["w_qr"].shape), _const_spec(w["w_kn"].shape),
                _const_spec(w["w_vt"].shape)]
    args = [x, g0, mod, mod, w["w_in"], w["g_qa"], w["g_kva"], w["w_qn"], w["w_qr"], w["w_kn"],
            w["w_vt"]]
    if rope:
        pos = lambda i: (i % n_per_batch, 0)
        pos_t = lambda i: (0, i % n_per_batch)
        in_specs += [pl.BlockSpec((tm, LANES), pos), pl.BlockSpec((tm, LANES), pos),
                     pl.BlockSpec((LANES, tm), pos_t), pl.BlockSpec((LANES, tm), pos_t)]
        args += list(rope_tabs)
    hn = MLA_HEADS * MLA_NOPE
    hr = MLA_HEADS * MLA_ROPE
    col = lambda i: (0, i)
    out_shape = [jax.ShapeDtypeStruct((hn, t), BF16), jax.ShapeDtypeStruct((hr, t), BF16),
                 jax.ShapeDtypeStruct((t, hn), BF16), jax.ShapeDtypeStruct((t, 2 * LANES), BF16),
                 jax.ShapeDtypeStruct((nb, MLA_HEADS, n_per_batch, MLA_V, tm), BF16)]
    out_specs = [pl.BlockSpec((hn, tm), col), pl.BlockSpec((hr, tm), col),
                 pl.BlockSpec((tm, hn), row), pl.BlockSpec((tm, 2 * LANES), row),
                 pl.BlockSpec((1, MLA_HEADS, 1, MLA_V, tm),
                              lambda i: (i // n_per_batch, 0, i % n_per_batch, 0, 0))]
    return pl.pallas_call(
        functools.partial(_mla_proj_kernel, rope=rope, qscale=qscale),
        grid=(n_tiles,), in_specs=in_specs, out_specs=out_specs, out_shape=out_shape,
        compiler_params=_params(("arbitrary",)),
        name="mla_proj_lat" if rope else "mla_proj_ctx",
    )(*args)


def _kv_group_sizes(n_chunks):
    if n_chunks <= 2:
        return [n_chunks] if n_chunks else []
    tail = [min(KV_GROUP, n_chunks) // 2] * 2
    body = n_chunks - sum(tail)
    assert body % KV_GROUP == 0
    return [KV_GROUP] * (body // KV_GROUP) + tail


def _mla_attn_kernel(*refs, n_lat, tk):
    if n_lat:
        (qn_ref, qr_ref, knc_ref, krc_ref, vtc_ref, knl_ref, krl_ref, vtl_ref, o_ref,
         acc_ref, s_ref, qa_ref, k2_ref) = refs
    else:
        qn_ref, qr_ref, knc_ref, krc_ref, vtc_ref, o_ref, acc_ref, s_ref, qa_ref, k2_ref = refs
    tq = qn_ref.shape[1]
    lc = knc_ref.shape[1]
    parity = pl.program_id(1) % 2

    def lat_keys(j):
        off = pl.multiple_of(j * tk, tk)
        return knl_ref[0, pl.ds(off, tk), :], krl_ref[0, pl.ds(off, tk), :]

    @pl.when(pl.program_id(2) == 0)
    def _():
        ones = jnp.ones((LANES, LANES), BF16)

        def sqnorm_max(kn, kr):
            r = jnp.dot(kn * kn + kr * kr, ones, preferred_element_type=F32)
            return jnp.max(r, axis=0, keepdims=True)
        mx = sqnorm_max(knc_ref[0], krc_ref[0])
        if n_lat:
            mx = lax.fori_loop(0, n_lat, lambda j, c: jnp.maximum(c, sqnorm_max(*lat_keys(j))), mx,
                               unroll=KV_UNROLL)
        k2_ref[...] = jnp.broadcast_to(mx * NORM_SLACK, k2_ref.shape)

    qn = qn_ref[...]
    qr = qr_ref[...]
    feat = lax.broadcasted_iota(jnp.int32, qr.shape, 0)
    own = ((feat >> 5) & 1) == parity
    qnf, qrf = qn.astype(F32), jnp.where(own, qr.astype(F32), 0.0)
    q2 = jnp.sum(qnf * qnf + qrf * qrf, axis=0, keepdims=True)
    ref = jnp.sqrt(q2 * k2_ref[0:1, 0:1])
    fast = jnp.max(ref) <= REF_LIMIT
    shift = jnp.where(fast, -ref, 0.0).astype(BF16)
    qa_ref[:LANES, :] = qn
    qa_ref[LANES:, :] = jnp.where(feat == LANES - 1 - 32 * parity, shift, qr)

    def scores(kn, kr):
        return jnp.dot(jnp.concatenate([kn, kr], axis=1), qa_ref[...],
                       preferred_element_type=F32)

    def finish(l):
        o_ref[...] = (acc_ref[...] * (1.0 / l)).T.astype(BF16)

    def shifted_softmax():
        def chunk(s, vt, l8):
            p = jnp.exp2(s)
            l8 = l8 + jnp.sum(p.reshape(-1, 8, tq), axis=0)
            return l8, jnp.dot(vt, p.astype(BF16), preferred_element_type=F32)

        sizes = _kv_group_sizes(n_lat)
        starts = [sum(sizes[:g]) for g in range(len(sizes))]
        last = max(len(sizes) - 1, 0)

        def n_rows(g):
            return (sizes[g] * tk if sizes else 0) + (lc if g == last else 0)

        def fill_scores(g):
            view = s_ref.at[g % 2]
            lat = sizes[g] * tk if sizes else 0
            if lat:
                rows = pl.ds(starts[g] * tk, lat)
                view[pl.ds(0, lat), :] = scores(knl_ref[0, rows, :], krl_ref[0, rows, :])
            if g == last:
                view[pl.ds(lat, lc), :] = scores(knc_ref[0], krc_ref[0])

        fill_scores(0)
        l8 = jnp.zeros((8, tq), F32)
        for g in range(last + 1):
            if g < last:
                fill_scores(g + 1)
            vts = [vtl_ref[0, 0, starts[g] + u] for u in range(sizes[g])] if sizes else []
            if g == last:
                vts.append(vtc_ref[0, 0, 0])
            l8, pv = chunk(s_ref[g % 2, pl.ds(0, n_rows(g)), :], jnp.concatenate(vts, axis=1), l8)
            if g == 0:
                acc_ref[...] = pv
            else:
                acc_ref[...] += pv
        finish(jnp.sum(l8, axis=0, keepdims=True))

    def online_softmax():
        def update(s_view, vt, m, l):
            m_new = jnp.maximum(m, jnp.max(s_view[...], axis=0, keepdims=True))
            alpha = jnp.exp2(m - m_new)
            p = jnp.exp2(s_view[...] - m_new)
            l_new = alpha * l + jnp.sum(p, axis=0, keepdims=True)
            acc_ref[...] = acc_ref[...] * alpha + jnp.dot(vt, p.astype(BF16),
                                                          preferred_element_type=F32)
            return m_new, l_new

        acc_ref[...] = jnp.zeros_like(acc_ref)
        m = jnp.full((1, tq), NEG_BIG, F32)
        l = jnp.zeros((1, tq), F32)
        slots = [s_ref.at[0, pl.ds(0, tk)], s_ref.at[1, pl.ds(0, tk)]] if n_lat else None
        if n_lat:
            slots[0][...] = scores(*lat_keys(0))
        ctx_view = s_ref.at[1, pl.ds(0, lc)]
        ctx_view[...] = scores(knc_ref[0], krc_ref[0])
        m, l = update(ctx_view, vtc_ref[0, 0, 0], m, l)
        if n_lat:
            def body(jj, carry):
                for u in range(KV_UNROLL):
                    j = KV_UNROLL * jj + u
                    slots[(u + 1) % 2][...] = scores(*lat_keys(jnp.minimum(j + 1, n_lat - 1)))
                    carry = update(slots[u % 2], vtl_ref[0, 0, j], *carry)
                return carry
            m, l = lax.fori_loop(0, n_lat // KV_UNROLL, body, (m, l))
        finish(l)

    lax.cond(fast, shifted_softmax, online_softmax)


def _mla_attn(qn, qr, ctx_kv, lat_kv, nb, tq):
    t = qn.shape[1]
    nq = t // nb // tq
    knc, krc, vtc = ctx_kv
    lc = knc.shape[1]
    qrow = lambda b, h, i: (b * nq + i, h)
    in_specs = [pl.BlockSpec((LANES, tq), lambda b, h, i: (h, b * nq + i)),
                pl.BlockSpec((LANES, tq), lambda b, h, i: (h // 2, b * nq + i)),
                pl.BlockSpec((1, lc, LANES), lambda b, h, i: (b, 0, h)),
                pl.BlockSpec((1, lc, LANES), lambda b, h, i: (b, 0, h % 2)),
                pl.BlockSpec((1, 1, 1, MLA_V, lc), lambda b, h, i: (b, h, 0, 0, 0))]
    args = [qn, qr, knc, krc, vtc]
    n_lat, tk = 0, 0
    if lat_kv is not None:
        knl, krl, vtl = lat_kv
        s = knl.shape[1]
        n_lat, tk = vtl.shape[2], vtl.shape[4]
        assert n_lat % KV_UNROLL == 0 and tk >= lc
        in_specs += [pl.BlockSpec((1, s, LANES), lambda b, h, i: (b, 0, h)),
                     pl.BlockSpec((1, s, LANES), lambda b, h, i: (b, 0, h % 2)),
                     pl.BlockSpec((1, 1, n_lat, MLA_V, tk), lambda b, h, i: (b, h, 0, 0, 0))]
        args += [knl, krl, vtl]
    scratch = [pltpu.VMEM((MLA_V, tq), F32), pltpu.VMEM((2, max(KV_GROUP * tk, lc), tq), F32),
               pltpu.VMEM((2 * LANES, tq), BF16), pltpu.VMEM((8, LANES), F32)]
    return pl.pallas_call(
        functools.partial(_mla_attn_kernel, n_lat=n_lat, tk=tk),
        grid=(nb, MLA_HEADS, nq), in_specs=in_specs,
        out_specs=pl.BlockSpec((tq, MLA_V), qrow),
        out_shape=jax.ShapeDtypeStruct((t, MLA_HEADS * MLA_V), BF16),
        scratch_shapes=scratch,
        compiler_params=_params(("arbitrary", "arbitrary", "arbitrary")),
        name="mla_attn_lat" if n_lat else "mla_attn_ctx",
    )(*args)


def _outproj_kernel(o_ref, w_ref, x_ref, g_ref, gt_ref, out_ref):
    y = jnp.dot(o_ref[...], w_ref[...], preferred_element_type=F32)
    out_ref[...] = x_ref[...] + gt_ref[0] * _rms(y, g_ref[...])


def _outproj(o, w_out, x, g1, mod, grp, tm):
    t, d = x.shape
    row = lambda i: (i, 0)
    return pl.pallas_call(
        _outproj_kernel,
        grid=(t // tm,),
        in_specs=[pl.BlockSpec((tm, o.shape[1]), row), _const_spec(w_out.shape),
                  pl.BlockSpec((tm, d), row), _const_spec((1, d)), _mod_spec(grp, 2, d)],
        out_specs=pl.BlockSpec((tm, d), row),
        out_shape=jax.ShapeDtypeStruct((t, d), F32),
        compiler_params=_params(("arbitrary",)),
        name="outproj",
    )(o, w_out, x, g1, mod)


def _mlp_kernel(x_ref, g2_ref, sc_ref, sh_ref, w1_ref, w2_ref, g3_ref, gt_ref, out_ref,
                f_ref, acc_ref):
    k = pl.program_id(1)

    def ff_chunk():
        u = jnp.maximum(jnp.dot(f_ref[...], w1_ref[...], preferred_element_type=F32), 0.0)
        return jnp.dot((u * u).astype(BF16), w2_ref[...], preferred_element_type=F32)

    @pl.when(k == 0)
    def _():
        _modulated_norm(x_ref, g2_ref, sc_ref, sh_ref, f_ref)
        acc_ref[...] = ff_chunk()

    @pl.when(k > 0)
    def _():
        acc_ref[...] += ff_chunk()

    @pl.when(k == pl.num_programs(1) - 1)
    def _():
        gain = gt_ref[0] * g3_ref[...]

        def rows(rs):
            y = acc_ref[rs, :]
            out_ref[rs, :] = x_ref[rs, :] + y * _inv_rms(y) * gain
        _row_chunks(acc_ref.shape[0], rows)


def _mlp(x, g2, g3, mod, grp, w1, w2, tm):
    t, d = x.shape
    dff = w1.shape[1]
    row = lambda i, k: (i, 0)
    return pl.pallas_call(
        _mlp_kernel,
        grid=(t // tm, dff // FF_TILE),
        in_specs=[pl.BlockSpec((tm, d), row), _const_spec((1, d)),
                  _mod_spec(grp, 4, d), _mod_spec(grp, 3, d),
                  pl.BlockSpec((d, FF_TILE), lambda i, k: (0, k)),
                  pl.BlockSpec((FF_TILE, d), lambda i, k: (k, 0)),
                  _const_spec((1, d)), _mod_spec(grp, 5, d)],
        out_specs=pl.BlockSpec((tm, d), row),
        out_shape=jax.ShapeDtypeStruct((t, d), F32),
        scratch_shapes=[pltpu.VMEM((tm, d), BF16), pltpu.VMEM((tm, d), F32)],
        compiler_params=_params(("arbitrary", "arbitrary")),
        name="mlp",
    )(x, g2, mod, mod, w1, w2, g3, mod)


def _swa_proj_kernel(*refs, rope, qscale):
    if rope:
        (x_ref, g_ref, sc_ref, sh_ref, wqt_ref, wk_ref, wvt_ref, cos_ref, sin_ref, cost_ref,
         sint_ref, q_ref, k2_ref, vt_ref) = refs
        cos, sin = cos_ref[...], sin_ref[...]
    else:
        x_ref, g_ref, sc_ref, sh_ref, wqt_ref, wk_ref, wvt_ref, q_ref, k2_ref, vt_ref = refs
    h = (_rms(x_ref[...], g_ref[...]) * (1.0 + sc_ref[0]) + sh_ref[0]).astype(BF16)
    qt = lax.dot_general(wqt_ref[...], h, NT_DIMS, preferred_element_type=F32) * qscale
    for t in range(qt.shape[0] // LANES):
        tile = qt[t * LANES:(t + 1) * LANES, :]
        if rope:
            swapped = jnp.concatenate([tile[HALF_TILE:], tile[:HALF_TILE]], axis=0)
            tile = tile * cost_ref[...] + swapped * sint_ref[...]
        q_ref[t * LANES:(t + 1) * LANES, :] = tile.astype(BF16)
    k = jnp.dot(h, wk_ref[...], preferred_element_type=F32)
    for c in range(SWA_KV_HEADS):
        tile = k[:, c * LANES:(c + 1) * LANES]
        if rope:
            tile = _rot(tile, cos, sin)
        first = _first_head_lanes(tile.shape)
        lane = lax.broadcasted_iota(jnp.int32, tile.shape, 1)
        k2_ref[:, 2 * c * LANES:(2 * c + 1) * LANES] = jnp.where(
            lane == _shift_lane(0), 1.0, jnp.where(first, tile, 0.0)).astype(BF16)
        k2_ref[:, (2 * c + 1) * LANES:(2 * c + 2) * LANES] = jnp.where(
            lane == _shift_lane(1), 1.0, jnp.where(first, 0.0, tile)).astype(BF16)
    vt_ref[...] = lax.dot_general(wvt_ref[...], h, NT_DIMS,
                                  preferred_element_type=F32).astype(BF16)


def _swa_proj(x, mod, grp, g0, w, rope_tabs, tm, n_per_batch):
    t, d = x.shape
    rope = rope_tabs is not None
    row = lambda i: (i, 0)
    dq = SWA_HEADS * SWA_HEAD_DIM
    dkv = SWA_KV_HEADS * SWA_HEAD_DIM
    w_qt, w_k, w_vt = w
    col = lambda i: (0, i)
    in_specs = [pl.BlockSpec((tm, d), row), _const_spec((1, d)),
                _mod_spec(grp, 1, d), _mod_spec(grp, 0, d), _const_spec(w_qt.shape),
                _const_spec(w_k.shape), _const_spec(w_vt.shape)]
    args = [x, g0, mod, mod, w_qt, w_k, w_vt]
    if rope:
        pos = lambda i: (i % n_per_batch, 0)
        pos_t = lambda i: (0, i % n_per_batch)
        in_specs += [pl.BlockSpec((tm, LANES), pos), pl.BlockSpec((tm, LANES), pos),
                     pl.BlockSpec((LANES, tm), pos_t), pl.BlockSpec((LANES, tm), pos_t)]
        args += list(rope_tabs)
    out_shape = [jax.ShapeDtypeStruct((dq, t), BF16),
                 jax.ShapeDtypeStruct((t, 2 * SWA_KV_HEADS * LANES), BF16),
                 jax.ShapeDtypeStruct((dkv, t), BF16)]
    out_specs = [pl.BlockSpec((dq, tm), col), pl.BlockSpec((tm, 2 * SWA_KV_HEADS * LANES), row),
                 pl.BlockSpec((dkv, tm), col)]
    return pl.pallas_call(
        functools.partial(_swa_proj_kernel, rope=rope, qscale=SWA_HEAD_DIM ** -0.5 * LOG2E),
        grid=(t // tm,), in_specs=in_specs, out_specs=out_specs, out_shape=out_shape,
        compiler_params=_params(("arbitrary",)),
        name="swa_proj_lat" if rope else "swa_proj_ctx",
    )(*args)


def _swa_attn_kernel(sink_ref, q_ref, kc_ref, kp_ref, kcur_ref, kn_ref, vc_ref, vp_ref, vcur_ref,
                     vn_ref, o_ref, s_ref, qa_ref, *, qb):
    kvh = pl.program_id(1)
    i = pl.program_id(2)
    lc = kc_ref.shape[0]
    span = SWA_QBLK + 2 * SWA_WINDOW
    n_pairs = SWA_GROUP // 2
    kc, vc = kc_ref[...], vc_ref[...]
    kwin = jnp.concatenate([kp_ref[...], kcur_ref[...], kn_ref[...]], axis=0)
    vwin = jnp.concatenate([vp_ref[...], vcur_ref[...], vn_ref[...]], axis=1)
    r = lax.broadcasted_iota(jnp.int32, (lc + span, SWA_QBLK), 0)
    rel = r - lc - lax.broadcasted_iota(jnp.int32, (lc + span, SWA_QBLK), 1)
    bias = jnp.where((r < lc) | ((rel >= 0) & (rel <= 2 * SWA_WINDOW)), 0.0, NEG_BIG)
    bias = jnp.concatenate([bias] * n_pairs, axis=1)
    pair = lax.broadcasted_iota(jnp.int32, (1, n_pairs * SWA_QBLK), 1) // SWA_QBLK
    keep_prev = jnp.where(i == 0, 0.0, 1.0)
    keep_next = jnp.where(i == pl.num_programs(2) - 1, 0.0, 1.0)

    ones = jnp.ones((LANES, LANES), BF16)
    kall = jnp.concatenate([kc[:, :LANES], kwin[:, :LANES]], axis=0)
    k2 = jnp.max(jnp.dot(kall * kall, ones, preferred_element_type=F32), axis=0, keepdims=True)
    k2 = k2[:, 0:1] * NORM_SLACK
    feat = lax.broadcasted_iota(jnp.int32, (LANES, n_pairs * SWA_QBLK), 0)
    first_rows = (feat & 32) == 0
    q_t, refs = [], []
    for blk in range(qb):
        qt = jnp.concatenate([q_ref[t * LANES:(t + 1) * LANES, blk * SWA_QBLK:(blk + 1) * SWA_QBLK]
                              for t in range(n_pairs)], axis=1)
        sq = qt.astype(F32)
        sq = sq * sq
        q2_first = jnp.sum(jnp.where(first_rows, sq, 0.0), axis=0, keepdims=True)
        q2_second = jnp.sum(sq, axis=0, keepdims=True) - q2_first
        q_t.append(qt)
        refs.append([jnp.sqrt(q2 * k2).astype(BF16) for q2 in (q2_first, q2_second)])
    ref_max = functools.reduce(jnp.maximum, [jnp.max(r.astype(F32)) for pr in refs for r in pr])
    fast = ref_max <= REF_LIMIT
    def set_queries(shifted):
        for blk in range(qb):
            for e in range(2):
                shift = -refs[blk][e] if shifted else jnp.zeros_like(refs[blk][e])
                qa_ref[blk, e] = jnp.where(feat == _shift_lane(e), shift, q_t[blk])

    def scores(blk, e):
        kcat = jnp.concatenate([kc[:, e * LANES:(e + 1) * LANES],
                                kwin[blk * SWA_QBLK:blk * SWA_QBLK + span,
                                     e * LANES:(e + 1) * LANES]], axis=0)
        return jnp.dot(kcat, qa_ref[blk, e], preferred_element_type=F32)

    def run(shifted):
        s_ref[0] = scores(0, 0)
        for blk in range(qb):
            rows = slice(blk * SWA_QBLK, (blk + 1) * SWA_QBLK)
            vt = jnp.concatenate([vc, vwin[:, blk * SWA_QBLK:blk * SWA_QBLK + span]], axis=1)
            halves = []
            for e in range(2):
                if e == 0:
                    s_ref[1] = scores(blk, 1)
                elif blk + 1 < qb:
                    s_ref[0] = scores(blk + 1, 0)
                s = s_ref[e] + bias
                sk = jnp.zeros(pair.shape, F32)
                for t in range(n_pairs):
                    sk = jnp.where(pair == t, sink_ref[kvh * SWA_GROUP + 2 * t + e] * LOG2E, sk)
                if shifted:
                    m = refs[blk][e].astype(F32)
                    p = jnp.exp2(s)
                else:
                    m = jnp.maximum(jnp.max(s, axis=0, keepdims=True), sk)
                    p = jnp.exp2(s - m)
                if blk == 0:
                    p = jnp.concatenate([p[:lc], p[lc:lc + SWA_WINDOW] * keep_prev,
                                         p[lc + SWA_WINDOW:]], axis=0)
                if blk == qb - 1:
                    p = jnp.concatenate([p[:lc + span - SWA_WINDOW],
                                         p[lc + span - SWA_WINDOW:] * keep_next], axis=0)
                den = jnp.sum(p, axis=0, keepdims=True) + jnp.exp2(sk - m)
                o = jnp.dot(vt, p.astype(BF16), preferred_element_type=F32)
                halves.append(o * (1.0 / den))
            both = jnp.concatenate(halves, axis=0)
            for t in range(n_pairs):
                tile = both[:, t * SWA_QBLK:(t + 1) * SWA_QBLK]
                o_ref[rows, t * LANES:(t + 1) * LANES] = tile.T.astype(BF16)

    set_queries(True)
    run(True)

    @pl.when(jnp.logical_not(fast))
    def _():
        set_queries(False)
        run(False)


def _swa_attn(sink, q, k2, vt, k2c, vtc, nb, seq, lc):
    t = q.shape[1]
    nblk = seq // SWA_QBLK
    qb = min(SWA_STEP_BLOCKS, nblk)
    nsteps = nblk // qb
    gq = SWA_GROUP * SWA_HEAD_DIM
    hd = SWA_HEAD_DIM
    prev_blk = lambda b, i: b * nblk + jnp.maximum(qb * i - 1, 0)
    next_blk = lambda b, i: b * nblk + jnp.minimum(qb * i + qb, nblk - 1)
    return pl.pallas_call(
        functools.partial(_swa_attn_kernel, qb=qb),
        grid=(nb, SWA_KV_HEADS, nsteps),
        in_specs=[pl.BlockSpec(memory_space=pltpu.SMEM),
                  pl.BlockSpec((gq, qb * SWA_QBLK), lambda b, h, i: (h, b * nsteps + i)),
                  pl.BlockSpec((lc, 2 * LANES), lambda b, h, i: (b, h)),
                  pl.BlockSpec((SWA_QBLK, 2 * LANES), lambda b, h, i: (prev_blk(b, i), h)),
                  pl.BlockSpec((qb * SWA_QBLK, 2 * LANES), lambda b, h, i: (b * nsteps + i, h)),
                  pl.BlockSpec((SWA_QBLK, 2 * LANES), lambda b, h, i: (next_blk(b, i), h)),
                  pl.BlockSpec((hd, lc), lambda b, h, i: (h, b)),
                  pl.BlockSpec((hd, SWA_QBLK), lambda b, h, i: (h, prev_blk(b, i))),
                  pl.BlockSpec((hd, qb * SWA_QBLK), lambda b, h, i: (h, b * nsteps + i)),
                  pl.BlockSpec((hd, SWA_QBLK), lambda b, h, i: (h, next_blk(b, i)))],
        out_specs=pl.BlockSpec((qb * SWA_QBLK, gq), lambda b, h, i: (b * nsteps + i, h)),
        out_shape=jax.ShapeDtypeStruct((t, SWA_HEADS * SWA_HEAD_DIM), BF16),
        scratch_shapes=[pltpu.VMEM((2, lc + SWA_QBLK + 2 * SWA_WINDOW, gq), F32),
                        pltpu.VMEM((qb, 2, LANES, gq), BF16)],
        compiler_params=_params(("arbitrary", "arbitrary", "arbitrary")),
        name="swa_attn",
    )(sink, q, k2c, k2, k2, k2, vtc, vt, vt, vt)


def _rope_tables(seq):
    rows = seq // GRID_W
    row = jnp.repeat(jnp.arange(rows, dtype=F32), GRID_W)
    col = jnp.tile(jnp.arange(GRID_W, dtype=F32), rows)
    n_freq = MLA_ROPE // 4
    freqs = ROPE_BASE ** (-jnp.arange(n_freq, dtype=F32) / n_freq)
    ang = jnp.concatenate([row[:, None] * freqs, col[:, None] * freqs], axis=-1)
    cos, sin = jnp.cos(ang), jnp.sin(ang)
    cos_t = jnp.concatenate([cos, cos, cos, cos], axis=-1)
    sin_t = jnp.concatenate([-sin, -sin, sin, sin], axis=-1)
    return cos_t, sin_t, cos_t.T, sin_t.T


def _pair_tiles(w, n_heads, half):
    k = w.shape[0]
    x1 = w[:, :, :half].reshape(k, n_heads // 2, 2 * half)
    x2 = w[:, :, half:].reshape(k, n_heads // 2, 2 * half)
    return jnp.concatenate([x1, x2], axis=2).reshape(k, n_heads * 2 * half)


def _mla_weights(w_in, g_qa, g_kva, w_qb, w_kvb):
    half = MLA_ROPE // 2
    lat = MLA_Q_LORA + MLA_KV_LORA
    k1, k2 = w_in[:, lat:lat + half], w_in[:, lat + half:]
    qb = w_qb.reshape(MLA_Q_LORA, MLA_HEADS, MLA_NOPE + MLA_ROPE)
    kvb = w_kvb.reshape(MLA_KV_LORA, MLA_HEADS, MLA_NOPE + MLA_V)
    return {
        "w_in": jnp.concatenate([w_in[:, :lat], k1, k1, k2, k2], axis=1).astype(BF16),
        "g_qa": g_qa.reshape(1, -1), "g_kva": g_kva.reshape(1, -1),
        "w_qn": qb[:, :, :MLA_NOPE].reshape(MLA_Q_LORA, -1).T.astype(BF16),
        "w_qr": _pair_tiles(qb[:, :, MLA_NOPE:], MLA_HEADS, half).T.astype(BF16),
        "w_kn": kvb[:, :, :MLA_NOPE].reshape(MLA_KV_LORA, -1).astype(BF16),
        "w_vt": kvb[:, :, MLA_NOPE:].reshape(MLA_KV_LORA, -1).T.astype(BF16),
    }


def _swa_weights(w_qkv):
    d = w_qkv.shape[0]
    half = SWA_HEAD_DIM // 2
    dq = SWA_HEADS * SWA_HEAD_DIM
    dkv = SWA_KV_HEADS * SWA_HEAD_DIM
    q = _pair_tiles(w_qkv[:, :dq].reshape(d, SWA_HEADS, SWA_HEAD_DIM), SWA_HEADS, half)
    k = w_qkv[:, dq:dq + dkv].reshape(d, SWA_KV_HEADS, SWA_HEAD_DIM)
    k1, k2 = k[:, :, :half], k[:, :, half:]
    k = jnp.concatenate([k1, k1, k2, k2], axis=2).reshape(d, SWA_KV_HEADS * LANES)
    return q.T.astype(BF16), k.astype(BF16), w_qkv[:, dq + dkv:].T.astype(BF16)


def kernel(x, c, ctx, c_ctx, w_mod, b_mod, g_norm, w_ff_in, w_ff_out, mla_w_in, mla_g_qa,
           mla_g_kva, mla_w_qb, mla_w_kvb, mla_w_out, swa_w_qkv, swa_sink, swa_w_out):
    nb, seq, d = x.shape
    lc = ctx.shape[1]
    depth = w_mod.shape[0]
    assert nb + 1 <= MOD_ROWS
    tm = min(TOKEN_TILE, seq)
    tq = min(ATTN_TQ, seq)
    n_per_batch = seq // tm
    grp_lat = lambda i: i // n_per_batch
    grp_ctx = lambda i: nb

    cmat = jnp.zeros((MOD_ROWS, d), F32).at[:nb].set(c).at[nb].set(c_ctx)
    mod_all = _modulation(cmat, w_mod, b_mod)
    rope_tabs = _rope_tables(seq)

    xl = x.reshape(nb * seq, d)
    xc = ctx.reshape(nb * lc, d)
    for i in range(depth):
        need_ctx = i < depth - 1
        mod = mod_all[i].reshape(MOD_ROWS, 1, 6 * d)
        g = g_norm[i].reshape(4, 1, d)
        j = i // 2
        if i % 2 == 0:
            w = _mla_weights(mla_w_in[j], mla_g_qa[j], mla_g_kva[j], mla_w_qb[j], mla_w_kvb[j])
            w_out = mla_w_out[j].astype(BF16)
            qn, qr, kn, kr, vt = _mla_proj(xl, mod, grp_lat, g[0], w, rope_tabs, tm, n_per_batch)
            qnc, qrc, knc, krc, vtc = _mla_proj(xc, mod, grp_ctx, g[0], w, None, lc, 1)
            ctx_kv = (knc.reshape(nb, lc, -1), krc.reshape(nb, lc, -1), vtc)
            lat_kv = (kn.reshape(nb, seq, -1), kr.reshape(nb, seq, -1), vt)
            o_l = _mla_attn(qn, qr, ctx_kv, lat_kv, nb, tq)
            o_c = _mla_attn(qnc, qrc, ctx_kv, None, nb, lc) if need_ctx else None
        else:
            w = _swa_weights(swa_w_qkv[j])
            w_out = swa_w_out[j].astype(BF16)
            q, k2, vt = _swa_proj(xl, mod, grp_lat, g[0], w, rope_tabs, tm, n_per_batch)
            qc, k2c, vtc = _swa_proj(xc, mod, grp_ctx, g[0], w, None, lc, 1)
            o_l = _swa_attn(swa_sink[j], q, k2, vt, k2c, vtc, nb, seq, lc)
            assert not need_ctx
            o_c = None
        w1 = _layer_to_bf16(w_ff_in, i)
        w2 = _layer_to_bf16(w_ff_out, i)
        xl = _outproj(o_l, w_out, xl, g[1], mod, grp_lat, tm)
        xl = _mlp(xl, g[2], g[3], mod, grp_lat, w1, w2, tm)
        if need_ctx:
            xc = _outproj(o_c, w_out, xc, g[1], mod, grp_ctx, lc)
            xc = _mlp(xc, g[2], g[3], mod, grp_ctx, w1, w2, lc)
    return xl.reshape(nb, seq, d)
```

```python
import functools
import math

import jax
import jax.numpy as jnp
from jax import lax
from jax.experimental import pallas as pl
from jax.experimental.pallas import tpu as pltpu

F32 = jnp.float32
BF16 = jnp.bfloat16

GRID_W = 64
ROPE_BASE = 10000.0
NORM_EPS = 1e-6
LOG2E = math.log2(math.e)
NEG_BIG = -1e30
REF_LIMIT = 60.0
NORM_SLACK = 1.03

MLA_HEADS = 16
MLA_Q_LORA = 512
MLA_KV_LORA = 512
MLA_NOPE = 128
MLA_ROPE = 64
MLA_V = 128

SWA_HEADS = 32
SWA_KV_HEADS = 4
SWA_HEAD_DIM = 64
SWA_WINDOW = 128
SWA_GROUP = SWA_HEADS // SWA_KV_HEADS
SWA_QBLK = 128
SWA_STEP_BLOCKS = 16

LANES = 128
HALF_TILE = 64

MOD_ROWS = 8
MOD_TN = 1024
TOKEN_TILE = 512
ROW_CHUNK = 32
ROW_UNROLL = 4
FF_TILE = 1024
ATTN_TQ = 512
KV_GROUP = 8
KV_UNROLL = 4
CAST_BLOCK_BYTES = 8 * 1024 * 1024
VMEM_LIMIT = 56 * 1024 * 1024

NT_DIMS = (((1,), (1,)), ((), ()))


def _rms(xf, g):
    ms = jnp.mean(xf * xf, axis=-1, keepdims=True)
    return xf * lax.rsqrt(ms + NORM_EPS) * g


def _inv_rms(xf):
    return lax.rsqrt(jnp.mean(xf * xf, axis=-1, keepdims=True) + NORM_EPS)


def _row_chunks(n_rows, fn):
    def body(i, carry):
        fn(pl.ds(pl.multiple_of(i * ROW_CHUNK, ROW_CHUNK), ROW_CHUNK))
        return carry
    lax.fori_loop(0, n_rows // ROW_CHUNK, body, 0, unroll=ROW_UNROLL)


def _modulated_norm(x_ref, g_ref, sc_ref, sh_ref, h_ref):
    gain = g_ref[...] * (1.0 + sc_ref[0])
    shift = sh_ref[0]

    def rows(rs):
        xf = x_ref[rs, :]
        h_ref[rs, :] = (xf * _inv_rms(xf) * gain + shift).astype(h_ref.dtype)
    _row_chunks(h_ref.shape[0], rows)


def _rot(tile, cos, sin):
    return tile * cos + pltpu.roll(tile, HALF_TILE, 1) * sin


def _first_head_lanes(shape):
    lane = lax.broadcasted_iota(jnp.int32, shape, 1)
    return (lane & 32) == 0


def _shift_lane(parity):
    return LANES - 1 - 32 * parity


def _params(sem):
    return pltpu.CompilerParams(dimension_semantics=sem, vmem_limit_bytes=VMEM_LIMIT)


def _const_spec(shape):
    nd = len(shape)
    return pl.BlockSpec(shape, lambda *_: (0,) * nd, pipeline_mode=pl.Buffered(1))


def _mod_spec(grp, which, d):
    return pl.BlockSpec((1, 1, d), lambda i, *_: (grp(i), 0, which))


def _mod_kernel(c_ref, w_ref, b_ref, o_ref):
    c = c_ref[...]
    a = c / (1.0 + jnp.exp(-c))
    o_ref[0] = jnp.dot(a, w_ref[0], preferred_element_type=F32,
                       precision=lax.Precision.HIGHEST) + b_ref[0]


def _modulation(cmat, w_mod, b_mod):
    depth, d, n = w_mod.shape
    return pl.pallas_call(
        _mod_kernel,
        grid=(depth, n // MOD_TN),
        in_specs=[pl.BlockSpec((MOD_ROWS, d), lambda l, j: (0, 0)),
                  pl.BlockSpec((1, d, MOD_TN), lambda l, j: (l, 0, j)),
                  pl.BlockSpec((1, 1, MOD_TN), lambda l, j: (l, 0, j))],
        out_specs=pl.BlockSpec((1, MOD_ROWS, MOD_TN), lambda l, j: (l, 0, j)),
        out_shape=jax.ShapeDtypeStruct((depth, MOD_ROWS, n), F32),
        compiler_params=_params(("arbitrary", "arbitrary")),
        name="modulation",
    )(cmat, w_mod, b_mod.reshape(depth, 1, n))


def _cast_kernel(w_ref, o_ref):
    o_ref[...] = w_ref[0].astype(o_ref.dtype)


def _layer_to_bf16(w_stack, layer):
    _, r, c = w_stack.shape
    br = max(8, min(r, CAST_BLOCK_BYTES // (4 * c)))
    assert r % br == 0
    return pl.pallas_call(
        _cast_kernel,
        grid=(r // br,),
        in_specs=[pl.BlockSpec((1, br, c), lambda j: (layer, j, 0))],
        out_specs=pl.BlockSpec((br, c), lambda j: (j, 0)),
        out_shape=jax.ShapeDtypeStruct((r, c), BF16),
        compiler_params=_params(("arbitrary",)),
        name="cast_bf16",
    )(w_stack)


def _mla_proj_kernel(*refs, rope, qscale):
    if rope:
        (x_ref, g_ref, sc_ref, sh_ref, win_ref, gqa_ref, gkva_ref, wqn_ref, wqr_ref, wkn_ref,
         wvt_ref, cos_ref, sin_ref, cost_ref, sint_ref, qn_ref, qr_ref, kn_ref, kr_ref,
         vt_ref) = refs
        cos, sin = cos_ref[...], sin_ref[...]
    else:
        (x_ref, g_ref, sc_ref, sh_ref, win_ref, gqa_ref, gkva_ref, wqn_ref, wqr_ref, wkn_ref,
         wvt_ref, qn_ref, qr_ref, kn_ref, kr_ref, vt_ref) = refs
    h = (_rms(x_ref[...], g_ref[...]) * (1.0 + sc_ref[0]) + sh_ref[0]).astype(BF16)
    p = jnp.dot(h, win_ref[...], preferred_element_type=F32)
    qa = _rms(p[:, :MLA_Q_LORA], gqa_ref[...]).astype(BF16)
    ckv = _rms(p[:, MLA_Q_LORA:MLA_Q_LORA + MLA_KV_LORA], gkva_ref[...]).astype(BF16)
    kr = p[:, MLA_Q_LORA + MLA_KV_LORA:]

    qn = lax.dot_general(wqn_ref[...], qa, NT_DIMS, preferred_element_type=F32) * qscale
    qn_ref[...] = qn.astype(BF16)
    qr = lax.dot_general(wqr_ref[...], qa, NT_DIMS, preferred_element_type=F32) * qscale
    for t in range(qr.shape[0] // LANES):
        tile = qr[t * LANES:(t + 1) * LANES, :]
        if rope:
            swapped = jnp.concatenate([tile[HALF_TILE:], tile[:HALF_TILE]], axis=0)
            tile = tile * cost_ref[...] + swapped * sint_ref[...]
        qr_ref[t * LANES:(t + 1) * LANES, :] = tile.astype(BF16)

    if rope:
        kr = _rot(kr, cos, sin)
    first = _first_head_lanes(kr.shape)
    lane = lax.broadcasted_iota(jnp.int32, kr.shape, 1)
    kr_ref[:, :LANES] = jnp.where(lane == _shift_lane(0), 1.0,
                                  jnp.where(first, kr, 0.0)).astype(BF16)
    kr_ref[:, LANES:] = jnp.where(lane == _shift_lane(1), 1.0,
                                  jnp.where(first, 0.0, kr)).astype(BF16)

    kn_ref[...] = jnp.dot(ckv, wkn_ref[...], preferred_element_type=F32).astype(BF16)
    vt = lax.dot_general(wvt_ref[...], ckv, NT_DIMS, preferred_element_type=F32).astype(BF16)
    for hd in range(MLA_HEADS):
        vt_ref[0, hd, 0] = vt[hd * MLA_V:(hd + 1) * MLA_V, :]


def _mla_proj(x, mod, grp, g0, w, rope_tabs, tm, n_per_batch):
    t, d = x.shape
    n_tiles = t // tm
    nb = n_tiles // n_per_batch
    rope = rope_tabs is not None
    qscale = (MLA_NOPE + MLA_ROPE) ** -0.5 * LOG2E
    row = lambda i: (i, 0)
    in_specs = [pl.BlockSpec((tm, d), row), _const_spec((1, d)),
                _mod_spec(grp, 1, d), _mod_spec(grp, 0, d),
                _const_spec(w["w_in"].shape), _const_spec((1, MLA_Q_LORA)),
                _const_spec((1, MLA_KV_LORA)), _const_spec(w["w_qn"].shape),
                _const_spec(w["w_qr"].shape), _const_spec(w["w_kn"].shape),
                _const_spec(w["w_vt"].shape)]
    args = [x, g0, mod, mod, w["w_in"], w["g_qa"], w["g_kva"], w["w_qn"], w["w_qr"], w["w_kn"],
            w["w_vt"]]
    if rope:
        pos = lambda i: (i % n_per_batch, 0)
        pos_t = lambda i: (0, i % n_per_batch)
        in_specs += [pl.BlockSpec((tm, LANES), pos), pl.BlockSpec((tm, LANES), pos),
                     pl.BlockSpec((LANES, tm), pos_t), pl.BlockSpec((LANES, tm), pos_t)]
        args += list(rope_tabs)
    hn = MLA_HEADS * MLA_NOPE
    hr = MLA_HEADS * MLA_ROPE
    col = lambda i: (0, i)
    out_shape = [jax.ShapeDtypeStruct((hn, t), BF16), jax.ShapeDtypeStruct((hr, t), BF16),
                 jax.ShapeDtypeStruct((t, hn), BF16), jax.ShapeDtypeStruct((t, 2 * LANES), BF16),
                 jax.ShapeDtypeStruct((nb, MLA_HEADS, n_per_batch, MLA_V, tm), BF16)]
    out_specs = [pl.BlockSpec((hn, tm), col), pl.BlockSpec((hr, tm), col),
                 pl.BlockSpec((tm, hn), row), pl.BlockSpec((tm, 2 * LANES), row),
                 pl.BlockSpec((1, MLA_HEADS, 1, MLA_V, tm),
                              lambda i: (i // n_per_batch, 0, i % n_per_batch, 0, 0))]
    return pl.pallas_call(
        functools.partial(_mla_proj_kernel, rope=rope, qscale=qscale),
        grid=(n_tiles,), in_specs=in_specs, out_specs=out_specs, out_shape=out_shape,
        compiler_params=_params(("arbitrary",)),
        name="mla_proj_lat" if rope else "mla_proj_ctx",
    )(*args)


def _kv_group_sizes(n_chunks):
    if n_chunks <= 2:
        return [n_chunks] if n_chunks else []
    tail = [min(KV_GROUP, n_chunks) // 2] * 2
    body = n_chunks - sum(tail)
    assert body % KV_GROUP == 0
    return [KV_GROUP] * (body // KV_GROUP) + tail


def _mla_attn_kernel(*refs, n_lat, tk):
    if n_lat:
        (qn_ref, qr_ref, knc_ref, krc_ref, vtc_ref, knl_ref, krl_ref, vtl_ref, o_ref,
         acc_ref, s_ref, qa_ref, k2_ref) = refs
    else:
        qn_ref, qr_ref, knc_ref, krc_ref, vtc_ref, o_ref, acc_ref, s_ref, qa_ref, k2_ref = refs
    tq = qn_ref.shape[1]
    lc = knc_ref.shape[1]
    parity = pl.program_id(1) % 2

    def lat_keys(j):
        off = pl.multiple_of(j * tk, tk)
        return knl_ref[0, pl.ds(off, tk), :], krl_ref[0, pl.ds(off, tk), :]

    @pl.when(pl.program_id(2) == 0)
    def _():
        ones = jnp.ones((LANES, LANES), BF16)

        def sqnorm_max(kn, kr):
            r = jnp.dot(kn * kn + kr * kr, ones, preferred_element_type=F32)
            return jnp.max(r, axis=0, keepdims=True)
        mx = sqnorm_max(knc_ref[0], krc_ref[0])
        if n_lat:
            mx = lax.fori_loop(0, n_lat, lambda j, c: jnp.maximum(c, sqnorm_max(*lat_keys(j))), mx,
                               unroll=KV_UNROLL)
        k2_ref[...] = jnp.broadcast_to(mx * NORM_SLACK, k2_ref.shape)

    qn = qn_ref[...]
    qr = qr_ref[...]
    feat = lax.broadcasted_iota(jnp.int32, qr.shape, 0)
    own = ((feat >> 5) & 1) == parity
    qnf, qrf = qn.astype(F32), jnp.where(own, qr.astype(F32), 0.0)
    q2 = jnp.sum(qnf * qnf + qrf * qrf, axis=0, keepdims=True)
    ref = jnp.sqrt(q2 * k2_ref[0:1, 0:1])
    fast = jnp.max(ref) <= REF_LIMIT
    shift = jnp.where(fast, -ref, 0.0).astype(BF16)
    qa_ref[:LANES, :] = qn
    qa_ref[LANES:, :] = jnp.where(feat == LANES - 1 - 32 * parity, shift, qr)

    def scores(kn, kr):
        return jnp.dot(jnp.concatenate([kn, kr], axis=1), qa_ref[...],
                       preferred_element_type=F32)

    def finish(l):
        o_ref[...] = (acc_ref[...] * (1.0 / l)).T.astype(BF16)

    def shifted_softmax():
        def chunk(s, vt, l8):
            p = jnp.exp2(s)
            l8 = l8 + jnp.sum(p.reshape(-1, 8, tq), axis=0)
            return l8, jnp.dot(vt, p.astype(BF16), preferred_element_type=F32)

        sizes = _kv_group_sizes(n_lat)
        starts = [sum(sizes[:g]) for g in range(len(sizes))]
        last = max(len(sizes) - 1, 0)

        def n_rows(g):
            return (sizes[g] * tk if sizes else 0) + (lc if g == last else 0)

        def fill_scores(g):
            view = s_ref.at[g % 2]
            lat = sizes[g] * tk if sizes else 0
            if lat:
                rows = pl.ds(starts[g] * tk, lat)
                view[pl.ds(0, lat), :] = scores(knl_ref[0, rows, :], krl_ref[0, rows, :])
            if g == last:
                view[pl.ds(lat, lc), :] = scores(knc_ref[0], krc_ref[0])

        fill_scores(0)
        l8 = jnp.zeros((8, tq), F32)
        for g in range(last + 1):
            if g < last:
                fill_scores(g + 1)
            vts = [vtl_ref[0, 0, starts[g] + u] for u in range(sizes[g])] if sizes else []
            if g == last:
                vts.append(vtc_ref[0, 0, 0])
            l8, pv = chunk(s_ref[g % 2, pl.ds(0, n_rows(g)), :], jnp.concatenate(vts, axis=1), l8)
            if g == 0:
                acc_ref[...] = pv
            else:
                acc_ref[...] += pv
        finish(jnp.sum(l8, axis=0, keepdims=True))

    def online_softmax():
        def update(s_view, vt, m, l):
            m_new = jnp.maximum(m, jnp.max(s_view[...], axis=0, keepdims=True))
            alpha = jnp.exp2(m - m_new)
            p = jnp.exp2(s_view[...] - m_new)
            l_new = alpha * l + jnp.sum(p, axis=0, keepdims=True)
            acc_ref[...] = acc_ref[...] * alpha + jnp.dot(vt, p.astype(BF16),
                                                          preferred_element_type=F32)
            return m_new, l_new

        acc_ref[...] = jnp.zeros_like(acc_ref)
        m = jnp.full((1, tq), NEG_BIG, F32)
        l = jnp.zeros((1, tq), F32)
        slots = [s_ref.at[0, pl.ds(0, tk)], s_ref.at[1, pl.ds(0, tk)]] if n_lat else None
        if n_lat:
            slots[0][...] = scores(*lat_keys(0))
        ctx_view = s_ref.at[1, pl.ds(0, lc)]
        ctx_view[...] = scores(knc_ref[0], krc_ref[0])
        m, l = update(ctx_view, vtc_ref[0, 0, 0], m, l)
        if n_lat:
            def body(jj, carry):
                for u in range(KV_UNROLL):
                    j = KV_UNROLL * jj + u
                    slots[(u + 1) % 2][...] = scores(*lat_keys(jnp.minimum(j + 1, n_lat - 1)))
                    carry = update(slots[u % 2], vtl_ref[0, 0, j], *carry)
                return carry
            m, l = lax.fori_loop(0, n_lat // KV_UNROLL, body, (m, l))
        finish(l)

    lax.cond(fast, shifted_softmax, online_softmax)


def _mla_attn(qn, qr, ctx_kv, lat_kv, nb, tq):
    t = qn.shape[1]
    nq = t // nb // tq
    knc, krc, vtc = ctx_kv
    lc = knc.shape[1]
    qrow = lambda b, h, i: (b * nq + i, h)
    in_specs = [pl.BlockSpec((LANES, tq), lambda b, h, i: (h, b * nq + i)),
                pl.BlockSpec((LANES, tq), lambda b, h, i: (h // 2, b * nq + i)),
                pl.BlockSpec((1, lc, LANES), lambda b, h, i: (b, 0, h)),
                pl.BlockSpec((1, lc, LANES), lambda b, h, i: (b, 0, h % 2)),
                pl.BlockSpec((1, 1, 1, MLA_V, lc), lambda b, h, i: (b, h, 0, 0, 0))]
    args = [qn, qr, knc, krc, vtc]
    n_lat, tk = 0, 0
    if lat_kv is not None:
        knl, krl, vtl = lat_kv
        s = knl.shape[1]
        n_lat, tk = vtl.shape[2], vtl.shape[4]
        assert n_lat % KV_UNROLL == 0 and tk >= lc
        in_specs += [pl.BlockSpec((1, s, LANES), lambda b, h, i: (b, 0, h)),
                     pl.BlockSpec((1, s, LANES), lambda b, h, i: (b, 0, h % 2)),
                     pl.BlockSpec((1, 1, n_lat, MLA_V, tk), lambda b, h, i: (b, h, 0, 0, 0))]
        args += [knl, krl, vtl]
    scratch = [pltpu.VMEM((MLA_V, tq), F32), pltpu.VMEM((2, max(KV_GROUP * tk, lc), tq), F32),
               pltpu.VMEM((2 * LANES, tq), BF16), pltpu.VMEM((8, LANES), F32)]
    return pl.pallas_call(
        functools.partial(_mla_attn_kernel, n_lat=n_lat, tk=tk),
        grid=(nb, MLA_HEADS, nq), in_specs=in_specs,
        out_specs=pl.BlockSpec((tq, MLA_V), qrow),
        out_shape=jax.ShapeDtypeStruct((t, MLA_HEADS * MLA_V), BF16),
        scratch_shapes=scratch,
        compiler_params=_params(("arbitrary", "arbitrary", "arbitrary")),
        name="mla_attn_lat" if n_lat else "mla_attn_ctx",
    )(*args)


def _outproj_kernel(o_ref, w_ref, x_ref, g_ref, gt_ref, out_ref):
    y = jnp.dot(o_ref[...], w_ref[...], preferred_element_type=F32)
    out_ref[...] = x_ref[...] + gt_ref[0] * _rms(y, g_ref[...])


def _outproj(o, w_out, x, g1, mod, grp, tm):
    t, d = x.shape
    row = lambda i: (i, 0)
    return pl.pallas_call(
        _outproj_kernel,
        grid=(t // tm,),
        in_specs=[pl.BlockSpec((tm, o.shape[1]), row), _const_spec(w_out.shape),
                  pl.BlockSpec((tm, d), row), _const_spec((1, d)), _mod_spec(grp, 2, d)],
        out_specs=pl.BlockSpec((tm, d), row),
        out_shape=jax.ShapeDtypeStruct((t, d), F32),
        compiler_params=_params(("arbitrary",)),
        name="outproj",
    )(o, w_out, x, g1, mod)


def _mlp_kernel(x_ref, g2_ref, sc_ref, sh_ref, w1_ref, w2_ref, g3_ref, gt_ref, out_ref,
                f_ref, acc_ref):
    k = pl.program_id(1)

    def ff_chunk():
        u = jnp.maximum(jnp.dot(f_ref[...], w1_ref[...], preferred_element_type=F32), 0.0)
        return jnp.dot((u * u).astype(BF16), w2_ref[...], preferred_element_type=F32)

    @pl.when(k == 0)
    def _():
        _modulated_norm(x_ref, g2_ref, sc_ref, sh_ref, f_ref)
        acc_ref[...] = ff_chunk()

    @pl.when(k > 0)
    def _():
        acc_ref[...] += ff_chunk()

    @pl.when(k == pl.num_programs(1) - 1)
    def _():
        gain = gt_ref[0] * g3_ref[...]

        def rows(rs):
            y = acc_ref[rs, :]
            out_ref[rs, :] = x_ref[rs, :] + y * _inv_rms(y) * gain
        _row_chunks(acc_ref.shape[0], rows)


def _mlp(x, g2, g3, mod, grp, w1, w2, tm):
    t, d = x.shape
    dff = w1.shape[1]
    row = lambda i, k: (i, 0)
    return pl.pallas_call(
        _mlp_kernel,
        grid=(t // tm, dff // FF_TILE),
        in_specs=[pl.BlockSpec((tm, d), row), _const_spec((1, d)),
                  _mod_spec(grp, 4, d), _mod_spec(grp, 3, d),
                  pl.BlockSpec((d, FF_TILE), lambda i, k: (0, k)),
                  pl.BlockSpec((FF_TILE, d), lambda i, k: (k, 0)),
                  _const_spec((1, d)), _mod_spec(grp, 5, d)],
        out_specs=pl.BlockSpec((tm, d), row),
        out_shape=jax.ShapeDtypeStruct((t, d), F32),
        scratch_shapes=[pltpu.VMEM((tm, d), BF16), pltpu.VMEM((tm, d), F32)],
        compiler_params=_params(("arbitrary", "arbitrary")),
        name="mlp",
    )(x, g2, mod, mod, w1, w2, g3, mod)


def _swa_proj_kernel(*refs, rope, qscale):
    if rope:
        (x_ref, g_ref, sc_ref, sh_ref, wqt_ref, wk_ref, wvt_ref, cos_ref, sin_ref, cost_ref,
         sint_ref, q_ref, k2_ref, vt_ref) = refs
        cos, sin = cos_ref[...], sin_ref[...]
    else:
        x_ref, g_ref, sc_ref, sh_ref, wqt_ref, wk_ref, wvt_ref, q_ref, k2_ref, vt_ref = refs
    h = (_rms(x_ref[...], g_ref[...]) * (1.0 + sc_ref[0]) + sh_ref[0]).astype(BF16)
    qt = lax.dot_general(wqt_ref[...], h, NT_DIMS, preferred_element_type=F32) * qscale
    for t in range(qt.shape[0] // LANES):
        tile = qt[t * LANES:(t + 1) * LANES, :]
        if rope:
            swapped = jnp.concatenate([tile[HALF_TILE:], tile[:HALF_TILE]], axis=0)
            tile = tile * cost_ref[...] + swapped * sint_ref[...]
        q_ref[t * LANES:(t + 1) * LANES, :] = tile.astype(BF16)
    k = jnp.dot(h, wk_ref[...], preferred_element_type=F32)
    for c in range(SWA_KV_HEADS):
        tile = k[:, c * LANES:(c + 1) * LANES]
        if rope:
            tile = _rot(tile, cos, sin)
        first = _first_head_lanes(tile.shape)
        lane = lax.broadcasted_iota(jnp.int32, tile.shape, 1)
        k2_ref[:, 2 * c * LANES:(2 * c + 1) * LANES] = jnp.where(
            lane == _shift_lane(0), 1.0, jnp.where(first, tile, 0.0)).astype(BF16)
        k2_ref[:, (2 * c + 1) * LANES:(2 * c + 2) * LANES] = jnp.where(
            lane == _shift_lane(1), 1.0, jnp.where(first, 0.0, tile)).astype(BF16)
    vt_ref[...] = lax.dot_general(wvt_ref[...], h, NT_DIMS,
                                  preferred_element_type=F32).astype(BF16)


def _swa_proj(x, mod, grp, g0, w, rope_tabs, tm, n_per_batch):
    t, d = x.shape
    rope = rope_tabs is not None
    row = lambda i: (i, 0)
    dq = SWA_HEADS * SWA_HEAD_DIM
    dkv = SWA_KV_HEADS * SWA_HEAD_DIM
    w_qt, w_k, w_vt = w
    col = lambda i: (0, i)
    in_specs = [pl.BlockSpec((tm, d), row), _const_spec((1, d)),
                _mod_spec(grp, 1, d), _mod_spec(grp, 0, d), _const_spec(w_qt.shape),
                _const_spec(w_k.shape), _const_spec(w_vt.shape)]
    args = [x, g0, mod, mod, w_qt, w_k, w_vt]
    if rope:
        pos = lambda i: (i % n_per_batch, 0)
        pos_t = lambda i: (0, i % n_per_batch)
        in_specs += [pl.BlockSpec((tm, LANES), pos), pl.BlockSpec((tm, LANES), pos),
                     pl.BlockSpec((LANES, tm), pos_t), pl.BlockSpec((LANES, tm), pos_t)]
        args += list(rope_tabs)
    out_shape = [jax.ShapeDtypeStruct((dq, t), BF16),
                 jax.ShapeDtypeStruct((t, 2 * SWA_KV_HEADS * LANES), BF16),
                 jax.ShapeDtypeStruct((dkv, t), BF16)]
    out_specs = [pl.BlockSpec((dq, tm), col), pl.BlockSpec((tm, 2 * SWA_KV_HEADS * LANES), row),
                 pl.BlockSpec((dkv, tm), col)]
    return pl.pallas_call(
        functools.partial(_swa_proj_kernel, rope=rope, qscale=SWA_HEAD_DIM ** -0.5 * LOG2E),
        grid=(t // tm,), in_specs=in_specs, out_specs=out_specs, out_shape=out_shape,
        compiler_params=_params(("arbitrary",)),
        name="swa_proj_lat" if rope else "swa_proj_ctx",
    )(*args)


def _swa_attn_kernel(sink_ref, q_ref, kc_ref, kp_ref, kcur_ref, kn_ref, vc_ref, vp_ref, vcur_ref,
                     vn_ref, o_ref, s_ref, qa_ref, *, qb):
    kvh = pl.program_id(1)
    i = pl.program_id(2)
    lc = kc_ref.shape[0]
    span = SWA_QBLK + 2 * SWA_WINDOW
    n_pairs = SWA_GROUP // 2
    kc, vc = kc_ref[...], vc_ref[...]
    kwin = jnp.concatenate([kp_ref[...], kcur_ref[...], kn_ref[...]], axis=0)
    vwin = jnp.concatenate([vp_ref[...], vcur_ref[...], vn_ref[...]], axis=1)
    r = lax.broadcasted_iota(jnp.int32, (lc + span, SWA_QBLK), 0)
    rel = r - lc - lax.broadcasted_iota(jnp.int32, (lc + span, SWA_QBLK), 1)
    bias = jnp.where((r < lc) | ((rel >= 0) & (rel <= 2 * SWA_WINDOW)), 0.0, NEG_BIG)
    bias = jnp.concatenate([bias] * n_pairs, axis=1)
    pair = lax.broadcasted_iota(jnp.int32, (1, n_pairs * SWA_QBLK), 1) // SWA_QBLK
    keep_prev = jnp.where(i == 0, 0.0, 1.0)
    keep_next = jnp.where(i == pl.num_programs(2) - 1, 0.0, 1.0)

    ones = jnp.ones((LANES, LANES), BF16)
    kall = jnp.concatenate([kc[:, :LANES], kwin[:, :LANES]], axis=0)
    k2 = jnp.max(jnp.dot(kall * kall, ones, preferred_element_type=F32), axis=0, keepdims=True)
    k2 = k2[:, 0:1] * NORM_SLACK
    feat = lax.broadcasted_iota(jnp.int32, (LANES, n_pairs * SWA_QBLK), 0)
    first_rows = (feat & 32) == 0
    q_t, refs = [], []
    for blk in range(qb):
        qt = jnp.concatenate([q_ref[t * LANES:(t + 1) * LANES, blk * SWA_QBLK:(blk + 1) * SWA_QBLK]
                              for t in range(n_pairs)], axis=1)
        sq = qt.astype(F32)
        sq = sq * sq
        q2_first = jnp.sum(jnp.where(first_rows, sq, 0.0), axis=0, keepdims=True)
        q2_second = jnp.sum(sq, axis=0, keepdims=True) - q2_first
        q_t.append(qt)
        refs.append([jnp.sqrt(q2 * k2).astype(BF16) for q2 in (q2_first, q2_second)])
    ref_max = functools.reduce(jnp.maximum, [jnp.max(r.astype(F32)) for pr in refs for r in pr])
    fast = ref_max <= REF_LIMIT
    def set_queries(shifted):
        for blk in range(qb):
            for e in range(2):
                shift = -refs[blk][e] if shifted else jnp.zeros_like(refs[blk][e])
                qa_ref[blk, e] = jnp.where(feat == _shift_lane(e), shift, q_t[blk])

    def scores(blk, e):
        kcat = jnp.concatenate([kc[:, e * LANES:(e + 1) * LANES],
                                kwin[blk * SWA_QBLK:blk * SWA_QBLK + span,
                                     e * LANES:(e + 1) * LANES]], axis=0)
        return jnp.dot(kcat, qa_ref[blk, e], preferred_element_type=F32)

    def run(shifted):
        s_ref[0] = scores(0, 0)
        for blk in range(qb):
            rows = slice(blk * SWA_QBLK, (blk + 1) * SWA_QBLK)
            vt = jnp.concatenate([vc, vwin[:, blk * SWA_QBLK:blk * SWA_QBLK + span]], axis=1)
            halves = []
            for e in range(2):
                if e == 0:
                    s_ref[1] = scores(blk, 1)
                elif blk + 1 < qb:
                    s_ref[0] = scores(blk + 1, 0)
                s = s_ref[e] + bias
                sk = jnp.zeros(pair.shape, F32)
                for t in range(n_pairs):
                    sk = jnp.where(pair == t, sink_ref[kvh * SWA_GROUP + 2 * t + e] * LOG2E, sk)
                if shifted:
                    m = refs[blk][e].astype(F32)
                    p = jnp.exp2(s)
                else:
                    m = jnp.maximum(jnp.max(s, axis=0, keepdims=True), sk)
                    p = jnp.exp2(s - m)
                if blk == 0:
                    p = jnp.concatenate([p[:lc], p[lc:lc + SWA_WINDOW] * keep_prev,
                                         p[lc + SWA_WINDOW:]], axis=0)
                if blk == qb - 1:
                    p = jnp.concatenate([p[:lc + span - SWA_WINDOW],
                                         p[lc + span - SWA_WINDOW:] * keep_next], axis=0)
                den = jnp.sum(p, axis=0, keepdims=True) + jnp.exp2(sk - m)
                o = jnp.dot(vt, p.astype(BF16), preferred_element_type=F32)
                halves.append(o * (1.0 / den))
            both = jnp.concatenate(halves, axis=0)
            for t in range(n_pairs):
                tile = both[:, t * SWA_QBLK:(t + 1) * SWA_QBLK]
                o_ref[rows, t * LANES:(t + 1) * LANES] = tile.T.astype(BF16)

    set_queries(True)
    run(True)

    @pl.when(jnp.logical_not(fast))
    def _():
        set_queries(False)
        run(False)


def _swa_attn(sink, q, k2, vt, k2c, vtc, nb, seq, lc):
    t = q.shape[1]
    nblk = seq // SWA_QBLK
    qb = min(SWA_STEP_BLOCKS, nblk)
    nsteps = nblk // qb
    gq = SWA_GROUP * SWA_HEAD_DIM
    hd = SWA_HEAD_DIM
    prev_blk = lambda b, i: b * nblk + jnp.maximum(qb * i - 1, 0)
    next_blk = lambda b, i: b * nblk + jnp.minimum(qb * i + qb, nblk - 1)
    return pl.pallas_call(
        functools.partial(_swa_attn_kernel, qb=qb),
        grid=(nb, SWA_KV_HEADS, nsteps),
        in_specs=[pl.BlockSpec(memory_space=pltpu.SMEM),
                  pl.BlockSpec((gq, qb * SWA_QBLK), lambda b, h, i: (h, b * nsteps + i)),
                  pl.BlockSpec((lc, 2 * LANES), lambda b, h, i: (b, h)),
                  pl.BlockSpec((SWA_QBLK, 2 * LANES), lambda b, h, i: (prev_blk(b, i), h)),
                  pl.BlockSpec((qb * SWA_QBLK, 2 * LANES), lambda b, h, i: (b * nsteps + i, h)),
                  pl.BlockSpec((SWA_QBLK, 2 * LANES), lambda b, h, i: (next_blk(b, i), h)),
                  pl.BlockSpec((hd, lc), lambda b, h, i: (h, b)),
                  pl.BlockSpec((hd, SWA_QBLK), lambda b, h, i: (h, prev_blk(b, i))),
                  pl.BlockSpec((hd, qb * SWA_QBLK), lambda b, h, i: (h, b * nsteps + i)),
                  pl.BlockSpec((hd, SWA_QBLK), lambda b, h, i: (h, next_blk(b, i)))],
        out_specs=pl.BlockSpec((qb * SWA_QBLK, gq), lambda b, h, i: (b * nsteps + i, h)),
        out_shape=jax.ShapeDtypeStruct((t, SWA_HEADS * SWA_HEAD_DIM), BF16),
        scratch_shapes=[pltpu.VMEM((2, lc + SWA_QBLK + 2 * SWA_WINDOW, gq), F32),
                        pltpu.VMEM((qb, 2, LANES, gq), BF16)],
        compiler_params=_params(("arbitrary", "arbitrary", "arbitrary")),
        name="swa_attn",
    )(sink, q, k2c, k2, k2, k2, vtc, vt, vt, vt)


def _rope_tables(seq):
    rows = seq // GRID_W
    row = jnp.repeat(jnp.arange(rows, dtype=F32), GRID_W)
    col = jnp.tile(jnp.arange(GRID_W, dtype=F32), rows)
    n_freq = MLA_ROPE // 4
    freqs = ROPE_BASE ** (-jnp.arange(n_freq, dtype=F32) / n_freq)
    ang = jnp.concatenate([row[:, None] * freqs, col[:, None] * freqs], axis=-1)
    cos, sin = jnp.cos(ang), jnp.sin(ang)
    cos_t = jnp.concatenate([cos, cos, cos, cos], axis=-1)
    sin_t = jnp.concatenate([-sin, -sin, sin, sin], axis=-1)
    return cos_t, sin_t, cos_t.T, sin_t.T


def _pair_tiles(w, n_heads, half):
    k = w.shape[0]
    x1 = w[:, :, :half].reshape(k, n_heads // 2, 2 * half)
    x2 = w[:, :, half:].reshape(k, n_heads // 2, 2 * half)
    return jnp.concatenate([x1, x2], axis=2).reshape(k, n_heads * 2 * half)


def _mla_weights(w_in, g_qa, g_kva, w_qb, w_kvb):
    half = MLA_ROPE // 2
    lat = MLA_Q_LORA + MLA_KV_LORA
    k1, k2 = w_in[:, lat:lat + half], w_in[:, lat + half:]
    qb = w_qb.reshape(MLA_Q_LORA, MLA_HEADS, MLA_NOPE + MLA_ROPE)
    kvb = w_kvb.reshape(MLA_KV_LORA, MLA_HEADS, MLA_NOPE + MLA_V)
    return {
        "w_in": jnp.concatenate([w_in[:, :lat], k1, k1, k2, k2], axis=1).astype(BF16),
        "g_qa": g_qa.reshape(1, -1), "g_kva": g_kva.reshape(1, -1),
        "w_qn": qb[:, :, :MLA_NOPE].reshape(MLA_Q_LORA, -1).T.astype(BF16),
        "w_qr": _pair_tiles(qb[:, :, MLA_NOPE:], MLA_HEADS, half).T.astype(BF16),
        "w_kn": kvb[:, :, :MLA_NOPE].reshape(MLA_KV_LORA, -1).astype(BF16),
        "w_vt": kvb[:, :, MLA_NOPE:].reshape(MLA_KV_LORA, -1).T.astype(BF16),
    }


def _swa_weights(w_qkv):
    d = w_qkv.shape[0]
    half = SWA_HEAD_DIM // 2
    dq = SWA_HEADS * SWA_HEAD_DIM
    dkv = SWA_KV_HEADS * SWA_HEAD_DIM
    q = _pair_tiles(w_qkv[:, :dq].reshape(d, SWA_HEADS, SWA_HEAD_DIM), SWA_HEADS, half)
    k = w_qkv[:, dq:dq + dkv].reshape(d, SWA_KV_HEADS, SWA_HEAD_DIM)
    k1, k2 = k[:, :, :half], k[:, :, half:]
    k = jnp.concatenate([k1, k1, k2, k2], axis=2).reshape(d, SWA_KV_HEADS * LANES)
    return q.T.astype(BF16), k.astype(BF16), w_qkv[:, dq + dkv:].T.astype(BF16)


def kernel(x, c, ctx, c_ctx, w_mod, b_mod, g_norm, w_ff_in, w_ff_out, mla_w_in, mla_g_qa,
           mla_g_kva, mla_w_qb, mla_w_kvb, mla_w_out, swa_w_qkv, swa_sink, swa_w_out):
    nb, seq, d = x.shape
    lc = ctx.shape[1]
    depth = w_mod.shape[0]
    assert nb + 1 <= MOD_ROWS
    tm = min(TOKEN_TILE, seq)
    tq = min(ATTN_TQ, seq)
    n_per_batch = seq // tm
    grp_lat = lambda i: i // n_per_batch
    grp_ctx = lambda i: nb

    cmat = jnp.zeros((MOD_ROWS, d), F32).at[:nb].set(c).at[nb].set(c_ctx)
    mod_all = _modulation(cmat, w_mod, b_mod)
    rope_tabs = _rope_tables(seq)

    xl = x.reshape(nb * seq, d)
    xc = ctx.reshape(nb * lc, d)
    for i in range(depth):
        need_ctx = i < depth - 1
        mod = mod_all[i].reshape(MOD_ROWS, 1, 6 * d)
        g = g_norm[i].reshape(4, 1, d)
        j = i // 2
        if i % 2 == 0:
            w = _mla_weights(mla_w_in[j], mla_g_qa[j], mla_g_kva[j], mla_w_qb[j], mla_w_kvb[j])
            w_out = mla_w_out[j].astype(BF16)
            qn, qr, kn, kr, vt = _mla_proj(xl, mod, grp_lat, g[0], w, rope_tabs, tm, n_per_batch)
            qnc, qrc, knc, krc, vtc = _mla_proj(xc, mod, grp_ctx, g[0], w, None, lc, 1)
            ctx_kv = (knc.reshape(nb, lc, -1), krc.reshape(nb, lc, -1), vtc)
            lat_kv = (kn.reshape(nb, seq, -1), kr.reshape(nb, seq, -1), vt)
            o_l = _mla_attn(qn, qr, ctx_kv, lat_kv, nb, tq)
            o_c = _mla_attn(qnc, qrc, ctx_kv, None, nb, lc) if need_ctx else None
        else:
            w = _swa_weights(swa_w_qkv[j])
            w_out = swa_w_out[j].astype(BF16)
            q, k2, vt = _swa_proj(xl, mod, grp_lat, g[0], w, rope_tabs, tm, n_per_batch)
            qc, k2c, vtc = _swa_proj(xc, mod, grp_ctx, g[0], w, None, lc, 1)
            o_l = _swa_attn(swa_sink[j], q, k2, vt, k2c, vtc, nb, seq, lc)
            assert not need_ctx
            o_c = None
        w1 = _layer_to_bf16(w_ff_in, i)
        w2 = _layer_to_bf16(w_ff_out, i)
        xl = _outproj(o_l, w_out, xl, g[1], mod, grp_lat, tm)
        xl = _mlp(xl, g[2], g[3], mod, grp_lat, w1, w2, tm)
        if need_ctx:
            xc = _outproj(o_c, w_out, xc, g[1], mod, grp_ctx, lc)
            xc = _mlp(xc, g[2], g[3], mod, grp_ctx, w1, w2, lc)
    return xl.reshape(nb, seq, d)
```

```python
import functools
import math

import jax
import jax.numpy as jnp
from jax import lax
from jax.experimental import pallas as pl
from jax.experimental.pallas import tpu as pltpu

F32 = jnp.float32
BF16 = jnp.bfloat16

GRID_W = 64
ROPE_BASE = 10000.0
NORM_EPS = 1e-6
LOG2E = math.log2(math.e)
NEG_BIG = -1e30
REF_LIMIT = 60.0
NORM_SLACK = 1.03

MLA_HEADS = 16
MLA_Q_LORA = 512
MLA_KV_LORA = 512
MLA_NOPE = 128
MLA_ROPE = 64
MLA_V = 128

SWA_HEADS = 32
SWA_KV_HEADS = 4
SWA_HEAD_DIM = 64
SWA_WINDOW = 128
SWA_GROUP = SWA_HEADS // SWA_KV_HEADS
SWA_QBLK = 128
SWA_STEP_BLOCKS = 16

LANES = 128
HALF_TILE = 64

MOD_ROWS = 8
MOD_TN = 2048
TOKEN_TILE = 512
ROW_CHUNK = 32
ROW_UNROLL = 8
FF_TILE = 1024
ATTN_TQ = 512
KV_GROUP = 8
KV_UNROLL = 4
CAST_BLOCK_BYTES = 8 * 1024 * 1024
VMEM_LIMIT = 56 * 1024 * 1024

NT_DIMS = (((1,), (1,)), ((), ()))


def _rms(xf, g):
    ms = jnp.mean(xf * xf, axis=-1, keepdims=True)
    return xf * lax.rsqrt(ms + NORM_EPS) * g


def _inv_rms(xf):
    return lax.rsqrt(jnp.mean(xf * xf, axis=-1, keepdims=True) + NORM_EPS)


def _row_chunks(n_rows, fn):
    def body(i, carry):
        fn(pl.ds(pl.multiple_of(i * ROW_CHUNK, ROW_CHUNK), ROW_CHUNK))
        return carry
    lax.fori_loop(0, n_rows // ROW_CHUNK, body, 0, unroll=ROW_UNROLL)


def _modulated_norm(x_ref, g_ref, sc_ref, sh_ref, h_ref):
    gain = g_ref[...] * (1.0 + sc_ref[0])
    shift = sh_ref[0]

    def rows(rs):
        xf = x_ref[rs, :]
        h_ref[rs, :] = (xf * _inv_rms(xf) * gain + shift).astype(h_ref.dtype)
    _row_chunks(h_ref.shape[0], rows)


def _rot(tile, cos, sin):
    return tile * cos + pltpu.roll(tile, HALF_TILE, 1) * sin


def _first_head_lanes(shape):
    lane = lax.broadcasted_iota(jnp.int32, shape, 1)
    return (lane & 32) == 0


def _shift_lane(parity):
    return LANES - 1 - 32 * parity


def _params(sem):
    return pltpu.CompilerParams(dimension_semantics=sem, vmem_limit_bytes=VMEM_LIMIT)


def _const_spec(shape):
    nd = len(shape)
    return pl.BlockSpec(shape, lambda *_: (0,) * nd, pipeline_mode=pl.Buffered(1))


def _mod_spec(grp, which, d):
    return pl.BlockSpec((1, 1, d), lambda i, *_: (grp(i), 0, which))


def _mod_kernel(c_ref, w_ref, b_ref, o_ref):
    c = c_ref[...]
    a = c / (1.0 + jnp.exp(-c))
    o_ref[0] = jnp.dot(a, w_ref[0], preferred_element_type=F32,
                       precision=lax.Precision.HIGHEST) + b_ref[0]


def _modulation(cmat, w_mod, b_mod):
    depth, d, n = w_mod.shape
    return pl.pallas_call(
        _mod_kernel,
        grid=(depth, n // MOD_TN),
        in_specs=[pl.BlockSpec((MOD_ROWS, d), lambda l, j: (0, 0)),
                  pl.BlockSpec((1, d, MOD_TN), lambda l, j: (l, 0, j)),
                  pl.BlockSpec((1, 1, MOD_TN), lambda l, j: (l, 0, j))],
        out_specs=pl.BlockSpec((1, MOD_ROWS, MOD_TN), lambda l, j: (l, 0, j)),
        out_shape=jax.ShapeDtypeStruct((depth, MOD_ROWS, n), F32),
        compiler_params=_params(("arbitrary", "arbitrary")),
        name="modulation",
    )(cmat, w_mod, b_mod.reshape(depth, 1, n))


def _cast_kernel(w_ref, o_ref):
    o_ref[...] = w_ref[0].astype(o_ref.dtype)


def _layer_to_bf16(w_stack, layer):
    _, r, c = w_stack.shape
    br = max(8, min(r, CAST_BLOCK_BYTES // (4 * c)))
    assert r % br == 0
    return pl.pallas_call(
        _cast_kernel,
        grid=(r // br,),
        in_specs=[pl.BlockSpec((1, br, c), lambda j: (layer, j, 0))],
        out_specs=pl.BlockSpec((br, c), lambda j: (j, 0)),
        out_shape=jax.ShapeDtypeStruct((r, c), BF16),
        compiler_params=_params(("arbitrary",)),
        name="cast_bf16",
    )(w_stack)


def _mla_proj_kernel(*refs, rope, qscale):
    if rope:
        (x_ref, g_ref, sc_ref, sh_ref, win_ref, gqa_ref, gkva_ref, wqn_ref, wqr_ref, wkn_ref,
         wvt_ref, cos_ref, sin_ref, cost_ref, sint_ref, qn_ref, qr_ref, kn_ref, kr_ref,
         vt_ref) = refs
        cos, sin = cos_ref[...], sin_ref[...]
    else:
        (x_ref, g_ref, sc_ref, sh_ref, win_ref, gqa_ref, gkva_ref, wqn_ref, wqr_ref, wkn_ref,
         wvt_ref, qn_ref, qr_ref, kn_ref, kr_ref, vt_ref) = refs
    h = (_rms(x_ref[...], g_ref[...]) * (1.0 + sc_ref[0]) + sh_ref[0]).astype(BF16)
    p = jnp.dot(h, win_ref[...], preferred_element_type=F32)
    qa = _rms(p[:, :MLA_Q_LORA], gqa_ref[...]).astype(BF16)
    ckv = _rms(p[:, MLA_Q_LORA:MLA_Q_LORA + MLA_KV_LORA], gkva_ref[...]).astype(BF16)
    kr = p[:, MLA_Q_LORA + MLA_KV_LORA:]

    qn = lax.dot_general(wqn_ref[...], qa, NT_DIMS, preferred_element_type=F32) * qscale
    qn_ref[...] = qn.astype(BF16)
    qr = lax.dot_general(wqr_ref[...], qa, NT_DIMS, preferred_element_type=F32) * qscale
    for t in range(qr.shape[0] // LANES):
        tile = qr[t * LANES:(t + 1) * LANES, :]
        if rope:
            swapped = jnp.concatenate([tile[HALF_TILE:], tile[:HALF_TILE]], axis=0)
            tile = tile * cost_ref[...] + swapped * sint_ref[...]
        qr_ref[t * LANES:(t + 1) * LANES, :] = tile.astype(BF16)

    if rope:
        kr = _rot(kr, cos, sin)
    first = _first_head_lanes(kr.shape)
    lane = lax.broadcasted_iota(jnp.int32, kr.shape, 1)
    kr_ref[:, :LANES] = jnp.where(lane == _shift_lane(0), 1.0,
                                  jnp.where(first, kr, 0.0)).astype(BF16)
    kr_ref[:, LANES:] = jnp.where(lane == _shift_lane(1), 1.0,
                                  jnp.where(first, 0.0, kr)).astype(BF16)

    kn_ref[...] = jnp.dot(ckv, wkn_ref[...], preferred_element_type=F32).astype(BF16)
    vt = lax.dot_general(wvt_ref[...], ckv, NT_DIMS, preferred_element_type=F32).astype(BF16)
    for hd in range(MLA_HEADS):
        vt_ref[0, hd, 0] = vt[hd * MLA_V:(hd + 1) * MLA_V, :]


def _mla_proj(x, mod, grp, g0, w, rope_tabs, tm, n_per_batch):
    t, d = x.shape
    n_tiles = t // tm
    nb = n_tiles // n_per_batch
    rope = rope_tabs is not None
    qscale = (MLA_NOPE + MLA_ROPE) ** -0.5 * LOG2E
    row = lambda i: (i, 0)
    in_specs = [pl.BlockSpec((tm, d), row), _const_spec((1, d)),
                _mod_spec(grp, 1, d), _mod_spec(grp, 0, d),
                _const_spec(w["w_in"].shape), _const_spec((1, MLA_Q_LORA)),
                _const_spec((1, MLA_KV_LORA)), _const_spec(w["w_qn"].shape),
                _const_spec(w["w_qr"].shape), _const_spec(w["w_kn"].shape),
                _const_spec(w["w_vt"].shape)]
    args = [x, g0, mod, mod, w["w_in"], w["g_qa"], w["g_kva"], w["w_qn"], w["w_qr"], w["w_kn"],
            w["w_vt"]]
    if rope:
        pos = lambda i: (i % n_per_batch, 0)
        pos_t = lambda i: (0, i % n_per_batch)
        in_specs += [pl.BlockSpec((tm, LANES), pos), pl.BlockSpec((tm, LANES), pos),
                     pl.BlockSpec((LANES, tm), pos_t), pl.BlockSpec((LANES, tm), pos_t)]
        args += list(rope_tabs)
    hn = MLA_HEADS * MLA_NOPE
    hr = MLA_HEADS * MLA_ROPE
    col = lambda i: (0, i)
    out_shape = [jax.ShapeDtypeStruct((hn, t), BF16), jax.ShapeDtypeStruct((hr, t), BF16),
                 jax.ShapeDtypeStruct((t, hn), BF16), jax.ShapeDtypeStruct((t, 2 * LANES), BF16),
                 jax.ShapeDtypeStruct((nb, MLA_HEADS, n_per_batch, MLA_V, tm), BF16)]
    out_specs = [pl.BlockSpec((hn, tm), col), pl.BlockSpec((hr, tm), col),
                 pl.BlockSpec((tm, hn), row), pl.BlockSpec((tm, 2 * LANES), row),
                 pl.BlockSpec((1, MLA_HEADS, 1, MLA_V, tm),
                              lambda i: (i // n_per_batch, 0, i % n_per_batch, 0, 0))]
    return pl.pallas_call(
        functools.partial(_mla_proj_kernel, rope=rope, qscale=qscale),
        grid=(n_tiles,), in_specs=in_specs, out_specs=out_specs, out_shape=out_shape,
        compiler_params=_params(("arbitrary",)),
        name="mla_proj_lat" if rope else "mla_proj_ctx",
    )(*args)


def _kv_group_sizes(n_chunks):
    if n_chunks <= 2:
        return [n_chunks] if n_chunks else []
    tail = [min(KV_GROUP, n_chunks) // 2] * 2
    body = n_chunks - sum(tail)
    assert body % KV_GROUP == 0
    return [KV_GROUP] * (body // KV_GROUP) + tail


def _mla_attn_kernel(*refs, n_lat, tk):
    if n_lat:
        (qn_ref, qr_ref, knc_ref, krc_ref, vtc_ref, knl_ref, krl_ref, vtl_ref, o_ref,
         acc_ref, s_ref, qa_ref, k2_ref) = refs
    else:
        qn_ref, qr_ref, knc_ref, krc_ref, vtc_ref, o_ref, acc_ref, s_ref, qa_ref, k2_ref = refs
    tq = qn_ref.shape[1]
    lc = knc_ref.shape[1]
    parity = pl.program_id(1) % 2

    def lat_keys(j):
        off = pl.multiple_of(j * tk, tk)
        return knl_ref[0, pl.ds(off, tk), :], krl_ref[0, pl.ds(off, tk), :]

    @pl.when(pl.program_id(2) == 0)
    def _():
        ones = jnp.ones((LANES, LANES), BF16)

        def sqnorm_max(kn, kr):
            r = jnp.dot(kn * kn + kr * kr, ones, preferred_element_type=F32)
            return jnp.max(r, axis=0, keepdims=True)
        mx = sqnorm_max(knc_ref[0], krc_ref[0])
        if n_lat:
            mx = lax.fori_loop(0, n_lat, lambda j, c: jnp.maximum(c, sqnorm_max(*lat_keys(j))), mx,
                               unroll=KV_UNROLL)
        k2_ref[...] = jnp.broadcast_to(mx * NORM_SLACK, k2_ref.shape)

    qn = qn_ref[...]
    qr = qr_ref[...]
    feat = lax.broadcasted_iota(jnp.int32, qr.shape, 0)
    own = ((feat >> 5) & 1) == parity
    qnf, qrf = qn.astype(F32), jnp.where(own, qr.astype(F32), 0.0)
    q2 = jnp.sum(qnf * qnf + qrf * qrf, axis=0, keepdims=True)
    ref = jnp.sqrt(q2 * k2_ref[0:1, 0:1])
    fast = jnp.max(ref) <= REF_LIMIT
    shift = jnp.where(fast, -ref, 0.0).astype(BF16)
    qa_ref[:LANES, :] = qn
    qa_ref[LANES:, :] = jnp.where(feat == LANES - 1 - 32 * parity, shift, qr)

    def scores(kn, kr):
        return jnp.dot(jnp.concatenate([kn, kr], axis=1), qa_ref[...],
                       preferred_element_type=F32)

    def finish(acc, l):
        o_ref[...] = (acc * (1.0 / l)).T.astype(BF16)

    def shifted_softmax():
        def chunk(s, vt, l8):
            p = jnp.exp2(s)
            l8 = l8 + jnp.sum(p.reshape(-1, 8, tq), axis=0)
            return l8, jnp.dot(vt, p.astype(BF16), preferred_element_type=F32)

        sizes = _kv_group_sizes(n_lat)
        starts = [sum(sizes[:g]) for g in range(len(sizes))]
        last = max(len(sizes) - 1, 0)

        def n_rows(g):
            return (sizes[g] * tk if sizes else 0) + (lc if g == last else 0)

        def fill_scores(g):
            view = s_ref.at[g % 2]
            lat = sizes[g] * tk if sizes else 0
            if lat:
                rows = pl.ds(starts[g] * tk, lat)
                view[pl.ds(0, lat), :] = scores(knl_ref[0, rows, :], krl_ref[0, rows, :])
            if g == last:
                view[pl.ds(lat, lc), :] = scores(knc_ref[0], krc_ref[0])

        fill_scores(0)
        l8 = jnp.zeros((8, tq), F32)
        for g in range(last + 1):
            if g < last:
                fill_scores(g + 1)
            vts = [vtl_ref[0, 0, starts[g] + u] for u in range(sizes[g])] if sizes else []
            if g == last:
                vts.append(vtc_ref[0, 0, 0])
            l8, pv = chunk(s_ref[g % 2, pl.ds(0, n_rows(g)), :], jnp.concatenate(vts, axis=1), l8)
            acc = pv if g == 0 else acc_ref[...] + pv
            if g < last:
                acc_ref[...] = acc
        finish(acc, jnp.sum(l8, axis=0, keepdims=True))

    def online_softmax():
        def update(s_view, vt, m, l):
            m_new = jnp.maximum(m, jnp.max(s_view[...], axis=0, keepdims=True))
            alpha = jnp.exp2(m - m_new)
            p = jnp.exp2(s_view[...] - m_new)
            l_new = alpha * l + jnp.sum(p, axis=0, keepdims=True)
            acc_ref[...] = acc_ref[...] * alpha + jnp.dot(vt, p.astype(BF16),
                                                          preferred_element_type=F32)
            return m_new, l_new

        acc_ref[...] = jnp.zeros_like(acc_ref)
        m = jnp.full((1, tq), NEG_BIG, F32)
        l = jnp.zeros((1, tq), F32)
        slots = [s_ref.at[0, pl.ds(0, tk)], s_ref.at[1, pl.ds(0, tk)]] if n_lat else None
        if n_lat:
            slots[0][...] = scores(*lat_keys(0))
        ctx_view = s_ref.at[1, pl.ds(0, lc)]
        ctx_view[...] = scores(knc_ref[0], krc_ref[0])
        m, l = update(ctx_view, vtc_ref[0, 0, 0], m, l)
        if n_lat:
            def body(jj, carry):
                for u in range(KV_UNROLL):
                    j = KV_UNROLL * jj + u
                    slots[(u + 1) % 2][...] = scores(*lat_keys(jnp.minimum(j + 1, n_lat - 1)))
                    carry = update(slots[u % 2], vtl_ref[0, 0, j], *carry)
                return carry
            m, l = lax.fori_loop(0, n_lat // KV_UNROLL, body, (m, l))
        finish(acc_ref[...], l)

    lax.cond(fast, shifted_softmax, online_softmax)


def _mla_attn(qn, qr, ctx_kv, lat_kv, nb, tq):
    t = qn.shape[1]
    nq = t // nb // tq
    knc, krc, vtc = ctx_kv
    lc = knc.shape[1]
    qrow = lambda b, h, i: (b * nq + i, h)
    in_specs = [pl.BlockSpec((LANES, tq), lambda b, h, i: (h, b * nq + i)),
                pl.BlockSpec((LANES, tq), lambda b, h, i: (h // 2, b * nq + i)),
                pl.BlockSpec((1, lc, LANES), lambda b, h, i: (b, 0, h)),
                pl.BlockSpec((1, lc, LANES), lambda b, h, i: (b, 0, h % 2)),
                pl.BlockSpec((1, 1, 1, MLA_V, lc), lambda b, h, i: (b, h, 0, 0, 0))]
    args = [qn, qr, knc, krc, vtc]
    n_lat, tk = 0, 0
    if lat_kv is not None:
        knl, krl, vtl = lat_kv
        s = knl.shape[1]
        n_lat, tk = vtl.shape[2], vtl.shape[4]
        assert n_lat % KV_UNROLL == 0 and tk >= lc
        in_specs += [pl.BlockSpec((1, s, LANES), lambda b, h, i: (b, 0, h)),
                     pl.BlockSpec((1, s, LANES), lambda b, h, i: (b, 0, h % 2)),
                     pl.BlockSpec((1, 1, n_lat, MLA_V, tk), lambda b, h, i: (b, h, 0, 0, 0))]
        args += [knl, krl, vtl]
    scratch = [pltpu.VMEM((MLA_V, tq), F32), pltpu.VMEM((2, max(KV_GROUP * tk, lc), tq), F32),
               pltpu.VMEM((2 * LANES, tq), BF16), pltpu.VMEM((8, LANES), F32)]
    return pl.pallas_call(
        functools.partial(_mla_attn_kernel, n_lat=n_lat, tk=tk),
        grid=(nb, MLA_HEADS, nq), in_specs=in_specs,
        out_specs=pl.BlockSpec((tq, MLA_V), qrow),
        out_shape=jax.ShapeDtypeStruct((t, MLA_HEADS * MLA_V), BF16),
        scratch_shapes=scratch,
        compiler_params=_params(("arbitrary", "arbitrary", "arbitrary")),
        name="mla_attn_lat" if n_lat else "mla_attn_ctx",
    )(*args)


def _outproj_kernel(o_ref, w_ref, x_ref, g_ref, gt_ref, out_ref):
    y = jnp.dot(o_ref[...], w_ref[...], preferred_element_type=F32)
    out_ref[...] = x_ref[...] + gt_ref[0] * _rms(y, g_ref[...])


def _outproj(o, w_out, x, g1, mod, grp, tm):
    t, d = x.shape
    row = lambda i: (i, 0)
    return pl.pallas_call(
        _outproj_kernel,
        grid=(t // tm,),
        in_specs=[pl.BlockSpec((tm, o.shape[1]), row), _const_spec(w_out.shape),
                  pl.BlockSpec((tm, d), row), _const_spec((1, d)), _mod_spec(grp, 2, d)],
        out_specs=pl.BlockSpec((tm, d), row),
        out_shape=jax.ShapeDtypeStruct((t, d), F32),
        compiler_params=_params(("arbitrary",)),
        name="outproj",
    )(o, w_out, x, g1, mod)


def _mlp_kernel(x_ref, g2_ref, sc_ref, sh_ref, w1_ref, w2_ref, g3_ref, gt_ref, out_ref,
                f_ref, acc_ref):
    k = pl.program_id(1)

    def ff_chunk():
        u = jnp.maximum(jnp.dot(f_ref[...], w1_ref[...], preferred_element_type=F32), 0.0)
        return jnp.dot((u * u).astype(BF16), w2_ref[...], preferred_element_type=F32)

    @pl.when(k == 0)
    def _():
        _modulated_norm(x_ref, g2_ref, sc_ref, sh_ref, f_ref)
        acc_ref[...] = ff_chunk()

    @pl.when(k > 0)
    def _():
        acc_ref[...] += ff_chunk()

    @pl.when(k == pl.num_programs(1) - 1)
    def _():
        gain = gt_ref[0] * g3_ref[...]

        def rows(rs):
            y = acc_ref[rs, :]
            out_ref[rs, :] = x_ref[rs, :] + y * _inv_rms(y) * gain
        _row_chunks(acc_ref.shape[0], rows)


def _mlp(x, g2, g3, mod, grp, w1, w2, tm):
    t, d = x.shape
    dff = w1.shape[1]
    row = lambda i, k: (i, 0)
    return pl.pallas_call(
        _mlp_kernel,
        grid=(t // tm, dff // FF_TILE),
        in_specs=[pl.BlockSpec((tm, d), row), _const_spec((1, d)),
                  _mod_spec(grp, 4, d), _mod_spec(grp, 3, d),
                  pl.BlockSpec((d, FF_TILE), lambda i, k: (0, k)),
                  pl.BlockSpec((FF_TILE, d), lambda i, k: (k, 0)),
                  _const_spec((1, d)), _mod_spec(grp, 5, d)],
        out_specs=pl.BlockSpec((tm, d), row),
        out_shape=jax.ShapeDtypeStruct((t, d), F32),
        scratch_shapes=[pltpu.VMEM((tm, d), BF16), pltpu.VMEM((tm, d), F32)],
        compiler_params=_params(("arbitrary", "arbitrary")),
        name="mlp",
    )(x, g2, mod, mod, w1, w2, g3, mod)


def _swa_proj_kernel(*refs, rope, qscale):
    if rope:
        (x_ref, g_ref, sc_ref, sh_ref, wqt_ref, wk_ref, wvt_ref, cos_ref, sin_ref, cost_ref,
         sint_ref, q_ref, k2_ref, vt_ref) = refs
        cos, sin = cos_ref[...], sin_ref[...]
    else:
        x_ref, g_ref, sc_ref, sh_ref, wqt_ref, wk_ref, wvt_ref, q_ref, k2_ref, vt_ref = refs
    h = (_rms(x_ref[...], g_ref[...]) * (1.0 + sc_ref[0]) + sh_ref[0]).astype(BF16)
    qt = lax.dot_general(wqt_ref[...], h, NT_DIMS, preferred_element_type=F32) * qscale
    for t in range(qt.shape[0] // LANES):
        tile = qt[t * LANES:(t + 1) * LANES, :]
        if rope:
            swapped = jnp.concatenate([tile[HALF_TILE:], tile[:HALF_TILE]], axis=0)
            tile = tile * cost_ref[...] + swapped * sint_ref[...]
        q_ref[t * LANES:(t + 1) * LANES, :] = tile.astype(BF16)
    k = jnp.dot(h, wk_ref[...], preferred_element_type=F32)
    for c in range(SWA_KV_HEADS):
        tile = k[:, c * LANES:(c + 1) * LANES]
        if rope:
            tile = _rot(tile, cos, sin)
        first = _first_head_lanes(tile.shape)
        lane = lax.broadcasted_iota(jnp.int32, tile.shape, 1)
        k2_ref[:, 2 * c * LANES:(2 * c + 1) * LANES] = jnp.where(
            lane == _shift_lane(0), 1.0, jnp.where(first, tile, 0.0)).astype(BF16)
        k2_ref[:, (2 * c + 1) * LANES:(2 * c + 2) * LANES] = jnp.where(
            lane == _shift_lane(1), 1.0, jnp.where(first, 0.0, tile)).astype(BF16)
    vt_ref[...] = lax.dot_general(wvt_ref[...], h, NT_DIMS,
                                  preferred_element_type=F32).astype(BF16)


def _swa_proj(x, mod, grp, g0, w, rope_tabs, tm, n_per_batch):
    t, d = x.shape
    rope = rope_tabs is not None
    row = lambda i: (i, 0)
    dq = SWA_HEADS * SWA_HEAD_DIM
    dkv = SWA_KV_HEADS * SWA_HEAD_DIM
    w_qt, w_k, w_vt = w
    col = lambda i: (0, i)
    in_specs = [pl.BlockSpec((tm, d), row), _const_spec((1, d)),
                _mod_spec(grp, 1, d), _mod_spec(grp, 0, d), _const_spec(w_qt.shape),
                _const_spec(w_k.shape), _const_spec(w_vt.shape)]
    args = [x, g0, mod, mod, w_qt, w_k, w_vt]
    if rope:
        pos = lambda i: (i % n_per_batch, 0)
        pos_t = lambda i: (0, i % n_per_batch)
        in_specs += [pl.BlockSpec((tm, LANES), pos), pl.BlockSpec((tm, LANES), pos),
                     pl.BlockSpec((LANES, tm), pos_t), pl.BlockSpec((LANES, tm), pos_t)]
        args += list(rope_tabs)
    out_shape = [jax.ShapeDtypeStruct((dq, t), BF16),
                 jax.ShapeDtypeStruct((t, 2 * SWA_KV_HEADS * LANES), BF16),
                 jax.ShapeDtypeStruct((dkv, t), BF16)]
    out_specs = [pl.BlockSpec((dq, tm), col), pl.BlockSpec((tm, 2 * SWA_KV_HEADS * LANES), row),
                 pl.BlockSpec((dkv, tm), col)]
    return pl.pallas_call(
        functools.partial(_swa_proj_kernel, rope=rope, qscale=SWA_HEAD_DIM ** -0.5 * LOG2E),
        grid=(t // tm,), in_specs=in_specs, out_specs=out_specs, out_shape=out_shape,
        compiler_params=_params(("arbitrary",)),
        name="swa_proj_lat" if rope else "swa_proj_ctx",
    )(*args)


def _swa_attn_kernel(sink_ref, q_ref, kc_ref, kp_ref, kcur_ref, kn_ref, vc_ref, vp_ref, vcur_ref,
                     vn_ref, o_ref, s_ref, qa_ref, *, qb):
    kvh = pl.program_id(1)
    i = pl.program_id(2)
    lc = kc_ref.shape[0]
    span = SWA_QBLK + 2 * SWA_WINDOW
    n_pairs = SWA_GROUP // 2
    kc, vc = kc_ref[...], vc_ref[...]
    kwin = jnp.concatenate([kp_ref[...], kcur_ref[...], kn_ref[...]], axis=0)
    vwin = jnp.concatenate([vp_ref[...], vcur_ref[...], vn_ref[...]], axis=1)
    r = lax.broadcasted_iota(jnp.int32, (lc + span, SWA_QBLK), 0)
    rel = r - lc - lax.broadcasted_iota(jnp.int32, (lc + span, SWA_QBLK), 1)
    bias = jnp.where((r < lc) | ((rel >= 0) & (rel <= 2 * SWA_WINDOW)), 0.0, NEG_BIG)
    bias = jnp.concatenate([bias] * n_pairs, axis=1)
    pair = lax.broadcasted_iota(jnp.int32, (1, n_pairs * SWA_QBLK), 1) // SWA_QBLK
    keep_prev = jnp.where(i == 0, 0.0, 1.0)
    keep_next = jnp.where(i == pl.num_programs(2) - 1, 0.0, 1.0)

    ones = jnp.ones((LANES, LANES), BF16)
    kall = jnp.concatenate([kc[:, :LANES], kwin[:, :LANES]], axis=0)
    k2 = jnp.max(jnp.dot(kall * kall, ones, preferred_element_type=F32), axis=0, keepdims=True)
    k2 = k2[:, 0:1] * NORM_SLACK
    feat = lax.broadcasted_iota(jnp.int32, (LANES, n_pairs * SWA_QBLK), 0)
    first_rows = (feat & 32) == 0
    q_t, refs = [], []
    for blk in range(qb):
        qt = jnp.concatenate([q_ref[t * LANES:(t + 1) * LANES, blk * SWA_QBLK:(blk + 1) * SWA_QBLK]
                              for t in range(n_pairs)], axis=1)
        sq = qt.astype(F32)
        sq = sq * sq
        q2_first = jnp.sum(jnp.where(first_rows, sq, 0.0), axis=0, keepdims=True)
        q2_second = jnp.sum(sq, axis=0, keepdims=True) - q2_first
        q_t.append(qt)
        refs.append([jnp.sqrt(q2 * k2).astype(BF16) for q2 in (q2_first, q2_second)])
    ref_max = functools.reduce(jnp.maximum, [jnp.max(r.astype(F32)) for pr in refs for r in pr])
    fast = ref_max <= REF_LIMIT
    def set_queries(shifted):
        for blk in range(qb):
            for e in range(2):
                shift = -refs[blk][e] if shifted else jnp.zeros_like(refs[blk][e])
                qa_ref[blk, e] = jnp.where(feat == _shift_lane(e), shift, q_t[blk])

    def scores(blk, e):
        kcat = jnp.concatenate([kc[:, e * LANES:(e + 1) * LANES],
                                kwin[blk * SWA_QBLK:blk * SWA_QBLK + span,
                                     e * LANES:(e + 1) * LANES]], axis=0)
        return jnp.dot(kcat, qa_ref[blk, e], preferred_element_type=F32)

    def run(shifted):
        s_ref[0] = scores(0, 0)
        for blk in range(qb):
            rows = slice(blk * SWA_QBLK, (blk + 1) * SWA_QBLK)
            vt = jnp.concatenate([vc, vwin[:, blk * SWA_QBLK:blk * SWA_QBLK + span]], axis=1)
            halves = []
            for e in range(2):
                if e == 0:
                    s_ref[1] = scores(blk, 1)
                elif blk + 1 < qb:
                    s_ref[0] = scores(blk + 1, 0)
                s = s_ref[e] + bias
                sk = jnp.zeros(pair.shape, F32)
                for t in range(n_pairs):
                    sk = jnp.where(pair == t, sink_ref[kvh * SWA_GROUP + 2 * t + e] * LOG2E, sk)
                if shifted:
                    m = refs[blk][e].astype(F32)
                    p = jnp.exp2(s)
                else:
                    m = jnp.maximum(jnp.max(s, axis=0, keepdims=True), sk)
                    p = jnp.exp2(s - m)
                if blk == 0:
                    p = jnp.concatenate([p[:lc], p[lc:lc + SWA_WINDOW] * keep_prev,
                                         p[lc + SWA_WINDOW:]], axis=0)
                if blk == qb - 1:
                    p = jnp.concatenate([p[:lc + span - SWA_WINDOW],
                                         p[lc + span - SWA_WINDOW:] * keep_next], axis=0)
                den = jnp.sum(p, axis=0, keepdims=True) + jnp.exp2(sk - m)
                o = jnp.dot(vt, p.astype(BF16), preferred_element_type=F32)
                halves.append(o * (1.0 / den))
            both = jnp.concatenate(halves, axis=0)
            for t in range(n_pairs):
                tile = both[:, t * SWA_QBLK:(t + 1) * SWA_QBLK]
                o_ref[rows, t * LANES:(t + 1) * LANES] = tile.T.astype(BF16)

    set_queries(True)
    run(True)

    @pl.when(jnp.logical_not(fast))
    def _():
        set_queries(False)
        run(False)


def _swa_attn(sink, q, k2, vt, k2c, vtc, nb, seq, lc):
    t = q.shape[1]
    nblk = seq // SWA_QBLK
    qb = min(SWA_STEP_BLOCKS, nblk)
    nsteps = nblk // qb
    gq = SWA_GROUP * SWA_HEAD_DIM
    hd = SWA_HEAD_DIM
    prev_blk = lambda b, i: b * nblk + jnp.maximum(qb * i - 1, 0)
    next_blk = lambda b, i: b * nblk + jnp.minimum(qb * i + qb, nblk - 1)
    return pl.pallas_call(
        functools.partial(_swa_attn_kernel, qb=qb),
        grid=(nb, SWA_KV_HEADS, nsteps),
        in_specs=[pl.BlockSpec(memory_space=pltpu.SMEM),
                  pl.BlockSpec((gq, qb * SWA_QBLK), lambda b, h, i: (h, b * nsteps + i)),
                  pl.BlockSpec((lc, 2 * LANES), lambda b, h, i: (b, h)),
                  pl.BlockSpec((SWA_QBLK, 2 * LANES), lambda b, h, i: (prev_blk(b, i), h)),
                  pl.BlockSpec((qb * SWA_QBLK, 2 * LANES), lambda b, h, i: (b * nsteps + i, h)),
                  pl.BlockSpec((SWA_QBLK, 2 * LANES), lambda b, h, i: (next_blk(b, i), h)),
                  pl.BlockSpec((hd, lc), lambda b, h, i: (h, b)),
                  pl.BlockSpec((hd, SWA_QBLK), lambda b, h, i: (h, prev_blk(b, i))),
                  pl.BlockSpec((hd, qb * SWA_QBLK), lambda b, h, i: (h, b * nsteps + i)),
                  pl.BlockSpec((hd, SWA_QBLK), lambda b, h, i: (h, next_blk(b, i)))],
        out_specs=pl.BlockSpec((qb * SWA_QBLK, gq), lambda b, h, i: (b * nsteps + i, h)),
        out_shape=jax.ShapeDtypeStruct((t, SWA_HEADS * SWA_HEAD_DIM), BF16),
        scratch_shapes=[pltpu.VMEM((2, lc + SWA_QBLK + 2 * SWA_WINDOW, gq), F32),
                        pltpu.VMEM((qb, 2, LANES, gq), BF16)],
        compiler_params=_params(("arbitrary", "arbitrary", "arbitrary")),
        name="swa_attn",
    )(sink, q, k2c, k2, k2, k2, vtc, vt, vt, vt)


def _rope_tables(seq):
    rows = seq // GRID_W
    row = jnp.repeat(jnp.arange(rows, dtype=F32), GRID_W)
    col = jnp.tile(jnp.arange(GRID_W, dtype=F32), rows)
    n_freq = MLA_ROPE // 4
    freqs = ROPE_BASE ** (-jnp.arange(n_freq, dtype=F32) / n_freq)
    ang = jnp.concatenate([row[:, None] * freqs, col[:, None] * freqs], axis=-1)
    cos, sin = jnp.cos(ang), jnp.sin(ang)
    cos_t = jnp.concatenate([cos, cos, cos, cos], axis=-1)
    sin_t = jnp.concatenate([-sin, -sin, sin, sin], axis=-1)
    return cos_t, sin_t, cos_t.T, sin_t.T


def _pair_tiles(w, n_heads, half):
    k = w.shape[0]
    x1 = w[:, :, :half].reshape(k, n_heads // 2, 2 * half)
    x2 = w[:, :, half:].reshape(k, n_heads // 2, 2 * half)
    return jnp.concatenate([x1, x2], axis=2).reshape(k, n_heads * 2 * half)


def _mla_weights(w_in, g_qa, g_kva, w_qb, w_kvb):
    half = MLA_ROPE // 2
    lat = MLA_Q_LORA + MLA_KV_LORA
    k1, k2 = w_in[:, lat:lat + half], w_in[:, lat + half:]
    qb = w_qb.reshape(MLA_Q_LORA, MLA_HEADS, MLA_NOPE + MLA_ROPE)
    kvb = w_kvb.reshape(MLA_KV_LORA, MLA_HEADS, MLA_NOPE + MLA_V)
    return {
        "w_in": jnp.concatenate([w_in[:, :lat], k1, k1, k2, k2], axis=1).astype(BF16),
        "g_qa": g_qa.reshape(1, -1), "g_kva": g_kva.reshape(1, -1),
        "w_qn": qb[:, :, :MLA_NOPE].reshape(MLA_Q_LORA, -1).T.astype(BF16),
        "w_qr": _pair_tiles(qb[:, :, MLA_NOPE:], MLA_HEADS, half).T.astype(BF16),
        "w_kn": kvb[:, :, :MLA_NOPE].reshape(MLA_KV_LORA, -1).astype(BF16),
        "w_vt": kvb[:, :, MLA_NOPE:].reshape(MLA_KV_LORA, -1).T.astype(BF16),
    }


def _swa_weights(w_qkv):
    d = w_qkv.shape[0]
    half = SWA_HEAD_DIM // 2
    dq = SWA_HEADS * SWA_HEAD_DIM
    dkv = SWA_KV_HEADS * SWA_HEAD_DIM
    q = _pair_tiles(w_qkv[:, :dq].reshape(d, SWA_HEADS, SWA_HEAD_DIM), SWA_HEADS, half)
    k = w_qkv[:, dq:dq + dkv].reshape(d, SWA_KV_HEADS, SWA_HEAD_DIM)
    k1, k2 = k[:, :, :half], k[:, :, half:]
    k = jnp.concatenate([k1, k1, k2, k2], axis=2).reshape(d, SWA_KV_HEADS * LANES)
    return q.T.astype(BF16), k.astype(BF16), w_qkv[:, dq + dkv:].T.astype(BF16)


def kernel(x, c, ctx, c_ctx, w_mod, b_mod, g_norm, w_ff_in, w_ff_out, mla_w_in, mla_g_qa,
           mla_g_kva, mla_w_qb, mla_w_kvb, mla_w_out, swa_w_qkv, swa_sink, swa_w_out):
    nb, seq, d = x.shape
    lc = ctx.shape[1]
    depth = w_mod.shape[0]
    assert nb + 1 <= MOD_ROWS
    tm = min(TOKEN_TILE, seq)
    tq = min(ATTN_TQ, seq)
    n_per_batch = seq // tm
    grp_lat = lambda i: i // n_per_batch
    grp_ctx = lambda i: nb

    cmat = jnp.zeros((MOD_ROWS, d), F32).at[:nb].set(c).at[nb].set(c_ctx)
    mod_all = _modulation(cmat, w_mod, b_mod)
    rope_tabs = _rope_tables(seq)

    xl = x.reshape(nb * seq, d)
    xc = ctx.reshape(nb * lc, d)
    for i in range(depth):
        need_ctx = i < depth - 1
        mod = mod_all[i].reshape(MOD_ROWS, 1, 6 * d)
        g = g_norm[i].reshape(4, 1, d)
        j = i // 2
        if i % 2 == 0:
            w = _mla_weights(mla_w_in[j], mla_g_qa[j], mla_g_kva[j], mla_w_qb[j], mla_w_kvb[j])
            w_out = mla_w_out[j].astype(BF16)
            qn, qr, kn, kr, vt = _mla_proj(xl, mod, grp_lat, g[0], w, rope_tabs, tm, n_per_batch)
            qnc, qrc, knc, krc, vtc = _mla_proj(xc, mod, grp_ctx, g[0], w, None, lc, 1)
            ctx_kv = (knc.reshape(nb, lc, -1), krc.reshape(nb, lc, -1), vtc)
            lat_kv = (kn.reshape(nb, seq, -1), kr.reshape(nb, seq, -1), vt)
            o_l = _mla_attn(qn, qr, ctx_kv, lat_kv, nb, tq)
            o_c = _mla_attn(qnc, qrc, ctx_kv, None, nb, lc) if need_ctx else None
        else:
            w = _swa_weights(swa_w_qkv[j])
            w_out = swa_w_out[j].astype(BF16)
            q, k2, vt = _swa_proj(xl, mod, grp_lat, g[0], w, rope_tabs, tm, n_per_batch)
            qc, k2c, vtc = _swa_proj(xc, mod, grp_ctx, g[0], w, None, lc, 1)
            o_l = _swa_attn(swa_sink[j], q, k2, vt, k2c, vtc, nb, seq, lc)
            assert not need_ctx
            o_c = None
        w1 = _layer_to_bf16(w_ff_in, i)
        w2 = _layer_to_bf16(w_ff_out, i)
        xl = _outproj(o_l, w_out, xl, g[1], mod, grp_lat, tm)
        xl = _mlp(xl, g[2], g[3], mod, grp_lat, w1, w2, tm)
        if need_ctx:
            xc = _outproj(o_c, w_out, xc, g[1], mod, grp_ctx, lc)
            xc = _mlp(xc, g[2], g[3], mod, grp_ctx, w1, w2, lc)
    return xl.reshape(nb, seq, d)
```

```python
import functools
import math

import jax
import jax.numpy as jnp
from jax import lax
from jax.experimental import pallas as pl
from jax.experimental.pallas import tpu as pltpu

F32 = jnp.float32
BF16 = jnp.bfloat16

GRID_W = 64
ROPE_BASE = 10000.0
NORM_EPS = 1e-6
LOG2E = math.log2(math.e)
NEG_BIG = -1e30
REF_LIMIT = 1.0
NORM_SLACK = 1.03

MLA_HEADS = 16
MLA_Q_LORA = 512
MLA_KV_LORA = 512
MLA_NOPE = 128
MLA_ROPE = 64
MLA_V = 128

SWA_HEADS = 32
SWA_KV_HEADS = 4
SWA_HEAD_DIM = 64
SWA_WINDOW = 128
SWA_GROUP = SWA_HEADS // SWA_KV_HEADS
SWA_QBLK = 128
SWA_STEP_BLOCKS = 16

LANES = 128
HALF_TILE = 64

MOD_ROWS = 8
MOD_TN = 2048
TOKEN_TILE = 512
ROW_CHUNK = 32
ROW_UNROLL = 8
FF_TILE = 1024
ATTN_TQ = 512
KV_GROUP = 8
KV_UNROLL = 4
CAST_BLOCK_BYTES = 8 * 1024 * 1024
VMEM_LIMIT = 56 * 1024 * 1024

NT_DIMS = (((1,), (1,)), ((), ()))


def _rms(xf, g):
    ms = jnp.mean(xf * xf, axis=-1, keepdims=True)
    return xf * lax.rsqrt(ms + NORM_EPS) * g


def _inv_rms(xf):
    return lax.rsqrt(jnp.mean(xf * xf, axis=-1, keepdims=True) + NORM_EPS)


def _row_chunks(n_rows, fn):
    def body(i, carry):
        fn(pl.ds(pl.multiple_of(i * ROW_CHUNK, ROW_CHUNK), ROW_CHUNK))
        return carry
    lax.fori_loop(0, n_rows // ROW_CHUNK, body, 0, unroll=ROW_UNROLL)


def _modulated_norm(x_ref, g_ref, sc_ref, sh_ref, h_ref):
    gain = g_ref[...] * (1.0 + sc_ref[0])
    shift = sh_ref[0]

    def rows(rs):
        xf = x_ref[rs, :]
        h_ref[rs, :] = (xf * _inv_rms(xf) * gain + shift).astype(h_ref.dtype)
    _row_chunks(h_ref.shape[0], rows)


def _rot(tile, cos, sin):
    return tile * cos + pltpu.roll(tile, HALF_TILE, 1) * sin


def _first_head_lanes(shape):
    lane = lax.broadcasted_iota(jnp.int32, shape, 1)
    return (lane & 32) == 0


def _shift_lane(parity):
    return LANES - 1 - 32 * parity


def _params(sem):
    return pltpu.CompilerParams(dimension_semantics=sem, vmem_limit_bytes=VMEM_LIMIT)


def _const_spec(shape):
    nd = len(shape)
    return pl.BlockSpec(shape, lambda *_: (0,) * nd, pipeline_mode=pl.Buffered(1))


def _mod_spec(grp, which, d):
    return pl.BlockSpec((1, 1, d), lambda i, *_: (grp(i), 0, which))


def _mod_kernel(c_ref, w_ref, b_ref, o_ref):
    c = c_ref[...]
    a = c / (1.0 + jnp.exp(-c))
    o_ref[0] = jnp.dot(a, w_ref[0], preferred_element_type=F32,
                       precision=lax.Precision.HIGHEST) + b_ref[0]


def _modulation(cmat, w_mod, b_mod):
    depth, d, n = w_mod.shape
    return pl.pallas_call(
        _mod_kernel,
        grid=(depth, n // MOD_TN),
        in_specs=[pl.BlockSpec((MOD_ROWS, d), lambda l, j: (0, 0)),
                  pl.BlockSpec((1, d, MOD_TN), lambda l, j: (l, 0, j)),
                  pl.BlockSpec((1, 1, MOD_TN), lambda l, j: (l, 0, j))],
        out_specs=pl.BlockSpec((1, MOD_ROWS, MOD_TN), lambda l, j: (l, 0, j)),
        out_shape=jax.ShapeDtypeStruct((depth, MOD_ROWS, n), F32),
        compiler_params=_params(("arbitrary", "arbitrary")),
        name="modulation",
    )(cmat, w_mod, b_mod.reshape(depth, 1, n))


def _cast_kernel(w_ref, o_ref):
    o_ref[...] = w_ref[0].astype(o_ref.dtype)


def _layer_to_bf16(w_stack, layer):
    _, r, c = w_stack.shape
    br = max(8, min(r, CAST_BLOCK_BYTES // (4 * c)))
    assert r % br == 0
    return pl.pallas_call(
        _cast_kernel,
        grid=(r // br,),
        in_specs=[pl.BlockSpec((1, br, c), lambda j: (layer, j, 0))],
        out_specs=pl.BlockSpec((br, c), lambda j: (j, 0)),
        out_shape=jax.ShapeDtypeStruct((r, c), BF16),
        compiler_params=_params(("arbitrary",)),
        name="cast_bf16",
    )(w_stack)


def _mla_proj_kernel(*refs, rope, qscale):
    if rope:
        (x_ref, g_ref, sc_ref, sh_ref, win_ref, gqa_ref, gkva_ref, wqn_ref, wqr_ref, wkn_ref,
         wvt_ref, cos_ref, sin_ref, cost_ref, sint_ref, qn_ref, qr_ref, kn_ref, kr_ref,
         vt_ref) = refs
        cos, sin = cos_ref[...], sin_ref[...]
    else:
        (x_ref, g_ref, sc_ref, sh_ref, win_ref, gqa_ref, gkva_ref, wqn_ref, wqr_ref, wkn_ref,
         wvt_ref, qn_ref, qr_ref, kn_ref, kr_ref, vt_ref) = refs
    h = (_rms(x_ref[...], g_ref[...]) * (1.0 + sc_ref[0]) + sh_ref[0]).astype(BF16)
    p = jnp.dot(h, win_ref[...], preferred_element_type=F32)
    qa = _rms(p[:, :MLA_Q_LORA], gqa_ref[...]).astype(BF16)
    ckv = _rms(p[:, MLA_Q_LORA:MLA_Q_LORA + MLA_KV_LORA], gkva_ref[...]).astype(BF16)
    kr = p[:, MLA_Q_LORA + MLA_KV_LORA:]

    qn = lax.dot_general(wqn_ref[...], qa, NT_DIMS, preferred_element_type=F32) * qscale
    qn_ref[...] = qn.astype(BF16)
    qr = lax.dot_general(wqr_ref[...], qa, NT_DIMS, preferred_element_type=F32) * qscale
    for t in range(qr.shape[0] // LANES):
        tile = qr[t * LANES:(t + 1) * LANES, :]
        if rope:
            swapped = jnp.concatenate([tile[HALF_TILE:], tile[:HALF_TILE]], axis=0)
            tile = tile * cost_ref[...] + swapped * sint_ref[...]
        qr_ref[t * LANES:(t + 1) * LANES, :] = tile.astype(BF16)

    if rope:
        kr = _rot(kr, cos, sin)
    first = _first_head_lanes(kr.shape)
    lane = lax.broadcasted_iota(jnp.int32, kr.shape, 1)
    kr_ref[:, :LANES] = jnp.where(lane == _shift_lane(0), 1.0,
                                  jnp.where(first, kr, 0.0)).astype(BF16)
    kr_ref[:, LANES:] = jnp.where(lane == _shift_lane(1), 1.0,
                                  jnp.where(first, 0.0, kr)).astype(BF16)

    kn_ref[...] = jnp.dot(ckv, wkn_ref[...], preferred_element_type=F32).astype(BF16)
    vt = lax.dot_general(wvt_ref[...], ckv, NT_DIMS, preferred_element_type=F32).astype(BF16)
    for hd in range(MLA_HEADS):
        vt_ref[0, hd, 0] = vt[hd * MLA_V:(hd + 1) * MLA_V, :]


def _mla_proj(x, mod, grp, g0, w, rope_tabs, tm, n_per_batch):
    t, d = x.shape
    n_tiles = t // tm
    nb = n_tiles // n_per_batch
    rope = rope_tabs is not None
    qscale = (MLA_NOPE + MLA_ROPE) ** -0.5 * LOG2E
    row = lambda i: (i, 0)
    in_specs = [pl.BlockSpec((tm, d), row), _const_spec((1, d)),
                _mod_spec(grp, 1, d), _mod_spec(grp, 0, d),
                _const_spec(w["w_in"].shape), _const_spec((1, MLA_Q_LORA)),
                _const_spec((1, MLA_KV_LORA)), _const_spec(w["w_qn"].shape),
                _const_spec(w["w_qr"].shape), _const_spec(w["w_kn"].shape),
                _const_spec(w["w_vt"].shape)]
    args = [x, g0, mod, mod, w["w_in"], w["g_qa"], w["g_kva"], w["w_qn"], w["w_qr"], w["w_kn"],
            w["w_vt"]]
    if rope:
        pos = lambda i: (i % n_per_batch, 0)
        pos_t = lambda i: (0, i % n_per_batch)
        in_specs += [pl.BlockSpec((tm, LANES), pos), pl.BlockSpec((tm, LANES), pos),
                     pl.BlockSpec((LANES, tm), pos_t), pl.BlockSpec((LANES, tm), pos_t)]
        args += list(rope_tabs)
    hn = MLA_HEADS * MLA_NOPE
    hr = MLA_HEADS * MLA_ROPE
    col = lambda i: (0, i)
    out_shape = [jax.ShapeDtypeStruct((hn, t), BF16), jax.ShapeDtypeStruct((hr, t), BF16),
                 jax.ShapeDtypeStruct((t, hn), BF16), jax.ShapeDtypeStruct((t, 2 * LANES), BF16),
                 jax.ShapeDtypeStruct((nb, MLA_HEADS, n_per_batch, MLA_V, tm), BF16)]
    out_specs = [pl.BlockSpec((hn, tm), col), pl.BlockSpec((hr, tm), col),
                 pl.BlockSpec((tm, hn), row), pl.BlockSpec((tm, 2 * LANES), row),
                 pl.BlockSpec((1, MLA_HEADS, 1, MLA_V, tm),
                              lambda i: (i // n_per_batch, 0, i % n_per_batch, 0, 0))]
    return pl.pallas_call(
        functools.partial(_mla_proj_kernel, rope=rope, qscale=qscale),
        grid=(n_tiles,), in_specs=in_specs, out_specs=out_specs, out_shape=out_shape,
        compiler_params=_params(("arbitrary",)),
        name="mla_proj_lat" if rope else "mla_proj_ctx",
    )(*args)


def _kv_group_sizes(n_chunks):
    if n_chunks <= 2:
        return [n_chunks] if n_chunks else []
    tail = [min(KV_GROUP, n_chunks) // 2] * 2
    body = n_chunks - sum(tail)
    assert body % KV_GROUP == 0
    return [KV_GROUP] * (body // KV_GROUP) + tail


def _mla_attn_kernel(*refs, n_lat, tk):
    if n_lat:
        (qn_ref, qr_ref, knc_ref, krc_ref, vtc_ref, knl_ref, krl_ref, vtl_ref, o_ref,
         acc_ref, s_ref, qa_ref, k2_ref) = refs
    else:
        qn_ref, qr_ref, knc_ref, krc_ref, vtc_ref, o_ref, acc_ref, s_ref, qa_ref, k2_ref = refs
    tq = qn_ref.shape[1]
    lc = knc_ref.shape[1]
    parity = pl.program_id(1) % 2

    def lat_keys(j):
        off = pl.multiple_of(j * tk, tk)
        return knl_ref[0, pl.ds(off, tk), :], krl_ref[0, pl.ds(off, tk), :]

    @pl.when(pl.program_id(2) == 0)
    def _():
        ones = jnp.ones((LANES, LANES), BF16)

        def sqnorm_max(kn, kr):
            r = jnp.dot(kn * kn + kr * kr, ones, preferred_element_type=F32)
            return jnp.max(r, axis=0, keepdims=True)
        mx = sqnorm_max(knc_ref[0], krc_ref[0])
        if n_lat:
            mx = lax.fori_loop(0, n_lat, lambda j, c: jnp.maximum(c, sqnorm_max(*lat_keys(j))), mx,
                               unroll=KV_UNROLL)
        k2_ref[...] = jnp.broadcast_to(mx * NORM_SLACK, k2_ref.shape)

    qn = qn_ref[...]
    qr = qr_ref[...]
    feat = lax.broadcasted_iota(jnp.int32, qr.shape, 0)
    own = ((feat >> 5) & 1) == parity
    qnf, qrf = qn.astype(F32), jnp.where(own, qr.astype(F32), 0.0)
    q2 = jnp.sum(qnf * qnf + qrf * qrf, axis=0, keepdims=True)
    ref = jnp.sqrt(q2 * k2_ref[0:1, 0:1])
    fast = jnp.max(ref) <= REF_LIMIT
    shift = jnp.where(fast, -ref, 0.0).astype(BF16)
    qa_ref[:LANES, :] = qn
    qa_ref[LANES:, :] = jnp.where(feat == LANES - 1 - 32 * parity, shift, qr)

    def scores(kn, kr):
        return jnp.dot(jnp.concatenate([kn, kr], axis=1), qa_ref[...],
                       preferred_element_type=F32)

    def finish(acc, l):
        o_ref[...] = (acc * (1.0 / l)).T.astype(BF16)

    def shifted_softmax():
        def chunk(s, vt, l8):
            p = jnp.exp2(s)
            l8 = l8 + jnp.sum(p.reshape(-1, 8, tq), axis=0)
            return l8, jnp.dot(vt, p.astype(BF16), preferred_element_type=F32)

        sizes = _kv_group_sizes(n_lat)
        starts = [sum(sizes[:g]) for g in range(len(sizes))]
        last = max(len(sizes) - 1, 0)

        def n_rows(g):
            return (sizes[g] * tk if sizes else 0) + (lc if g == last else 0)

        def fill_scores(g):
            view = s_ref.at[g % 2]
            lat = sizes[g] * tk if sizes else 0
            if lat:
                rows = pl.ds(starts[g] * tk, lat)
                view[pl.ds(0, lat), :] = scores(knl_ref[0, rows, :], krl_ref[0, rows, :])
            if g == last:
                view[pl.ds(lat, lc), :] = scores(knc_ref[0], krc_ref[0])

        fill_scores(0)
        l8 = jnp.zeros((8, tq), F32)
        for g in range(last + 1):
            if g < last:
                fill_scores(g + 1)
            vts = [vtl_ref[0, 0, starts[g] + u] for u in range(sizes[g])] if sizes else []
            if g == last:
                vts.append(vtc_ref[0, 0, 0])
            l8, pv = chunk(s_ref[g % 2, pl.ds(0, n_rows(g)), :], jnp.concatenate(vts, axis=1), l8)
            acc = pv if g == 0 else acc_ref[...] + pv
            if g < last:
                acc_ref[...] = acc
        finish(acc, jnp.sum(l8, axis=0, keepdims=True))

    def online_softmax():
        def update(s_view, vt, m, l):
            m_new = jnp.maximum(m, jnp.max(s_view[...], axis=0, keepdims=True))
            alpha = jnp.exp2(m - m_new)
            p = jnp.exp2(s_view[...] - m_new)
            l_new = alpha * l + jnp.sum(p, axis=0, keepdims=True)
            acc_ref[...] = acc_ref[...] * alpha + jnp.dot(vt, p.astype(BF16),
                                                          preferred_element_type=F32)
            return m_new, l_new

        acc_ref[...] = jnp.zeros_like(acc_ref)
        m = jnp.full((1, tq), NEG_BIG, F32)
        l = jnp.zeros((1, tq), F32)
        slots = [s_ref.at[0, pl.ds(0, tk)], s_ref.at[1, pl.ds(0, tk)]] if n_lat else None
        if n_lat:
            slots[0][...] = scores(*lat_keys(0))
        ctx_view = s_ref.at[1, pl.ds(0, lc)]
        ctx_view[...] = scores(knc_ref[0], krc_ref[0])
        m, l = update(ctx_view, vtc_ref[0, 0, 0], m, l)
        if n_lat:
            def body(jj, carry):
                for u in range(KV_UNROLL):
                    j = KV_UNROLL * jj + u
                    slots[(u + 1) % 2][...] = scores(*lat_keys(jnp.minimum(j + 1, n_lat - 1)))
                    carry = update(slots[u % 2], vtl_ref[0, 0, j], *carry)
                return carry
            m, l = lax.fori_loop(0, n_lat // KV_UNROLL, body, (m, l))
        finish(acc_ref[...], l)

    lax.cond(fast, shifted_softmax, online_softmax)


def _mla_attn(qn, qr, ctx_kv, lat_kv, nb, tq):
    t = qn.shape[1]
    nq = t // nb // tq
    knc, krc, vtc = ctx_kv
    lc = knc.shape[1]
    qrow = lambda b, h, i: (b * nq + i, h)
    in_specs = [pl.BlockSpec((LANES, tq), lambda b, h, i: (h, b * nq + i)),
                pl.BlockSpec((LANES, tq), lambda b, h, i: (h // 2, b * nq + i)),
                pl.BlockSpec((1, lc, LANES), lambda b, h, i: (b, 0, h)),
                pl.BlockSpec((1, lc, LANES), lambda b, h, i: (b, 0, h % 2)),
                pl.BlockSpec((1, 1, 1, MLA_V, lc), lambda b, h, i: (b, h, 0, 0, 0))]
    args = [qn, qr, knc, krc, vtc]
    n_lat, tk = 0, 0
    if lat_kv is not None:
        knl, krl, vtl = lat_kv
        s = knl.shape[1]
        n_lat, tk = vtl.shape[2], vtl.shape[4]
        assert n_lat % KV_UNROLL == 0 and tk >= lc
        in_specs += [pl.BlockSpec((1, s, LANES), lambda b, h, i: (b, 0, h)),
                     pl.BlockSpec((1, s, LANES), lambda b, h, i: (b, 0, h % 2)),
                     pl.BlockSpec((1, 1, n_lat, MLA_V, tk), lambda b, h, i: (b, h, 0, 0, 0))]
        args += [knl, krl, vtl]
    scratch = [pltpu.VMEM((MLA_V, tq), F32), pltpu.VMEM((2, max(KV_GROUP * tk, lc), tq), F32),
               pltpu.VMEM((2 * LANES, tq), BF16), pltpu.VMEM((8, LANES), F32)]
    return pl.pallas_call(
        functools.partial(_mla_attn_kernel, n_lat=n_lat, tk=tk),
        grid=(nb, MLA_HEADS, nq), in_specs=in_specs,
        out_specs=pl.BlockSpec((tq, MLA_V), qrow),
        out_shape=jax.ShapeDtypeStruct((t, MLA_HEADS * MLA_V), BF16),
        scratch_shapes=scratch,
        compiler_params=_params(("arbitrary", "arbitrary", "arbitrary")),
        name="mla_attn_lat" if n_lat else "mla_attn_ctx",
    )(*args)


def _outproj_kernel(o_ref, w_ref, x_ref, g_ref, gt_ref, out_ref):
    y = jnp.dot(o_ref[...], w_ref[...], preferred_element_type=F32)
    out_ref[...] = x_ref[...] + gt_ref[0] * _rms(y, g_ref[...])


def _outproj(o, w_out, x, g1, mod, grp, tm):
    t, d = x.shape
    row = lambda i: (i, 0)
    return pl.pallas_call(
        _outproj_kernel,
        grid=(t // tm,),
        in_specs=[pl.BlockSpec((tm, o.shape[1]), row), _const_spec(w_out.shape),
                  pl.BlockSpec((tm, d), row), _const_spec((1, d)), _mod_spec(grp, 2, d)],
        out_specs=pl.BlockSpec((tm, d), row),
        out_shape=jax.ShapeDtypeStruct((t, d), F32),
        compiler_params=_params(("arbitrary",)),
        name="outproj",
    )(o, w_out, x, g1, mod)


def _mlp_kernel(x_ref, g2_ref, sc_ref, sh_ref, w1_ref, w2_ref, g3_ref, gt_ref, out_ref,
                f_ref, acc_ref):
    k = pl.program_id(1)

    def ff_chunk():
        u = jnp.maximum(jnp.dot(f_ref[...], w1_ref[...], preferred_element_type=F32), 0.0)
        return jnp.dot((u * u).astype(BF16), w2_ref[...], preferred_element_type=F32)

    @pl.when(k == 0)
    def _():
        _modulated_norm(x_ref, g2_ref, sc_ref, sh_ref, f_ref)
        acc_ref[...] = ff_chunk()

    @pl.when(k > 0)
    def _():
        acc_ref[...] += ff_chunk()

    @pl.when(k == pl.num_programs(1) - 1)
    def _():
        gain = gt_ref[0] * g3_ref[...]

        def rows(rs):
            y = acc_ref[rs, :]
            out_ref[rs, :] = x_ref[rs, :] + y * _inv_rms(y) * gain
        _row_chunks(acc_ref.shape[0], rows)


def _mlp(x, g2, g3, mod, grp, w1, w2, tm):
    t, d = x.shape
    dff = w1.shape[1]
    row = lambda i, k: (i, 0)
    return pl.pallas_call(
        _mlp_kernel,
        grid=(t // tm, dff // FF_TILE),
        in_specs=[pl.BlockSpec((tm, d), row), _const_spec((1, d)),
                  _mod_spec(grp, 4, d), _mod_spec(grp, 3, d),
                  pl.BlockSpec((d, FF_TILE), lambda i, k: (0, k)),
                  pl.BlockSpec((FF_TILE, d), lambda i, k: (k, 0)),
                  _const_spec((1, d)), _mod_spec(grp, 5, d)],
        out_specs=pl.BlockSpec((tm, d), row),
        out_shape=jax.ShapeDtypeStruct((t, d), F32),
        scratch_shapes=[pltpu.VMEM((tm, d), BF16), pltpu.VMEM((tm, d), F32)],
        compiler_params=_params(("arbitrary", "arbitrary")),
        name="mlp",
    )(x, g2, mod, mod, w1, w2, g3, mod)


def _swa_proj_kernel(*refs, rope, qscale):
    if rope:
        (x_ref, g_ref, sc_ref, sh_ref, wqt_ref, wk_ref, wvt_ref, cos_ref, sin_ref, cost_ref,
         sint_ref, q_ref, k2_ref, vt_ref) = refs
        cos, sin = cos_ref[...], sin_ref[...]
    else:
        x_ref, g_ref, sc_ref, sh_ref, wqt_ref, wk_ref, wvt_ref, q_ref, k2_ref, vt_ref = refs
    h = (_rms(x_ref[...], g_ref[...]) * (1.0 + sc_ref[0]) + sh_ref[0]).astype(BF16)
    qt = lax.dot_general(wqt_ref[...], h, NT_DIMS, preferred_element_type=F32) * qscale
    for t in range(qt.shape[0] // LANES):
        tile = qt[t * LANES:(t + 1) * LANES, :]
        if rope:
            swapped = jnp.concatenate([tile[HALF_TILE:], tile[:HALF_TILE]], axis=0)
            tile = tile * cost_ref[...] + swapped * sint_ref[...]
        q_ref[t * LANES:(t + 1) * LANES, :] = tile.astype(BF16)
    k = jnp.dot(h, wk_ref[...], preferred_element_type=F32)
    for c in range(SWA_KV_HEADS):
        tile = k[:, c * LANES:(c + 1) * LANES]
        if rope:
            tile = _rot(tile, cos, sin)
        first = _first_head_lanes(tile.shape)
        lane = lax.broadcasted_iota(jnp.int32, tile.shape, 1)
        k2_ref[:, 2 * c * LANES:(2 * c + 1) * LANES] = jnp.where(
            lane == _shift_lane(0), 1.0, jnp.where(first, tile, 0.0)).astype(BF16)
        k2_ref[:, (2 * c + 1) * LANES:(2 * c + 2) * LANES] = jnp.where(
            lane == _shift_lane(1), 1.0, jnp.where(first, 0.0, tile)).astype(BF16)
    vt_ref[...] = lax.dot_general(wvt_ref[...], h, NT_DIMS,
                                  preferred_element_type=F32).astype(BF16)


def _swa_proj(x, mod, grp, g0, w, rope_tabs, tm, n_per_batch):
    t, d = x.shape
    rope = rope_tabs is not None
    row = lambda i: (i, 0)
    dq = SWA_HEADS * SWA_HEAD_DIM
    dkv = SWA_KV_HEADS * SWA_HEAD_DIM
    w_qt, w_k, w_vt = w
    col = lambda i: (0, i)
    in_specs = [pl.BlockSpec((tm, d), row), _const_spec((1, d)),
                _mod_spec(grp, 1, d), _mod_spec(grp, 0, d), _const_spec(w_qt.shape),
                _const_spec(w_k.shape), _const_spec(w_vt.shape)]
    args = [x, g0, mod, mod, w_qt, w_k, w_vt]
    if rope:
        pos = lambda i: (i % n_per_batch, 0)
        pos_t = lambda i: (0, i % n_per_batch)
        in_specs += [pl.BlockSpec((tm, LANES), pos), pl.BlockSpec((tm, LANES), pos),
                     pl.BlockSpec((LANES, tm), pos_t), pl.BlockSpec((LANES, tm), pos_t)]
        args += list(rope_tabs)
    out_shape = [jax.ShapeDtypeStruct((dq, t), BF16),
                 jax.ShapeDtypeStruct((t, 2 * SWA_KV_HEADS * LANES), BF16),
                 jax.ShapeDtypeStruct((dkv, t), BF16)]
    out_specs = [pl.BlockSpec((dq, tm), col), pl.BlockSpec((tm, 2 * SWA_KV_HEADS * LANES), row),
                 pl.BlockSpec((dkv, tm), col)]
    return pl.pallas_call(
        functools.partial(_swa_proj_kernel, rope=rope, qscale=SWA_HEAD_DIM ** -0.5 * LOG2E),
        grid=(t // tm,), in_specs=in_specs, out_specs=out_specs, out_shape=out_shape,
        compiler_params=_params(("arbitrary",)),
        name="swa_proj_lat" if rope else "swa_proj_ctx",
    )(*args)


def _swa_attn_kernel(sink_ref, q_ref, kc_ref, kp_ref, kcur_ref, kn_ref, vc_ref, vp_ref, vcur_ref,
                     vn_ref, o_ref, s_ref, qa_ref, *, qb):
    kvh = pl.program_id(1)
    i = pl.program_id(2)
    lc = kc_ref.shape[0]
    span = SWA_QBLK + 2 * SWA_WINDOW
    n_pairs = SWA_GROUP // 2
    kc, vc = kc_ref[...], vc_ref[...]
    kwin = jnp.concatenate([kp_ref[...], kcur_ref[...], kn_ref[...]], axis=0)
    vwin = jnp.concatenate([vp_ref[...], vcur_ref[...], vn_ref[...]], axis=1)
    r = lax.broadcasted_iota(jnp.int32, (lc + span, SWA_QBLK), 0)
    rel = r - lc - lax.broadcasted_iota(jnp.int32, (lc + span, SWA_QBLK), 1)
    bias = jnp.where((r < lc) | ((rel >= 0) & (rel <= 2 * SWA_WINDOW)), 0.0, NEG_BIG)
    bias = jnp.concatenate([bias] * n_pairs, axis=1)
    pair = lax.broadcasted_iota(jnp.int32, (1, n_pairs * SWA_QBLK), 1) // SWA_QBLK
    keep_prev = jnp.where(i == 0, 0.0, 1.0)
    keep_next = jnp.where(i == pl.num_programs(2) - 1, 0.0, 1.0)

    ones = jnp.ones((LANES, LANES), BF16)
    kall = jnp.concatenate([kc[:, :LANES], kwin[:, :LANES]], axis=0)
    k2 = jnp.max(jnp.dot(kall * kall, ones, preferred_element_type=F32), axis=0, keepdims=True)
    k2 = k2[:, 0:1] * NORM_SLACK
    feat = lax.broadcasted_iota(jnp.int32, (LANES, n_pairs * SWA_QBLK), 0)
    first_rows = (feat & 32) == 0
    q_t, refs = [], []
    for blk in range(qb):
        qt = jnp.concatenate([q_ref[t * LANES:(t + 1) * LANES, blk * SWA_QBLK:(blk + 1) * SWA_QBLK]
                              for t in range(n_pairs)], axis=1)
        sq = qt.astype(F32)
        sq = sq * sq
        q2_first = jnp.sum(jnp.where(first_rows, sq, 0.0), axis=0, keepdims=True)
        q2_second = jnp.sum(sq, axis=0, keepdims=True) - q2_first
        q_t.append(qt)
        refs.append([jnp.sqrt(q2 * k2).astype(BF16) for q2 in (q2_first, q2_second)])
    ref_max = functools.reduce(jnp.maximum, [jnp.max(r.astype(F32)) for pr in refs for r in pr])
    fast = ref_max <= REF_LIMIT
    def set_queries(shifted):
        for blk in range(qb):
            for e in range(2):
                shift = -refs[blk][e] if shifted else jnp.zeros_like(refs[blk][e])
                qa_ref[blk, e] = jnp.where(feat == _shift_lane(e), shift, q_t[blk])

    def scores(blk, e):
        kcat = jnp.concatenate([kc[:, e * LANES:(e + 1) * LANES],
                                kwin[blk * SWA_QBLK:blk * SWA_QBLK + span,
                                     e * LANES:(e + 1) * LANES]], axis=0)
        return jnp.dot(kcat, qa_ref[blk, e], preferred_element_type=F32)

    def run(shifted):
        s_ref[0] = scores(0, 0)
        for blk in range(qb):
            rows = slice(blk * SWA_QBLK, (blk + 1) * SWA_QBLK)
            vt = jnp.concatenate([vc, vwin[:, blk * SWA_QBLK:blk * SWA_QBLK + span]], axis=1)
            halves = []
            for e in range(2):
                if e == 0:
                    s_ref[1] = scores(blk, 1)
                elif blk + 1 < qb:
                    s_ref[0] = scores(blk + 1, 0)
                s = s_ref[e] + bias
                sk = jnp.zeros(pair.shape, F32)
                for t in range(n_pairs):
                    sk = jnp.where(pair == t, sink_ref[kvh * SWA_GROUP + 2 * t + e] * LOG2E, sk)
                if shifted:
                    m = refs[blk][e].astype(F32)
                    p = jnp.exp2(s)
                else:
                    m = jnp.maximum(jnp.max(s, axis=0, keepdims=True), sk)
                    p = jnp.exp2(s - m)
                if blk == 0:
                    p = jnp.concatenate([p[:lc], p[lc:lc + SWA_WINDOW] * keep_prev,
                                         p[lc + SWA_WINDOW:]], axis=0)
                if blk == qb - 1:
                    p = jnp.concatenate([p[:lc + span - SWA_WINDOW],
                                         p[lc + span - SWA_WINDOW:] * keep_next], axis=0)
                den = jnp.sum(p, axis=0, keepdims=True) + jnp.exp2(sk - m)
                o = jnp.dot(vt, p.astype(BF16), preferred_element_type=F32)
                halves.append(o * (1.0 / den))
            both = jnp.concatenate(halves, axis=0)
            for t in range(n_pairs):
                tile = both[:, t * SWA_QBLK:(t + 1) * SWA_QBLK]
                o_ref[rows, t * LANES:(t + 1) * LANES] = tile.T.astype(BF16)

    set_queries(True)
    run(True)

    @pl.when(jnp.logical_not(fast))
    def _():
        set_queries(False)
        run(False)


def _swa_attn(sink, q, k2, vt, k2c, vtc, nb, seq, lc):
    t = q.shape[1]
    nblk = seq // SWA_QBLK
    qb = min(SWA_STEP_BLOCKS, nblk)
    nsteps = nblk // qb
    gq = SWA_GROUP * SWA_HEAD_DIM
    hd = SWA_HEAD_DIM
    prev_blk = lambda b, i: b * nblk + jnp.maximum(qb * i - 1, 0)
    next_blk = lambda b, i: b * nblk + jnp.minimum(qb * i + qb, nblk - 1)
    return pl.pallas_call(
        functools.partial(_swa_attn_kernel, qb=qb),
        grid=(nb, SWA_KV_HEADS, nsteps),
        in_specs=[pl.BlockSpec(memory_space=pltpu.SMEM),
                  pl.BlockSpec((gq, qb * SWA_QBLK), lambda b, h, i: (h, b * nsteps + i)),
                  pl.BlockSpec((lc, 2 * LANES), lambda b, h, i: (b, h)),
                  pl.BlockSpec((SWA_QBLK, 2 * LANES), lambda b, h, i: (prev_blk(b, i), h)),
                  pl.BlockSpec((qb * SWA_QBLK, 2 * LANES), lambda b, h, i: (b * nsteps + i, h)),
                  pl.BlockSpec((SWA_QBLK, 2 * LANES), lambda b, h, i: (next_blk(b, i), h)),
                  pl.BlockSpec((hd, lc), lambda b, h, i: (h, b)),
                  pl.BlockSpec((hd, SWA_QBLK), lambda b, h, i: (h, prev_blk(b, i))),
                  pl.BlockSpec((hd, qb * SWA_QBLK), lambda b, h, i: (h, b * nsteps + i)),
                  pl.BlockSpec((hd, SWA_QBLK), lambda b, h, i: (h, next_blk(b, i)))],
        out_specs=pl.BlockSpec((qb * SWA_QBLK, gq), lambda b, h, i: (b * nsteps + i, h)),
        out_shape=jax.ShapeDtypeStruct((t, SWA_HEADS * SWA_HEAD_DIM), BF16),
        scratch_shapes=[pltpu.VMEM((2, lc + SWA_QBLK + 2 * SWA_WINDOW, gq), F32),
                        pltpu.VMEM((qb, 2, LANES, gq), BF16)],
        compiler_params=_params(("arbitrary", "arbitrary", "arbitrary")),
        name="swa_attn",
    )(sink, q, k2c, k2, k2, k2, vtc, vt, vt, vt)


def _rope_tables(seq):
    rows = seq // GRID_W
    row = jnp.repeat(jnp.arange(rows, dtype=F32), GRID_W)
    col = jnp.tile(jnp.arange(GRID_W, dtype=F32), rows)
    n_freq = MLA_ROPE // 4
    freqs = ROPE_BASE ** (-jnp.arange(n_freq, dtype=F32) / n_freq)
    ang = jnp.concatenate([row[:, None] * freqs, col[:, None] * freqs], axis=-1)
    cos, sin = jnp.cos(ang), jnp.sin(ang)
    cos_t = jnp.concatenate([cos, cos, cos, cos], axis=-1)
    sin_t = jnp.concatenate([-sin, -sin, sin, sin], axis=-1)
    return cos_t, sin_t, cos_t.T, sin_t.T


def _pair_tiles(w, n_heads, half):
    k = w.shape[0]
    x1 = w[:, :, :half].reshape(k, n_heads // 2, 2 * half)
    x2 = w[:, :, half:].reshape(k, n_heads // 2, 2 * half)
    return jnp.concatenate([x1, x2], axis=2).reshape(k, n_heads * 2 * half)


def _mla_weights(w_in, g_qa, g_kva, w_qb, w_kvb):
    half = MLA_ROPE // 2
    lat = MLA_Q_LORA + MLA_KV_LORA
    k1, k2 = w_in[:, lat:lat + half], w_in[:, lat + half:]
    qb = w_qb.reshape(MLA_Q_LORA, MLA_HEADS, MLA_NOPE + MLA_ROPE)
    kvb = w_kvb.reshape(MLA_KV_LORA, MLA_HEADS, MLA_NOPE + MLA_V)
    return {
        "w_in": jnp.concatenate([w_in[:, :lat], k1, k1, k2, k2], axis=1).astype(BF16),
        "g_qa": g_qa.reshape(1, -1), "g_kva": g_kva.reshape(1, -1),
        "w_qn": qb[:, :, :MLA_NOPE].reshape(MLA_Q_LORA, -1).T.astype(BF16),
        "w_qr": _pair_tiles(qb[:, :, MLA_NOPE:], MLA_HEADS, half).T.astype(BF16),
        "w_kn": kvb[:, :, :MLA_NOPE].reshape(MLA_KV_LORA, -1).astype(BF16),
        "w_vt": kvb[:, :, MLA_NOPE:].reshape(MLA_KV_LORA, -1).T.astype(BF16),
    }


def _swa_weights(w_qkv):
    d = w_qkv.shape[0]
    half = SWA_HEAD_DIM // 2
    dq = SWA_HEADS * SWA_HEAD_DIM
    dkv = SWA_KV_HEADS * SWA_HEAD_DIM
    q = _pair_tiles(w_qkv[:, :dq].reshape(d, SWA_HEADS, SWA_HEAD_DIM), SWA_HEADS, half)
    k = w_qkv[:, dq:dq + dkv].reshape(d, SWA_KV_HEADS, SWA_HEAD_DIM)
    k1, k2 = k[:, :, :half], k[:, :, half:]
    k = jnp.concatenate([k1, k1, k2, k2], axis=2).reshape(d, SWA_KV_HEADS * LANES)
    return q.T.astype(BF16), k.astype(BF16), w_qkv[:, dq + dkv:].T.astype(BF16)


def kernel(x, c, ctx, c_ctx, w_mod, b_mod, g_norm, w_ff_in, w_ff_out, mla_w_in, mla_g_qa,
           mla_g_kva, mla_w_qb, mla_w_kvb, mla_w_out, swa_w_qkv, swa_sink, swa_w_out):
    nb, seq, d = x.shape
    lc = ctx.shape[1]
    depth = w_mod.shape[0]
    assert nb + 1 <= MOD_ROWS
    tm = min(TOKEN_TILE, seq)
    tq = min(ATTN_TQ, seq)
    n_per_batch = seq // tm
    grp_lat = lambda i: i // n_per_batch
    grp_ctx = lambda i: nb

    cmat = jnp.zeros((MOD_ROWS, d), F32).at[:nb].set(c).at[nb].set(c_ctx)
    mod_all = _modulation(cmat, w_mod, b_mod)
    rope_tabs = _rope_tables(seq)

    xl = x.reshape(nb * seq, d)
    xc = ctx.reshape(nb * lc, d)
    for i in range(depth):
        need_ctx = i < depth - 1
        mod = mod_all[i].reshape(MOD_ROWS, 1, 6 * d)
        g = g_norm[i].reshape(4, 1, d)
        j = i // 2
        if i % 2 == 0:
            w = _mla_weights(mla_w_in[j], mla_g_qa[j], mla_g_kva[j], mla_w_qb[j], mla_w_kvb[j])
            w_out = mla_w_out[j].astype(BF16)
            qn, qr, kn, kr, vt = _mla_proj(xl, mod, grp_lat, g[0], w, rope_tabs, tm, n_per_batch)
            qnc, qrc, knc, krc, vtc = _mla_proj(xc, mod, grp_ctx, g[0], w, None, lc, 1)
            ctx_kv = (knc.reshape(nb, lc, -1), krc.reshape(nb, lc, -1), vtc)
            lat_kv = (kn.reshape(nb, seq, -1), kr.reshape(nb, seq, -1), vt)
            o_l = _mla_attn(qn, qr, ctx_kv, lat_kv, nb, tq)
            o_c = _mla_attn(qnc, qrc, ctx_kv, None, nb, lc) if need_ctx else None
        else:
            w = _swa_weights(swa_w_qkv[j])
            w_out = swa_w_out[j].astype(BF16)
            q, k2, vt = _swa_proj(xl, mod, grp_lat, g[0], w, rope_tabs, tm, n_per_batch)
            qc, k2c, vtc = _swa_proj(xc, mod, grp_ctx, g[0], w, None, lc, 1)
            o_l = _swa_attn(swa_sink[j], q, k2, vt, k2c, vtc, nb, seq, lc)
            assert not need_ctx
            o_c = None
        w1 = _layer_to_bf16(w_ff_in, i)
        w2 = _layer_to_bf16(w_ff_out, i)
        xl = _outproj(o_l, w_out, xl, g[1], mod, grp_lat, tm)
        xl = _mlp(xl, g[2], g[3], mod, grp_lat, w1, w2, tm)
        if need_ctx:
            xc = _outproj(o_c, w_out, xc, g[1], mod, grp_ctx, lc)
            xc = _mlp(xc, g[2], g[3], mod, grp_ctx, w1, w2, lc)
    return xl.reshape(nb, seq, d)
```

```python
import functools
import math

import jax
import jax.numpy as jnp
from jax import lax
from jax.experimental import pallas as pl
from jax.experimental.pallas import tpu as pltpu

F32 = jnp.float32
BF16 = jnp.bfloat16

GRID_W = 64
ROPE_BASE = 10000.0
NORM_EPS = 1e-6
LOG2E = math.log2(math.e)
NEG_BIG = -1e30
REF_LIMIT = 60.0
NORM_SLACK = 1.03

MLA_HEADS = 16
MLA_Q_LORA = 512
MLA_KV_LORA = 512
MLA_NOPE = 128
MLA_ROPE = 64
MLA_V = 128

SWA_HEADS = 32
SWA_KV_HEADS = 4
SWA_HEAD_DIM = 64
SWA_WINDOW = 128
SWA_GROUP = SWA_HEADS // SWA_KV_HEADS
SWA_QBLK = 128
SWA_STEP_BLOCKS = 16

LANES = 128
HALF_TILE = 64

MOD_ROWS = 8
MOD_TN = 2048
TOKEN_TILE = 512
ROW_CHUNK = 32
ROW_UNROLL = 8
FF_TILE = 1024
ATTN_TQ = 512
KV_GROUP = 8
KV_UNROLL = 4
CAST_BLOCK_BYTES = 8 * 1024 * 1024
VMEM_LIMIT = 58 * 1024 * 1024

NT_DIMS = (((1,), (1,)), ((), ()))


def _rms(xf, g):
    ms = jnp.mean(xf * xf, axis=-1, keepdims=True)
    return xf * lax.rsqrt(ms + NORM_EPS) * g


def _inv_rms(xf):
    return lax.rsqrt(jnp.mean(xf * xf, axis=-1, keepdims=True) + NORM_EPS)


def _row_chunks(n_rows, fn):
    def body(i, carry):
        fn(pl.ds(pl.multiple_of(i * ROW_CHUNK, ROW_CHUNK), ROW_CHUNK))
        return carry
    lax.fori_loop(0, n_rows // ROW_CHUNK, body, 0, unroll=ROW_UNROLL)


def _modulated_norm(x_ref, g_ref, sc_ref, sh_ref, h_ref):
    gain = g_ref[...] * (1.0 + sc_ref[0])
    shift = sh_ref[0]

    def rows(rs):
        xf = x_ref[rs, :]
        h_ref[rs, :] = (xf * _inv_rms(xf) * gain + shift).astype(h_ref.dtype)
    _row_chunks(h_ref.shape[0], rows)


def _rot(tile, cos, sin):
    return tile * cos + pltpu.roll(tile, HALF_TILE, 1) * sin


def _first_head_lanes(shape):
    lane = lax.broadcasted_iota(jnp.int32, shape, 1)
    return (lane & 32) == 0


def _shift_lane(parity):
    return LANES - 1 - 32 * parity


def _params(sem):
    return pltpu.CompilerParams(dimension_semantics=sem, vmem_limit_bytes=VMEM_LIMIT)


def _const_spec(shape):
    nd = len(shape)
    return pl.BlockSpec(shape, lambda *_: (0,) * nd, pipeline_mode=pl.Buffered(1))


def _mod_spec(grp, which, d):
    return pl.BlockSpec((1, 1, d), lambda i, *_: (grp(i), 0, which))


def _mod_kernel(c_ref, w_ref, b_ref, o_ref):
    c = c_ref[...]
    a = c / (1.0 + jnp.exp(-c))
    o_ref[0] = jnp.dot(a, w_ref[0], preferred_element_type=F32,
                       precision=lax.Precision.HIGHEST) + b_ref[0]


def _modulation(cmat, w_mod, b_mod):
    depth, d, n = w_mod.shape
    return pl.pallas_call(
        _mod_kernel,
        grid=(depth, n // MOD_TN),
        in_specs=[pl.BlockSpec((MOD_ROWS, d), lambda l, j: (0, 0)),
                  pl.BlockSpec((1, d, MOD_TN), lambda l, j: (l, 0, j)),
                  pl.BlockSpec((1, 1, MOD_TN), lambda l, j: (l, 0, j))],
        out_specs=pl.BlockSpec((1, MOD_ROWS, MOD_TN), lambda l, j: (l, 0, j)),
        out_shape=jax.ShapeDtypeStruct((depth, MOD_ROWS, n), F32),
        compiler_params=_params(("arbitrary", "arbitrary")),
        name="modulation",
    )(cmat, w_mod, b_mod.reshape(depth, 1, n))


def _cast_kernel(w_ref, o_ref):
    o_ref[...] = w_ref[0].astype(o_ref.dtype)


def _layer_to_bf16(w_stack, layer):
    _, r, c = w_stack.shape
    br = max(8, min(r, CAST_BLOCK_BYTES // (4 * c)))
    assert r % br == 0
    return pl.pallas_call(
        _cast_kernel,
        grid=(r // br,),
        in_specs=[pl.BlockSpec((1, br, c), lambda j: (layer, j, 0))],
        out_specs=pl.BlockSpec((br, c), lambda j: (j, 0)),
        out_shape=jax.ShapeDtypeStruct((r, c), BF16),
        compiler_params=_params(("arbitrary",)),
        name="cast_bf16",
    )(w_stack)


def _mla_proj_kernel(*refs, rope, qscale):
    if rope:
        (x_ref, g_ref, sc_ref, sh_ref, win_ref, gqa_ref, gkva_ref, wqn_ref, wqr_ref, wkn_ref,
         wvt_ref, cos_ref, sin_ref, cost_ref, sint_ref, qn_ref, qr_ref, kn_ref, kr_ref,
         vt_ref) = refs
        cos, sin = cos_ref[...], sin_ref[...]
    else:
        (x_ref, g_ref, sc_ref, sh_ref, win_ref, gqa_ref, gkva_ref, wqn_ref, wqr_ref, wkn_ref,
         wvt_ref, qn_ref, qr_ref, kn_ref, kr_ref, vt_ref) = refs
    h = (_rms(x_ref[...], g_ref[...]) * (1.0 + sc_ref[0]) + sh_ref[0]).astype(BF16)
    p = jnp.dot(h, win_ref[...], preferred_element_type=F32)
    qa = _rms(p[:, :MLA_Q_LORA], gqa_ref[...]).astype(BF16)
    ckv = _rms(p[:, MLA_Q_LORA:MLA_Q_LORA + MLA_KV_LORA], gkva_ref[...]).astype(BF16)
    kr = p[:, MLA_Q_LORA + MLA_KV_LORA:]

    qn = lax.dot_general(wqn_ref[...], qa, NT_DIMS, preferred_element_type=F32) * qscale
    qn_ref[...] = qn.astype(BF16)
    qr = lax.dot_general(wqr_ref[...], qa, NT_DIMS, preferred_element_type=F32) * qscale
    for t in range(qr.shape[0] // LANES):
        tile = qr[t * LANES:(t + 1) * LANES, :]
        if rope:
            swapped = jnp.concatenate([tile[HALF_TILE:], tile[:HALF_TILE]], axis=0)
            tile = tile * cost_ref[...] + swapped * sint_ref[...]
        qr_ref[t * LANES:(t + 1) * LANES, :] = tile.astype(BF16)

    if rope:
        kr = _rot(kr, cos, sin)
    first = _first_head_lanes(kr.shape)
    lane = lax.broadcasted_iota(jnp.int32, kr.shape, 1)
    kr_ref[:, :LANES] = jnp.where(lane == _shift_lane(0), 1.0,
                                  jnp.where(first, kr, 0.0)).astype(BF16)
    kr_ref[:, LANES:] = jnp.where(lane == _shift_lane(1), 1.0,
                                  jnp.where(first, 0.0, kr)).astype(BF16)

    kn_ref[...] = jnp.dot(ckv, wkn_ref[...], preferred_element_type=F32).astype(BF16)
    vt = lax.dot_general(wvt_ref[...], ckv, NT_DIMS, preferred_element_type=F32).astype(BF16)
    for hd in range(MLA_HEADS):
        vt_ref[0, hd, 0] = vt[hd * MLA_V:(hd + 1) * MLA_V, :]


def _mla_proj(x, mod, grp, g0, w, rope_tabs, tm, n_per_batch):
    t, d = x.shape
    n_tiles = t // tm
    nb = n_tiles // n_per_batch
    rope = rope_tabs is not None
    qscale = (MLA_NOPE + MLA_ROPE) ** -0.5 * LOG2E
    row = lambda i: (i, 0)
    in_specs = [pl.BlockSpec((tm, d), row), _const_spec((1, d)),
                _mod_spec(grp, 1, d), _mod_spec(grp, 0, d),
                _const_spec(w["w_in"].shape), _const_spec((1, MLA_Q_LORA)),
                _const_spec((1, MLA_KV_LORA)), _const_spec(w["w_qn"].shape),
                _const_spec(w["w_qr"].shape), _const_spec(w["w_kn"].shape),
                _const_spec(w["w_vt"].shape)]
    args = [x, g0, mod, mod, w["w_in"], w["g_qa"], w["g_kva"], w["w_qn"], w["w_qr"], w["w_kn"],
            w["w_vt"]]
    if rope:
        pos = lambda i: (i % n_per_batch, 0)
        pos_t = lambda i: (0, i % n_per_batch)
        in_specs += [pl.BlockSpec((tm, LANES), pos), pl.BlockSpec((tm, LANES), pos),
                     pl.BlockSpec((LANES, tm), pos_t), pl.BlockSpec((LANES, tm), pos_t)]
        args += list(rope_tabs)
    hn = MLA_HEADS * MLA_NOPE
    hr = MLA_HEADS * MLA_ROPE
    col = lambda i: (0, i)
    out_shape = [jax.ShapeDtypeStruct((hn, t), BF16), jax.ShapeDtypeStruct((hr, t), BF16),
                 jax.ShapeDtypeStruct((t, hn), BF16), jax.ShapeDtypeStruct((t, 2 * LANES), BF16),
                 jax.ShapeDtypeStruct((nb, MLA_HEADS, n_per_batch, MLA_V, tm), BF16)]
    out_specs = [pl.BlockSpec((hn, tm), col), pl.BlockSpec((hr, tm), col),
                 pl.BlockSpec((tm, hn), row), pl.BlockSpec((tm, 2 * LANES), row),
                 pl.BlockSpec((1, MLA_HEADS, 1, MLA_V, tm),
                              lambda i: (i // n_per_batch, 0, i % n_per_batch, 0, 0))]
    return pl.pallas_call(
        functools.partial(_mla_proj_kernel, rope=rope, qscale=qscale),
        grid=(n_tiles,), in_specs=in_specs, out_specs=out_specs, out_shape=out_shape,
        compiler_params=_params(("arbitrary",)),
        name="mla_proj_lat" if rope else "mla_proj_ctx",
    )(*args)


def _kv_group_sizes(n_chunks):
    full, rest = divmod(n_chunks, KV_GROUP)
    return [KV_GROUP] * full + ([rest] if rest else [])


def _mla_attn_kernel(*refs, n_lat, tk):
    if n_lat:
        (qn_ref, qr_ref, knc_ref, krc_ref, vtc_ref, knl_ref, krl_ref, vtl_ref, o_ref,
         acc_ref, s_ref, qa_ref, k2_ref) = refs
    else:
        qn_ref, qr_ref, knc_ref, krc_ref, vtc_ref, o_ref, acc_ref, s_ref, qa_ref, k2_ref = refs
    tq = qn_ref.shape[1]
    lc = knc_ref.shape[1]
    parity = pl.program_id(1) % 2

    def lat_keys(j):
        off = pl.multiple_of(j * tk, tk)
        return knl_ref[0, pl.ds(off, tk), :], krl_ref[0, pl.ds(off, tk), :]

    @pl.when(pl.program_id(2) == 0)
    def _():
        ones = jnp.ones((LANES, LANES), BF16)

        def sqnorm_max(kn, kr):
            r = jnp.dot(kn * kn + kr * kr, ones, preferred_element_type=F32)
            return jnp.max(r, axis=0, keepdims=True)
        mx = sqnorm_max(knc_ref[0], krc_ref[0])
        if n_lat:
            mx = lax.fori_loop(0, n_lat, lambda j, c: jnp.maximum(c, sqnorm_max(*lat_keys(j))), mx,
                               unroll=KV_UNROLL)
        k2_ref[...] = jnp.broadcast_to(mx * NORM_SLACK, k2_ref.shape)

    qn = qn_ref[...]
    qr = qr_ref[...]
    feat = lax.broadcasted_iota(jnp.int32, qr.shape, 0)
    own = ((feat >> 5) & 1) == parity
    qnf, qrf = qn.astype(F32), jnp.where(own, qr.astype(F32), 0.0)
    q2 = jnp.sum(qnf * qnf + qrf * qrf, axis=0, keepdims=True)
    ref = jnp.sqrt(q2 * k2_ref[0:1, 0:1])
    fast = jnp.max(ref) <= REF_LIMIT
    shift = jnp.where(fast, -ref, 0.0).astype(BF16)
    qa_ref[:LANES, :] = qn
    qa_ref[LANES:, :] = jnp.where(feat == LANES - 1 - 32 * parity, shift, qr)

    def scores(kn, kr):
        return jnp.dot(jnp.concatenate([kn, kr], axis=1), qa_ref[...],
                       preferred_element_type=F32)

    def finish(acc, l):
        o_ref[...] = (acc * (1.0 / l)).T.astype(BF16)

    def shifted_softmax():
        def chunk(s, vt, l8):
            p = jnp.exp2(s)
            l8 = l8 + jnp.sum(p.reshape(-1, 8, tq), axis=0)
            return l8, jnp.dot(vt, p.astype(BF16), preferred_element_type=F32)

        sizes = _kv_group_sizes(n_lat)
        starts = [sum(sizes[:g]) for g in range(len(sizes))]
        last = max(len(sizes) - 1, 0)

        def n_rows(g):
            return (sizes[g] * tk if sizes else 0) + (lc if g == last else 0)

        def fill_scores(g):
            view = s_ref.at[g % 2]
            lat = sizes[g] * tk if sizes else 0
            if lat:
                rows = pl.ds(starts[g] * tk, lat)
                view[pl.ds(0, lat), :] = scores(knl_ref[0, rows, :], krl_ref[0, rows, :])
            if g == last:
                view[pl.ds(lat, lc), :] = scores(knc_ref[0], krc_ref[0])

        fill_scores(0)
        l8 = jnp.zeros((8, tq), F32)
        for g in range(last + 1):
            if g < last:
                fill_scores(g + 1)
            vts = [vtl_ref[0, 0, starts[g] + u] for u in range(sizes[g])] if sizes else []
            if g == last:
                vts.append(vtc_ref[0, 0, 0])
            l8, pv = chunk(s_ref[g % 2, pl.ds(0, n_rows(g)), :], jnp.concatenate(vts, axis=1), l8)
            acc = pv if g == 0 else acc_ref[...] + pv
            if g < last:
                acc_ref[...] = acc
        finish(acc, jnp.sum(l8, axis=0, keepdims=True))

    def online_softmax():
        def update(s_view, vt, m, l):
            m_new = jnp.maximum(m, jnp.max(s_view[...], axis=0, keepdims=True))
            alpha = jnp.exp2(m - m_new)
            p = jnp.exp2(s_view[...] - m_new)
            l_new = alpha * l + jnp.sum(p, axis=0, keepdims=True)
            acc_ref[...] = acc_ref[...] * alpha + jnp.dot(vt, p.astype(BF16),
                                                          preferred_element_type=F32)
            return m_new, l_new

        acc_ref[...] = jnp.zeros_like(acc_ref)
        m = jnp.full((1, tq), NEG_BIG, F32)
        l = jnp.zeros((1, tq), F32)
        slots = [s_ref.at[0, pl.ds(0, tk)], s_ref.at[1, pl.ds(0, tk)]] if n_lat else None
        if n_lat:
            slots[0][...] = scores(*lat_keys(0))
        ctx_view = s_ref.at[1, pl.ds(0, lc)]
        ctx_view[...] = scores(knc_ref[0], krc_ref[0])
        m, l = update(ctx_view, vtc_ref[0, 0, 0], m, l)
        if n_lat:
            def body(jj, carry):
                for u in range(KV_UNROLL):
                    j = KV_UNROLL * jj + u
                    slots[(u + 1) % 2][...] = scores(*lat_keys(jnp.minimum(j + 1, n_lat - 1)))
                    carry = update(slots[u % 2], vtl_ref[0, 0, j], *carry)
                return carry
            m, l = lax.fori_loop(0, n_lat // KV_UNROLL, body, (m, l))
        finish(acc_ref[...], l)

    lax.cond(fast, shifted_softmax, online_softmax)


def _mla_attn(qn, qr, ctx_kv, lat_kv, nb, tq):
    t = qn.shape[1]
    nq = t // nb // tq
    knc, krc, vtc = ctx_kv
    lc = knc.shape[1]
    qrow = lambda b, h, i: (b * nq + i, h)
    in_specs = [pl.BlockSpec((LANES, tq), lambda b, h, i: (h, b * nq + i)),
                pl.BlockSpec((LANES, tq), lambda b, h, i: (h // 2, b * nq + i)),
                pl.BlockSpec((1, lc, LANES), lambda b, h, i: (b, 0, h)),
                pl.BlockSpec((1, lc, LANES), lambda b, h, i: (b, 0, h % 2)),
                pl.BlockSpec((1, 1, 1, MLA_V, lc), lambda b, h, i: (b, h, 0, 0, 0))]
    args = [qn, qr, knc, krc, vtc]
    n_lat, tk = 0, 0
    if lat_kv is not None:
        knl, krl, vtl = lat_kv
        s = knl.shape[1]
        n_lat, tk = vtl.shape[2], vtl.shape[4]
        assert n_lat % KV_UNROLL == 0 and tk >= lc
        in_specs += [pl.BlockSpec((1, s, LANES), lambda b, h, i: (b, 0, h)),
                     pl.BlockSpec((1, s, LANES), lambda b, h, i: (b, 0, h % 2)),
                     pl.BlockSpec((1, 1, n_lat, MLA_V, tk), lambda b, h, i: (b, h, 0, 0, 0))]
        args += [knl, krl, vtl]
    scratch = [pltpu.VMEM((MLA_V, tq), F32), pltpu.VMEM((2, KV_GROUP * tk + lc, tq), F32),
               pltpu.VMEM((2 * LANES, tq), BF16), pltpu.VMEM((8, LANES), F32)]
    return pl.pallas_call(
        functools.partial(_mla_attn_kernel, n_lat=n_lat, tk=tk),
        grid=(nb, MLA_HEADS, nq), in_specs=in_specs,
        out_specs=pl.BlockSpec((tq, MLA_V), qrow),
        out_shape=jax.ShapeDtypeStruct((t, MLA_HEADS * MLA_V), BF16),
        scratch_shapes=scratch,
        compiler_params=_params(("arbitrary", "arbitrary", "arbitrary")),
        name="mla_attn_lat" if n_lat else "mla_attn_ctx",
    )(*args)


def _outproj_kernel(o_ref, w_ref, x_ref, g_ref, gt_ref, out_ref):
    y = jnp.dot(o_ref[...], w_ref[...], preferred_element_type=F32)
    out_ref[...] = x_ref[...] + gt_ref[0] * _rms(y, g_ref[...])


def _outproj(o, w_out, x, g1, mod, grp, tm):
    t, d = x.shape
    row = lambda i: (i, 0)
    return pl.pallas_call(
        _outproj_kernel,
        grid=(t // tm,),
        in_specs=[pl.BlockSpec((tm, o.shape[1]), row), _const_spec(w_out.shape),
                  pl.BlockSpec((tm, d), row), _const_spec((1, d)), _mod_spec(grp, 2, d)],
        out_specs=pl.BlockSpec((tm, d), row),
        out_shape=jax.ShapeDtypeStruct((t, d), F32),
        compiler_params=_params(("arbitrary",)),
        name="outproj",
    )(o, w_out, x, g1, mod)


def _mlp_kernel(x_ref, g2_ref, sc_ref, sh_ref, w1_ref, w2_ref, g3_ref, gt_ref, out_ref,
                f_ref, acc_ref):
    k = pl.program_id(1)

    def ff_chunk():
        u = jnp.maximum(jnp.dot(f_ref[...], w1_ref[...], preferred_element_type=F32), 0.0)
        return jnp.dot((u * u).astype(BF16), w2_ref[...], preferred_element_type=F32)

    @pl.when(k == 0)
    def _():
        _modulated_norm(x_ref, g2_ref, sc_ref, sh_ref, f_ref)
        acc_ref[...] = ff_chunk()

    @pl.when(k > 0)
    def _():
        acc_ref[...] += ff_chunk()

    @pl.when(k == pl.num_programs(1) - 1)
    def _():
        gain = gt_ref[0] * g3_ref[...]

        def rows(rs):
            y = acc_ref[rs, :]
            out_ref[rs, :] = x_ref[rs, :] + y * _inv_rms(y) * gain
        _row_chunks(acc_ref.shape[0], rows)


def _mlp(x, g2, g3, mod, grp, w1, w2, tm):
    t, d = x.shape
    dff = w1.shape[1]
    row = lambda i, k: (i, 0)
    return pl.pallas_call(
        _mlp_kernel,
        grid=(t // tm, dff // FF_TILE),
        in_specs=[pl.BlockSpec((tm, d), row), _const_spec((1, d)),
                  _mod_spec(grp, 4, d), _mod_spec(grp, 3, d),
                  pl.BlockSpec((d, FF_TILE), lambda i, k: (0, k)),
                  pl.BlockSpec((FF_TILE, d), lambda i, k: (k, 0)),
                  _const_spec((1, d)), _mod_spec(grp, 5, d)],
        out_specs=pl.BlockSpec((tm, d), row),
        out_shape=jax.ShapeDtypeStruct((t, d), F32),
        scratch_shapes=[pltpu.VMEM((tm, d), BF16), pltpu.VMEM((tm, d), F32)],
        compiler_params=_params(("arbitrary", "arbitrary")),
        name="mlp",
    )(x, g2, mod, mod, w1, w2, g3, mod)


def _swa_proj_kernel(*refs, rope, qscale):
    if rope:
        (x_ref, g_ref, sc_ref, sh_ref, wqt_ref, wk_ref, wvt_ref, cos_ref, sin_ref, cost_ref,
         sint_ref, q_ref, k2_ref, vt_ref) = refs
        cos, sin = cos_ref[...], sin_ref[...]
    else:
        x_ref, g_ref, sc_ref, sh_ref, wqt_ref, wk_ref, wvt_ref, q_ref, k2_ref, vt_ref = refs
    h = (_rms(x_ref[...], g_ref[...]) * (1.0 + sc_ref[0]) + sh_ref[0]).astype(BF16)
    qt = lax.dot_general(wqt_ref[...], h, NT_DIMS, preferred_element_type=F32) * qscale
    for t in range(qt.shape[0] // LANES):
        tile = qt[t * LANES:(t + 1) * LANES, :]
        if rope:
            swapped = jnp.concatenate([tile[HALF_TILE:], tile[:HALF_TILE]], axis=0)
            tile = tile * cost_ref[...] + swapped * sint_ref[...]
        q_ref[t * LANES:(t + 1) * LANES, :] = tile.astype(BF16)
    k = jnp.dot(h, wk_ref[...], preferred_element_type=F32)
    for c in range(SWA_KV_HEADS):
        tile = k[:, c * LANES:(c + 1) * LANES]
        if rope:
            tile = _rot(tile, cos, sin)
        first = _first_head_lanes(tile.shape)
        lane = lax.broadcasted_iota(jnp.int32, tile.shape, 1)
        k2_ref[:, 2 * c * LANES:(2 * c + 1) * LANES] = jnp.where(
            lane == _shift_lane(0), 1.0, jnp.where(first, tile, 0.0)).astype(BF16)
        k2_ref[:, (2 * c + 1) * LANES:(2 * c + 2) * LANES] = jnp.where(
            lane == _shift_lane(1), 1.0, jnp.where(first, 0.0, tile)).astype(BF16)
    vt_ref[...] = lax.dot_general(wvt_ref[...], h, NT_DIMS,
                                  preferred_element_type=F32).astype(BF16)


def _swa_proj(x, mod, grp, g0, w, rope_tabs, tm, n_per_batch):
    t, d = x.shape
    rope = rope_tabs is not None
    row = lambda i: (i, 0)
    dq = SWA_HEADS * SWA_HEAD_DIM
    dkv = SWA_KV_HEADS * SWA_HEAD_DIM
    w_qt, w_k, w_vt = w
    col = lambda i: (0, i)
    in_specs = [pl.BlockSpec((tm, d), row), _const_spec((1, d)),
                _mod_spec(grp, 1, d), _mod_spec(grp, 0, d), _const_spec(w_qt.shape),
                _const_spec(w_k.shape), _const_spec(w_vt.shape)]
    args = [x, g0, mod, mod, w_qt, w_k, w_vt]
    if rope:
        pos = lambda i: (i % n_per_batch, 0)
        pos_t = lambda i: (0, i % n_per_batch)
        in_specs += [pl.BlockSpec((tm, LANES), pos), pl.BlockSpec((tm, LANES), pos),
                     pl.BlockSpec((LANES, tm), pos_t), pl.BlockSpec((LANES, tm), pos_t)]
        args += list(rope_tabs)
    out_shape = [jax.ShapeDtypeStruct((dq, t), BF16),
                 jax.ShapeDtypeStruct((t, 2 * SWA_KV_HEADS * LANES), BF16),
                 jax.ShapeDtypeStruct((dkv, t), BF16)]
    out_specs = [pl.BlockSpec((dq, tm), col), pl.BlockSpec((tm, 2 * SWA_KV_HEADS * LANES), row),
                 pl.BlockSpec((dkv, tm), col)]
    return pl.pallas_call(
        functools.partial(_swa_proj_kernel, rope=rope, qscale=SWA_HEAD_DIM ** -0.5 * LOG2E),
        grid=(t // tm,), in_specs=in_specs, out_specs=out_specs, out_shape=out_shape,
        compiler_params=_params(("arbitrary",)),
        name="swa_proj_lat" if rope else "swa_proj_ctx",
    )(*args)


def _swa_attn_kernel(sink_ref, q_ref, kc_ref, kp_ref, kcur_ref, kn_ref, vc_ref, vp_ref, vcur_ref,
                     vn_ref, o_ref, s_ref, qa_ref, *, qb):
    kvh = pl.program_id(1)
    i = pl.program_id(2)
    lc = kc_ref.shape[0]
    span = SWA_QBLK + 2 * SWA_WINDOW
    n_pairs = SWA_GROUP // 2
    kc, vc = kc_ref[...], vc_ref[...]
    kwin = jnp.concatenate([kp_ref[...], kcur_ref[...], kn_ref[...]], axis=0)
    vwin = jnp.concatenate([vp_ref[...], vcur_ref[...], vn_ref[...]], axis=1)
    r = lax.broadcasted_iota(jnp.int32, (lc + span, SWA_QBLK), 0)
    rel = r - lc - lax.broadcasted_iota(jnp.int32, (lc + span, SWA_QBLK), 1)
    bias = jnp.where((r < lc) | ((rel >= 0) & (rel <= 2 * SWA_WINDOW)), 0.0, NEG_BIG)
    bias = jnp.concatenate([bias] * n_pairs, axis=1)
    pair = lax.broadcasted_iota(jnp.int32, (1, n_pairs * SWA_QBLK), 1) // SWA_QBLK
    keep_prev = jnp.where(i == 0, 0.0, 1.0)
    keep_next = jnp.where(i == pl.num_programs(2) - 1, 0.0, 1.0)

    ones = jnp.ones((LANES, LANES), BF16)
    kall = jnp.concatenate([kc[:, :LANES], kwin[:, :LANES]], axis=0)
    k2 = jnp.max(jnp.dot(kall * kall, ones, preferred_element_type=F32), axis=0, keepdims=True)
    k2 = k2[:, 0:1] * NORM_SLACK
    feat = lax.broadcasted_iota(jnp.int32, (LANES, n_pairs * SWA_QBLK), 0)
    first_rows = (feat & 32) == 0
    q_t, refs = [], []
    for blk in range(qb):
        qt = jnp.concatenate([q_ref[t * LANES:(t + 1) * LANES, blk * SWA_QBLK:(blk + 1) * SWA_QBLK]
                              for t in range(n_pairs)], axis=1)
        sq = qt.astype(F32)
        sq = sq * sq
        q2_first = jnp.sum(jnp.where(first_rows, sq, 0.0), axis=0, keepdims=True)
        q2_second = jnp.sum(sq, axis=0, keepdims=True) - q2_first
        q_t.append(qt)
        refs.append([jnp.sqrt(q2 * k2).astype(BF16) for q2 in (q2_first, q2_second)])
    ref_max = functools.reduce(jnp.maximum, [jnp.max(r.astype(F32)) for pr in refs for r in pr])
    fast = ref_max <= REF_LIMIT
    def set_queries(shifted):
        for blk in range(qb):
            for e in range(2):
                shift = -refs[blk][e] if shifted else jnp.zeros_like(refs[blk][e])
                qa_ref[blk, e] = jnp.where(feat == _shift_lane(e), shift, q_t[blk])

    def scores(blk, e):
        kcat = jnp.concatenate([kc[:, e * LANES:(e + 1) * LANES],
                                kwin[blk * SWA_QBLK:blk * SWA_QBLK + span,
                                     e * LANES:(e + 1) * LANES]], axis=0)
        return jnp.dot(kcat, qa_ref[blk, e], preferred_element_type=F32)

    def run(shifted):
        s_ref[0] = scores(0, 0)
        for blk in range(qb):
            rows = slice(blk * SWA_QBLK, (blk + 1) * SWA_QBLK)
            vt = jnp.concatenate([vc, vwin[:, blk * SWA_QBLK:blk * SWA_QBLK + span]], axis=1)
            halves = []
            for e in range(2):
                if e == 0:
                    s_ref[1] = scores(blk, 1)
                elif blk + 1 < qb:
                    s_ref[0] = scores(blk + 1, 0)
                s = s_ref[e] + bias
                sk = jnp.zeros(pair.shape, F32)
                for t in range(n_pairs):
                    sk = jnp.where(pair == t, sink_ref[kvh * SWA_GROUP + 2 * t + e] * LOG2E, sk)
                if shifted:
                    m = refs[blk][e].astype(F32)
                    p = jnp.exp2(s)
                else:
                    m = jnp.maximum(jnp.max(s, axis=0, keepdims=True), sk)
                    p = jnp.exp2(s - m)
                if blk == 0:
                    p = jnp.concatenate([p[:lc], p[lc:lc + SWA_WINDOW] * keep_prev,
                                         p[lc + SWA_WINDOW:]], axis=0)
                if blk == qb - 1:
                    p = jnp.concatenate([p[:lc + span - SWA_WINDOW],
                                         p[lc + span - SWA_WINDOW:] * keep_next], axis=0)
                den = jnp.sum(p, axis=0, keepdims=True) + jnp.exp2(sk - m)
                o = jnp.dot(vt, p.astype(BF16), preferred_element_type=F32)
                halves.append(o * (1.0 / den))
            both = jnp.concatenate(halves, axis=0)
            for t in range(n_pairs):
                tile = both[:, t * SWA_QBLK:(t + 1) * SWA_QBLK]
                o_ref[rows, t * LANES:(t + 1) * LANES] = tile.T.astype(BF16)

    set_queries(True)
    run(True)

    @pl.when(jnp.logical_not(fast))
    def _():
        set_queries(False)
        run(False)


def _swa_attn(sink, q, k2, vt, k2c, vtc, nb, seq, lc):
    t = q.shape[1]
    nblk = seq // SWA_QBLK
    qb = min(SWA_STEP_BLOCKS, nblk)
    nsteps = nblk // qb
    gq = SWA_GROUP * SWA_HEAD_DIM
    hd = SWA_HEAD_DIM
    prev_blk = lambda b, i: b * nblk + jnp.maximum(qb * i - 1, 0)
    next_blk = lambda b, i: b * nblk + jnp.minimum(qb * i + qb, nblk - 1)
    return pl.pallas_call(
        functools.partial(_swa_attn_kernel, qb=qb),
        grid=(nb, SWA_KV_HEADS, nsteps),
        in_specs=[pl.BlockSpec(memory_space=pltpu.SMEM),
                  pl.BlockSpec((gq, qb * SWA_QBLK), lambda b, h, i: (h, b * nsteps + i)),
                  pl.BlockSpec((lc, 2 * LANES), lambda b, h, i: (b, h)),
                  pl.BlockSpec((SWA_QBLK, 2 * LANES), lambda b, h, i: (prev_blk(b, i), h)),
                  pl.BlockSpec((qb * SWA_QBLK, 2 * LANES), lambda b, h, i: (b * nsteps + i, h)),
                  pl.BlockSpec((SWA_QBLK, 2 * LANES), lambda b, h, i: (next_blk(b, i), h)),
                  pl.BlockSpec((hd, lc), lambda b, h, i: (h, b)),
                  pl.BlockSpec((hd, SWA_QBLK), lambda b, h, i: (h, prev_blk(b, i))),
                  pl.BlockSpec((hd, qb * SWA_QBLK), lambda b, h, i: (h, b * nsteps + i)),
                  pl.BlockSpec((hd, SWA_QBLK), lambda b, h, i: (h, next_blk(b, i)))],
        out_specs=pl.BlockSpec((qb * SWA_QBLK, gq), lambda b, h, i: (b * nsteps + i, h)),
        out_shape=jax.ShapeDtypeStruct((t, SWA_HEADS * SWA_HEAD_DIM), BF16),
        scratch_shapes=[pltpu.VMEM((2, lc + SWA_QBLK + 2 * SWA_WINDOW, gq), F32),
                        pltpu.VMEM((qb, 2, LANES, gq), BF16)],
        compiler_params=_params(("arbitrary", "arbitrary", "arbitrary")),
        name="swa_attn",
    )(sink, q, k2c, k2, k2, k2, vtc, vt, vt, vt)


def _rope_tables(seq):
    rows = seq // GRID_W
    row = jnp.repeat(jnp.arange(rows, dtype=F32), GRID_W)
    col = jnp.tile(jnp.arange(GRID_W, dtype=F32), rows)
    n_freq = MLA_ROPE // 4
    freqs = ROPE_BASE ** (-jnp.arange(n_freq, dtype=F32) / n_freq)
    ang = jnp.concatenate([row[:, None] * freqs, col[:, None] * freqs], axis=-1)
    cos, sin = jnp.cos(ang), jnp.sin(ang)
    cos_t = jnp.concatenate([cos, cos, cos, cos], axis=-1)
    sin_t = jnp.concatenate([-sin, -sin, sin, sin], axis=-1)
    return cos_t, sin_t, cos_t.T, sin_t.T


def _pair_tiles(w, n_heads, half):
    k = w.shape[0]
    x1 = w[:, :, :half].reshape(k, n_heads // 2, 2 * half)
    x2 = w[:, :, half:].reshape(k, n_heads // 2, 2 * half)
    return jnp.concatenate([x1, x2], axis=2).reshape(k, n_heads * 2 * half)


def _mla_weights(w_in, g_qa, g_kva, w_qb, w_kvb):
    half = MLA_ROPE // 2
    lat = MLA_Q_LORA + MLA_KV_LORA
    k1, k2 = w_in[:, lat:lat + half], w_in[:, lat + half:]
    qb = w_qb.reshape(MLA_Q_LORA, MLA_HEADS, MLA_NOPE + MLA_ROPE)
    kvb = w_kvb.reshape(MLA_KV_LORA, MLA_HEADS, MLA_NOPE + MLA_V)
    return {
        "w_in": jnp.concatenate([w_in[:, :lat], k1, k1, k2, k2], axis=1).astype(BF16),
        "g_qa": g_qa.reshape(1, -1), "g_kva": g_kva.reshape(1, -1),
        "w_qn": qb[:, :, :MLA_NOPE].reshape(MLA_Q_LORA, -1).T.astype(BF16),
        "w_qr": _pair_tiles(qb[:, :, MLA_NOPE:], MLA_HEADS, half).T.astype(BF16),
        "w_kn": kvb[:, :, :MLA_NOPE].reshape(MLA_KV_LORA, -1).astype(BF16),
        "w_vt": kvb[:, :, MLA_NOPE:].reshape(MLA_KV_LORA, -1).T.astype(BF16),
    }


def _swa_weights(w_qkv):
    d = w_qkv.shape[0]
    half = SWA_HEAD_DIM // 2
    dq = SWA_HEADS * SWA_HEAD_DIM
    dkv = SWA_KV_HEADS * SWA_HEAD_DIM
    q = _pair_tiles(w_qkv[:, :dq].reshape(d, SWA_HEADS, SWA_HEAD_DIM), SWA_HEADS, half)
    k = w_qkv[:, dq:dq + dkv].reshape(d, SWA_KV_HEADS, SWA_HEAD_DIM)
    k1, k2 = k[:, :, :half], k[:, :, half:]
    k = jnp.concatenate([k1, k1, k2, k2], axis=2).reshape(d, SWA_KV_HEADS * LANES)
    return q.T.astype(BF16), k.astype(BF16), w_qkv[:, dq + dkv:].T.astype(BF16)


def kernel(x, c, ctx, c_ctx, w_mod, b_mod, g_norm, w_ff_in, w_ff_out, mla_w_in, mla_g_qa,
           mla_g_kva, mla_w_qb, mla_w_kvb, mla_w_out, swa_w_qkv, swa_sink, swa_w_out):
    nb, seq, d = x.shape
    lc = ctx.shape[1]
    depth = w_mod.shape[0]
    assert nb + 1 <= MOD_ROWS
    tm = min(TOKEN_TILE, seq)
    tq = min(ATTN_TQ, seq)
    n_per_batch = seq // tm
    grp_lat = lambda i: i // n_per_batch
    grp_ctx = lambda i: nb

    cmat = jnp.zeros((MOD_ROWS, d), F32).at[:nb].set(c).at[nb].set(c_ctx)
    mod_all = _modulation(cmat, w_mod, b_mod)
    rope_tabs = _rope_tables(seq)

    xl = x.reshape(nb * seq, d)
    xc = ctx.reshape(nb * lc, d)
    for i in range(depth):
        need_ctx = i < depth - 1
        mod = mod_all[i].reshape(MOD_ROWS, 1, 6 * d)
        g = g_norm[i].reshape(4, 1, d)
        j = i // 2
        if i % 2 == 0:
            w = _mla_weights(mla_w_in[j], mla_g_qa[j], mla_g_kva[j], mla_w_qb[j], mla_w_kvb[j])
            w_out = mla_w_out[j].astype(BF16)
            qn, qr, kn, kr, vt = _mla_proj(xl, mod, grp_lat, g[0], w, rope_tabs, tm, n_per_batch)
            qnc, qrc, knc, krc, vtc = _mla_proj(xc, mod, grp_ctx, g[0], w, None, lc, 1)
            ctx_kv = (knc.reshape(nb, lc, -1), krc.reshape(nb, lc, -1), vtc)
            lat_kv = (kn.reshape(nb, seq, -1), kr.reshape(nb, seq, -1), vt)
            o_l = _mla_attn(qn, qr, ctx_kv, lat_kv, nb, tq)
            o_c = _mla_attn(qnc, qrc, ctx_kv, None, nb, lc) if need_ctx else None
        else:
            w = _swa_weights(swa_w_qkv[j])
            w_out = swa_w_out[j].astype(BF16)
            q, k2, vt = _swa_proj(xl, mod, grp_lat, g[0], w, rope_tabs, tm, n_per_batch)
            qc, k2c, vtc = _swa_proj(xc, mod, grp_ctx, g[0], w, None, lc, 1)
            o_l = _swa_attn(swa_sink[j], q, k2, vt, k2c, vtc, nb, seq, lc)
            assert not need_ctx
            o_c = None
        w1 = _layer_to_bf16(w_ff_in, i)
        w2 = _layer_to_bf16(w_ff_out, i)
        xl = _outproj(o_l, w_out, xl, g[1], mod, grp_lat, tm)
        xl = _mlp(xl, g[2], g[3], mod, grp_lat, w1, w2, tm)
        if need_ctx:
            xc = _outproj(o_c, w_out, xc, g[1], mod, grp_ctx, lc)
            xc = _mlp(xc, g[2], g[3], mod, grp_ctx, w1, w2, lc)
    return xl.reshape(nb, seq, d)
```
